```python
import math
import jax, jax.numpy as jnp
from jax import lax
import numpy as np

D_MODEL = 1024
BATCH = 16
SEQ = 2048
DEPTH = 2
DEC_BATCH = 16
DEC_SEQ = 16
PAST_LEN = 1024

CHUNK = 64
Q_BLOCK = 128
N_EVEN = (DEPTH + 1) // 2
N_ODD = DEPTH // 2
EPS = 1e-6
NEG = -1e30
D_FF = 2816
A_HEADS = 8
A_HEAD_DIM = 64
A_PAST_CHUNKS = 8
A_BAND = (A_PAST_CHUNKS + 1) * CHUNK
A_REL_CLIP = 64
B_HEADS = 4
B_HEAD_DIM = 64
T5_BUCKETS = 32
T5_MAX_DIST = 128
C_HEADS = 16
C_NOPE = 64
C_ROPE = 32
C_V = 64
C_Q_LORA = 384
C_KV_LORA = 256
ROPE_THETA = 10000.0

A_W = A_HEADS * A_HEAD_DIM
B_W = B_HEADS * 2 * B_HEAD_DIM
E_IN_W = 3 * A_W + 3 * B_W
E_OUT_W = A_W + B_W
C_IN_W = C_Q_LORA + C_KV_LORA + C_ROPE

kernel_name = 'chunked_relbias_diffattn_mla_macaron_stream_step'

F32 = jnp.float32


def rmsnorm(x, g):
    x32 = x.astype(F32)
    y = x32 * lax.rsqrt(jnp.mean(x32 * x32, axis=-1, keepdims=True) + EPS)
    return (y * g.astype(F32)).astype(x.dtype)


def swiglu_ffn(x, g, w_gu, w_down):
    h = rmsnorm(x, g) @ w_gu
    gate, up = jnp.split(h, 2, axis=-1)
    return (jax.nn.silu(gate) * up) @ w_down


def masked_softmax(logits, mask):
    return jax.nn.softmax(jnp.where(mask, logits, NEG), axis=-1)


def chunk_causal_mask(q_pos, k_pos):
    return (k_pos[None, :] // CHUNK) <= (q_pos[:, None] // CHUNK)


def t5_bucket(rel):
    nb = T5_BUCKETS // 2
    max_exact = nb // 2
    ret = jnp.where(rel > 0, nb, 0)
    n = jnp.abs(rel)
    n_f = jnp.maximum(n, 1).astype(F32)
    large = max_exact + (jnp.log(n_f / max_exact) / math.log(T5_MAX_DIST / max_exact)
                         * (nb - max_exact)).astype(jnp.int32)
    large = jnp.minimum(large, nb - 1)
    return ret + jnp.where(n < max_exact, n, large)


def rope(x, pos):
    half = x.shape[-1] // 2
    inv = ROPE_THETA ** (-jnp.arange(half, dtype=F32) / half)
    ang = pos.astype(F32)[:, None] * inv[None, :]
    cos = jnp.cos(ang)[None, :, None, :]
    sin = jnp.sin(ang)[None, :, None, :]
    x32 = x.astype(F32)
    x1, x2 = x32[..., :half], x32[..., half:]
    return jnp.concatenate([x1 * cos - x2 * sin, x1 * sin + x2 * cos], axis=-1).astype(x.dtype)


def sweep_queries(fn, n_q, *q_arrays):
    def blk(i):
        start = i * Q_BLOCK
        qs = [lax.dynamic_slice_in_dim(a, start, Q_BLOCK, axis=1) for a in q_arrays]
        return fn(start + jnp.arange(Q_BLOCK), *qs)
    out = lax.map(blk, jnp.arange(n_q // Q_BLOCK))
    out = jnp.moveaxis(out, 0, 1)
    return out.reshape((out.shape[0], n_q) + out.shape[3:])


def a_attend(q, k, v, q_pos, k_pos, rel_table):
    logits = jnp.einsum('bqhd,bkhd->bhqk', q, k, preferred_element_type=F32) * (A_HEAD_DIM ** -0.5)
    rel = jnp.clip(q_pos[:, None] - k_pos[None, :], -A_REL_CLIP, A_REL_CLIP) + A_REL_CLIP
    logits = logits + rel_table[:, rel].astype(F32)
    qc, kc = q_pos[:, None] // CHUNK, k_pos[None, :] // CHUNK
    mask = (k_pos[None, :] >= 0) & (kc <= qc) & (kc >= qc - A_PAST_CHUNKS)
    p = masked_softmax(logits, mask)
    return jnp.einsum('bhqk,bkhd->bqhd', p.astype(v.dtype), v)


def a_prompt(q, k, v, rel_table):
    b, s = q.shape[:2]
    pad = A_PAST_CHUNKS * CHUNK
    kp = jnp.pad(k, ((0, 0), (pad, 0), (0, 0), (0, 0)))
    vp = jnp.pad(v, ((0, 0), (pad, 0), (0, 0), (0, 0)))

    def one_chunk(c):
        start = c * CHUNK
        qc = lax.dynamic_slice_in_dim(q, start, CHUNK, axis=1)
        kb = lax.dynamic_slice_in_dim(kp, start, A_BAND, axis=1)
        vb = lax.dynamic_slice_in_dim(vp, start, A_BAND, axis=1)
        q_pos = start + jnp.arange(CHUNK)
        k_pos = start - pad + jnp.arange(A_BAND)
        return a_attend(qc, kb, vb, q_pos, k_pos, rel_table)

    out = lax.map(one_chunk, jnp.arange(s // CHUNK))
    return jnp.moveaxis(out, 0, 1).reshape(b, s, A_HEADS, A_HEAD_DIM)


def a_sample(q, k, v, cache_k, cache_v, q_pos, rel_table):
    n_c = cache_k.shape[1]
    k_all = jnp.concatenate([cache_k, k], axis=1)
    v_all = jnp.concatenate([cache_v, v], axis=1)
    k_pos = jnp.concatenate([PAST_LEN - n_c + jnp.arange(n_c), q_pos])
    return a_attend(q, k_all, v_all, q_pos, k_pos, rel_table)


def diff_lambda(lq1, lk1, lq2, lk2, lam_init):
    return (jnp.exp(jnp.sum(lq1.astype(F32) * lk1.astype(F32)))
            - jnp.exp(jnp.sum(lq2.astype(F32) * lk2.astype(F32))) + lam_init)


def b_attend(q, k, v, q_pos, k_pos, t5_table, lam):
    q2 = q.reshape(q.shape[:3] + (2, B_HEAD_DIM))
    k2 = k.reshape(k.shape[:3] + (2, B_HEAD_DIM))
    logits = jnp.einsum('bqhmd,bkhmd->bhmqk', q2, k2, preferred_element_type=F32) * (B_HEAD_DIM ** -0.5)
    bias = t5_table[t5_bucket(k_pos[None, :] - q_pos[:, None])]
    logits = logits + jnp.transpose(bias, (2, 0, 1))[:, None].astype(F32)
    p = masked_softmax(logits, chunk_causal_mask(q_pos, k_pos))
    w = p[:, :, 0] - lam * p[:, :, 1]
    return jnp.einsum('bhqk,bkhe->bqhe', w.astype(v.dtype), v)


def even_project(h, w_in):
    b, s, _ = h.shape
    cuts = [A_W, 2 * A_W, 3 * A_W, 3 * A_W + B_W, 3 * A_W + 2 * B_W]
    aq, ak, av, bq, bk, bv = jnp.split(h @ w_in, cuts, axis=-1)
    ra = lambda t: t.reshape(b, s, A_HEADS, A_HEAD_DIM)
    rb = lambda t: t.reshape(b, s, B_HEADS, 2 * B_HEAD_DIM)
    return ra(aq), ra(ak), ra(av), rb(bq), rb(bk), rb(bv)


def even_merge(a_out, b_out, lam_init, subln_g, w_out):
    b, s = a_out.shape[:2]
    b_n = rmsnorm(b_out, subln_g) * (1.0 - lam_init)
    o = jnp.concatenate([a_out.reshape(b, s, A_W), b_n.reshape(b, s, B_W)], axis=-1)
    return o @ w_out


def c_project(h, pos, w_in, g_q, g_kv, w_q_up):
    b, s, _ = h.shape
    cq, ckv, kr = jnp.split(h @ w_in, [C_Q_LORA, C_Q_LORA + C_KV_LORA], axis=-1)
    q = (rmsnorm(cq, g_q) @ w_q_up).reshape(b, s, C_HEADS, C_NOPE + C_ROPE)
    q_nope = q[..., :C_NOPE]
    q_rope = rope(q[..., C_NOPE:], pos)
    ckv = rmsnorm(ckv, g_kv)
    kr = rope(kr[:, :, None, :], pos)[:, :, 0, :]
    return q_nope, q_rope, ckv, kr


def c_expand(c_kv, w_kv_up):
    b, k, _ = c_kv.shape
    kv = (c_kv @ w_kv_up).reshape(b, k, C_HEADS, C_NOPE + C_V)
    return kv[..., :C_NOPE], kv[..., C_NOPE:]


def c_attend(q_nope, q_rope, k_nope, v, k_rope, q_pos, k_pos):
    scale = (C_NOPE + C_ROPE) ** -0.5
    logits = (jnp.einsum('bqhd,bkhd->bhqk', q_nope, k_nope, preferred_element_type=F32)
              + jnp.einsum('bqhr,bkr->bhqk', q_rope, k_rope, preferred_element_type=F32)) * scale
    p = masked_softmax(logits, chunk_causal_mask(q_pos, k_pos))
    return jnp.einsum('bhqk,bkhd->bqhd', p.astype(v.dtype), v)


def setup_inputs(seed: int = 0) -> dict:
    key = jax.random.key(seed)
    ks = iter(jax.random.split(key, 40))
    nrm = lambda shape, scale: jax.random.normal(next(ks), shape, F32) * scale
    gain = lambda shape: 1.0 + 0.01 * jax.random.normal(next(ks), shape, F32)
    a_cache = min(A_PAST_CHUNKS * CHUNK, PAST_LEN)
    return {
        'x_prompt': nrm((BATCH, SEQ, D_MODEL), 1.0),
        'x_sample': nrm((DEC_BATCH, DEC_SEQ, D_MODEL), 1.0),
        'cache_a_k': nrm((N_EVEN, DEC_BATCH, a_cache, A_HEADS, A_HEAD_DIM), 1.0),
        'cache_a_v': nrm((N_EVEN, DEC_BATCH, a_cache, A_HEADS, A_HEAD_DIM), 1.0),
        'cache_b_k': nrm((N_EVEN, DEC_BATCH, PAST_LEN, B_HEADS, 2 * B_HEAD_DIM), 1.0),
        'cache_b_v': nrm((N_EVEN, DEC_BATCH, PAST_LEN, B_HEADS, 2 * B_HEAD_DIM), 1.0),
        'cache_c_kv': nrm((N_ODD, DEC_BATCH, PAST_LEN, C_KV_LORA), 1.0),
        'cache_c_kr': nrm((N_ODD, DEC_BATCH, PAST_LEN, C_ROPE), 1.0),
        't5_bias': nrm((T5_BUCKETS, B_HEADS), 0.2),
        'ffn1_norm': gain((DEPTH, D_MODEL)),
        'ffn1_w_gu': nrm((DEPTH, D_MODEL, 2 * D_FF), D_MODEL ** -0.5),
        'ffn1_w_down': nrm((DEPTH, D_FF, D_MODEL), D_FF ** -0.5),
        'mix_norm': gain((DEPTH, D_MODEL)),
        'ffn2_norm': gain((DEPTH, D_MODEL)),
        'ffn2_w_gu': nrm((DEPTH, D_MODEL, 2 * D_FF), D_MODEL ** -0.5),
        'ffn2_w_down': nrm((DEPTH, D_FF, D_MODEL), D_FF ** -0.5),
        'e_w_in': nrm((N_EVEN, D_MODEL, E_IN_W), D_MODEL ** -0.5),
        'a_rel_bias': nrm((N_EVEN, A_HEADS, 2 * A_REL_CLIP + 1), 0.2),
        'b_lambda_q1': nrm((N_EVEN, B_HEAD_DIM), 0.1),
        'b_lambda_k1': nrm((N_EVEN, B_HEAD_DIM), 0.1),
        'b_lambda_q2': nrm((N_EVEN, B_HEAD_DIM), 0.1),
        'b_lambda_k2': nrm((N_EVEN, B_HEAD_DIM), 0.1),
        'b_subln': gain((N_EVEN, 2 * B_HEAD_DIM)),
        'e_w_out': nrm((N_EVEN, E_OUT_W, D_MODEL), E_OUT_W ** -0.5),
        'c_w_in': nrm((N_ODD, D_MODEL, C_IN_W), D_MODEL ** -0.5),
        'c_q_norm': gain((N_ODD, C_Q_LORA)),
        'c_kv_norm': gain((N_ODD, C_KV_LORA)),
        'c_w_q_up': nrm((N_ODD, C_Q_LORA, C_HEADS * (C_NOPE + C_ROPE)), C_Q_LORA ** -0.5),
        'c_w_kv_up': nrm((N_ODD, C_KV_LORA, C_HEADS * (C_NOPE + C_V)), C_KV_LORA ** -0.5),
        'c_w_out': nrm((N_ODD, C_HEADS * C_V, D_MODEL), (C_HEADS * C_V) ** -0.5),
        'final_norm': gain((D_MODEL,)),
    }


def reference(x_prompt, x_sample, cache_a_k, cache_a_v, cache_b_k, cache_b_v, cache_c_kv, cache_c_kr,
              t5_bias, ffn1_norm, ffn1_w_gu, ffn1_w_down, mix_norm, ffn2_norm, ffn2_w_gu, ffn2_w_down,
              e_w_in, a_rel_bias, b_lambda_q1, b_lambda_k1, b_lambda_q2, b_lambda_k2, b_subln, e_w_out,
              c_w_in, c_q_norm, c_kv_norm, c_w_q_up, c_w_kv_up, c_w_out, final_norm):
    yp, ys = x_prompt, x_sample
    seq, t_new = x_prompt.shape[1], x_sample.shape[1]
    pos_p = jnp.arange(seq)
    pos_s = PAST_LEN + jnp.arange(t_new)
    pos_all_s = jnp.arange(PAST_LEN + t_new)
    a_keep = min(A_PAST_CHUNKS * CHUNK, seq)
    pak, pav, pbk, pbv, pckv, pckr = [], [], [], [], [], []
    sak, sav, sbk, sbv, sckv, sckr = [], [], [], [], [], []

    for layer in range(DEPTH):
        yp = yp + 0.5 * swiglu_ffn(yp, ffn1_norm[layer], ffn1_w_gu[layer], ffn1_w_down[layer])
        ys = ys + 0.5 * swiglu_ffn(ys, ffn1_norm[layer], ffn1_w_gu[layer], ffn1_w_down[layer])
        hp = rmsnorm(yp, mix_norm[layer])
        hs = rmsnorm(ys, mix_norm[layer])
        if layer % 2 == 0:
            e = layer // 2
            lam_init = 0.8 - 0.6 * math.exp(-0.3 * layer)
            lam = diff_lambda(b_lambda_q1[e], b_lambda_k1[e], b_lambda_q2[e], b_lambda_k2[e], lam_init)
            aq, ak, av, bq, bk, bv = even_project(hp, e_w_in[e])
            a_out = a_prompt(aq, ak, av, a_rel_bias[e])
            b_out = sweep_queries(lambda qp, qb: b_attend(qb, bk, bv, qp, pos_p, t5_bias, lam), seq, bq)
            yp = yp + even_merge(a_out, b_out, lam_init, b_subln[e], e_w_out[e])
            pak.append(ak[:, seq - a_keep:])
            pav.append(av[:, seq - a_keep:])
            pbk.append(bk)
            pbv.append(bv)
            aq, ak, av, bq, bk, bv = even_project(hs, e_w_in[e])
            a_out = a_sample(aq, ak, av, cache_a_k[e], cache_a_v[e], pos_s, a_rel_bias[e])
            kb_all = jnp.concatenate([cache_b_k[e], bk], axis=1)
            vb_all = jnp.concatenate([cache_b_v[e], bv], axis=1)
            b_out = b_attend(bq, kb_all, vb_all, pos_s, pos_all_s, t5_bias, lam)
            ys = ys + even_merge(a_out, b_out, lam_init, b_subln[e], e_w_out[e])
            sak.append(ak)
            sav.append(av)
            sbk.append(bk)
            sbv.append(bv)
        else:
            o = layer // 2
            qn, qr, ckv, kr = c_project(hp, pos_p, c_w_in[o], c_q_norm[o], c_kv_norm[o], c_w_q_up[o])
            kn, vv = c_expand(ckv, c_w_kv_up[o])
            c_out = sweep_queries(lambda qp, a, b: c_attend(a, b, kn, vv, kr, qp, pos_p), seq, qn, qr)
            yp = yp + c_out.reshape(c_out.shape[0], seq, C_HEADS * C_V) @ c_w_out[o]
            pckv.append(ckv)
            pckr.append(kr)
            qn, qr, ckv, kr = c_project(hs, pos_s, c_w_in[o], c_q_norm[o], c_kv_norm[o], c_w_q_up[o])
            ckv_all = jnp.concatenate([cache_c_kv[o], ckv], axis=1)
            kr_all = jnp.concatenate([cache_c_kr[o], kr], axis=1)
            kn, vv = c_expand(ckv_all, c_w_kv_up[o])
            c_out = c_attend(qn, qr, kn, vv, kr_all, pos_s, pos_all_s)
            ys = ys + c_out.reshape(c_out.shape[0], t_new, C_HEADS * C_V) @ c_w_out[o]
            sckv.append(ckv)
            sckr.append(kr)
        yp = yp + 0.5 * swiglu_ffn(yp, ffn2_norm[layer], ffn2_w_gu[layer], ffn2_w_down[layer])
        ys = ys + 0.5 * swiglu_ffn(ys, ffn2_norm[layer], ffn2_w_gu[layer], ffn2_w_down[layer])

    y_prompt = rmsnorm(yp, final_norm)
    y_sample = rmsnorm(ys, final_norm)
    p_a_k, p_a_v = jnp.stack(pak), jnp.stack(pav)
    p_b_k, p_b_v = jnp.stack(pbk), jnp.stack(pbv)
    p_c_kv, p_c_kr = jnp.stack(pckv), jnp.stack(pckr)
    s_a_k, s_a_v = jnp.stack(sak), jnp.stack(sav)
    s_b_k, s_b_v = jnp.stack(sbk), jnp.stack(sbv)
    s_c_kv, s_c_kr = jnp.stack(sckv), jnp.stack(sckr)
    return (y_prompt, y_sample, p_a_k, p_a_v, p_b_k, p_b_v, p_c_kv, p_c_kr,
            s_a_k, s_a_v, s_b_k, s_b_v, s_c_kv, s_c_kr)
```

```python
import functools
import math

import jax
import jax.numpy as jnp
from jax import lax
from jax.experimental import pallas as pl
from jax.experimental.pallas import tpu as pltpu

F32 = jnp.float32
BF16 = jnp.bfloat16

EPS = 1e-6
NEG = -1e30
CHUNK = 64
A_HEADS = 8
A_HEAD_DIM = 64
A_PAST_CHUNKS = 8
A_REL_CLIP = 64
B_HEADS = 4
B_HEAD_DIM = 64
T5_BUCKETS = 32
T5_MAX_DIST = 128
C_HEADS = 16
C_NOPE = 64
C_ROPE = 32
C_V = 64
C_Q_LORA = 384
C_KV_LORA = 256
ROPE_THETA = 10000.0

LANES = 128
VMEM_LIMIT = 56 * 1024 * 1024
ROW_TILE = 512
ATT_TILE = 256
FFN_CHUNKS = 2


def _rmsnorm(x, g):
    return x * lax.rsqrt(jnp.mean(x * x, axis=-1, keepdims=True) + EPS) * g


def _dot(a, b):
    return jnp.dot(a, b, preferred_element_type=F32)


def _dot_nt(a, b):
    return lax.dot_general(a, b, (((1,), (1,)), ((), ())), preferred_element_type=F32)


def _const_spec(a):
    nd = a.ndim
    return pl.BlockSpec(a.shape, lambda *_: (0,) * nd, pipeline_mode=pl.Buffered(1))


def _params(sem):
    return pltpu.CompilerParams(dimension_semantics=sem, vmem_limit_bytes=VMEM_LIMIT)


def _ffn_kernel(n_pre, with_final, *refs):
    x_ref = refs[0]
    pres = [(refs[1 + 2 * i], refs[2 + 2 * i]) for i in range(n_pre)]
    idx = 1 + 2 * n_pre
    g_ref, wgu_ref, wd_ref = refs[idx:idx + 3]
    idx += 3
    gf_ref = refs[idx] if with_final else None
    out_ref = refs[-1]

    x = x_ref[...]
    for o_ref, w_ref in pres:
        x = x + _dot(o_ref[...], w_ref[...])
    xn = _rmsnorm(x, g_ref[...]).astype(BF16)
    d_ff = wd_ref.shape[0]
    fc = d_ff // FFN_CHUNKS
    acc = jnp.zeros_like(x)
    for c in range(FFN_CHUNKS):
        gate = _dot(xn, wgu_ref[:, c * fc:(c + 1) * fc])
        up = _dot(xn, wgu_ref[:, d_ff + c * fc:d_ff + (c + 1) * fc])
        act = (jax.nn.silu(gate) * up).astype(BF16)
        acc = acc + _dot(act, wd_ref[c * fc:(c + 1) * fc, :])
    y = x + 0.5 * acc
    if with_final:
        y = _rmsnorm(y, gf_ref[...])
    out_ref[...] = y


def _ffn(x, pres, g, wgu, wd, g_final=None):
    t, d = x.shape
    tm = min(ROW_TILE, t)
    row = lambda w: pl.BlockSpec((tm, w), lambda i: (i, 0))
    args, specs = [x], [row(d)]
    for o, w in pres:
        args += [o, w]
        specs += [row(o.shape[1]), _const_spec(w)]
    args += [g, wgu, wd]
    specs += [_const_spec(g), _const_spec(wgu), _const_spec(wd)]
    if g_final is not None:
        args.append(g_final)
        specs.append(_const_spec(g_final))
    return pl.pallas_call(
        functools.partial(_ffn_kernel, len(pres), g_final is not None),
        grid=(t // tm,),
        in_specs=specs,
        out_specs=row(d),
        out_shape=jax.ShapeDtypeStruct((t, d), F32),
        compiler_params=_params(("arbitrary",)),
        name="ffn",
    )(*args)


def _eproj_kernel(x_ref, g_ref, w_ref, aq, ak, av, bq, bk, bv, akf, avf, bkf, bvf):
    hn = _rmsnorm(x_ref[...], g_ref[...]).astype(BF16)
    t = _dot(hn, w_ref[...])
    w = aq.shape[1]
    parts = [t[:, i * w:(i + 1) * w] for i in range(6)]
    aq[...] = (parts[0] * (A_HEAD_DIM ** -0.5)).astype(BF16)
    ak[...] = parts[1].astype(BF16)
    av[...] = parts[2].astype(BF16)
    bq[...] = (parts[3] * (B_HEAD_DIM ** -0.5)).astype(BF16)
    bk[...] = parts[4].astype(BF16)
    bv[...] = parts[5].astype(BF16)
    akf[...] = parts[1]
    avf[...] = parts[2]
    bkf[...] = parts[4]
    bvf[...] = parts[5]


def _eproj(x, g, w):
    t, d = x.shape
    tm = min(ROW_TILE, t)
    wd = w.shape[1] // 6
    row = lambda n: pl.BlockSpec((tm, n), lambda i: (i, 0))
    shapes = [jax.ShapeDtypeStruct((t, wd), BF16)] * 6 + [jax.ShapeDtypeStruct((t, wd), F32)] * 4
    return pl.pallas_call(
        _eproj_kernel,
        grid=(t // tm,),
        in_specs=[row(d), _const_spec(g), _const_spec(w)],
        out_specs=[row(wd)] * 10,
        out_shape=shapes,
        compiler_params=_params(("arbitrary",)),
        name="eproj",
    )(x, g, w)


def _cproj_kernel(x_ref, g_ref, win_ref, gq_ref, gkv_ref, wq_ref, cq_ref, sq_ref, ck_ref, sk_ref,
                  q_out, ckvf_out, ckvb_out, kr_out):
    hn = _rmsnorm(x_ref[...], g_ref[...]).astype(BF16)
    t = _dot(hn, win_ref[...])
    cqn = _rmsnorm(t[:, :C_Q_LORA], gq_ref[...]).astype(BF16)
    ckv = _rmsnorm(t[:, C_Q_LORA:C_Q_LORA + C_KV_LORA], gkv_ref[...])
    ckvf_out[...] = ckv
    ckvb_out[...] = ckv.astype(BF16)
    tg = t[:, C_Q_LORA + C_KV_LORA:]
    kr_out[...] = tg * ck_ref[...] + pltpu.roll(tg, LANES - C_ROPE, 1) * sk_ref[...]
    q = _dot(cqn, wq_ref[...])
    cq, sq = cq_ref[...], sq_ref[...]
    for h in range(C_HEADS):
        qh = q[:, h * LANES:(h + 1) * LANES]
        qf = qh * cq + pltpu.roll(qh, LANES - C_ROPE, 1) * sq
        q_out[:, h * LANES:(h + 1) * LANES] = qf.astype(BF16)


def _cproj(x, g, win, gq, gkv, wq, tabs):
    t, d = x.shape
    tm = min(ROW_TILE, t)
    nper = tabs[0].shape[0] // tm
    row = lambda n: pl.BlockSpec((tm, n), lambda i: (i, 0))
    tab = pl.BlockSpec((tm, LANES), lambda i: (i % nper, 0))
    shapes = [jax.ShapeDtypeStruct((t, C_HEADS * LANES), BF16),
              jax.ShapeDtypeStruct((t, C_KV_LORA), F32),
              jax.ShapeDtypeStruct((t, C_KV_LORA), BF16),
              jax.ShapeDtypeStruct((t, LANES), F32)]
    return pl.pallas_call(
        _cproj_kernel,
        grid=(t // tm,),
        in_specs=[row(d), _const_spec(g), _const_spec(win), _const_spec(gq), _const_spec(gkv),
                  _const_spec(wq), tab, tab, tab, tab],
        out_specs=[row(C_HEADS * LANES), row(C_KV_LORA), row(C_KV_LORA), row(LANES)],
        out_shape=shapes,
        compiler_params=_params(("arbitrary",)),
        name="cproj",
    )(x, g, win, gq, gkv, wq, *tabs)


def _cexpand_kernel(ckv_ref, kr_ref, wk_ref, wv_ref, k_out, v_out):
    ckv = ckv_ref[...]
    kd = _dot(ckv, wk_ref[...])
    kr = kr_ref[...]
    for h in range(C_HEADS):
        k_out[:, h * LANES:(h + 1) * LANES] = (kd[:, h * LANES:(h + 1) * LANES] + kr).astype(BF16)
    v_out[...] = _dot(ckv, wv_ref[...]).astype(BF16)


def _cexpand(ckv, kr, wk, wv, tm):
    t = ckv.shape[0]
    row = lambda n: pl.BlockSpec((tm, n), lambda i: (i, 0))
    return pl.pallas_call(
        _cexpand_kernel,
        grid=(t // tm,),
        in_specs=[row(C_KV_LORA), row(LANES), _const_spec(wk), _const_spec(wv)],
        out_specs=[row(C_HEADS * LANES), row(C_HEADS * C_V)],
        out_shape=[jax.ShapeDtypeStruct((t, C_HEADS * LANES), BF16),
                   jax.ShapeDtypeStruct((t, C_HEADS * C_V), BF16)],
        compiler_params=_params(("arbitrary",)),
        name="cexpand",
    )(ckv, kr, wk, wv)


def _bias_kernel(n_rows, shift, idx_ref, tab_ref, out_ref):
    idx = idx_ref[...]
    n_heads = out_ref.shape[0]
    for h in range(n_heads):
        base = tab_ref[h, shift] if shift is not None else 0.0

        def body(r, acc):
            return jnp.where(idx == r, tab_ref[h, r] - base, acc)

        out_ref[h] = lax.fori_loop(0, n_rows, body, jnp.full(idx.shape, NEG, F32))


def _bias_expand(idx, table, shift=None):
    n_heads, n_rows = table.shape
    return pl.pallas_call(
        functools.partial(_bias_kernel, n_rows, shift),
        in_specs=[pl.BlockSpec(idx.shape, lambda: (0, 0)),
                  pl.BlockSpec(memory_space=pltpu.SMEM)],
        out_specs=pl.BlockSpec((n_heads,) + idx.shape, lambda: (0, 0, 0)),
        out_shape=jax.ShapeDtypeStruct((n_heads,) + idx.shape, F32),
        compiler_params=pltpu.CompilerParams(vmem_limit_bytes=VMEM_LIMIT),
        name="bias_expand",
    )(idx, table)


def _a_attend(q, kwin, vwin, bias_ref, valid):
    lane = lax.broadcasted_iota(jnp.int32, (1, LANES), 1)
    low = lane < A_HEAD_DIM
    outs = []
    for p in range(A_HEADS // 2):
        sl = slice(p * LANES, (p + 1) * LANES)
        qp, kp, vp = q[:, sl], kwin[:, sl], vwin[:, sl]
        halves = []
        for sub in range(2):
            qm = jnp.where(low if sub == 0 else jnp.logical_not(low), qp, jnp.zeros_like(qp))
            s = _dot_nt(qm, kp) + bias_ref[2 * p + sub]
            if valid is not None:
                s = jnp.where(valid, s, NEG)
            m = jnp.max(s, axis=-1, keepdims=True)
            e = jnp.exp(s - m)
            l = jnp.sum(e, axis=-1, keepdims=True)
            halves.append(_dot(e.astype(BF16), vp) / l)
        outs.append(jnp.where(low, halves[0], halves[1]))
    return outs


def _a_prompt_kernel(q_ref, k_ref, v_ref, bias_ref, out_ref):
    tq = q_ref.shape[1]
    win = bias_ref.shape[2]
    pad = win - tq
    start = pl.multiple_of(pl.program_id(1) * tq, tq)
    kwin = k_ref[0, pl.ds(start, win), :]
    vwin = v_ref[0, pl.ds(start, win), :]
    kpos = lax.broadcasted_iota(jnp.int32, (1, win), 1) + (start - pad)
    outs = _a_attend(q_ref[0], kwin, vwin, bias_ref, kpos >= 0)
    for p, o in enumerate(outs):
        out_ref[0, :, p * LANES:(p + 1) * LANES] = o.astype(BF16)


def _a_prompt(q, k_pad, v_pad, bias):
    b, s, w = q.shape
    tq = bias.shape[1]
    kv = pl.BlockSpec((1,) + k_pad.shape[1:], lambda i, j: (i, 0, 0))
    return pl.pallas_call(
        _a_prompt_kernel,
        grid=(b, s // tq),
        in_specs=[pl.BlockSpec((1, tq, w), lambda i, j: (i, j, 0)), kv, kv, _const_spec(bias)],
        out_specs=pl.BlockSpec((1, tq, w), lambda i, j: (i, j, 0)),
        out_shape=jax.ShapeDtypeStruct((b, s, w), BF16),
        compiler_params=_params(("arbitrary", "arbitrary")),
        name="a_prompt",
    )(q, k_pad, v_pad, bias)


def _a_sample_kernel(q_ref, k_ref, v_ref, bias_ref, out_ref):
    outs = _a_attend(q_ref[0], k_ref[0], v_ref[0], bias_ref, None)
    for p, o in enumerate(outs):
        out_ref[0, :, p * LANES:(p + 1) * LANES] = o.astype(BF16)


def _a_sample(q, k_all, v_all, bias):
    b, tq, w = q.shape
    kv = pl.BlockSpec((1,) + k_all.shape[1:], lambda i: (i, 0, 0))
    return pl.pallas_call(
        _a_sample_kernel,
        grid=(b,),
        in_specs=[pl.BlockSpec((1, tq, w), lambda i: (i, 0, 0)), kv, kv, _const_spec(bias)],
        out_specs=pl.BlockSpec((1, tq, w), lambda i: (i, 0, 0)),
        out_shape=jax.ShapeDtypeStruct((b, tq, w), BF16),
        compiler_params=_params(("arbitrary",)),
        name="a_sample",
    )(q, k_all, v_all, bias)


def _b_lambda(lq1, lk1, lq2, lk2, lam_init):
    s1 = jnp.sum(lq1[...] * lk1[...], axis=-1, keepdims=True)
    s2 = jnp.sum(lq2[...] * lk2[...], axis=-1, keepdims=True)
    return jnp.exp(s1) - jnp.exp(s2) + lam_init


def _b_stack_queries(q):
    lane = lax.broadcasted_iota(jnp.int32, (1, LANES), 1)
    low = lane < B_HEAD_DIM
    qs = []
    for h in range(B_HEADS):
        qh = q[:, h * LANES:(h + 1) * LANES]
        zero = jnp.zeros_like(qh)
        qs.append(jnp.concatenate([jnp.where(low, qh, zero), jnp.where(low, zero, qh)], axis=0))
    return qs


def _b_finish(o, lam, g, lam_init, tq):
    ob = o[:tq] - lam * o[tq:]
    return _rmsnorm(ob, g) * (1.0 - lam_init)


def _b_prompt_kernel(lam_init, q_ref, k_ref, v_ref, bias_ref, lq1, lk1, lq2, lk2, g_ref, out_ref,
                     m_ref, l_ref, acc_ref):
    tq = q_ref.shape[1]
    tk = bias_ref.shape[3]
    i = pl.program_id(1)
    m_ref[...] = jnp.full(m_ref.shape, NEG, F32)
    l_ref[...] = jnp.zeros(l_ref.shape, F32)
    acc_ref[...] = jnp.zeros(acc_ref.shape, F32)
    qs = _b_stack_queries(q_ref[0])

    def tile(j, bias_sel):
        start = pl.multiple_of(j * tk, tk)
        ks = k_ref[0, pl.ds(start, tk), :]
        vs = v_ref[0, pl.ds(start, tk), :]
        for h in range(B_HEADS):
            sl = slice(h * LANES, (h + 1) * LANES)
            s = _dot_nt(qs[h], ks[:, sl])
            if bias_sel is not None:
                bias = bias_ref[bias_sel, h]
                s = s + jnp.concatenate([bias, bias], axis=0)
            halves = [s[:, c * LANES:(c + 1) * LANES] for c in range(tk // LANES)]
            mx = halves[0]
            for hv in halves[1:]:
                mx = jnp.maximum(mx, hv)
            m_old = m_ref[h]
            m_new = jnp.maximum(m_old, jnp.max(mx, axis=-1, keepdims=True))
            alpha = jnp.exp(m_old - m_new)
            ps = [jnp.exp(hv - m_new) for hv in halves]
            psum = ps[0]
            for pv in ps[1:]:
                psum = psum + pv
            l_ref[h] = alpha * l_ref[h] + psum
            p = jnp.concatenate(ps, axis=1).astype(BF16)
            acc_ref[h] = alpha * acc_ref[h] + _dot(p, vs[:, sl])
            m_ref[h] = m_new

    def far(j, carry):
        tile(j, None)
        return carry

    lax.fori_loop(0, i - 1, far, 0)

    @pl.when(i >= 1)
    def _():
        tile(i - 1, 0)

    tile(i, 1)

    lam = _b_lambda(lq1, lk1, lq2, lk2, lam_init)
    for h in range(B_HEADS):
        l = jnp.sum(l_ref[h], axis=-1, keepdims=True)
        bn = _b_finish(acc_ref[h] / l, lam, g_ref[...], lam_init, tq)
        out_ref[0, :, h * LANES:(h + 1) * LANES] = bn.astype(BF16)


def _b_prompt(q, k, v, bias, lams, g, lam_init):
    b, s, w = q.shape
    tq = bias.shape[2]
    kv = pl.BlockSpec((1, s, w), lambda i, j: (i, 0, 0))
    scratch = pltpu.VMEM((B_HEADS, 2 * tq, LANES), F32)
    return pl.pallas_call(
        functools.partial(_b_prompt_kernel, lam_init),
        grid=(b, s // tq),
        in_specs=[pl.BlockSpec((1, tq, w), lambda i, j: (i, j, 0)), kv, kv, _const_spec(bias)]
                 + [_const_spec(x) for x in lams] + [_const_spec(g)],
        out_specs=pl.BlockSpec((1, tq, w), lambda i, j: (i, j, 0)),
        out_shape=jax.ShapeDtypeStruct((b, s, w), BF16),
        scratch_shapes=[scratch, scratch, scratch],
        compiler_params=_params(("arbitrary", "arbitrary")),
        name="b_prompt",
    )(q, k, v, bias, *lams, g)


def _b_sample_kernel(lam_init, q_ref, k_ref, v_ref, bias_ref, lq1, lk1, lq2, lk2, g_ref, out_ref):
    tq = q_ref.shape[1]
    qs = _b_stack_queries(q_ref[0])
    ks, vs = k_ref[0], v_ref[0]
    lam = _b_lambda(lq1, lk1, lq2, lk2, lam_init)
    for h in range(B_HEADS):
        sl = slice(h * LANES, (h + 1) * LANES)
        bias = bias_ref[h]
        s = _dot_nt(qs[h], ks[:, sl]) + jnp.concatenate([bias, bias], axis=0)
        m = jnp.max(s, axis=-1, keepdims=True)
        e = jnp.exp(s - m)
        l = jnp.sum(e, axis=-1, keepdims=True)
        o = _dot(e.astype(BF16), vs[:, sl]) / l
        bn = _b_finish(o, lam, g_ref[...], lam_init, tq)
        out_ref[0, :, sl] = bn.astype(BF16)


def _b_sample(q, k_all, v_all, bias, lams, g, lam_init):
    b, tq, w = q.shape
    kv = pl.BlockSpec((1,) + k_all.shape[1:], lambda i: (i, 0, 0))
    return pl.pallas_call(
        functools.partial(_b_sample_kernel, lam_init),
        grid=(b,),
        in_specs=[pl.BlockSpec((1, tq, w), lambda i: (i, 0, 0)), kv, kv, _const_spec(bias)]
                 + [_const_spec(x) for x in lams] + [_const_spec(g)],
        out_specs=pl.BlockSpec((1, tq, w), lambda i: (i, 0, 0)),
        out_shape=jax.ShapeDtypeStruct((b, tq, w), BF16),
        compiler_params=_params(("arbitrary",)),
        name="b_sample",
    )(q, k_all, v_all, bias, *lams, g)


def _c_prompt_kernel(q_ref, k_ref, v_ref, out_ref, m_ref, l_ref, acc_ref):
    tq = q_ref.shape[1]
    tk = tq
    i = pl.program_id(1)
    m_ref[...] = jnp.full(m_ref.shape, NEG, F32)
    l_ref[...] = jnp.zeros(l_ref.shape, F32)
    acc_ref[...] = jnp.zeros(acc_ref.shape, F32)

    def tile(j, masked):
        start = pl.multiple_of(j * tk, tk)
        if masked:
            rc = lax.broadcasted_iota(jnp.int32, (tq, LANES), 0) // CHUNK
        for h in range(C_HEADS):
            sl = slice(h * LANES, (h + 1) * LANES)
            vl = slice((h // 2) * LANES, (h // 2 + 1) * LANES)
            s = _dot_nt(q_ref[0, :, sl], k_ref[0, pl.ds(start, tk), sl])
            halves = [s[:, c * LANES:(c + 1) * LANES] for c in range(tk // LANES)]
            if masked:
                for c in range(len(halves)):
                    kc = (lax.broadcasted_iota(jnp.int32, (tq, LANES), 1) + c * LANES) // CHUNK
                    halves[c] = jnp.where(kc <= rc, halves[c], NEG)
            mx = halves[0]
            for hv in halves[1:]:
                mx = jnp.maximum(mx, hv)
            m_old = m_ref[h]
            m_new = jnp.maximum(m_old, jnp.max(mx, axis=-1, keepdims=True))
            alpha = jnp.exp(m_old - m_new)
            ps = [jnp.exp(hv - m_new) for hv in halves]
            psum = ps[0]
            for pv in ps[1:]:
                psum = psum + pv
            l_ref[h] = alpha * l_ref[h] + psum
            p = jnp.concatenate(ps, axis=1).astype(BF16)
            acc_ref[h] = alpha * acc_ref[h] + _dot(p, v_ref[0, pl.ds(start, tk), vl])
            m_ref[h] = m_new

    def full(j, carry):
        tile(j, False)
        return carry

    lax.fori_loop(0, i, full, 0)
    tile(i, True)

    low = lax.broadcasted_iota(jnp.int32, (1, LANES), 1) < C_V
    for p in range(C_HEADS // 2):
        o = []
        for h in (2 * p, 2 * p + 1):
            l = jnp.sum(l_ref[h], axis=-1, keepdims=True)
            o.append(acc_ref[h] / l)
        out_ref[0, :, p * LANES:(p + 1) * LANES] = jnp.where(low, o[0], o[1]).astype(BF16)


def _c_prompt(q, k, v):
    b, s, wq = q.shape
    wv = v.shape[2]
    tq = ATT_TILE
    scratch = pltpu.VMEM((C_HEADS, tq, LANES), F32)
    return pl.pallas_call(
        _c_prompt_kernel,
        grid=(b, s // tq),
        in_specs=[pl.BlockSpec((1, tq, wq), lambda i, j: (i, j, 0)),
                  pl.BlockSpec((1, s, wq), lambda i, j: (i, 0, 0)),
                  pl.BlockSpec((1, s, wv), lambda i, j: (i, 0, 0))],
        out_specs=pl.BlockSpec((1, tq, wv), lambda i, j: (i, j, 0)),
        out_shape=jax.ShapeDtypeStruct((b, s, wv), BF16),
        scratch_shapes=[scratch, scratch, scratch],
        compiler_params=_params(("arbitrary", "arbitrary")),
        name="c_prompt",
    )(q, k, v)


def _c_sample_kernel(q_ref, k_ref, v_ref, mask_ref, out_ref):
    low = lax.broadcasted_iota(jnp.int32, (1, LANES), 1) < C_V
    mask = mask_ref[...]
    for p in range(C_HEADS // 2):
        vp = v_ref[0, :, p * LANES:(p + 1) * LANES]
        o = []
        for h in (2 * p, 2 * p + 1):
            sl = slice(h * LANES, (h + 1) * LANES)
            s = _dot_nt(q_ref[0, :, sl], k_ref[0, :, sl]) + mask
            m = jnp.max(s, axis=-1, keepdims=True)
            e = jnp.exp(s - m)
            l = jnp.sum(e, axis=-1, keepdims=True)
            o.append(_dot(e.astype(BF16), vp) / l)
        out_ref[0, :, p * LANES:(p + 1) * LANES] = jnp.where(low, o[0], o[1]).astype(BF16)


def _c_sample(q, k_all, v_all, mask):
    b, tq, wq = q.shape
    kk = k_all.shape[1]
    wv = v_all.shape[2]
    return pl.pallas_call(
        _c_sample_kernel,
        grid=(b,),
        in_specs=[pl.BlockSpec((1, tq, wq), lambda i: (i, 0, 0)),
                  pl.BlockSpec((1, kk, wq), lambda i: (i, 0, 0)),
                  pl.BlockSpec((1, kk, wv), lambda i: (i, 0, 0)),
                  _const_spec(mask)],
        out_specs=pl.BlockSpec((1, tq, wv), lambda i: (i, 0, 0)),
        out_shape=jax.ShapeDtypeStruct((b, tq, wv), BF16),
        compiler_params=_params(("arbitrary",)),
        name="c_sample",
    )(q, k_all, v_all, mask)


def _t5_bucket(rel):
    nb = T5_BUCKETS // 2
    max_exact = nb // 2
    ret = jnp.where(rel > 0, nb, 0)
    n = jnp.abs(rel)
    n_f = jnp.maximum(n, 1).astype(F32)
    large = max_exact + (jnp.log(n_f / max_exact) / math.log(T5_MAX_DIST / max_exact)
                         * (nb - max_exact)).astype(jnp.int32)
    large = jnp.minimum(large, nb - 1)
    return ret + jnp.where(n < max_exact, n, large)


def _a_index(q_pos, k_pos, k_real):
    rel = jnp.clip(q_pos[:, None] - k_pos[None, :], -A_REL_CLIP, A_REL_CLIP) + A_REL_CLIP
    qc, kc = q_pos[:, None] // CHUNK, k_pos[None, :] // CHUNK
    ok = k_real[None, :] & (kc <= qc) & (kc >= qc - A_PAST_CHUNKS)
    return jnp.where(ok, rel, -1).astype(jnp.int32)


def _b_index(q_pos, k_pos, k_real):
    idx = _t5_bucket(k_pos[None, :] - q_pos[:, None])
    ok = k_real[None, :] & ((k_pos[None, :] // CHUNK) <= (q_pos[:, None] // CHUNK))
    return jnp.where(ok, idx, -1).astype(jnp.int32)


def _rope_tables(pos, scale):
    half = C_ROPE // 2
    inv = ROPE_THETA ** (-jnp.arange(half, dtype=F32) / half)
    ang = pos.astype(F32)[:, None] * inv[None, :]
    cos = jnp.concatenate([jnp.cos(ang)] * 2, axis=-1)
    sin = jnp.concatenate([jnp.sin(ang)] * 2, axis=-1)
    n = pos.shape[0]
    z_nope = jnp.zeros((n, C_NOPE), F32)
    z_tail = jnp.zeros((n, LANES - C_NOPE - C_ROPE), F32)
    cq = jnp.concatenate([jnp.full((n, C_NOPE), scale, F32), cos * scale, z_tail], axis=-1)
    sq = jnp.concatenate([z_nope, sin * scale, z_tail], axis=-1)
    ck = jnp.concatenate([z_nope, cos, z_tail], axis=-1)
    sk = jnp.concatenate([z_nope, sin, z_tail], axis=-1)
    return cq, sq, ck, sk


def _rot_cols(w):
    half = w.shape[-1] // 2
    return jnp.concatenate([-w[..., half:], w[..., :half]], axis=-1)


def _pad_rows(x, n):
    return jnp.pad(x, ((0, 0), (0, n - x.shape[1]), (0, 0)))


def kernel(x_prompt, x_sample, cache_a_k, cache_a_v, cache_b_k, cache_b_v, cache_c_kv, cache_c_kr,
           t5_bias, ffn1_norm, ffn1_w_gu, ffn1_w_down, mix_norm, ffn2_norm, ffn2_w_gu, ffn2_w_down,
           e_w_in, a_rel_bias, b_lambda_q1, b_lambda_k1, b_lambda_q2, b_lambda_k2, b_subln, e_w_out,
           c_w_in, c_q_norm, c_kv_norm, c_w_q_up, c_w_kv_up, c_w_out, final_norm):
    batch, seq, d = x_prompt.shape
    dec_batch, t_new, _ = x_sample.shape
    past = cache_b_k.shape[2]
    n_cache_a = cache_a_k.shape[2]
    a_w = A_HEADS * A_HEAD_DIM
    b_w = B_HEADS * 2 * B_HEAD_DIM
    tq = ATT_TILE
    a_pad = A_PAST_CHUNKS * CHUNK
    a_keep = min(a_pad, seq)
    assert tq + 1 >= T5_MAX_DIST and tq % CHUNK == 0 and a_pad % tq == 0
    far_bucket = T5_BUCKETS // 2 - 1
    lam_init = 0.8 - 0.6 * math.exp(-0.3 * 0)
    c_scale = (C_NOPE + C_ROPE) ** -0.5
    row2 = lambda v: v.reshape(1, -1)

    wgu1, wd1 = ffn1_w_gu.astype(BF16), ffn1_w_down.astype(BF16)
    wgu2, wd2 = ffn2_w_gu.astype(BF16), ffn2_w_down.astype(BF16)
    e_in = e_w_in[0].astype(BF16)
    e_out_a, e_out_b = e_w_out[0, :a_w].astype(BF16), e_w_out[0, a_w:].astype(BF16)
    w_in = c_w_in[0]
    w_kr = w_in[:, C_Q_LORA + C_KV_LORA:]
    c_in = jnp.concatenate([w_in[:, :C_Q_LORA + C_KV_LORA], jnp.zeros((d, C_NOPE), F32),
                            w_kr, _rot_cols(w_kr)], axis=-1).astype(BF16)
    wq = c_w_q_up[0].reshape(C_Q_LORA, C_HEADS, C_NOPE + C_ROPE)
    wq_rope = wq[..., C_NOPE:]
    c_q = jnp.concatenate([wq[..., :C_NOPE], wq_rope, _rot_cols(wq_rope)], axis=-1)
    c_q = c_q.reshape(C_Q_LORA, C_HEADS * LANES).astype(BF16)
    wkv = c_w_kv_up[0].reshape(C_KV_LORA, C_HEADS, C_NOPE + C_V)
    c_k = jnp.concatenate([wkv[..., :C_NOPE], jnp.zeros_like(wkv[..., C_NOPE:])], axis=-1)
    c_k = c_k.reshape(C_KV_LORA, C_HEADS * LANES).astype(BF16)
    c_v = wkv[..., C_NOPE:].reshape(C_KV_LORA, C_HEADS * C_V).astype(BF16)
    c_out_w = c_w_out[0].astype(BF16)
    lams = [row2(b_lambda_q1[0]), row2(b_lambda_k1[0]), row2(b_lambda_q2[0]), row2(b_lambda_k2[0])]
    subln = row2(b_subln[0])

    pos_s = past + jnp.arange(t_new)
    r = jnp.arange(tq)
    a_idx_p = _a_index(a_pad + r, jnp.arange(a_pad + tq), jnp.ones((a_pad + tq,), bool))
    ka = n_cache_a + t_new
    ka_pad = -(-ka // LANES) * LANES
    a_kpos = jnp.concatenate([past - n_cache_a + jnp.arange(n_cache_a), pos_s,
                              jnp.zeros((ka_pad - ka,), jnp.int32)])
    a_real = jnp.arange(ka_pad) < ka
    a_idx_s = _a_index(pos_s, a_kpos, a_real & (a_kpos >= 0))
    b_idx_p = jnp.stack([_b_index(tq + r, r, jnp.ones((tq,), bool)),
                         _b_index(r, r, jnp.ones((tq,), bool))])
    kb = past + t_new
    kb_pad = -(-kb // LANES) * LANES
    b_kpos = jnp.arange(kb_pad)
    b_real = b_kpos < kb
    b_idx_s = _b_index(pos_s, b_kpos, b_real)
    c_mask_s = jnp.where(b_real[None, :] & ((b_kpos[None, :] // CHUNK) <= (pos_s[:, None] // CHUNK)),
                         0.0, NEG).astype(F32)

    a_bias_p = _bias_expand(a_idx_p, a_rel_bias[0])
    a_bias_s = _bias_expand(a_idx_s, a_rel_bias[0])
    t5_t = t5_bias.T
    b_bias_p = jnp.stack([_bias_expand(b_idx_p[0], t5_t, far_bucket),
                          _bias_expand(b_idx_p[1], t5_t, far_bucket)])
    b_bias_s = _bias_expand(b_idx_s, t5_t, far_bucket)

    xp = x_prompt.reshape(batch * seq, d)
    xs = x_sample.reshape(dec_batch * t_new, d)
    g1, gm, g2 = row2(ffn1_norm[0]), row2(mix_norm[0]), row2(ffn2_norm[0])

    xp = _ffn(xp, [], g1, wgu1[0], wd1[0])
    xs = _ffn(xs, [], g1, wgu1[0], wd1[0])

    aq, ak, av, bq, bk, bv, akf, avf, bkf, bvf = _eproj(xp, gm, e_in)
    to3 = lambda t, n: t.reshape(n, -1, t.shape[-1])
    pad_front = lambda t: jnp.pad(to3(t, batch), ((0, 0), (a_pad, 0), (0, 0)))
    a_out = _a_prompt(to3(aq, batch), pad_front(ak), pad_front(av), a_bias_p)
    b_out = _b_prompt(to3(bq, batch), to3(bk, batch), to3(bv, batch), b_bias_p, lams, subln, lam_init)
    p_a_k = akf.reshape(batch, seq, A_HEADS, A_HEAD_DIM)[:, seq - a_keep:][None]
    p_a_v = avf.reshape(batch, seq, A_HEADS, A_HEAD_DIM)[:, seq - a_keep:][None]
    p_b_k = bkf.reshape(1, batch, seq, B_HEADS, 2 * B_HEAD_DIM)
    p_b_v = bvf.reshape(1, batch, seq, B_HEADS, 2 * B_HEAD_DIM)
    xp = _ffn(xp, [(a_out.reshape(-1, a_w), e_out_a), (b_out.reshape(-1, b_w), e_out_b)],
              g2, wgu2[0], wd2[0])

    aq, ak, av, bq, bk, bv, akf, avf, bkf, bvf = _eproj(xs, gm, e_in)
    cat = lambda cache, new, n: _pad_rows(jnp.concatenate(
        [cache.reshape(dec_batch, cache.shape[1], -1).astype(BF16), to3(new, dec_batch)], axis=1), n)
    a_out = _a_sample(to3(aq, dec_batch), cat(cache_a_k[0], ak, ka_pad), cat(cache_a_v[0], av, ka_pad),
                      a_bias_s)
    b_out = _b_sample(to3(bq, dec_batch), cat(cache_b_k[0], bk, kb_pad), cat(cache_b_v[0], bv, kb_pad),
                      b_bias_s, lams, subln, lam_init)
    s_a_k = akf.reshape(1, dec_batch, t_new, A_HEADS, A_HEAD_DIM)
    s_a_v = avf.reshape(1, dec_batch, t_new, A_HEADS, A_HEAD_DIM)
    s_b_k = bkf.reshape(1, dec_batch, t_new, B_HEADS, 2 * B_HEAD_DIM)
    s_b_v = bvf.reshape(1, dec_batch, t_new, B_HEADS, 2 * B_HEAD_DIM)
    xs = _ffn(xs, [(a_out.reshape(-1, a_w), e_out_a), (b_out.reshape(-1, b_w), e_out_b)],
              g2, wgu2[0], wd2[0])

    g1, gm, g2 = row2(ffn1_norm[1]), row2(mix_norm[1]), row2(ffn2_norm[1])
    gq, gkv, gfin = row2(c_q_norm[0]), row2(c_kv_norm[0]), row2(final_norm)
    xp = _ffn(xp, [], g1, wgu1[1], wd1[1])
    xs = _ffn(xs, [], g1, wgu1[1], wd1[1])

    q, ckvf, ckvb, kr = _cproj(xp, gm, c_in, gq, gkv, c_q, _rope_tables(jnp.arange(seq), c_scale))
    k, v = _cexpand(ckvb, kr, c_k, c_v, min(ROW_TILE, batch * seq))
    c_out = _c_prompt(to3(q, batch), to3(k, batch), to3(v, batch))
    p_c_kv = ckvf.reshape(1, batch, seq, C_KV_LORA)
    p_c_kr = kr[:, C_NOPE:C_NOPE + C_ROPE].reshape(1, batch, seq, C_ROPE)
    y_prompt = _ffn(xp, [(c_out.reshape(-1, C_HEADS * C_V), c_out_w)], g2, wgu2[1], wd2[1], gfin)

    q, ckvf, ckvb, kr = _cproj(xs, gm, c_in, gq, gkv, c_q,
                               _rope_tables(jnp.tile(pos_s, dec_batch), c_scale))
    ckv_all = _pad_rows(jnp.concatenate([cache_c_kv[0].astype(BF16), to3(ckvb, dec_batch)], axis=1), kb_pad)
    kr_cache = jnp.pad(cache_c_kr[0], ((0, 0), (0, 0), (C_NOPE, LANES - C_NOPE - C_ROPE)))
    kr_all = _pad_rows(jnp.concatenate([kr_cache, to3(kr, dec_batch)], axis=1), kb_pad)
    k, v = _cexpand(ckv_all.reshape(-1, C_KV_LORA), kr_all.reshape(-1, LANES), c_k, c_v, kb_pad)
    c_out = _c_sample(to3(q, dec_batch), to3(k, dec_batch), to3(v, dec_batch), c_mask_s)
    s_c_kv = ckvf.reshape(1, dec_batch, t_new, C_KV_LORA)
    s_c_kr = kr[:, C_NOPE:C_NOPE + C_ROPE].reshape(1, dec_batch, t_new, C_ROPE)
    y_sample = _ffn(xs, [(c_out.reshape(-1, C_HEADS * C_V), c_out_w)], g2, wgu2[1], wd2[1], gfin)

    return (y_prompt.reshape(batch, seq, d), y_sample.reshape(dec_batch, t_new, d),
            p_a_k, p_a_v, p_b_k, p_b_v, p_c_kv, p_c_kr,
            s_a_k, s_a_v, s_b_k, s_b_v, s_c_kv, s_c_kr)
```

```python
import functools
import math

import jax
import jax.numpy as jnp
import numpy as np
from jax import lax
from jax.experimental import pallas as pl
from jax.experimental.pallas import tpu as pltpu

F32 = jnp.float32
BF16 = jnp.bfloat16

EPS = 1e-6
NEG = -1e30
CHUNK = 64
A_HEADS = 8
A_HEAD_DIM = 64
A_PAST_CHUNKS = 8
A_REL_CLIP = 64
B_HEADS = 4
B_HEAD_DIM = 64
T5_BUCKETS = 32
T5_MAX_DIST = 128
C_HEADS = 16
C_NOPE = 64
C_ROPE = 32
C_V = 64
C_Q_LORA = 384
C_KV_LORA = 256
ROPE_THETA = 10000.0

LANES = 128
VMEM_LIMIT = 56 * 1024 * 1024
ROW_TILE = 512
ATT_TILE = 256
FFN_CHUNKS = 2


def _rmsnorm(x, g):
    return x * lax.rsqrt(jnp.mean(x * x, axis=-1, keepdims=True) + EPS) * g


def _dot(a, b):
    return jnp.dot(a, b, preferred_element_type=F32)


def _dot_nt(a, b):
    return lax.dot_general(a, b, (((1,), (1,)), ((), ())), preferred_element_type=F32)


def _const_spec(a):
    nd = a.ndim
    return pl.BlockSpec(a.shape, lambda *_: (0,) * nd, pipeline_mode=pl.Buffered(1))


def _params(sem):
    return pltpu.CompilerParams(dimension_semantics=sem, vmem_limit_bytes=VMEM_LIMIT)


def _ffn_kernel(n_pre, with_final, *refs):
    x_ref = refs[0]
    pres = [(refs[1 + 2 * i], refs[2 + 2 * i]) for i in range(n_pre)]
    idx = 1 + 2 * n_pre
    g_ref, wgu_ref, wd_ref = refs[idx:idx + 3]
    idx += 3
    gf_ref = refs[idx] if with_final else None
    out_ref = refs[-1]

    x = x_ref[...]
    for o_ref, w_ref in pres:
        x = x + _dot(o_ref[...], w_ref[...])
    xn = _rmsnorm(x, g_ref[...]).astype(BF16)
    d_ff = wd_ref.shape[0]
    fc = d_ff // FFN_CHUNKS
    acc = jnp.zeros_like(x)
    for c in range(FFN_CHUNKS):
        gate = _dot(xn, wgu_ref[:, c * fc:(c + 1) * fc])
        up = _dot(xn, wgu_ref[:, d_ff + c * fc:d_ff + (c + 1) * fc])
        act = (jax.nn.silu(gate) * up).astype(BF16)
        acc = acc + _dot(act, wd_ref[c * fc:(c + 1) * fc, :])
    y = x + 0.5 * acc
    if with_final:
        y = _rmsnorm(y, gf_ref[...])
    out_ref[...] = y


def _ffn(x, pres, g, wgu, wd, g_final=None):
    t, d = x.shape
    tm = min(ROW_TILE, t)
    row = lambda w: pl.BlockSpec((tm, w), lambda i: (i, 0))
    args, specs = [x], [row(d)]
    for o, w in pres:
        args += [o, w]
        specs += [row(o.shape[1]), _const_spec(w)]
    args += [g, wgu, wd]
    specs += [_const_spec(g), _const_spec(wgu), _const_spec(wd)]
    if g_final is not None:
        args.append(g_final)
        specs.append(_const_spec(g_final))
    return pl.pallas_call(
        functools.partial(_ffn_kernel, len(pres), g_final is not None),
        grid=(t // tm,),
        in_specs=specs,
        out_specs=row(d),
        out_shape=jax.ShapeDtypeStruct((t, d), F32),
        compiler_params=_params(("arbitrary",)),
        name="ffn",
    )(*args)


def _eproj_kernel(x_ref, g_ref, w_ref, aq, ak, av, bq, bk, bv, akf, avf, bkf, bvf):
    hn = _rmsnorm(x_ref[...], g_ref[...]).astype(BF16)
    t = _dot(hn, w_ref[...])
    w = aq.shape[1]
    parts = [t[:, i * w:(i + 1) * w] for i in range(6)]
    aq[...] = (parts[0] * (A_HEAD_DIM ** -0.5)).astype(BF16)
    ak[...] = parts[1].astype(BF16)
    av[...] = parts[2].astype(BF16)
    bq[...] = (parts[3] * (B_HEAD_DIM ** -0.5)).astype(BF16)
    bk[...] = parts[4].astype(BF16)
    bv[...] = parts[5].astype(BF16)
    akf[...] = parts[1]
    avf[...] = parts[2]
    bkf[...] = parts[4]
    bvf[...] = parts[5]


def _eproj(x, g, w):
    t, d = x.shape
    tm = min(ROW_TILE, t)
    wd = w.shape[1] // 6
    row = lambda n: pl.BlockSpec((tm, n), lambda i: (i, 0))
    shapes = [jax.ShapeDtypeStruct((t, wd), BF16)] * 6 + [jax.ShapeDtypeStruct((t, wd), F32)] * 4
    return pl.pallas_call(
        _eproj_kernel,
        grid=(t // tm,),
        in_specs=[row(d), _const_spec(g), _const_spec(w)],
        out_specs=[row(wd)] * 10,
        out_shape=shapes,
        compiler_params=_params(("arbitrary",)),
        name="eproj",
    )(x, g, w)


def _cproj_kernel(x_ref, g_ref, win_ref, gq_ref, gkv_ref, wq_ref, cq_ref, sq_ref, ck_ref, sk_ref,
                  q_out, ckvf_out, ckvb_out, kr_out):
    hn = _rmsnorm(x_ref[...], g_ref[...]).astype(BF16)
    t = _dot(hn, win_ref[...])
    cqn = _rmsnorm(t[:, :C_Q_LORA], gq_ref[...]).astype(BF16)
    ckv = _rmsnorm(t[:, C_Q_LORA:C_Q_LORA + C_KV_LORA], gkv_ref[...])
    ckvf_out[...] = ckv
    ckvb_out[...] = ckv.astype(BF16)
    tg = t[:, C_Q_LORA + C_KV_LORA:]
    kr_out[...] = tg * ck_ref[...] + pltpu.roll(tg, LANES - C_ROPE, 1) * sk_ref[...]
    q = _dot(cqn, wq_ref[...])
    cq, sq = cq_ref[...], sq_ref[...]
    for h in range(C_HEADS):
        qh = q[:, h * LANES:(h + 1) * LANES]
        qf = qh * cq + pltpu.roll(qh, LANES - C_ROPE, 1) * sq
        q_out[:, h * LANES:(h + 1) * LANES] = qf.astype(BF16)


def _cproj(x, g, win, gq, gkv, wq, tabs):
    t, d = x.shape
    tm = min(ROW_TILE, t)
    nper = tabs[0].shape[0] // tm
    row = lambda n: pl.BlockSpec((tm, n), lambda i: (i, 0))
    tab = pl.BlockSpec((tm, LANES), lambda i: (i % nper, 0))
    shapes = [jax.ShapeDtypeStruct((t, C_HEADS * LANES), BF16),
              jax.ShapeDtypeStruct((t, C_KV_LORA), F32),
              jax.ShapeDtypeStruct((t, C_KV_LORA), BF16),
              jax.ShapeDtypeStruct((t, LANES), F32)]
    return pl.pallas_call(
        _cproj_kernel,
        grid=(t // tm,),
        in_specs=[row(d), _const_spec(g), _const_spec(win), _const_spec(gq), _const_spec(gkv),
                  _const_spec(wq), tab, tab, tab, tab],
        out_specs=[row(C_HEADS * LANES), row(C_KV_LORA), row(C_KV_LORA), row(LANES)],
        out_shape=shapes,
        compiler_params=_params(("arbitrary",)),
        name="cproj",
    )(x, g, win, gq, gkv, wq, *tabs)


def _cexpand_kernel(ckv_ref, kr_ref, wk_ref, wv_ref, k_out, v_out):
    ckv = ckv_ref[...]
    kd = _dot(ckv, wk_ref[...])
    vd = _dot(ckv, wv_ref[...])
    kr = kr_ref[...]
    ones = (lax.broadcasted_iota(jnp.int32, (1, LANES), 1) >= C_V).astype(F32)
    for h in range(C_HEADS):
        sl = slice(h * LANES, (h + 1) * LANES)
        k_out[:, sl] = (kd[:, sl] + kr).astype(BF16)
        v_out[:, sl] = (vd[:, sl] + ones).astype(BF16)


def _cexpand(ckv, kr, wk, wv, tm):
    t = ckv.shape[0]
    row = lambda n: pl.BlockSpec((tm, n), lambda i: (i, 0))
    return pl.pallas_call(
        _cexpand_kernel,
        grid=(t // tm,),
        in_specs=[row(C_KV_LORA), row(LANES), _const_spec(wk), _const_spec(wv)],
        out_specs=[row(C_HEADS * LANES), row(C_HEADS * LANES)],
        out_shape=[jax.ShapeDtypeStruct((t, C_HEADS * LANES), BF16)] * 2,
        compiler_params=_params(("arbitrary",)),
        name="cexpand",
    )(ckv, kr, wk, wv)


def _bias_kernel(n_rows, shift, layout, idx_ref, tab_ref, out_ref):
    idx = idx_ref[...]
    n_heads = out_ref.shape[0]
    bq = out_ref.shape[1] // len(layout)
    bk = out_ref.shape[2] // len(layout[0])
    for h in range(n_heads):
        base = tab_ref[h, shift] if shift is not None else 0.0

        def body(r, acc):
            return jnp.where(idx == r, tab_ref[h, r] - base, acc)

        vals = lax.fori_loop(0, n_rows, body, jnp.full(idx.shape, NEG, F32))
        for a, row in enumerate(layout):
            for t, u in enumerate(row):
                out_ref[h, a * bq:(a + 1) * bq, t * bk:(t + 1) * bk] = vals[u * bq:(u + 1) * bq, :]


def _bias_expand(idx, table, shift=None, layout=((0,),)):
    n_heads, n_rows = table.shape
    n_blocks = 1 + max(max(row) for row in layout)
    out_shape = (n_heads, idx.shape[0] // n_blocks * len(layout), idx.shape[1] * len(layout[0]))
    return pl.pallas_call(
        functools.partial(_bias_kernel, n_rows, shift, layout),
        in_specs=[pl.BlockSpec(idx.shape, lambda: (0, 0)),
                  pl.BlockSpec(memory_space=pltpu.SMEM)],
        out_specs=pl.BlockSpec(out_shape, lambda: (0, 0, 0)),
        out_shape=jax.ShapeDtypeStruct(out_shape, F32),
        compiler_params=pltpu.CompilerParams(vmem_limit_bytes=VMEM_LIMIT),
        name="bias_expand",
    )(idx, table)


def _unique_blocks(idx, bq, bk):
    q, k = idx.shape
    blocks = idx.reshape(q // bq, bq, k // bk, bk).transpose(0, 2, 1, 3).reshape(-1, bq, bk)
    uniq, inv = np.unique(blocks, axis=0, return_inverse=True)
    layout = tuple(tuple(int(u) for u in row) for row in inv.reshape(q // bq, k // bk))
    return jnp.asarray(uniq.reshape(-1, bk), jnp.int32), layout


def _a_attend(q, kwin, vwin, bias_ref, valid):
    lane = lax.broadcasted_iota(jnp.int32, (1, LANES), 1)
    low = lane < A_HEAD_DIM
    outs = []
    for p in range(A_HEADS // 2):
        sl = slice(p * LANES, (p + 1) * LANES)
        qp, kp, vp = q[:, sl], kwin[:, sl], vwin[:, sl]
        halves = []
        for sub in range(2):
            qm = jnp.where(low if sub == 0 else jnp.logical_not(low), qp, jnp.zeros_like(qp))
            s = _dot_nt(qm, kp) + bias_ref[2 * p + sub]
            if valid is not None:
                s = jnp.where(valid, s, NEG)
            m = jnp.max(s, axis=-1, keepdims=True)
            e = jnp.exp(s - m)
            l = jnp.sum(e, axis=-1, keepdims=True)
            halves.append(_dot(e.astype(BF16), vp) / l)
        outs.append(jnp.where(low, halves[0], halves[1]))
    return outs


def _a_prompt_kernel(q_ref, k_ref, v_ref, bias_ref, out_ref):
    tq = q_ref.shape[1]
    win = bias_ref.shape[2]
    n_tiles = win // tq
    j = pl.program_id(1)
    ks, vs = [], []
    for t in range(n_tiles):
        start = pl.multiple_of(jnp.maximum(j - (n_tiles - 1 - t), 0) * tq, tq)
        ks.append(k_ref[0, pl.ds(start, tq), :])
        vs.append(v_ref[0, pl.ds(start, tq), :])
    kwin = jnp.concatenate(ks, axis=0)
    vwin = jnp.concatenate(vs, axis=0)
    kpos = lax.broadcasted_iota(jnp.int32, (1, win), 1) + (j - (n_tiles - 1)) * tq
    outs = _a_attend(q_ref[0], kwin, vwin, bias_ref, kpos >= 0)
    for p, o in enumerate(outs):
        out_ref[0, :, p * LANES:(p + 1) * LANES] = o.astype(BF16)


def _a_prompt(q, k, v, bias):
    b, s, w = q.shape
    tq = bias.shape[1]
    kv = pl.BlockSpec((1, s, w), lambda i, j: (i, 0, 0))
    return pl.pallas_call(
        _a_prompt_kernel,
        grid=(b, s // tq),
        in_specs=[pl.BlockSpec((1, tq, w), lambda i, j: (i, j, 0)), kv, kv, _const_spec(bias)],
        out_specs=pl.BlockSpec((1, tq, w), lambda i, j: (i, j, 0)),
        out_shape=jax.ShapeDtypeStruct((b, s, w), BF16),
        compiler_params=_params(("arbitrary", "arbitrary")),
        name="a_prompt",
    )(q, k, v, bias)


def _a_sample_kernel(q_ref, k_ref, v_ref, bias_ref, out_ref):
    outs = _a_attend(q_ref[0], k_ref[0], v_ref[0], bias_ref, None)
    for p, o in enumerate(outs):
        out_ref[0, :, p * LANES:(p + 1) * LANES] = o.astype(BF16)


def _a_sample(q, k_all, v_all, bias):
    b, tq, w = q.shape
    kv = pl.BlockSpec((1,) + k_all.shape[1:], lambda i: (i, 0, 0))
    return pl.pallas_call(
        _a_sample_kernel,
        grid=(b,),
        in_specs=[pl.BlockSpec((1, tq, w), lambda i: (i, 0, 0)), kv, kv, _const_spec(bias)],
        out_specs=pl.BlockSpec((1, tq, w), lambda i: (i, 0, 0)),
        out_shape=jax.ShapeDtypeStruct((b, tq, w), BF16),
        compiler_params=_params(("arbitrary",)),
        name="a_sample",
    )(q, k_all, v_all, bias)


def _b_lambda(lq1, lk1, lq2, lk2, lam_init):
    s1 = jnp.sum(lq1[...] * lk1[...], axis=-1, keepdims=True)
    s2 = jnp.sum(lq2[...] * lk2[...], axis=-1, keepdims=True)
    return jnp.exp(s1) - jnp.exp(s2) + lam_init


def _b_stack_queries(q):
    lane = lax.broadcasted_iota(jnp.int32, (1, LANES), 1)
    low = lane < B_HEAD_DIM
    qs = []
    for h in range(B_HEADS):
        qh = q[:, h * LANES:(h + 1) * LANES]
        zero = jnp.zeros_like(qh)
        qs.append(jnp.concatenate([jnp.where(low, qh, zero), jnp.where(low, zero, qh)], axis=0))
    return qs


def _b_finish(o, lam, g, lam_init, tq):
    ob = o[:tq] - lam * o[tq:]
    return _rmsnorm(ob, g) * (1.0 - lam_init)


def _b_prompt_kernel(lam_init, q_ref, k_ref, v_ref, bias_ref, lq1, lk1, lq2, lk2, g_ref, out_ref,
                     m_ref, l_ref, acc_ref):
    tq = q_ref.shape[1]
    tk = bias_ref.shape[3]
    i = pl.program_id(1)
    m_ref[...] = jnp.full(m_ref.shape, NEG, F32)
    l_ref[...] = jnp.zeros(l_ref.shape, F32)
    acc_ref[...] = jnp.zeros(acc_ref.shape, F32)
    qs = _b_stack_queries(q_ref[0])

    def tile(j, bias_sel):
        start = pl.multiple_of(j * tk, tk)
        ks = k_ref[0, pl.ds(start, tk), :]
        vs = v_ref[0, pl.ds(start, tk), :]
        for h in range(B_HEADS):
            sl = slice(h * LANES, (h + 1) * LANES)
            s = _dot_nt(qs[h], ks[:, sl])
            if bias_sel is not None:
                bias = bias_ref[bias_sel, h]
                s = s + jnp.concatenate([bias, bias], axis=0)
            halves = [s[:, c * LANES:(c + 1) * LANES] for c in range(tk // LANES)]
            mx = halves[0]
            for hv in halves[1:]:
                mx = jnp.maximum(mx, hv)
            m_old = m_ref[h]
            m_new = jnp.maximum(m_old, jnp.max(mx, axis=-1, keepdims=True))
            alpha = jnp.exp(m_old - m_new)
            ps = [jnp.exp(hv - m_new) for hv in halves]
            psum = ps[0]
            for pv in ps[1:]:
                psum = psum + pv
            l_ref[h] = alpha * l_ref[h] + psum
            p = jnp.concatenate(ps, axis=1).astype(BF16)
            acc_ref[h] = alpha * acc_ref[h] + _dot(p, vs[:, sl])
            m_ref[h] = m_new

    def far(j, carry):
        tile(j, None)
        return carry

    lax.fori_loop(0, i - 1, far, 0)

    @pl.when(i >= 1)
    def _():
        tile(i - 1, 0)

    tile(i, 1)

    lam = _b_lambda(lq1, lk1, lq2, lk2, lam_init)
    for h in range(B_HEADS):
        l = jnp.sum(l_ref[h], axis=-1, keepdims=True)
        bn = _b_finish(acc_ref[h] / l, lam, g_ref[...], lam_init, tq)
        out_ref[0, :, h * LANES:(h + 1) * LANES] = bn.astype(BF16)


def _b_prompt(q, k, v, bias, lams, g, lam_init):
    b, s, w = q.shape
    tq = bias.shape[2]
    kv = pl.BlockSpec((1, s, w), lambda i, j: (i, 0, 0))
    scratch = pltpu.VMEM((B_HEADS, 2 * tq, LANES), F32)
    return pl.pallas_call(
        functools.partial(_b_prompt_kernel, lam_init),
        grid=(b, s // tq),
        in_specs=[pl.BlockSpec((1, tq, w), lambda i, j: (i, j, 0)), kv, kv, _const_spec(bias)]
                 + [_const_spec(x) for x in lams] + [_const_spec(g)],
        out_specs=pl.BlockSpec((1, tq, w), lambda i, j: (i, j, 0)),
        out_shape=jax.ShapeDtypeStruct((b, s, w), BF16),
        scratch_shapes=[scratch, scratch, scratch],
        compiler_params=_params(("arbitrary", "arbitrary")),
        name="b_prompt",
    )(q, k, v, bias, *lams, g)


def _b_sample_kernel(lam_init, q_ref, k_ref, v_ref, bias_ref, lq1, lk1, lq2, lk2, g_ref, out_ref):
    tq = q_ref.shape[1]
    qs = _b_stack_queries(q_ref[0])
    ks, vs = k_ref[0], v_ref[0]
    lam = _b_lambda(lq1, lk1, lq2, lk2, lam_init)
    for h in range(B_HEADS):
        sl = slice(h * LANES, (h + 1) * LANES)
        bias = bias_ref[h]
        s = _dot_nt(qs[h], ks[:, sl]) + jnp.concatenate([bias, bias], axis=0)
        m = jnp.max(s, axis=-1, keepdims=True)
        e = jnp.exp(s - m)
        l = jnp.sum(e, axis=-1, keepdims=True)
        o = _dot(e.astype(BF16), vs[:, sl]) / l
        bn = _b_finish(o, lam, g_ref[...], lam_init, tq)
        out_ref[0, :, sl] = bn.astype(BF16)


def _b_sample(q, k_all, v_all, bias, lams, g, lam_init):
    b, tq, w = q.shape
    kv = pl.BlockSpec((1,) + k_all.shape[1:], lambda i: (i, 0, 0))
    return pl.pallas_call(
        functools.partial(_b_sample_kernel, lam_init),
        grid=(b,),
        in_specs=[pl.BlockSpec((1, tq, w), lambda i: (i, 0, 0)), kv, kv, _const_spec(bias)]
                 + [_const_spec(x) for x in lams] + [_const_spec(g)],
        out_specs=pl.BlockSpec((1, tq, w), lambda i: (i, 0, 0)),
        out_shape=jax.ShapeDtypeStruct((b, tq, w), BF16),
        compiler_params=_params(("arbitrary",)),
        name="b_sample",
    )(q, k_all, v_all, bias, *lams, g)


def _c_merge_pair(acc_even, acc_odd):
    low = lax.broadcasted_iota(jnp.int32, (1, LANES), 1) < C_V
    sw_even = pltpu.roll(acc_even, C_V, 1)
    sw_odd = pltpu.roll(acc_odd, C_V, 1)
    return jnp.where(low, acc_even, sw_odd) / jnp.where(low, sw_even, acc_odd)


def _c_prompt_kernel(q_ref, k_ref, v_ref, out_ref, m_ref, acc_ref, mask_ref):
    tq = q_ref.shape[1]
    tk = tq
    n_lane_tiles = tk // LANES
    i = pl.program_id(1)
    m_ref[...] = jnp.full(m_ref.shape, NEG, F32)
    acc_ref[...] = jnp.zeros(acc_ref.shape, F32)
    shift = CHUNK.bit_length() - 1
    rc = jnp.right_shift(lax.broadcasted_iota(jnp.int32, (tq, tk), 0), shift)
    kc = jnp.right_shift(lax.broadcasted_iota(jnp.int32, (tq, tk), 1), shift)
    mask_ref[...] = jnp.where(kc <= rc, 0.0, NEG)

    def tile(j, masked):
        start = pl.multiple_of(j * tk, tk)
        for h in range(C_HEADS):
            sl = slice(h * LANES, (h + 1) * LANES)
            s = _dot_nt(q_ref[0, :, sl], k_ref[0, pl.ds(start, tk), sl])
            parts = [s[:, c * LANES:(c + 1) * LANES] for c in range(n_lane_tiles)]
            if masked:
                parts = [pt + mask_ref[:, c * LANES:(c + 1) * LANES] for c, pt in enumerate(parts)]
            mx = parts[0]
            for pt in parts[1:]:
                mx = jnp.maximum(mx, pt)
            m_old = m_ref[h]
            m_new = jnp.maximum(m_old, jnp.max(mx, axis=-1, keepdims=True))
            alpha = jnp.exp2(m_old - m_new)
            p = jnp.concatenate([jnp.exp2(pt - m_new) for pt in parts], axis=1).astype(BF16)
            acc_ref[h] = alpha * acc_ref[h] + _dot(p, v_ref[0, pl.ds(start, tk), sl])
            m_ref[h] = m_new

    def full(j, carry):
        tile(j, False)
        return carry

    lax.fori_loop(0, i, full, 0)
    tile(i, True)

    def finish(p, carry):
        merged = _c_merge_pair(acc_ref[2 * p], acc_ref[2 * p + 1])
        out_ref[0, :, pl.ds(pl.multiple_of(p * LANES, LANES), LANES)] = merged.astype(BF16)
        return carry

    lax.fori_loop(0, C_HEADS // 2, finish, 0)


def _c_prompt(q, k, v):
    b, s, w = q.shape
    tq = ATT_TILE
    scratch = pltpu.VMEM((C_HEADS, tq, LANES), F32)
    kv = pl.BlockSpec((1, s, w), lambda i, j: (i, 0, 0))
    return pl.pallas_call(
        _c_prompt_kernel,
        grid=(b, s // tq),
        in_specs=[pl.BlockSpec((1, tq, w), lambda i, j: (i, j, 0)), kv, kv],
        out_specs=pl.BlockSpec((1, tq, C_HEADS * C_V), lambda i, j: (i, j, 0)),
        out_shape=jax.ShapeDtypeStruct((b, s, C_HEADS * C_V), BF16),
        scratch_shapes=[scratch, scratch, pltpu.VMEM((tq, tq), F32)],
        compiler_params=_params(("arbitrary", "arbitrary")),
        name="c_prompt",
    )(q, k, v)


def _c_sample_kernel(q_ref, k_ref, v_ref, mask_ref, out_ref):
    mask = mask_ref[...]
    for p in range(C_HEADS // 2):
        accs = []
        for h in (2 * p, 2 * p + 1):
            sl = slice(h * LANES, (h + 1) * LANES)
            s = _dot_nt(q_ref[0, :, sl], k_ref[0, :, sl]) + mask
            e = jnp.exp2(s - jnp.max(s, axis=-1, keepdims=True))
            accs.append(_dot(e.astype(BF16), v_ref[0, :, sl]))
        out_ref[0, :, p * LANES:(p + 1) * LANES] = _c_merge_pair(*accs).astype(BF16)


def _c_sample(q, k_all, v_all, mask):
    b, tq, w = q.shape
    kv = pl.BlockSpec((1, k_all.shape[1], w), lambda i: (i, 0, 0))
    return pl.pallas_call(
        _c_sample_kernel,
        grid=(b,),
        in_specs=[pl.BlockSpec((1, tq, w), lambda i: (i, 0, 0)), kv, kv, _const_spec(mask)],
        out_specs=pl.BlockSpec((1, tq, C_HEADS * C_V), lambda i: (i, 0, 0)),
        out_shape=jax.ShapeDtypeStruct((b, tq, C_HEADS * C_V), BF16),
        compiler_params=_params(("arbitrary",)),
        name="c_sample",
    )(q, k_all, v_all, mask)


def _t5_bucket(rel):
    nb = T5_BUCKETS // 2
    max_exact = nb // 2
    ret = jnp.where(rel > 0, nb, 0)
    n = jnp.abs(rel)
    n_f = jnp.maximum(n, 1).astype(F32)
    large = max_exact + (jnp.log(n_f / max_exact) / math.log(T5_MAX_DIST / max_exact)
                         * (nb - max_exact)).astype(jnp.int32)
    large = jnp.minimum(large, nb - 1)
    return ret + jnp.where(n < max_exact, n, large)


def _a_index(q_pos, k_pos, k_real):
    rel = np.clip(q_pos[:, None] - k_pos[None, :], -A_REL_CLIP, A_REL_CLIP) + A_REL_CLIP
    qc, kc = q_pos[:, None] // CHUNK, k_pos[None, :] // CHUNK
    ok = k_real[None, :] & (kc <= qc) & (kc >= qc - A_PAST_CHUNKS)
    return np.where(ok, rel, -1).astype(np.int32)


def _b_index(q_pos, k_pos, k_real):
    idx = _t5_bucket(k_pos[None, :] - q_pos[:, None])
    ok = k_real[None, :] & ((k_pos[None, :] // CHUNK) <= (q_pos[:, None] // CHUNK))
    return jnp.where(ok, idx, -1).astype(jnp.int32)


def _rope_tables(pos, scale):
    half = C_ROPE // 2
    inv = ROPE_THETA ** (-jnp.arange(half, dtype=F32) / half)
    ang = pos.astype(F32)[:, None] * inv[None, :]
    cos = jnp.concatenate([jnp.cos(ang)] * 2, axis=-1)
    sin = jnp.concatenate([jnp.sin(ang)] * 2, axis=-1)
    n = pos.shape[0]
    z_nope = jnp.zeros((n, C_NOPE), F32)
    z_tail = jnp.zeros((n, LANES - C_NOPE - C_ROPE), F32)
    cq = jnp.concatenate([jnp.full((n, C_NOPE), scale, F32), cos * scale, z_tail], axis=-1)
    sq = jnp.concatenate([z_nope, sin * scale, z_tail], axis=-1)
    ck = jnp.concatenate([z_nope, cos, z_tail], axis=-1)
    sk = jnp.concatenate([z_nope, sin, z_tail], axis=-1)
    return cq, sq, ck, sk


def _rot_cols(w):
    half = w.shape[-1] // 2
    return jnp.concatenate([-w[..., half:], w[..., :half]], axis=-1)


def _pad_rows(x, n):
    return jnp.pad(x, ((0, 0), (0, n - x.shape[1]), (0, 0)))


def kernel(x_prompt, x_sample, cache_a_k, cache_a_v, cache_b_k, cache_b_v, cache_c_kv, cache_c_kr,
           t5_bias, ffn1_norm, ffn1_w_gu, ffn1_w_down, mix_norm, ffn2_norm, ffn2_w_gu, ffn2_w_down,
           e_w_in, a_rel_bias, b_lambda_q1, b_lambda_k1, b_lambda_q2, b_lambda_k2, b_subln, e_w_out,
           c_w_in, c_q_norm, c_kv_norm, c_w_q_up, c_w_kv_up, c_w_out, final_norm):
    batch, seq, d = x_prompt.shape
    dec_batch, t_new, _ = x_sample.shape
    past = cache_b_k.shape[2]
    n_cache_a = cache_a_k.shape[2]
    a_w = A_HEADS * A_HEAD_DIM
    b_w = B_HEADS * 2 * B_HEAD_DIM
    tq = ATT_TILE
    a_pad = A_PAST_CHUNKS * CHUNK
    a_keep = min(a_pad, seq)
    assert tq + 1 >= T5_MAX_DIST and tq % CHUNK == 0 and a_pad % tq == 0
    far_bucket = T5_BUCKETS // 2 - 1
    lam_init = 0.8 - 0.6 * math.exp(-0.3 * 0)
    c_scale = (C_NOPE + C_ROPE) ** -0.5 * math.log2(math.e)
    row2 = lambda v: v.reshape(1, -1)

    n_layers = ffn1_w_gu.shape[0]
    wgu1 = [ffn1_w_gu[i].astype(BF16) for i in range(n_layers)]
    wd1 = [ffn1_w_down[i].astype(BF16) for i in range(n_layers)]
    wgu2 = [ffn2_w_gu[i].astype(BF16) for i in range(n_layers)]
    wd2 = [ffn2_w_down[i].astype(BF16) for i in range(n_layers)]
    e_in = e_w_in[0].astype(BF16)
    e_out_a, e_out_b = e_w_out[0, :a_w].astype(BF16), e_w_out[0, a_w:].astype(BF16)
    w_in = c_w_in[0]
    w_kr = w_in[:, C_Q_LORA + C_KV_LORA:]
    c_in = jnp.concatenate([w_in[:, :C_Q_LORA + C_KV_LORA], jnp.zeros((d, C_NOPE), F32),
                            w_kr, _rot_cols(w_kr)], axis=-1).astype(BF16)
    wq = c_w_q_up[0].reshape(C_Q_LORA, C_HEADS, C_NOPE + C_ROPE)
    wq_rope = wq[..., C_NOPE:]
    c_q = jnp.concatenate([wq[..., :C_NOPE], wq_rope, _rot_cols(wq_rope)], axis=-1)
    c_q = c_q.reshape(C_Q_LORA, C_HEADS * LANES).astype(BF16)
    wkv = c_w_kv_up[0].reshape(C_KV_LORA, C_HEADS, C_NOPE + C_V)
    c_k = jnp.concatenate([wkv[..., :C_NOPE], jnp.zeros_like(wkv[..., C_NOPE:])], axis=-1)
    c_k = c_k.reshape(C_KV_LORA, C_HEADS * LANES).astype(BF16)
    c_v = jnp.concatenate([wkv[..., C_NOPE:], jnp.zeros_like(wkv[..., :LANES - C_V])], axis=-1)
    c_v = c_v.reshape(C_KV_LORA, C_HEADS * LANES).astype(BF16)
    c_out_w = c_w_out[0].astype(BF16)
    lams = [row2(b_lambda_q1[0]), row2(b_lambda_k1[0]), row2(b_lambda_q2[0]), row2(b_lambda_k2[0])]
    subln = row2(b_subln[0])

    pos_s = past + jnp.arange(t_new)
    r = jnp.arange(tq)
    a_idx_p, a_lay_p = _unique_blocks(
        _a_index(a_pad + np.arange(tq), np.arange(a_pad + tq), np.ones((a_pad + tq,), bool)), CHUNK, LANES)
    ka = n_cache_a + t_new
    ka_pad = -(-ka // LANES) * LANES
    pos_s_np = past + np.arange(t_new)
    a_kpos = np.concatenate([past - n_cache_a + np.arange(n_cache_a), pos_s_np,
                             np.zeros((ka_pad - ka,), np.int64)])
    a_real = np.arange(ka_pad) < ka
    a_idx_s, a_lay_s = _unique_blocks(_a_index(pos_s_np, a_kpos, a_real & (a_kpos >= 0)), t_new, LANES)
    b_idx_p = jnp.stack([_b_index(tq + r, r, jnp.ones((tq,), bool)),
                         _b_index(r, r, jnp.ones((tq,), bool))])
    kb = past + t_new
    kb_pad = -(-kb // LANES) * LANES
    b_kpos = jnp.arange(kb_pad)
    b_real = b_kpos < kb
    b_idx_s = _b_index(pos_s, b_kpos, b_real)
    c_mask_s = jnp.where(b_real[None, :] & ((b_kpos[None, :] // CHUNK) <= (pos_s[:, None] // CHUNK)),
                         0.0, NEG).astype(F32)

    a_bias_p = _bias_expand(a_idx_p, a_rel_bias[0], None, a_lay_p)
    a_bias_s = _bias_expand(a_idx_s, a_rel_bias[0], None, a_lay_s)
    t5_t = t5_bias.T
    b_bias_p = jnp.stack([_bias_expand(b_idx_p[0], t5_t, far_bucket),
                          _bias_expand(b_idx_p[1], t5_t, far_bucket)])
    b_bias_s = _bias_expand(b_idx_s, t5_t, far_bucket)

    xp = x_prompt.reshape(batch * seq, d)
    xs = x_sample.reshape(dec_batch * t_new, d)
    g1, gm, g2 = row2(ffn1_norm[0]), row2(mix_norm[0]), row2(ffn2_norm[0])

    xp = _ffn(xp, [], g1, wgu1[0], wd1[0])
    xs = _ffn(xs, [], g1, wgu1[0], wd1[0])

    aq, ak, av, bq, bk, bv, akf, avf, bkf, bvf = _eproj(xp, gm, e_in)
    to3 = lambda t, n: t.reshape(n, -1, t.shape[-1])
    a_out = _a_prompt(to3(aq, batch), to3(ak, batch), to3(av, batch), a_bias_p)
    b_out = _b_prompt(to3(bq, batch), to3(bk, batch), to3(bv, batch), b_bias_p, lams, subln, lam_init)
    p_a_k = akf.reshape(batch, seq, A_HEADS, A_HEAD_DIM)[:, seq - a_keep:][None]
    p_a_v = avf.reshape(batch, seq, A_HEADS, A_HEAD_DIM)[:, seq - a_keep:][None]
    p_b_k = bkf.reshape(1, batch, seq, B_HEADS, 2 * B_HEAD_DIM)
    p_b_v = bvf.reshape(1, batch, seq, B_HEADS, 2 * B_HEAD_DIM)
    xp = _ffn(xp, [(a_out.reshape(-1, a_w), e_out_a), (b_out.reshape(-1, b_w), e_out_b)],
              g2, wgu2[0], wd2[0])

    aq, ak, av, bq, bk, bv, akf, avf, bkf, bvf = _eproj(xs, gm, e_in)
    cat = lambda cache, new, n: _pad_rows(jnp.concatenate(
        [cache.reshape(dec_batch, cache.shape[1], -1).astype(BF16), to3(new, dec_batch)], axis=1), n)
    a_out = _a_sample(to3(aq, dec_batch), cat(cache_a_k[0], ak, ka_pad), cat(cache_a_v[0], av, ka_pad),
                      a_bias_s)
    b_out = _b_sample(to3(bq, dec_batch), cat(cache_b_k[0], bk, kb_pad), cat(cache_b_v[0], bv, kb_pad),
                      b_bias_s, lams, subln, lam_init)
    s_a_k = akf.reshape(1, dec_batch, t_new, A_HEADS, A_HEAD_DIM)
    s_a_v = avf.reshape(1, dec_batch, t_new, A_HEADS, A_HEAD_DIM)
    s_b_k = bkf.reshape(1, dec_batch, t_new, B_HEADS, 2 * B_HEAD_DIM)
    s_b_v = bvf.reshape(1, dec_batch, t_new, B_HEADS, 2 * B_HEAD_DIM)
    xs = _ffn(xs, [(a_out.reshape(-1, a_w), e_out_a), (b_out.reshape(-1, b_w), e_out_b)],
              g2, wgu2[0], wd2[0])

    g1, gm, g2 = row2(ffn1_norm[1]), row2(mix_norm[1]), row2(ffn2_norm[1])
    gq, gkv, gfin = row2(c_q_norm[0]), row2(c_kv_norm[0]), row2(final_norm)
    xp = _ffn(xp, [], g1, wgu1[1], wd1[1])
    xs = _ffn(xs, [], g1, wgu1[1], wd1[1])

    q, ckvf, ckvb, kr = _cproj(xp, gm, c_in, gq, gkv, c_q, _rope_tables(jnp.arange(seq), c_scale))
    k, v = _cexpand(ckvb, kr, c_k, c_v, min(ROW_TILE, batch * seq))
    c_out = _c_prompt(to3(q, batch), to3(k, batch), to3(v, batch))
    p_c_kv = ckvf.reshape(1, batch, seq, C_KV_LORA)
    p_c_kr = kr[:, C_NOPE:C_NOPE + C_ROPE].reshape(1, batch, seq, C_ROPE)
    y_prompt = _ffn(xp, [(c_out.reshape(-1, C_HEADS * C_V), c_out_w)], g2, wgu2[1], wd2[1], gfin)

    q, ckvf, ckvb, kr = _cproj(xs, gm, c_in, gq, gkv, c_q,
                               _rope_tables(jnp.tile(pos_s, dec_batch), c_scale))
    ckv_all = _pad_rows(jnp.concatenate([cache_c_kv[0].astype(BF16), to3(ckvb, dec_batch)], axis=1), kb_pad)
    kr_cache = jnp.pad(cache_c_kr[0], ((0, 0), (0, 0), (C_NOPE, LANES - C_NOPE - C_ROPE)))
    kr_all = _pad_rows(jnp.concatenate([kr_cache, to3(kr, dec_batch)], axis=1), kb_pad)
    k, v = _cexpand(ckv_all.reshape(-1, C_KV_LORA), kr_all.reshape(-1, LANES), c_k, c_v, kb_pad)
    c_out = _c_sample(to3(q, dec_batch), to3(k, dec_batch), to3(v, dec_batch), c_mask_s)
    s_c_kv = ckvf.reshape(1, dec_batch, t_new, C_KV_LORA)
    s_c_kr = kr[:, C_NOPE:C_NOPE + C_ROPE].reshape(1, dec_batch, t_new, C_ROPE)
    y_sample = _ffn(xs, [(c_out.reshape(-1, C_HEADS * C_V), c_out_w)], g2, wgu2[1], wd2[1], gfin)

    return (y_prompt.reshape(batch, seq, d), y_sample.reshape(dec_batch, t_new, d),
            p_a_k, p_a_v, p_b_k, p_b_v, p_c_kv, p_c_kr,
            s_a_k, s_a_v, s_b_k, s_b_v, s_c_kv, s_c_kr)
```

```python
import functools
import math

import jax
import jax.numpy as jnp
import numpy as np
from jax import lax
from jax.experimental import pallas as pl
from jax.experimental.pallas import tpu as pltpu

F32 = jnp.float32
BF16 = jnp.bfloat16

EPS = 1e-6
NEG = -1e30
LOG2E = math.log2(math.e)
CHUNK = 64
A_HEADS = 8
A_HEAD_DIM = 64
A_PAST_CHUNKS = 8
A_REL_CLIP = 64
B_HEADS = 4
B_HEAD_DIM = 64
T5_BUCKETS = 32
T5_MAX_DIST = 128
C_HEADS = 16
C_NOPE = 64
C_ROPE = 32
C_V = 64
C_Q_LORA = 384
C_KV_LORA = 256
ROPE_THETA = 10000.0

LANES = 128
VMEM_LIMIT = 56 * 1024 * 1024
ROW_TILE = 512
ATT_TILE = 256
MXU_WIDTH = 256
FFN_CHUNKS = 2


def _ffn_chunks(d_ff):
    tiles = d_ff // MXU_WIDTH
    assert tiles * MXU_WIDTH == d_ff
    per = -(-tiles // FFN_CHUNKS)
    edges = [min(c * per, tiles) * MXU_WIDTH for c in range(FFN_CHUNKS + 1)]
    return [(lo, hi) for lo, hi in zip(edges[:-1], edges[1:]) if hi > lo]


def _rmsnorm(x, g):
    return x * lax.rsqrt(jnp.mean(x * x, axis=-1, keepdims=True) + EPS) * g


def _dot(a, b):
    return jnp.dot(a, b, preferred_element_type=F32)


def _dot_nt(a, b):
    return lax.dot_general(a, b, (((1,), (1,)), ((), ())), preferred_element_type=F32)


def _const_spec(a):
    nd = a.ndim
    return pl.BlockSpec(a.shape, lambda *_: (0,) * nd, pipeline_mode=pl.Buffered(1))


def _params(sem, flags=None):
    return pltpu.CompilerParams(dimension_semantics=sem, vmem_limit_bytes=VMEM_LIMIT, flags=flags)


def _ffn_kernel(n_pre, with_final, *refs):
    x_ref = refs[0]
    pres = [(refs[1 + 2 * i], refs[2 + 2 * i]) for i in range(n_pre)]
    idx = 1 + 2 * n_pre
    g_ref, wgu_ref, wd_ref = refs[idx:idx + 3]
    idx += 3
    gf_ref = refs[idx] if with_final else None
    out_ref = refs[-1]

    x = x_ref[...]
    for o_ref, w_ref in pres:
        x = x + _dot(o_ref[...], w_ref[...])
    xn = _rmsnorm(x, g_ref[...]).astype(BF16)
    d_ff = wd_ref.shape[0]
    acc = jnp.zeros_like(x)
    for lo, hi in _ffn_chunks(d_ff):
        gate = _dot(xn, wgu_ref[:, lo:hi])
        up = _dot(xn, wgu_ref[:, d_ff + lo:d_ff + hi])
        act = (jax.nn.silu(gate) * up).astype(BF16)
        acc = acc + _dot(act, wd_ref[lo:hi, :])
    y = x + 0.5 * acc
    if with_final:
        y = _rmsnorm(y, gf_ref[...])
    out_ref[...] = y


def _ffn(x, pres, g, wgu, wd, g_final=None):
    t, d = x.shape
    tm = min(ROW_TILE, t)
    row = lambda w: pl.BlockSpec((tm, w), lambda i: (i, 0))
    args, specs = [x], [row(d)]
    for o, w in pres:
        args += [o, w]
        specs += [row(o.shape[1]), _const_spec(w)]
    args += [g, wgu, wd]
    specs += [_const_spec(g), _const_spec(wgu), _const_spec(wd)]
    if g_final is not None:
        args.append(g_final)
        specs.append(_const_spec(g_final))
    return pl.pallas_call(
        functools.partial(_ffn_kernel, len(pres), g_final is not None),
        grid=(t // tm,),
        in_specs=specs,
        out_specs=row(d),
        out_shape=jax.ShapeDtypeStruct((t, d), F32),
        compiler_params=_params(("arbitrary",)),
        name="ffn",
    )(*args)


def _eproj_kernel(x_ref, g_ref, w_ref, aq, ak, av, bq, bk, bv, akf, avf, bkf, bvf):
    hn = _rmsnorm(x_ref[...], g_ref[...]).astype(BF16)
    t = _dot(hn, w_ref[...])
    w = aq.shape[1]
    parts = [t[:, i * w:(i + 1) * w] for i in range(6)]
    aq[...] = (parts[0] * (A_HEAD_DIM ** -0.5 * LOG2E)).astype(BF16)
    ak[...] = parts[1].astype(BF16)
    av[...] = parts[2].astype(BF16)
    bq[...] = (parts[3] * (B_HEAD_DIM ** -0.5 * LOG2E)).astype(BF16)
    bk[...] = parts[4].astype(BF16)
    bv[...] = parts[5].astype(BF16)
    akf[...] = parts[1]
    avf[...] = parts[2]
    bkf[...] = parts[4]
    bvf[...] = parts[5]


def _eproj(x, g, w):
    t, d = x.shape
    tm = min(ROW_TILE, t)
    wd = w.shape[1] // 6
    row = lambda n: pl.BlockSpec((tm, n), lambda i: (i, 0))
    shapes = [jax.ShapeDtypeStruct((t, wd), BF16)] * 6 + [jax.ShapeDtypeStruct((t, wd), F32)] * 4
    return pl.pallas_call(
        _eproj_kernel,
        grid=(t // tm,),
        in_specs=[row(d), _const_spec(g), _const_spec(w)],
        out_specs=[row(wd)] * 10,
        out_shape=shapes,
        compiler_params=_params(("arbitrary",)),
        name="eproj",
    )(x, g, w)


def _cproj_kernel(x_ref, g_ref, win_ref, gq_ref, gkv_ref, wq_ref, cq_ref, sq_ref, ck_ref, sk_ref,
                  q_out, ckvf_out, ckvb_out, kr_out):
    hn = _rmsnorm(x_ref[...], g_ref[...]).astype(BF16)
    t = _dot(hn, win_ref[...])
    cqn = _rmsnorm(t[:, :C_Q_LORA], gq_ref[...]).astype(BF16)
    ckv = _rmsnorm(t[:, C_Q_LORA:C_Q_LORA + C_KV_LORA], gkv_ref[...])
    ckvf_out[...] = ckv
    ckvb_out[...] = ckv.astype(BF16)
    tg = t[:, C_Q_LORA + C_KV_LORA:]
    kr_out[...] = tg * ck_ref[...] + pltpu.roll(tg, LANES - C_ROPE, 1) * sk_ref[...]
    q = _dot(cqn, wq_ref[...])
    cq, sq = cq_ref[...], sq_ref[...]
    for h in range(C_HEADS):
        qh = q[:, h * LANES:(h + 1) * LANES]
        qf = qh * cq + pltpu.roll(qh, LANES - C_ROPE, 1) * sq
        q_out[:, h * LANES:(h + 1) * LANES] = qf.astype(BF16)


def _cproj(x, g, win, gq, gkv, wq, tabs):
    t, d = x.shape
    tm = min(ROW_TILE, t)
    nper = tabs[0].shape[0] // tm
    row = lambda n: pl.BlockSpec((tm, n), lambda i: (i, 0))
    tab = pl.BlockSpec((tm, LANES), lambda i: (i % nper, 0))
    shapes = [jax.ShapeDtypeStruct((t, C_HEADS * LANES), BF16),
              jax.ShapeDtypeStruct((t, C_KV_LORA), F32),
              jax.ShapeDtypeStruct((t, C_KV_LORA), BF16),
              jax.ShapeDtypeStruct((t, LANES), F32)]
    return pl.pallas_call(
        _cproj_kernel,
        grid=(t // tm,),
        in_specs=[row(d), _const_spec(g), _const_spec(win), _const_spec(gq), _const_spec(gkv),
                  _const_spec(wq), tab, tab, tab, tab],
        out_specs=[row(C_HEADS * LANES), row(C_KV_LORA), row(C_KV_LORA), row(LANES)],
        out_shape=shapes,
        compiler_params=_params(("arbitrary",)),
        name="cproj",
    )(x, g, win, gq, gkv, wq, *tabs)


def _cexpand_kernel(ckv_ref, kr_ref, wk_ref, wv_ref, k_out, v_out):
    ckv = ckv_ref[...]
    kd = _dot(ckv, wk_ref[...])
    vd = _dot(ckv, wv_ref[...])
    kr = kr_ref[...]
    high = (lax.broadcasted_iota(jnp.int32, (1, LANES), 1) >= C_V).astype(F32)
    for h in range(C_HEADS):
        sl = slice(h * LANES, (h + 1) * LANES)
        k_out[:, sl] = (kd[:, sl] + kr).astype(BF16)
        v_out[:, sl] = (vd[:, sl] + (high if h % 2 == 0 else 1.0 - high)).astype(BF16)


def _cexpand(ckv, kr, wk, wv, tm):
    t = ckv.shape[0]
    row = lambda n: pl.BlockSpec((tm, n), lambda i: (i, 0))
    return pl.pallas_call(
        _cexpand_kernel,
        grid=(t // tm,),
        in_specs=[row(C_KV_LORA), row(LANES), _const_spec(wk), _const_spec(wv)],
        out_specs=[row(C_HEADS * LANES), row(C_HEADS * LANES)],
        out_shape=[jax.ShapeDtypeStruct((t, C_HEADS * LANES), BF16)] * 2,
        compiler_params=_params(("arbitrary",)),
        name="cexpand",
    )(ckv, kr, wk, wv)


def _bias_kernel(n_rows, shift, layout, idx_ref, tab_ref, out_ref):
    idx = idx_ref[...]
    n_heads = out_ref.shape[0]
    bq = out_ref.shape[1] // len(layout)
    bk = out_ref.shape[2] // len(layout[0])
    for h in range(n_heads):
        base = tab_ref[h, shift] if shift is not None else 0.0

        def body(r, acc):
            return jnp.where(idx == r, (tab_ref[h, r] - base) * LOG2E, acc)

        vals = lax.fori_loop(0, n_rows, body, jnp.full(idx.shape, NEG, F32))
        for a, row in enumerate(layout):
            for t, u in enumerate(row):
                out_ref[h, a * bq:(a + 1) * bq, t * bk:(t + 1) * bk] = vals[u * bq:(u + 1) * bq, :]


def _bias_expand(idx, table, shift=None, layout=((0,),)):
    n_heads, n_rows = table.shape
    n_blocks = 1 + max(max(row) for row in layout)
    out_shape = (n_heads, idx.shape[0] // n_blocks * len(layout), idx.shape[1] * len(layout[0]))
    return pl.pallas_call(
        functools.partial(_bias_kernel, n_rows, shift, layout),
        in_specs=[pl.BlockSpec(idx.shape, lambda: (0, 0)),
                  pl.BlockSpec(memory_space=pltpu.SMEM)],
        out_specs=pl.BlockSpec(out_shape, lambda: (0, 0, 0)),
        out_shape=jax.ShapeDtypeStruct(out_shape, F32),
        compiler_params=pltpu.CompilerParams(vmem_limit_bytes=VMEM_LIMIT),
        name="bias_expand",
    )(idx, table)


def _unique_blocks(idx, bq, bk):
    q, k = idx.shape
    blocks = idx.reshape(q // bq, bq, k // bk, bk).transpose(0, 2, 1, 3).reshape(-1, bq, bk)
    uniq, inv = np.unique(blocks, axis=0, return_inverse=True)
    layout = tuple(tuple(int(u) for u in row) for row in inv.reshape(q // bq, k // bk))
    return jnp.asarray(uniq.reshape(-1, bk), jnp.int32), layout


def _a_attend(q, kwin, vwin, bias_ref, valid):
    lane = lax.broadcasted_iota(jnp.int32, (1, LANES), 1)
    low = lane < A_HEAD_DIM
    outs = []
    for p in range(A_HEADS // 2):
        sl = slice(p * LANES, (p + 1) * LANES)
        qp, kp, vp = q[:, sl], kwin[:, sl], vwin[:, sl]
        halves = []
        for sub in range(2):
            qm = jnp.where(low if sub == 0 else jnp.logical_not(low), qp, jnp.zeros_like(qp))
            s = _dot_nt(qm, kp) + bias_ref[2 * p + sub]
            if valid is not None:
                s = jnp.where(valid, s, NEG)
            m = jnp.max(s, axis=-1, keepdims=True)
            e = jnp.exp2(s - m)
            l = jnp.sum(e, axis=-1, keepdims=True)
            halves.append(_dot(e.astype(BF16), vp) / l)
        outs.append(jnp.where(low, halves[0], halves[1]))
    return outs


def _a_prompt_kernel(q_ref, k_ref, v_ref, bias_ref, out_ref):
    tq = q_ref.shape[1]
    win = bias_ref.shape[2]
    n_tiles = win // tq
    j = pl.program_id(1)
    ks, vs = [], []
    for t in range(n_tiles):
        start = pl.multiple_of(jnp.maximum(j - (n_tiles - 1 - t), 0) * tq, tq)
        ks.append(k_ref[0, pl.ds(start, tq), :])
        vs.append(v_ref[0, pl.ds(start, tq), :])
    kwin = jnp.concatenate(ks, axis=0)
    vwin = jnp.concatenate(vs, axis=0)
    kpos = lax.broadcasted_iota(jnp.int32, (1, win), 1) + (j - (n_tiles - 1)) * tq
    outs = _a_attend(q_ref[0], kwin, vwin, bias_ref, kpos >= 0)
    for p, o in enumerate(outs):
        out_ref[0, :, p * LANES:(p + 1) * LANES] = o.astype(BF16)


def _a_prompt(q, k, v, bias):
    b, s, w = q.shape
    tq = bias.shape[1]
    kv = pl.BlockSpec((1, s, w), lambda i, j: (i, 0, 0))
    return pl.pallas_call(
        _a_prompt_kernel,
        grid=(b, s // tq),
        in_specs=[pl.BlockSpec((1, tq, w), lambda i, j: (i, j, 0)), kv, kv, _const_spec(bias)],
        out_specs=pl.BlockSpec((1, tq, w), lambda i, j: (i, j, 0)),
        out_shape=jax.ShapeDtypeStruct((b, s, w), BF16),
        compiler_params=_params(("arbitrary", "arbitrary")),
        name="a_prompt",
    )(q, k, v, bias)


def _a_sample_kernel(q_ref, k_ref, v_ref, bias_ref, out_ref):
    outs = _a_attend(q_ref[0], k_ref[0], v_ref[0], bias_ref, None)
    for p, o in enumerate(outs):
        out_ref[0, :, p * LANES:(p + 1) * LANES] = o.astype(BF16)


def _a_sample(q, k_all, v_all, bias):
    b, tq, w = q.shape
    kv = pl.BlockSpec((1,) + k_all.shape[1:], lambda i: (i, 0, 0))
    return pl.pallas_call(
        _a_sample_kernel,
        grid=(b,),
        in_specs=[pl.BlockSpec((1, tq, w), lambda i: (i, 0, 0)), kv, kv, _const_spec(bias)],
        out_specs=pl.BlockSpec((1, tq, w), lambda i: (i, 0, 0)),
        out_shape=jax.ShapeDtypeStruct((b, tq, w), BF16),
        compiler_params=_params(("arbitrary",)),
        name="a_sample",
    )(q, k_all, v_all, bias)


def _b_lambda(lq1, lk1, lq2, lk2, lam_init):
    s1 = jnp.sum(lq1[...] * lk1[...], axis=-1, keepdims=True)
    s2 = jnp.sum(lq2[...] * lk2[...], axis=-1, keepdims=True)
    return jnp.exp(s1) - jnp.exp(s2) + lam_init


def _b_stack_queries(q):
    lane = lax.broadcasted_iota(jnp.int32, (1, LANES), 1)
    low = lane < B_HEAD_DIM
    qs = []
    for h in range(B_HEADS):
        qh = q[:, h * LANES:(h + 1) * LANES]
        zero = jnp.zeros_like(qh)
        qs.append(jnp.concatenate([jnp.where(low, qh, zero), jnp.where(low, zero, qh)], axis=0))
    return qs


def _b_finish(o, lam, g, lam_init, tq):
    ob = o[:tq] - lam * o[tq:]
    return _rmsnorm(ob, g) * (1.0 - lam_init)


def _b_prompt_kernel(lam_init, q_ref, k_ref, v_ref, bias_ref, lq1, lk1, lq2, lk2, g_ref, out_ref,
                     qs_ref, m_ref, l_ref, acc_ref):
    tq = q_ref.shape[1]
    tk = bias_ref.shape[3]
    i = pl.program_id(1)
    for h, qh in enumerate(_b_stack_queries(q_ref[0])):
        qs_ref[h] = qh

    def tile(j, bias_sel, first, n_tiles=1):
        width = n_tiles * tk
        start = pl.multiple_of(j * tk, tk)
        for h in range(B_HEADS):
            sl = slice(h * LANES, (h + 1) * LANES)
            s = _dot_nt(qs_ref[h], k_ref[0, pl.ds(start, width), sl])
            parts = [s[:, c * LANES:(c + 1) * LANES] for c in range(width // LANES)]
            if bias_sel is not None:
                parts = [jnp.concatenate([pt[:tq] + bias_ref[bias_sel, h, :, c * LANES:(c + 1) * LANES],
                                          pt[tq:] + bias_ref[bias_sel, h, :, c * LANES:(c + 1) * LANES]], axis=0)
                         for c, pt in enumerate(parts)]
            mx = parts[0]
            for pt in parts[1:]:
                mx = jnp.maximum(mx, pt)
            row_max = jnp.max(mx, axis=-1, keepdims=True)
            m_new = jnp.broadcast_to(row_max, (2 * tq, LANES)) if first else jnp.maximum(m_ref[h], row_max)
            ps = [jnp.exp2(pt - m_new) for pt in parts]
            psum = ps[0]
            for pv in ps[1:]:
                psum = psum + pv
            pv = _dot(jnp.concatenate(ps, axis=1).astype(BF16), v_ref[0, pl.ds(start, width), sl])
            if first:
                l_ref[h] = psum
                acc_ref[h] = pv
            else:
                alpha = jnp.exp2(m_ref[h] - m_new)
                l_ref[h] = alpha * l_ref[h] + psum
                acc_ref[h] = alpha * acc_ref[h] + pv
            m_ref[h] = m_new

    tile(i, 1, True)

    @pl.when(i >= 1)
    def _():
        tile(i - 1, 0, False)

    n_far = jnp.maximum(i - 1, 0)

    def far_pair(p, carry):
        tile(2 * p, None, False, 2)
        return carry

    lax.fori_loop(0, jnp.right_shift(n_far, 1), far_pair, 0)

    @pl.when(jnp.bitwise_and(n_far, 1) == 1)
    def _():
        tile(n_far - 1, None, False)

    lam = _b_lambda(lq1, lk1, lq2, lk2, lam_init)

    group = 2

    def finish(g, carry):
        for u in range(group):
            h = g * group + u
            l = jnp.sum(l_ref[h], axis=-1, keepdims=True)
            bn = _b_finish(acc_ref[h] / l, lam, g_ref[...], lam_init, tq)
            out_ref[0, :, pl.ds(pl.multiple_of(h * LANES, LANES), LANES)] = bn.astype(BF16)
        return carry

    lax.fori_loop(0, B_HEADS // group, finish, 0)


def _b_prompt(q, k, v, bias, lams, g, lam_init):
    b, s, w = q.shape
    tq = bias.shape[2]
    kv = pl.BlockSpec((1, s, w), lambda i, j: (i, 0, 0))
    scratch = pltpu.VMEM((B_HEADS, 2 * tq, LANES), F32)
    return pl.pallas_call(
        functools.partial(_b_prompt_kernel, lam_init),
        grid=(b, s // tq),
        in_specs=[pl.BlockSpec((1, tq, w), lambda i, j: (i, j, 0)), kv, kv, _const_spec(bias)]
                 + [_const_spec(x) for x in lams] + [_const_spec(g)],
        out_specs=pl.BlockSpec((1, tq, w), lambda i, j: (i, j, 0)),
        out_shape=jax.ShapeDtypeStruct((b, s, w), BF16),
        scratch_shapes=[pltpu.VMEM((B_HEADS, 2 * tq, LANES), BF16), scratch, scratch, scratch],
        compiler_params=_params(("arbitrary", "arbitrary")),
        name="b_prompt",
    )(q, k, v, bias, *lams, g)


def _b_sample_kernel(lam_init, q_ref, k_ref, v_ref, bias_ref, lq1, lk1, lq2, lk2, g_ref, out_ref):
    tq = q_ref.shape[1]
    qs = _b_stack_queries(q_ref[0])
    ks, vs = k_ref[0], v_ref[0]
    lam = _b_lambda(lq1, lk1, lq2, lk2, lam_init)
    for h in range(B_HEADS):
        sl = slice(h * LANES, (h + 1) * LANES)
        bias = bias_ref[h]
        s = _dot_nt(qs[h], ks[:, sl]) + jnp.concatenate([bias, bias], axis=0)
        m = jnp.max(s, axis=-1, keepdims=True)
        e = jnp.exp2(s - m)
        l = jnp.sum(e, axis=-1, keepdims=True)
        o = _dot(e.astype(BF16), vs[:, sl]) / l
        bn = _b_finish(o, lam, g_ref[...], lam_init, tq)
        out_ref[0, :, sl] = bn.astype(BF16)


def _b_sample(q, k_all, v_all, bias, lams, g, lam_init):
    b, tq, w = q.shape
    kv = pl.BlockSpec((1,) + k_all.shape[1:], lambda i: (i, 0, 0))
    return pl.pallas_call(
        functools.partial(_b_sample_kernel, lam_init),
        grid=(b,),
        in_specs=[pl.BlockSpec((1, tq, w), lambda i: (i, 0, 0)), kv, kv, _const_spec(bias)]
                 + [_const_spec(x) for x in lams] + [_const_spec(g)],
        out_specs=pl.BlockSpec((1, tq, w), lambda i: (i, 0, 0)),
        out_shape=jax.ShapeDtypeStruct((b, tq, w), BF16),
        compiler_params=_params(("arbitrary",)),
        name="b_sample",
    )(q, k_all, v_all, bias, *lams, g)


def _c_merge_pair(acc_even, acc_odd):
    low = lax.broadcasted_iota(jnp.int32, (1, LANES), 1) < C_V
    den = pltpu.roll(jnp.where(low, acc_odd, acc_even), C_V, 1)
    return jnp.where(low, acc_even, acc_odd) / den


def _c_prompt_kernel(q_ref, k_ref, v_ref, out_ref, m_ref, acc_ref, mask_ref):
    tq = q_ref.shape[1]
    tk = tq
    n_lane_tiles = tk // LANES
    i = pl.program_id(1)

    @pl.when((pl.program_id(0) == 0) & (i == 0))
    def _():
        shift = CHUNK.bit_length() - 1
        rc = jnp.right_shift(lax.broadcasted_iota(jnp.int32, (tq, tk), 0), shift)
        kc = jnp.right_shift(lax.broadcasted_iota(jnp.int32, (tq, tk), 1), shift)
        mask_ref[...] = jnp.where(kc <= rc, 0.0, NEG)

    def tile(j, first, n_tiles=1):
        width = n_tiles * tk
        start = pl.multiple_of(j * tk, tk)
        for h in range(C_HEADS):
            sl = slice(h * LANES, (h + 1) * LANES)
            s = _dot_nt(q_ref[0, :, sl], k_ref[0, pl.ds(start, width), sl])
            parts = [s[:, c * LANES:(c + 1) * LANES] for c in range(n_tiles * n_lane_tiles)]
            if first:
                parts = [pt + mask_ref[:, c * LANES:(c + 1) * LANES] for c, pt in enumerate(parts)]
            mx = parts[0]
            for pt in parts[1:]:
                mx = jnp.maximum(mx, pt)
            row_max = jnp.max(mx, axis=-1, keepdims=True)
            m_new = jnp.broadcast_to(row_max, (tq, LANES)) if first else jnp.maximum(m_ref[h], row_max)
            p = jnp.concatenate([jnp.exp2(pt - m_new) for pt in parts], axis=1).astype(BF16)
            pv = _dot(p, v_ref[0, pl.ds(start, width), sl])
            acc_ref[h] = pv if first else jnp.exp2(m_ref[h] - m_new) * acc_ref[h] + pv
            m_ref[h] = m_new

    tile(i, True)

    def full(j, carry):
        tile(j, False)
        return carry

    lax.fori_loop(0, i, full, 0)

    group = 4

    def finish(g, carry):
        for u in range(group):
            p = g * group + u
            merged = _c_merge_pair(acc_ref[2 * p], acc_ref[2 * p + 1])
            out_ref[0, :, pl.ds(pl.multiple_of(p * LANES, LANES), LANES)] = merged.astype(BF16)
        return carry

    lax.fori_loop(0, C_HEADS // 2 // group, finish, 0)


def _c_prompt(q, k, v):
    b, s, w = q.shape
    tq = ATT_TILE
    scratch = pltpu.VMEM((C_HEADS, tq, LANES), F32)
    kv = pl.BlockSpec((1, s, w), lambda i, j: (i, 0, 0))
    return pl.pallas_call(
        _c_prompt_kernel,
        grid=(b, s // tq),
        in_specs=[pl.BlockSpec((1, tq, w), lambda i, j: (i, j, 0)), kv, kv],
        out_specs=pl.BlockSpec((1, tq, C_HEADS * C_V), lambda i, j: (i, j, 0)),
        out_shape=jax.ShapeDtypeStruct((b, s, C_HEADS * C_V), BF16),
        scratch_shapes=[scratch, scratch, pltpu.VMEM((tq, tq), F32)],
        compiler_params=_params(("arbitrary", "arbitrary")),
        name="c_prompt",
    )(q, k, v)


def _c_sample_kernel(q_ref, k_ref, v_ref, mask_ref, out_ref):
    mask = mask_ref[...]
    for p in range(C_HEADS // 2):
        accs = []
        for h in (2 * p, 2 * p + 1):
            sl = slice(h * LANES, (h + 1) * LANES)
            s = _dot_nt(q_ref[0, :, sl], k_ref[0, :, sl]) + mask
            e = jnp.exp2(s - jnp.max(s, axis=-1, keepdims=True))
            accs.append(_dot(e.astype(BF16), v_ref[0, :, sl]))
        out_ref[0, :, p * LANES:(p + 1) * LANES] = _c_merge_pair(*accs).astype(BF16)


def _c_sample(q, k_all, v_all, mask):
    b, tq, w = q.shape
    kv = pl.BlockSpec((1, k_all.shape[1], w), lambda i: (i, 0, 0))
    return pl.pallas_call(
        _c_sample_kernel,
        grid=(b,),
        in_specs=[pl.BlockSpec((1, tq, w), lambda i: (i, 0, 0)), kv, kv, _const_spec(mask)],
        out_specs=pl.BlockSpec((1, tq, C_HEADS * C_V), lambda i: (i, 0, 0)),
        out_shape=jax.ShapeDtypeStruct((b, tq, C_HEADS * C_V), BF16),
        compiler_params=_params(("arbitrary",)),
        name="c_sample",
    )(q, k_all, v_all, mask)


def _t5_bucket(rel):
    nb = T5_BUCKETS // 2
    max_exact = nb // 2
    ret = jnp.where(rel > 0, nb, 0)
    n = jnp.abs(rel)
    n_f = jnp.maximum(n, 1).astype(F32)
    large = max_exact + (jnp.log(n_f / max_exact) / math.log(T5_MAX_DIST / max_exact)
                         * (nb - max_exact)).astype(jnp.int32)
    large = jnp.minimum(large, nb - 1)
    return ret + jnp.where(n < max_exact, n, large)


def _a_index(q_pos, k_pos, k_real):
    rel = np.clip(q_pos[:, None] - k_pos[None, :], -A_REL_CLIP, A_REL_CLIP) + A_REL_CLIP
    qc, kc = q_pos[:, None] // CHUNK, k_pos[None, :] // CHUNK
    ok = k_real[None, :] & (kc <= qc) & (kc >= qc - A_PAST_CHUNKS)
    return np.where(ok, rel, -1).astype(np.int32)


def _b_index(q_pos, k_pos, k_real):
    idx = _t5_bucket(k_pos[None, :] - q_pos[:, None])
    ok = k_real[None, :] & ((k_pos[None, :] // CHUNK) <= (q_pos[:, None] // CHUNK))
    return jnp.where(ok, idx, -1).astype(jnp.int32)


def _rope_tables(pos, scale):
    half = C_ROPE // 2
    inv = ROPE_THETA ** (-jnp.arange(half, dtype=F32) / half)
    ang = pos.astype(F32)[:, None] * inv[None, :]
    cos = jnp.concatenate([jnp.cos(ang)] * 2, axis=-1)
    sin = jnp.concatenate([jnp.sin(ang)] * 2, axis=-1)
    n = pos.shape[0]
    z_nope = jnp.zeros((n, C_NOPE), F32)
    z_tail = jnp.zeros((n, LANES - C_NOPE - C_ROPE), F32)
    cq = jnp.concatenate([jnp.full((n, C_NOPE), scale, F32), cos * scale, z_tail], axis=-1)
    sq = jnp.concatenate([z_nope, sin * scale, z_tail], axis=-1)
    ck = jnp.concatenate([z_nope, cos, z_tail], axis=-1)
    sk = jnp.concatenate([z_nope, sin, z_tail], axis=-1)
    return cq, sq, ck, sk


def _rot_cols(w):
    half = w.shape[-1] // 2
    return jnp.concatenate([-w[..., half:], w[..., :half]], axis=-1)


def _pad_rows(x, n):
    return jnp.pad(x, ((0, 0), (0, n - x.shape[1]), (0, 0)))


def kernel(x_prompt, x_sample, cache_a_k, cache_a_v, cache_b_k, cache_b_v, cache_c_kv, cache_c_kr,
           t5_bias, ffn1_norm, ffn1_w_gu, ffn1_w_down, mix_norm, ffn2_norm, ffn2_w_gu, ffn2_w_down,
           e_w_in, a_rel_bias, b_lambda_q1, b_lambda_k1, b_lambda_q2, b_lambda_k2, b_subln, e_w_out,
           c_w_in, c_q_norm, c_kv_norm, c_w_q_up, c_w_kv_up, c_w_out, final_norm):
    batch, seq, d = x_prompt.shape
    dec_batch, t_new, _ = x_sample.shape
    past = cache_b_k.shape[2]
    n_cache_a = cache_a_k.shape[2]
    a_w = A_HEADS * A_HEAD_DIM
    b_w = B_HEADS * 2 * B_HEAD_DIM
    tq = ATT_TILE
    a_pad = A_PAST_CHUNKS * CHUNK
    a_keep = min(a_pad, seq)
    assert tq + 1 >= T5_MAX_DIST and tq % CHUNK == 0 and a_pad % tq == 0
    far_bucket = T5_BUCKETS // 2 - 1
    lam_init = 0.8 - 0.6 * math.exp(-0.3 * 0)
    c_scale = (C_NOPE + C_ROPE) ** -0.5 * math.log2(math.e)
    row2 = lambda v: v.reshape(1, -1)

    n_layers = ffn1_w_gu.shape[0]
    wgu1 = [ffn1_w_gu[i].astype(BF16) for i in range(n_layers)]
    wd1 = [ffn1_w_down[i].astype(BF16) for i in range(n_layers)]
    wgu2 = [ffn2_w_gu[i].astype(BF16) for i in range(n_layers)]
    wd2 = [ffn2_w_down[i].astype(BF16) for i in range(n_layers)]
    e_in = e_w_in[0].astype(BF16)
    e_out_a, e_out_b = e_w_out[0, :a_w].astype(BF16), e_w_out[0, a_w:].astype(BF16)
    w_in = c_w_in[0]
    w_kr = w_in[:, C_Q_LORA + C_KV_LORA:]
    c_in = jnp.concatenate([w_in[:, :C_Q_LORA + C_KV_LORA], jnp.zeros((d, C_NOPE), F32),
                            w_kr, _rot_cols(w_kr)], axis=-1).astype(BF16)
    wq = c_w_q_up[0].reshape(C_Q_LORA, C_HEADS, C_NOPE + C_ROPE)
    wq_rope = wq[..., C_NOPE:]
    c_q = jnp.concatenate([wq[..., :C_NOPE], wq_rope, _rot_cols(wq_rope)], axis=-1)
    c_q = c_q.reshape(C_Q_LORA, C_HEADS * LANES).astype(BF16)
    wkv = c_w_kv_up[0].reshape(C_KV_LORA, C_HEADS, C_NOPE + C_V)
    c_k = jnp.concatenate([wkv[..., :C_NOPE], jnp.zeros_like(wkv[..., C_NOPE:])], axis=-1)
    c_k = c_k.reshape(C_KV_LORA, C_HEADS * LANES).astype(BF16)
    wv_pairs = wkv[..., C_NOPE:].reshape(C_KV_LORA, C_HEADS // 2, 2, C_V)
    z_v = jnp.zeros_like(wv_pairs[:, :, 0])
    c_v = jnp.stack([jnp.concatenate([wv_pairs[:, :, 0], z_v], axis=-1),
                     jnp.concatenate([z_v, wv_pairs[:, :, 1]], axis=-1)], axis=2)
    c_v = c_v.reshape(C_KV_LORA, C_HEADS * LANES).astype(BF16)
    c_out_w = c_w_out[0].astype(BF16)
    lams = [row2(b_lambda_q1[0]), row2(b_lambda_k1[0]), row2(b_lambda_q2[0]), row2(b_lambda_k2[0])]
    subln = row2(b_subln[0])

    pos_s = past + jnp.arange(t_new)
    r = jnp.arange(tq)
    a_idx_p, a_lay_p = _unique_blocks(
        _a_index(a_pad + np.arange(tq), np.arange(a_pad + tq), np.ones((a_pad + tq,), bool)), CHUNK, LANES)
    ka = n_cache_a + t_new
    ka_pad = -(-ka // LANES) * LANES
    pos_s_np = past + np.arange(t_new)
    a_kpos = np.concatenate([past - n_cache_a + np.arange(n_cache_a), pos_s_np,
                             np.zeros((ka_pad - ka,), np.int64)])
    a_real = np.arange(ka_pad) < ka
    a_idx_s, a_lay_s = _unique_blocks(_a_index(pos_s_np, a_kpos, a_real & (a_kpos >= 0)), t_new, LANES)
    b_idx_p = jnp.stack([_b_index(tq + r, r, jnp.ones((tq,), bool)),
                         _b_index(r, r, jnp.ones((tq,), bool))])
    kb = past + t_new
    kb_pad = -(-kb // LANES) * LANES
    b_kpos = jnp.arange(kb_pad)
    b_real = b_kpos < kb
    b_idx_s = _b_index(pos_s, b_kpos, b_real)
    c_mask_s = jnp.where(b_real[None, :] & ((b_kpos[None, :] // CHUNK) <= (pos_s[:, None] // CHUNK)),
                         0.0, NEG).astype(F32)

    a_bias_p = _bias_expand(a_idx_p, a_rel_bias[0], None, a_lay_p)
    a_bias_s = _bias_expand(a_idx_s, a_rel_bias[0], None, a_lay_s)
    t5_t = t5_bias.T
    b_bias_p = jnp.stack([_bias_expand(b_idx_p[0], t5_t, far_bucket),
                          _bias_expand(b_idx_p[1], t5_t, far_bucket)])
    b_bias_s = _bias_expand(b_idx_s, t5_t, far_bucket)

    xp = x_prompt.reshape(batch * seq, d)
    xs = x_sample.reshape(dec_batch * t_new, d)
    g1, gm, g2 = row2(ffn1_norm[0]), row2(mix_norm[0]), row2(ffn2_norm[0])

    xp = _ffn(xp, [], g1, wgu1[0], wd1[0])
    xs = _ffn(xs, [], g1, wgu1[0], wd1[0])

    aq, ak, av, bq, bk, bv, akf, avf, bkf, bvf = _eproj(xp, gm, e_in)
    to3 = lambda t, n: t.reshape(n, -1, t.shape[-1])
    a_out = _a_prompt(to3(aq, batch), to3(ak, batch), to3(av, batch), a_bias_p)
    b_out = _b_prompt(to3(bq, batch), to3(bk, batch), to3(bv, batch), b_bias_p, lams, subln, lam_init)
    p_a_k = akf.reshape(batch, seq, A_HEADS, A_HEAD_DIM)[:, seq - a_keep:][None]
    p_a_v = avf.reshape(batch, seq, A_HEADS, A_HEAD_DIM)[:, seq - a_keep:][None]
    p_b_k = bkf.reshape(1, batch, seq, B_HEADS, 2 * B_HEAD_DIM)
    p_b_v = bvf.reshape(1, batch, seq, B_HEADS, 2 * B_HEAD_DIM)
    xp = _ffn(xp, [(a_out.reshape(-1, a_w), e_out_a), (b_out.reshape(-1, b_w), e_out_b)],
              g2, wgu2[0], wd2[0])

    aq, ak, av, bq, bk, bv, akf, avf, bkf, bvf = _eproj(xs, gm, e_in)
    cat = lambda cache, new, n: _pad_rows(jnp.concatenate(
        [cache.reshape(dec_batch, cache.shape[1], -1).astype(BF16), to3(new, dec_batch)], axis=1), n)
    a_out = _a_sample(to3(aq, dec_batch), cat(cache_a_k[0], ak, ka_pad), cat(cache_a_v[0], av, ka_pad),
                      a_bias_s)
    b_out = _b_sample(to3(bq, dec_batch), cat(cache_b_k[0], bk, kb_pad), cat(cache_b_v[0], bv, kb_pad),
                      b_bias_s, lams, subln, lam_init)
    s_a_k = akf.reshape(1, dec_batch, t_new, A_HEADS, A_HEAD_DIM)
    s_a_v = avf.reshape(1, dec_batch, t_new, A_HEADS, A_HEAD_DIM)
    s_b_k = bkf.reshape(1, dec_batch, t_new, B_HEADS, 2 * B_HEAD_DIM)
    s_b_v = bvf.reshape(1, dec_batch, t_new, B_HEADS, 2 * B_HEAD_DIM)
    xs = _ffn(xs, [(a_out.reshape(-1, a_w), e_out_a), (b_out.reshape(-1, b_w), e_out_b)],
              g2, wgu2[0], wd2[0])

    g1, gm, g2 = row2(ffn1_norm[1]), row2(mix_norm[1]), row2(ffn2_norm[1])
    gq, gkv, gfin = row2(c_q_norm[0]), row2(c_kv_norm[0]), row2(final_norm)
    xp = _ffn(xp, [], g1, wgu1[1], wd1[1])
    xs = _ffn(xs, [], g1, wgu1[1], wd1[1])

    q, ckvf, ckvb, kr = _cproj(xp, gm, c_in, gq, gkv, c_q, _rope_tables(jnp.arange(seq), c_scale))
    k, v = _cexpand(ckvb, kr, c_k, c_v, min(ROW_TILE, batch * seq))
    c_out = _c_prompt(to3(q, batch), to3(k, batch), to3(v, batch))
    p_c_kv = ckvf.reshape(1, batch, seq, C_KV_LORA)
    p_c_kr = kr[:, C_NOPE:C_NOPE + C_ROPE].reshape(1, batch, seq, C_ROPE)
    y_prompt = _ffn(xp, [(c_out.reshape(-1, C_HEADS * C_V), c_out_w)], g2, wgu2[1], wd2[1], gfin)

    q, ckvf, ckvb, kr = _cproj(xs, gm, c_in, gq, gkv, c_q,
                               _rope_tables(jnp.tile(pos_s, dec_batch), c_scale))
    ckv_all = _pad_rows(jnp.concatenate([cache_c_kv[0].astype(BF16), to3(ckvb, dec_batch)], axis=1), kb_pad)
    kr_cache = jnp.pad(cache_c_kr[0], ((0, 0), (0, 0), (C_NOPE, LANES - C_NOPE - C_ROPE)))
    kr_all = _pad_rows(jnp.concatenate([kr_cache, to3(kr, dec_batch)], axis=1), kb_pad)
    k, v = _cexpand(ckv_all.reshape(-1, C_KV_LORA), kr_all.reshape(-1, LANES), c_k, c_v, kb_pad)
    c_out = _c_sample(to3(q, dec_batch), to3(k, dec_batch), to3(v, dec_batch), c_mask_s)
    s_c_kv = ckvf.reshape(1, dec_batch, t_new, C_KV_LORA)
    s_c_kr = kr[:, C_NOPE:C_NOPE + C_ROPE].reshape(1, dec_batch, t_new, C_ROPE)
    y_sample = _ffn(xs, [(c_out.reshape(-1, C_HEADS * C_V), c_out_w)], g2, wgu2[1], wd2[1], gfin)

    return (y_prompt.reshape(batch, seq, d), y_sample.reshape(dec_batch, t_new, d),
            p_a_k, p_a_v, p_b_k, p_b_v, p_c_kv, p_c_kr,
            s_a_k, s_a_v, s_b_k, s_b_v, s_c_kv, s_c_kr)
```

```python
import functools
import math

import jax
import jax.numpy as jnp
import numpy as np
from jax import lax
from jax.experimental import pallas as pl
from jax.experimental.pallas import tpu as pltpu

F32 = jnp.float32
BF16 = jnp.bfloat16

EPS = 1e-6
NEG = -1e30
LOG2E = math.log2(math.e)
CHUNK = 64
A_HEADS = 8
A_HEAD_DIM = 64
A_PAST_CHUNKS = 8
A_REL_CLIP = 64
B_HEADS = 4
B_HEAD_DIM = 64
T5_BUCKETS = 32
T5_MAX_DIST = 128
C_HEADS = 16
C_NOPE = 64
C_ROPE = 32
C_V = 64
C_Q_LORA = 384
C_KV_LORA = 256
ROPE_THETA = 10000.0

LANES = 128
VMEM_LIMIT = 56 * 1024 * 1024
ROW_TILE = 512
ATT_TILE = 256
A_TILE = 256
MXU_WIDTH = 256
FFN_CHUNKS = 2


def _ffn_chunks(d_ff):
    tiles = d_ff // MXU_WIDTH
    assert tiles * MXU_WIDTH == d_ff
    per = -(-tiles // FFN_CHUNKS)
    edges = [min(c * per, tiles) * MXU_WIDTH for c in range(FFN_CHUNKS + 1)]
    return [(lo, hi) for lo, hi in zip(edges[:-1], edges[1:]) if hi > lo]


def _rmsnorm(x, g):
    return x * lax.rsqrt(jnp.mean(x * x, axis=-1, keepdims=True) + EPS) * g


def _dot(a, b):
    return jnp.dot(a, b, preferred_element_type=F32)


def _dot_nt(a, b):
    return lax.dot_general(a, b, (((1,), (1,)), ((), ())), preferred_element_type=F32)


def _const_spec(a):
    nd = a.ndim
    return pl.BlockSpec(a.shape, lambda *_: (0,) * nd, pipeline_mode=pl.Buffered(1))


def _params(sem, flags=None):
    return pltpu.CompilerParams(dimension_semantics=sem, vmem_limit_bytes=VMEM_LIMIT, flags=flags)


def _weight_chunk_copy(src, stage, sems, c, slot):
    rows = stage.shape[1]
    return pltpu.make_async_copy(src.at[pl.ds(c * rows, rows), :], stage.at[slot], sems.at[slot])


def _load_weight_as_bf16(src, stage, sems, dst):
    rows = stage.shape[1]
    n = src.shape[0] // rows
    _weight_chunk_copy(src, stage, sems, 0, 0).start()
    for c in range(n):
        slot = c % 2
        if c + 1 < n:
            _weight_chunk_copy(src, stage, sems, c + 1, 1 - slot).start()
        _weight_chunk_copy(src, stage, sems, c, slot).wait()
        dst[c * rows:(c + 1) * rows, :] = stage[slot].astype(BF16)


def _ffn_kernel(n_pre, with_final, layer, *refs):
    x_ref = refs[0]
    pres = [(refs[1 + 2 * i], refs[2 + 2 * i]) for i in range(n_pre)]
    idx = 1 + 2 * n_pre
    g_ref, wgu_in, wd_in = refs[idx:idx + 3]
    idx += 3
    gf_ref = refs[idx] if with_final else None
    idx += with_final
    out_ref = refs[idx]
    if layer is None:
        wgu_ref, wd_ref = wgu_in, wd_in
    else:
        wgu_out, wd_out, wgu_ref, wd_ref, gu_stage, d_stage, in_sems, out_sems = refs[idx + 1:]
        copies_out = [pltpu.make_async_copy(wgu_ref, wgu_out, out_sems.at[0]),
                      pltpu.make_async_copy(wd_ref, wd_out, out_sems.at[1])]

        @pl.when(pl.program_id(0) == 0)
        def _():
            _load_weight_as_bf16(wgu_in.at[layer], gu_stage, in_sems, wgu_ref)
            _load_weight_as_bf16(wd_in.at[layer], d_stage, in_sems, wd_ref)
            for cp in copies_out:
                cp.start()

    x = x_ref[...]
    for o_ref, w_ref in pres:
        x = x + _dot(o_ref[...], w_ref[...])
    xn = _rmsnorm(x, g_ref[...]).astype(BF16)
    d_ff = wd_ref.shape[0]
    acc = jnp.zeros_like(x)
    for lo, hi in _ffn_chunks(d_ff):
        gate = _dot(xn, wgu_ref[:, lo:hi])
        up = _dot(xn, wgu_ref[:, d_ff + lo:d_ff + hi])
        act = (jax.nn.silu(gate) * up).astype(BF16)
        acc = acc + _dot(act, wd_ref[lo:hi, :])
    y = x + 0.5 * acc
    if with_final:
        y = _rmsnorm(y, gf_ref[...])
    out_ref[...] = y

    if layer is not None:
        @pl.when(pl.program_id(0) == pl.num_programs(0) - 1)
        def _():
            for cp in copies_out:
                cp.wait()


def _ffn(x, pres, g, wgu, wd, g_final=None, layer=None):
    t, d = x.shape
    tm = min(ROW_TILE, t)
    row = lambda w: pl.BlockSpec((tm, w), lambda i: (i, 0))
    hbm = pl.BlockSpec(memory_space=pl.ANY)
    args, specs = [x], [row(d)]
    for o, w in pres:
        args += [o, w]
        specs += [row(o.shape[1]), _const_spec(w)]
    args += [g, wgu, wd]
    specs += [_const_spec(g)] + ([_const_spec(wgu), _const_spec(wd)] if layer is None else [hbm, hbm])
    if g_final is not None:
        args.append(g_final)
        specs.append(_const_spec(g_final))
    out_specs, out_shape, scratch = row(d), jax.ShapeDtypeStruct((t, d), F32), []
    if layer is not None:
        gu_shape, d_shape = wgu.shape[1:], wd.shape[1:]
        chunks = 8
        out_specs = [out_specs, hbm, hbm]
        out_shape = [out_shape, jax.ShapeDtypeStruct(gu_shape, BF16), jax.ShapeDtypeStruct(d_shape, BF16)]
        scratch = [pltpu.VMEM(gu_shape, BF16), pltpu.VMEM(d_shape, BF16),
                   pltpu.VMEM((2, gu_shape[0] // chunks, gu_shape[1]), F32),
                   pltpu.VMEM((2, d_shape[0] // chunks, d_shape[1]), F32),
                   pltpu.SemaphoreType.DMA((2,)), pltpu.SemaphoreType.DMA((2,))]
    return pl.pallas_call(
        functools.partial(_ffn_kernel, len(pres), g_final is not None, layer),
        grid=(t // tm,),
        in_specs=specs,
        out_specs=out_specs,
        out_shape=out_shape,
        scratch_shapes=scratch,
        compiler_params=_params(("arbitrary",)),
        name="ffn",
    )(*args)


def _store_values_with_ones(out_ref, v):
    low = lax.broadcasted_iota(jnp.int32, (1, LANES), 1) < LANES // 2
    for p in range(v.shape[1] // LANES):
        vp = v[:, p * LANES:(p + 1) * LANES]
        out_ref[:, 2 * p * LANES:(2 * p + 1) * LANES] = jnp.where(low, vp, 1.0).astype(out_ref.dtype)
        out_ref[:, (2 * p + 1) * LANES:(2 * p + 2) * LANES] = jnp.where(low, 1.0, vp).astype(out_ref.dtype)


def _merge_head_pair(acc_even, acc_odd):
    half = LANES // 2
    low = lax.broadcasted_iota(jnp.int32, (1, LANES), 1) < half
    den = pltpu.roll(jnp.where(low, acc_odd, acc_even), half, 1)
    return jnp.where(low, acc_even, acc_odd) / den


def _eproj_kernel(x_ref, g_ref, w_ref, aq, ak, av, bq, bk, bv, akf, avf, bkf, bvf):
    hn = _rmsnorm(x_ref[...], g_ref[...]).astype(BF16)
    t = _dot(hn, w_ref[...])
    w = aq.shape[1]
    parts = [t[:, i * w:(i + 1) * w] for i in range(6)]
    aq[...] = (parts[0] * (A_HEAD_DIM ** -0.5 * LOG2E)).astype(BF16)
    ak[...] = parts[1].astype(BF16)
    _store_values_with_ones(av, parts[2])
    bq[...] = (parts[3] * (B_HEAD_DIM ** -0.5 * LOG2E)).astype(BF16)
    bk[...] = parts[4].astype(BF16)
    bv[...] = parts[5].astype(BF16)
    akf[...] = parts[1]
    avf[...] = parts[2]
    bkf[...] = parts[4]
    bvf[...] = parts[5]


def _eproj(x, g, w):
    t, d = x.shape
    tm = min(ROW_TILE, t)
    wd = w.shape[1] // 6
    row = lambda n: pl.BlockSpec((tm, n), lambda i: (i, 0))
    widths = [wd, wd, 2 * wd, wd, wd, wd]
    shapes = [jax.ShapeDtypeStruct((t, n), BF16) for n in widths] + [jax.ShapeDtypeStruct((t, wd), F32)] * 4
    return pl.pallas_call(
        _eproj_kernel,
        grid=(t // tm,),
        in_specs=[row(d), _const_spec(g), _const_spec(w)],
        out_specs=[row(n) for n in widths] + [row(wd)] * 4,
        out_shape=shapes,
        compiler_params=_params(("arbitrary",)),
        name="eproj",
    )(x, g, w)


def _cproj_kernel(x_ref, g_ref, win_ref, gq_ref, gkv_ref, wq_ref, cq_ref, sq_ref, ck_ref, sk_ref,
                  q_out, ckvf_out, ckvb_out, kr_out):
    hn = _rmsnorm(x_ref[...], g_ref[...]).astype(BF16)
    t = _dot(hn, win_ref[...])
    cqn = _rmsnorm(t[:, :C_Q_LORA], gq_ref[...]).astype(BF16)
    ckv = _rmsnorm(t[:, C_Q_LORA:C_Q_LORA + C_KV_LORA], gkv_ref[...])
    ckvf_out[...] = ckv
    ckvb_out[...] = ckv.astype(BF16)
    tg = t[:, C_Q_LORA + C_KV_LORA:]
    kr_out[...] = tg * ck_ref[...] + pltpu.roll(tg, LANES - C_ROPE, 1) * sk_ref[...]
    q = _dot(cqn, wq_ref[...])
    cq, sq = cq_ref[...], sq_ref[...]
    for h in range(C_HEADS):
        qh = q[:, h * LANES:(h + 1) * LANES]
        qf = qh * cq + pltpu.roll(qh, LANES - C_ROPE, 1) * sq
        q_out[:, h * LANES:(h + 1) * LANES] = qf.astype(BF16)


def _cproj(x, g, win, gq, gkv, wq, tabs):
    t, d = x.shape
    tm = min(ROW_TILE, t)
    nper = tabs[0].shape[0] // tm
    row = lambda n: pl.BlockSpec((tm, n), lambda i: (i, 0))
    tab = pl.BlockSpec((tm, LANES), lambda i: (i % nper, 0))
    shapes = [jax.ShapeDtypeStruct((t, C_HEADS * LANES), BF16),
              jax.ShapeDtypeStruct((t, C_KV_LORA), F32),
              jax.ShapeDtypeStruct((t, C_KV_LORA), BF16),
              jax.ShapeDtypeStruct((t, LANES), F32)]
    return pl.pallas_call(
        _cproj_kernel,
        grid=(t // tm,),
        in_specs=[row(d), _const_spec(g), _const_spec(win), _const_spec(gq), _const_spec(gkv),
                  _const_spec(wq), tab, tab, tab, tab],
        out_specs=[row(C_HEADS * LANES), row(C_KV_LORA), row(C_KV_LORA), row(LANES)],
        out_shape=shapes,
        compiler_params=_params(("arbitrary",)),
        name="cproj",
    )(x, g, win, gq, gkv, wq, *tabs)


def _cexpand_kernel(ckv_ref, kr_ref, wk_ref, wv_ref, k_out, v_out):
    ckv = ckv_ref[...]
    kd = _dot(ckv, wk_ref[...])
    vd = _dot(ckv, wv_ref[...])
    kr = kr_ref[...]
    high = (lax.broadcasted_iota(jnp.int32, (1, LANES), 1) >= C_V).astype(F32)
    for h in range(C_HEADS):
        sl = slice(h * LANES, (h + 1) * LANES)
        k_out[:, sl] = (kd[:, sl] + kr).astype(BF16)
        v_out[:, sl] = (vd[:, sl] + (high if h % 2 == 0 else 1.0 - high)).astype(BF16)


def _cexpand(ckv, kr, wk, wv, tm):
    t = ckv.shape[0]
    row = lambda n: pl.BlockSpec((tm, n), lambda i: (i, 0))
    return pl.pallas_call(
        _cexpand_kernel,
        grid=(t // tm,),
        in_specs=[row(C_KV_LORA), row(LANES), _const_spec(wk), _const_spec(wv)],
        out_specs=[row(C_HEADS * LANES), row(C_HEADS * LANES)],
        out_shape=[jax.ShapeDtypeStruct((t, C_HEADS * LANES), BF16)] * 2,
        compiler_params=_params(("arbitrary",)),
        name="cexpand",
    )(ckv, kr, wk, wv)


def _bias_kernel(n_rows, shift, layout, idx_ref, tab_ref, out_ref):
    idx = idx_ref[...]
    n_heads = out_ref.shape[0]
    bq = out_ref.shape[1] // len(layout)
    bk = out_ref.shape[2] // len(layout[0])
    for h in range(n_heads):
        base = tab_ref[h, shift] if shift is not None else 0.0

        def body(r, acc):
            return jnp.where(idx == r, (tab_ref[h, r] - base) * LOG2E, acc)

        vals = lax.fori_loop(0, n_rows, body, jnp.full(idx.shape, NEG, F32))
        for a, row in enumerate(layout):
            for t, u in enumerate(row):
                out_ref[h, a * bq:(a + 1) * bq, t * bk:(t + 1) * bk] = vals[u * bq:(u + 1) * bq, :]


def _bias_expand(idx, table, shift=None, layout=((0,),)):
    n_heads, n_rows = table.shape
    n_blocks = 1 + max(max(row) for row in layout)
    out_shape = (n_heads, idx.shape[0] // n_blocks * len(layout), idx.shape[1] * len(layout[0]))
    return pl.pallas_call(
        functools.partial(_bias_kernel, n_rows, shift, layout),
        in_specs=[pl.BlockSpec(idx.shape, lambda: (0, 0)),
                  pl.BlockSpec(memory_space=pltpu.SMEM)],
        out_specs=pl.BlockSpec(out_shape, lambda: (0, 0, 0)),
        out_shape=jax.ShapeDtypeStruct(out_shape, F32),
        compiler_params=pltpu.CompilerParams(vmem_limit_bytes=VMEM_LIMIT),
        name="bias_expand",
    )(idx, table)


def _unique_blocks(idx, bq, bk):
    q, k = idx.shape
    blocks = idx.reshape(q // bq, bq, k // bk, bk).transpose(0, 2, 1, 3).reshape(-1, bq, bk)
    uniq, inv = np.unique(blocks, axis=0, return_inverse=True)
    layout = tuple(tuple(int(u) for u in row) for row in inv.reshape(q // bq, k // bk))
    return jnp.asarray(uniq.reshape(-1, bk), jnp.int32), layout


def _a_attend(q, kwin, vwin, bias_ref, valid, out_ref, s_ref):
    low = lax.broadcasted_iota(jnp.int32, (1, LANES), 1) < A_HEAD_DIM
    for h in range(A_HEADS):
        sl = slice(h // 2 * LANES, (h // 2 + 1) * LANES)
        qp = q[:, sl]
        qm = jnp.where(low if h % 2 == 0 else jnp.logical_not(low), qp, jnp.zeros_like(qp))
        s = _dot_nt(qm, kwin[:, sl]) + bias_ref[h]
        if valid is not None:
            s = jnp.where(valid, s, NEG)
        s_ref[h] = s - jnp.max(s, axis=-1, keepdims=True)
    accs = []
    for h in range(A_HEADS):
        accs.append(_dot(jnp.exp2(s_ref[h]).astype(BF16), vwin[:, h * LANES:(h + 1) * LANES]))
        if h % 2 == 1:
            out_ref[0, :, h // 2 * LANES:(h // 2 + 1) * LANES] = _merge_head_pair(*accs).astype(BF16)
            accs = []


def _a_prompt_kernel(q_ref, k_ref, v_ref, bias_ref, out_ref, s_ref):
    tq = q_ref.shape[1]
    win = bias_ref.shape[2]
    n_tiles = win // tq
    j = pl.program_id(1)

    def run(check_positions):
        ks, vs = [], []
        for t in range(n_tiles):
            start = pl.multiple_of(jnp.maximum(j - (n_tiles - 1 - t), 0) * tq, tq)
            ks.append(k_ref[0, pl.ds(start, tq), :])
            vs.append(v_ref[0, pl.ds(start, tq), :])
        valid = None
        if check_positions:
            valid = lax.broadcasted_iota(jnp.int32, (1, win), 1) + (j - (n_tiles - 1)) * tq >= 0
        _a_attend(q_ref[0], jnp.concatenate(ks, axis=0), jnp.concatenate(vs, axis=0), bias_ref, valid,
                  out_ref, s_ref)

    @pl.when(j < n_tiles - 1)
    def _():
        run(True)

    @pl.when(j >= n_tiles - 1)
    def _():
        run(False)


def _a_prompt(q, k, v, bias):
    b, s, w = q.shape
    tq = bias.shape[1]
    kv = lambda a: pl.BlockSpec((1, s, a.shape[2]), lambda i, j: (i, 0, 0))
    return pl.pallas_call(
        _a_prompt_kernel,
        grid=(b, s // tq),
        in_specs=[pl.BlockSpec((1, tq, w), lambda i, j: (i, j, 0)), kv(k), kv(v), _const_spec(bias)],
        out_specs=pl.BlockSpec((1, tq, w), lambda i, j: (i, j, 0)),
        out_shape=jax.ShapeDtypeStruct((b, s, w), BF16),
        scratch_shapes=[pltpu.VMEM(bias.shape, F32)],
        compiler_params=_params(("arbitrary", "arbitrary")),
        name="a_prompt",
    )(q, k, v, bias)


def _a_sample_kernel(q_ref, k_ref, v_ref, bias_ref, out_ref, s_ref):
    _a_attend(q_ref[0], k_ref[0], v_ref[0], bias_ref, None, out_ref, s_ref)


def _a_sample(q, k_all, v_all, bias):
    b, tq, w = q.shape
    kv = lambda a: pl.BlockSpec((1,) + a.shape[1:], lambda i: (i, 0, 0))
    return pl.pallas_call(
        _a_sample_kernel,
        grid=(b,),
        in_specs=[pl.BlockSpec((1, tq, w), lambda i: (i, 0, 0)), kv(k_all), kv(v_all), _const_spec(bias)],
        out_specs=pl.BlockSpec((1, tq, w), lambda i: (i, 0, 0)),
        out_shape=jax.ShapeDtypeStruct((b, tq, w), BF16),
        scratch_shapes=[pltpu.VMEM(bias.shape, F32)],
        compiler_params=_params(("arbitrary",)),
        name="a_sample",
    )(q, k_all, v_all, bias)


def _b_lambda(lq1, lk1, lq2, lk2, lam_init):
    s1 = jnp.sum(lq1[...] * lk1[...], axis=-1, keepdims=True)
    s2 = jnp.sum(lq2[...] * lk2[...], axis=-1, keepdims=True)
    return jnp.exp(s1) - jnp.exp(s2) + lam_init


def _b_stack_queries(q):
    lane = lax.broadcasted_iota(jnp.int32, (1, LANES), 1)
    low = lane < B_HEAD_DIM
    qs = []
    for h in range(B_HEADS):
        qh = q[:, h * LANES:(h + 1) * LANES]
        zero = jnp.zeros_like(qh)
        qs.append(jnp.concatenate([jnp.where(low, qh, zero), jnp.where(low, zero, qh)], axis=0))
    return qs


def _b_finish(o, lam, g, lam_init, tq):
    ob = o[:tq] - lam * o[tq:]
    return _rmsnorm(ob, g) * (1.0 - lam_init)


def _b_prompt_kernel(lam_init, q_ref, k_ref, v_ref, bias_ref, lq1, lk1, lq2, lk2, g_ref, out_ref,
                     qs_ref, m_ref, alpha_ref, l_ref, acc_ref, s_ref):
    tq = q_ref.shape[1]
    tk = bias_ref.shape[3]
    i = pl.program_id(1)
    for h, qh in enumerate(_b_stack_queries(q_ref[0])):
        qs_ref[h] = qh

    def tile(j, bias_sel, first, n_tiles=1):
        width = n_tiles * tk
        lane_tiles = range(width // LANES)
        start = pl.multiple_of(j * tk, tk)
        for h in range(B_HEADS):
            sl = slice(h * LANES, (h + 1) * LANES)
            s = _dot_nt(qs_ref[h], k_ref[0, pl.ds(start, width), sl])
            parts = [s[:, c * LANES:(c + 1) * LANES] for c in lane_tiles]
            if bias_sel is not None:
                parts = [jnp.concatenate([pt[:tq] + bias_ref[bias_sel, h, :, c * LANES:(c + 1) * LANES],
                                          pt[tq:] + bias_ref[bias_sel, h, :, c * LANES:(c + 1) * LANES]], axis=0)
                         for c, pt in enumerate(parts)]
            mx = parts[0]
            for pt in parts[1:]:
                mx = jnp.maximum(mx, pt)
            row_max = jnp.max(mx, axis=-1, keepdims=True)
            if first:
                m_new = jnp.broadcast_to(row_max, (2 * tq, LANES))
            else:
                m_new = jnp.maximum(m_ref[h], row_max)
                alpha_ref[h] = jnp.exp2(m_ref[h] - m_new)
            m_ref[h] = m_new
            for c, pt in enumerate(parts):
                s_ref[h, :, c * LANES:(c + 1) * LANES] = pt
        for h in range(B_HEADS):
            sl = slice(h * LANES, (h + 1) * LANES)
            m_new = m_ref[h]
            ps = [jnp.exp2(s_ref[h, :, c * LANES:(c + 1) * LANES] - m_new) for c in lane_tiles]
            psum = ps[0]
            for pv in ps[1:]:
                psum = psum + pv
            pv = _dot(jnp.concatenate(ps, axis=1).astype(BF16), v_ref[0, pl.ds(start, width), sl])
            if first:
                l_ref[h] = psum
                acc_ref[h] = pv
            else:
                alpha = alpha_ref[h]
                l_ref[h] = alpha * l_ref[h] + psum
                acc_ref[h] = alpha * acc_ref[h] + pv

    tile(i, 1, True)

    @pl.when(i >= 1)
    def _():
        tile(i - 1, 0, False)

    n_far = jnp.maximum(i - 1, 0)

    def far_pair(p, carry):
        tile(2 * p, None, False, 2)
        return carry

    lax.fori_loop(0, jnp.right_shift(n_far, 1), far_pair, 0)

    @pl.when(jnp.bitwise_and(n_far, 1) == 1)
    def _():
        tile(n_far - 1, None, False)

    lam = _b_lambda(lq1, lk1, lq2, lk2, lam_init)

    group = 2

    def finish(g, carry):
        for u in range(group):
            h = g * group + u
            l = jnp.sum(l_ref[h], axis=-1, keepdims=True)
            bn = _b_finish(acc_ref[h] / l, lam, g_ref[...], lam_init, tq)
            out_ref[0, :, pl.ds(pl.multiple_of(h * LANES, LANES), LANES)] = bn.astype(BF16)
        return carry

    lax.fori_loop(0, B_HEADS // group, finish, 0)


def _b_prompt(q, k, v, bias, lams, g, lam_init):
    b, s, w = q.shape
    tq = bias.shape[2]
    kv = pl.BlockSpec((1, s, w), lambda i, j: (i, 0, 0))
    scratch = pltpu.VMEM((B_HEADS, 2 * tq, LANES), F32)
    return pl.pallas_call(
        functools.partial(_b_prompt_kernel, lam_init),
        grid=(b, s // tq),
        in_specs=[pl.BlockSpec((1, tq, w), lambda i, j: (i, j, 0)), kv, kv, _const_spec(bias)]
                 + [_const_spec(x) for x in lams] + [_const_spec(g)],
        out_specs=pl.BlockSpec((1, tq, w), lambda i, j: (i, j, 0)),
        out_shape=jax.ShapeDtypeStruct((b, s, w), BF16),
        scratch_shapes=[pltpu.VMEM((B_HEADS, 2 * tq, LANES), BF16), scratch, scratch, scratch, scratch,
                        pltpu.VMEM((B_HEADS, 2 * tq, 2 * tq), F32)],
        compiler_params=_params(("arbitrary", "arbitrary")),
        name="b_prompt",
    )(q, k, v, bias, *lams, g)


def _b_sample_kernel(lam_init, q_ref, k_ref, v_ref, bias_ref, lq1, lk1, lq2, lk2, g_ref, out_ref):
    tq = q_ref.shape[1]
    qs = _b_stack_queries(q_ref[0])
    ks, vs = k_ref[0], v_ref[0]
    lam = _b_lambda(lq1, lk1, lq2, lk2, lam_init)
    for h in range(B_HEADS):
        sl = slice(h * LANES, (h + 1) * LANES)
        bias = bias_ref[h]
        s = _dot_nt(qs[h], ks[:, sl]) + jnp.concatenate([bias, bias], axis=0)
        m = jnp.max(s, axis=-1, keepdims=True)
        e = jnp.exp2(s - m)
        l = jnp.sum(e, axis=-1, keepdims=True)
        o = _dot(e.astype(BF16), vs[:, sl]) / l
        bn = _b_finish(o, lam, g_ref[...], lam_init, tq)
        out_ref[0, :, sl] = bn.astype(BF16)


def _b_sample(q, k_all, v_all, bias, lams, g, lam_init):
    b, tq, w = q.shape
    kv = pl.BlockSpec((1,) + k_all.shape[1:], lambda i: (i, 0, 0))
    return pl.pallas_call(
        functools.partial(_b_sample_kernel, lam_init),
        grid=(b,),
        in_specs=[pl.BlockSpec((1, tq, w), lambda i: (i, 0, 0)), kv, kv, _const_spec(bias)]
                 + [_const_spec(x) for x in lams] + [_const_spec(g)],
        out_specs=pl.BlockSpec((1, tq, w), lambda i: (i, 0, 0)),
        out_shape=jax.ShapeDtypeStruct((b, tq, w), BF16),
        compiler_params=_params(("arbitrary",)),
        name="b_sample",
    )(q, k_all, v_all, bias, *lams, g)


def _c_prompt_kernel(q_ref, k_ref, v_ref, out_ref, m_ref, alpha_ref, acc_ref, mask_ref, s_ref):
    tq = q_ref.shape[1]
    tk = tq
    n_lane_tiles = tk // LANES
    i = pl.program_id(1)

    @pl.when((pl.program_id(0) == 0) & (i == 0))
    def _():
        shift = CHUNK.bit_length() - 1
        rc = jnp.right_shift(lax.broadcasted_iota(jnp.int32, (tq, tk), 0), shift)
        kc = jnp.right_shift(lax.broadcasted_iota(jnp.int32, (tq, tk), 1), shift)
        mask_ref[...] = jnp.where(kc <= rc, 0.0, NEG)

    def tile(j, first):
        start = pl.multiple_of(j * tk, tk)
        for h in range(C_HEADS):
            sl = slice(h * LANES, (h + 1) * LANES)
            s = _dot_nt(q_ref[0, :, sl], k_ref[0, pl.ds(start, tk), sl])
            parts = [s[:, c * LANES:(c + 1) * LANES] for c in range(n_lane_tiles)]
            if first:
                parts = [pt + mask_ref[:, c * LANES:(c + 1) * LANES] for c, pt in enumerate(parts)]
            mx = parts[0]
            for pt in parts[1:]:
                mx = jnp.maximum(mx, pt)
            row_max = jnp.max(mx, axis=-1, keepdims=True)
            if first:
                m_new = jnp.broadcast_to(row_max, (tq, LANES))
            else:
                m_new = jnp.maximum(m_ref[h], row_max)
                alpha_ref[h] = jnp.exp2(m_ref[h] - m_new)
            m_ref[h] = m_new
            for c, pt in enumerate(parts):
                s_ref[h, :, c * LANES:(c + 1) * LANES] = pt
        for h in range(C_HEADS):
            sl = slice(h * LANES, (h + 1) * LANES)
            m_new = m_ref[h]
            p = jnp.concatenate([jnp.exp2(s_ref[h, :, c * LANES:(c + 1) * LANES] - m_new)
                                 for c in range(n_lane_tiles)], axis=1).astype(BF16)
            pv = _dot(p, v_ref[0, pl.ds(start, tk), sl])
            acc_ref[h] = pv if first else alpha_ref[h] * acc_ref[h] + pv

    tile(i, True)

    def full(j, carry):
        tile(j, False)
        return carry

    lax.fori_loop(0, i, full, 0)

    group = 4

    def finish(g, carry):
        for u in range(group):
            p = g * group + u
            merged = _merge_head_pair(acc_ref[2 * p], acc_ref[2 * p + 1])
            out_ref[0, :, pl.ds(pl.multiple_of(p * LANES, LANES), LANES)] = merged.astype(BF16)
        return carry

    lax.fori_loop(0, C_HEADS // 2 // group, finish, 0)


def _c_prompt(q, k, v):
    b, s, w = q.shape
    tq = ATT_TILE
    scratch = pltpu.VMEM((C_HEADS, tq, LANES), F32)
    kv = pl.BlockSpec((1, s, w), lambda i, j: (i, 0, 0))
    return pl.pallas_call(
        _c_prompt_kernel,
        grid=(b, s // tq),
        in_specs=[pl.BlockSpec((1, tq, w), lambda i, j: (i, j, 0)), kv, kv],
        out_specs=pl.BlockSpec((1, tq, C_HEADS * C_V), lambda i, j: (i, j, 0)),
        out_shape=jax.ShapeDtypeStruct((b, s, C_HEADS * C_V), BF16),
        scratch_shapes=[scratch, scratch, scratch, pltpu.VMEM((tq, tq), F32),
                        pltpu.VMEM((C_HEADS, tq, tq), F32)],
        compiler_params=_params(("arbitrary", "arbitrary")),
        name="c_prompt",
    )(q, k, v)


def _c_sample_kernel(q_ref, k_ref, v_ref, mask_ref, out_ref):
    mask = mask_ref[...]
    for p in range(C_HEADS // 2):
        accs = []
        for h in (2 * p, 2 * p + 1):
            sl = slice(h * LANES, (h + 1) * LANES)
            s = _dot_nt(q_ref[0, :, sl], k_ref[0, :, sl]) + mask
            e = jnp.exp2(s - jnp.max(s, axis=-1, keepdims=True))
            accs.append(_dot(e.astype(BF16), v_ref[0, :, sl]))
        out_ref[0, :, p * LANES:(p + 1) * LANES] = _merge_head_pair(*accs).astype(BF16)


def _c_sample(q, k_all, v_all, mask):
    b, tq, w = q.shape
    kv = pl.BlockSpec((1, k_all.shape[1], w), lambda i: (i, 0, 0))
    return pl.pallas_call(
        _c_sample_kernel,
        grid=(b,),
        in_specs=[pl.BlockSpec((1, tq, w), lambda i: (i, 0, 0)), kv, kv, _const_spec(mask)],
        out_specs=pl.BlockSpec((1, tq, C_HEADS * C_V), lambda i: (i, 0, 0)),
        out_shape=jax.ShapeDtypeStruct((b, tq, C_HEADS * C_V), BF16),
        compiler_params=_params(("arbitrary",)),
        name="c_sample",
    )(q, k_all, v_all, mask)


def _t5_bucket(rel):
    nb = T5_BUCKETS // 2
    max_exact = nb // 2
    ret = jnp.where(rel > 0, nb, 0)
    n = jnp.abs(rel)
    n_f = jnp.maximum(n, 1).astype(F32)
    large = max_exact + (jnp.log(n_f / max_exact) / math.log(T5_MAX_DIST / max_exact)
                         * (nb - max_exact)).astype(jnp.int32)
    large = jnp.minimum(large, nb - 1)
    return ret + jnp.where(n < max_exact, n, large)


def _a_index(q_pos, k_pos, k_real):
    rel = np.clip(q_pos[:, None] - k_pos[None, :], -A_REL_CLIP, A_REL_CLIP) + A_REL_CLIP
    qc, kc = q_pos[:, None] // CHUNK, k_pos[None, :] // CHUNK
    ok = k_real[None, :] & (kc <= qc) & (kc >= qc - A_PAST_CHUNKS)
    return np.where(ok, rel, -1).astype(np.int32)


def _b_index(q_pos, k_pos, k_real):
    idx = _t5_bucket(k_pos[None, :] - q_pos[:, None])
    ok = k_real[None, :] & ((k_pos[None, :] // CHUNK) <= (q_pos[:, None] // CHUNK))
    return jnp.where(ok, idx, -1).astype(jnp.int32)


def _rope_tables(pos, scale):
    half = C_ROPE // 2
    inv = ROPE_THETA ** (-jnp.arange(half, dtype=F32) / half)
    ang = pos.astype(F32)[:, None] * inv[None, :]
    cos = jnp.concatenate([jnp.cos(ang)] * 2, axis=-1)
    sin = jnp.concatenate([jnp.sin(ang)] * 2, axis=-1)
    n = pos.shape[0]
    z_nope = jnp.zeros((n, C_NOPE), F32)
    z_tail = jnp.zeros((n, LANES - C_NOPE - C_ROPE), F32)
    cq = jnp.concatenate([jnp.full((n, C_NOPE), scale, F32), cos * scale, z_tail], axis=-1)
    sq = jnp.concatenate([z_nope, sin * scale, z_tail], axis=-1)
    ck = jnp.concatenate([z_nope, cos, z_tail], axis=-1)
    sk = jnp.concatenate([z_nope, sin, z_tail], axis=-1)
    return cq, sq, ck, sk


def _rot_cols(w):
    half = w.shape[-1] // 2
    return jnp.concatenate([-w[..., half:], w[..., :half]], axis=-1)


def _values_with_ones(v):
    ones = jnp.ones_like(v)
    even = (jnp.arange(v.shape[-2]) % 2 == 0)[:, None]
    return jnp.where(even, jnp.concatenate([v, ones], axis=-1), jnp.concatenate([ones, v], axis=-1))


def _pad_rows(x, n):
    return jnp.pad(x, ((0, 0), (0, n - x.shape[1]), (0, 0)))


def kernel(x_prompt, x_sample, cache_a_k, cache_a_v, cache_b_k, cache_b_v, cache_c_kv, cache_c_kr,
           t5_bias, ffn1_norm, ffn1_w_gu, ffn1_w_down, mix_norm, ffn2_norm, ffn2_w_gu, ffn2_w_down,
           e_w_in, a_rel_bias, b_lambda_q1, b_lambda_k1, b_lambda_q2, b_lambda_k2, b_subln, e_w_out,
           c_w_in, c_q_norm, c_kv_norm, c_w_q_up, c_w_kv_up, c_w_out, final_norm):
    batch, seq, d = x_prompt.shape
    dec_batch, t_new, _ = x_sample.shape
    past = cache_b_k.shape[2]
    n_cache_a = cache_a_k.shape[2]
    a_w = A_HEADS * A_HEAD_DIM
    b_w = B_HEADS * 2 * B_HEAD_DIM
    tq = ATT_TILE
    a_pad = A_PAST_CHUNKS * CHUNK
    a_keep = min(a_pad, seq)
    assert tq + 1 >= T5_MAX_DIST and tq % CHUNK == 0 and a_pad % tq == 0
    far_bucket = T5_BUCKETS // 2 - 1
    lam_init = 0.8 - 0.6 * math.exp(-0.3 * 0)
    c_scale = (C_NOPE + C_ROPE) ** -0.5 * math.log2(math.e)
    row2 = lambda v: v.reshape(1, -1)

    e_in = e_w_in[0].astype(BF16)
    e_out_a, e_out_b = e_w_out[0, :a_w].astype(BF16), e_w_out[0, a_w:].astype(BF16)
    w_in = c_w_in[0]
    w_kr = w_in[:, C_Q_LORA + C_KV_LORA:]
    c_in = jnp.concatenate([w_in[:, :C_Q_LORA + C_KV_LORA], jnp.zeros((d, C_NOPE), F32),
                            w_kr, _rot_cols(w_kr)], axis=-1).astype(BF16)
    wq = c_w_q_up[0].reshape(C_Q_LORA, C_HEADS, C_NOPE + C_ROPE)
    wq_rope = wq[..., C_NOPE:]
    c_q = jnp.concatenate([wq[..., :C_NOPE], wq_rope, _rot_cols(wq_rope)], axis=-1)
    c_q = c_q.reshape(C_Q_LORA, C_HEADS * LANES).astype(BF16)
    wkv = c_w_kv_up[0].reshape(C_KV_LORA, C_HEADS, C_NOPE + C_V)
    c_k = jnp.concatenate([wkv[..., :C_NOPE], jnp.zeros_like(wkv[..., C_NOPE:])], axis=-1)
    c_k = c_k.reshape(C_KV_LORA, C_HEADS * LANES).astype(BF16)
    wv_pairs = wkv[..., C_NOPE:].reshape(C_KV_LORA, C_HEADS // 2, 2, C_V)
    z_v = jnp.zeros_like(wv_pairs[:, :, 0])
    c_v = jnp.stack([jnp.concatenate([wv_pairs[:, :, 0], z_v], axis=-1),
                     jnp.concatenate([z_v, wv_pairs[:, :, 1]], axis=-1)], axis=2)
    c_v = c_v.reshape(C_KV_LORA, C_HEADS * LANES).astype(BF16)
    c_out_w = c_w_out[0].astype(BF16)
    lams = [row2(b_lambda_q1[0]), row2(b_lambda_k1[0]), row2(b_lambda_q2[0]), row2(b_lambda_k2[0])]
    subln = row2(b_subln[0])

    pos_s = past + jnp.arange(t_new)
    r = jnp.arange(tq)
    assert A_TILE % CHUNK == 0 and a_pad % A_TILE == 0
    a_idx_p, a_lay_p = _unique_blocks(
        _a_index(a_pad + np.arange(A_TILE), np.arange(a_pad + A_TILE), np.ones((a_pad + A_TILE,), bool)),
        CHUNK, LANES)
    ka = n_cache_a + t_new
    ka_pad = -(-ka // LANES) * LANES
    pos_s_np = past + np.arange(t_new)
    a_kpos = np.concatenate([past - n_cache_a + np.arange(n_cache_a), pos_s_np,
                             np.zeros((ka_pad - ka,), np.int64)])
    a_real = np.arange(ka_pad) < ka
    a_idx_s, a_lay_s = _unique_blocks(_a_index(pos_s_np, a_kpos, a_real & (a_kpos >= 0)), t_new, LANES)
    b_idx_p = jnp.stack([_b_index(tq + r, r, jnp.ones((tq,), bool)),
                         _b_index(r, r, jnp.ones((tq,), bool))])
    kb = past + t_new
    kb_pad = -(-kb // LANES) * LANES
    b_kpos = jnp.arange(kb_pad)
    b_real = b_kpos < kb
    b_idx_s = _b_index(pos_s, b_kpos, b_real)
    c_mask_s = jnp.where(b_real[None, :] & ((b_kpos[None, :] // CHUNK) <= (pos_s[:, None] // CHUNK)),
                         0.0, NEG).astype(F32)

    a_bias_p = _bias_expand(a_idx_p, a_rel_bias[0], None, a_lay_p)
    a_bias_s = _bias_expand(a_idx_s, a_rel_bias[0], None, a_lay_s)
    t5_t = t5_bias.T
    b_bias_p = jnp.stack([_bias_expand(b_idx_p[0], t5_t, far_bucket),
                          _bias_expand(b_idx_p[1], t5_t, far_bucket)])
    b_bias_s = _bias_expand(b_idx_s, t5_t, far_bucket)

    xp = x_prompt.reshape(batch * seq, d)
    xs = x_sample.reshape(dec_batch * t_new, d)
    g1, gm, g2 = row2(ffn1_norm[0]), row2(mix_norm[0]), row2(ffn2_norm[0])

    xp, wgu, wd = _ffn(xp, [], g1, ffn1_w_gu, ffn1_w_down, layer=0)
    xs = _ffn(xs, [], g1, wgu, wd)

    aq, ak, av, bq, bk, bv, akf, avf, bkf, bvf = _eproj(xp, gm, e_in)
    to3 = lambda t, n: t.reshape(n, -1, t.shape[-1])
    a_out = _a_prompt(to3(aq, batch), to3(ak, batch), to3(av, batch), a_bias_p)
    b_out = _b_prompt(to3(bq, batch), to3(bk, batch), to3(bv, batch), b_bias_p, lams, subln, lam_init)
    p_a_k = akf.reshape(batch, seq, A_HEADS, A_HEAD_DIM)[:, seq - a_keep:][None]
    p_a_v = avf.reshape(batch, seq, A_HEADS, A_HEAD_DIM)[:, seq - a_keep:][None]
    p_b_k = bkf.reshape(1, batch, seq, B_HEADS, 2 * B_HEAD_DIM)
    p_b_v = bvf.reshape(1, batch, seq, B_HEADS, 2 * B_HEAD_DIM)
    xp, wgu, wd = _ffn(xp, [(a_out.reshape(-1, a_w), e_out_a), (b_out.reshape(-1, b_w), e_out_b)],
                       g2, ffn2_w_gu, ffn2_w_down, layer=0)

    aq, ak, av, bq, bk, bv, akf, avf, bkf, bvf = _eproj(xs, gm, e_in)
    cat = lambda cache, new, n: _pad_rows(jnp.concatenate(
        [cache.reshape(dec_batch, cache.shape[1], -1).astype(BF16), to3(new, dec_batch)], axis=1), n)
    a_out = _a_sample(to3(aq, dec_batch), cat(cache_a_k[0], ak, ka_pad),
                      cat(_values_with_ones(cache_a_v[0]), av, ka_pad), a_bias_s)
    b_out = _b_sample(to3(bq, dec_batch), cat(cache_b_k[0], bk, kb_pad), cat(cache_b_v[0], bv, kb_pad),
                      b_bias_s, lams, subln, lam_init)
    s_a_k = akf.reshape(1, dec_batch, t_new, A_HEADS, A_HEAD_DIM)
    s_a_v = avf.reshape(1, dec_batch, t_new, A_HEADS, A_HEAD_DIM)
    s_b_k = bkf.reshape(1, dec_batch, t_new, B_HEADS, 2 * B_HEAD_DIM)
    s_b_v = bvf.reshape(1, dec_batch, t_new, B_HEADS, 2 * B_HEAD_DIM)
    xs = _ffn(xs, [(a_out.reshape(-1, a_w), e_out_a), (b_out.reshape(-1, b_w), e_out_b)], g2, wgu, wd)

    g1, gm, g2 = row2(ffn1_norm[1]), row2(mix_norm[1]), row2(ffn2_norm[1])
    gq, gkv, gfin = row2(c_q_norm[0]), row2(c_kv_norm[0]), row2(final_norm)
    xp, wgu, wd = _ffn(xp, [], g1, ffn1_w_gu, ffn1_w_down, layer=1)
    xs = _ffn(xs, [], g1, wgu, wd)

    q, ckvf, ckvb, kr = _cproj(xp, gm, c_in, gq, gkv, c_q, _rope_tables(jnp.arange(seq), c_scale))
    k, v = _cexpand(ckvb, kr, c_k, c_v, min(ROW_TILE, batch * seq))
    c_out = _c_prompt(to3(q, batch), to3(k, batch), to3(v, batch))
    p_c_kv = ckvf.reshape(1, batch, seq, C_KV_LORA)
    p_c_kr = kr[:, C_NOPE:C_NOPE + C_ROPE].reshape(1, batch, seq, C_ROPE)
    y_prompt, wgu, wd = _ffn(xp, [(c_out.reshape(-1, C_HEADS * C_V), c_out_w)], g2, ffn2_w_gu, ffn2_w_down,
                             gfin, layer=1)

    q, ckvf, ckvb, kr = _cproj(xs, gm, c_in, gq, gkv, c_q,
                               _rope_tables(jnp.tile(pos_s, dec_batch), c_scale))
    ckv_all = _pad_rows(jnp.concatenate([cache_c_kv[0].astype(BF16), to3(ckvb, dec_batch)], axis=1), kb_pad)
    kr_cache = jnp.pad(cache_c_kr[0], ((0, 0), (0, 0), (C_NOPE, LANES - C_NOPE - C_ROPE)))
    kr_all = _pad_rows(jnp.concatenate([kr_cache, to3(kr, dec_batch)], axis=1), kb_pad)
    k, v = _cexpand(ckv_all.reshape(-1, C_KV_LORA), kr_all.reshape(-1, LANES), c_k, c_v, kb_pad)
    c_out = _c_sample(to3(q, dec_batch), to3(k, dec_batch), to3(v, dec_batch), c_mask_s)
    s_c_kv = ckvf.reshape(1, dec_batch, t_new, C_KV_LORA)
    s_c_kr = kr[:, C_NOPE:C_NOPE + C_ROPE].reshape(1, dec_batch, t_new, C_ROPE)
    y_sample = _ffn(xs, [(c_out.reshape(-1, C_HEADS * C_V), c_out_w)], g2, wgu, wd, gfin)

    return (y_prompt.reshape(batch, seq, d), y_sample.reshape(dec_batch, t_new, d),
            p_a_k, p_a_v, p_b_k, p_b_v, p_c_kv, p_c_kr,
            s_a_k, s_a_v, s_b_k, s_b_v, s_c_kv, s_c_kr)
```

```python
import functools
import math

import jax
import jax.numpy as jnp
import numpy as np
from jax import lax
from jax.experimental import pallas as pl
from jax.experimental.pallas import tpu as pltpu

F32 = jnp.float32
BF16 = jnp.bfloat16

EPS = 1e-6
NEG = -1e30
LOG2E = math.log2(math.e)
CHUNK = 64
A_HEADS = 8
A_HEAD_DIM = 64
A_PAST_CHUNKS = 8
A_REL_CLIP = 64
B_HEADS = 4
B_HEAD_DIM = 64
T5_BUCKETS = 32
T5_MAX_DIST = 128
C_HEADS = 16
C_NOPE = 64
C_ROPE = 32
C_V = 64
C_Q_LORA = 384
C_KV_LORA = 256
ROPE_THETA = 10000.0

LANES = 128
VMEM_LIMIT = 56 * 1024 * 1024
ROW_TILE = 512
ATT_TILE = 256
A_TILE = 256
MXU_WIDTH = 256
FFN_CHUNKS = 2


def _ffn_chunks(d_ff):
    tiles = d_ff // MXU_WIDTH
    assert tiles * MXU_WIDTH == d_ff
    per = -(-tiles // FFN_CHUNKS)
    edges = [min(c * per, tiles) * MXU_WIDTH for c in range(FFN_CHUNKS + 1)]
    return [(lo, hi) for lo, hi in zip(edges[:-1], edges[1:]) if hi > lo]


def _rmsnorm(x, g):
    return x * lax.rsqrt(jnp.mean(x * x, axis=-1, keepdims=True) + EPS) * g


def _dot(a, b):
    return jnp.dot(a, b, preferred_element_type=F32)


def _dot_nt(a, b):
    return lax.dot_general(a, b, (((1,), (1,)), ((), ())), preferred_element_type=F32)


def _const_spec(a):
    nd = a.ndim
    return pl.BlockSpec(a.shape, lambda *_: (0,) * nd, pipeline_mode=pl.Buffered(1))


def _params(sem, flags=None):
    return pltpu.CompilerParams(dimension_semantics=sem, vmem_limit_bytes=VMEM_LIMIT, flags=flags)


def _weight_chunk_copy(src, stage, sems, c, slot):
    rows = stage.shape[1]
    return pltpu.make_async_copy(src.at[pl.ds(c * rows, rows), :], stage.at[slot], sems.at[slot])


def _load_weight_as_bf16(src, stage, sems, dst):
    rows = stage.shape[1]
    n = src.shape[0] // rows
    _weight_chunk_copy(src, stage, sems, 0, 0).start()
    for c in range(n):
        slot = c % 2
        if c + 1 < n:
            _weight_chunk_copy(src, stage, sems, c + 1, 1 - slot).start()
        _weight_chunk_copy(src, stage, sems, c, slot).wait()
        dst[c * rows:(c + 1) * rows, :] = stage[slot].astype(BF16)


def _ffn_kernel(n_pre, with_final, layer, *refs):
    x_ref = refs[0]
    pres = [(refs[1 + 2 * i], refs[2 + 2 * i]) for i in range(n_pre)]
    idx = 1 + 2 * n_pre
    g_ref, wgu_in, wd_in = refs[idx:idx + 3]
    idx += 3
    gf_ref = refs[idx] if with_final else None
    idx += with_final
    out_ref = refs[idx]
    if layer is None:
        wgu_ref, wd_ref = wgu_in, wd_in
    else:
        wgu_out, wd_out, wgu_ref, wd_ref, gu_stage, d_stage, in_sems, out_sems = refs[idx + 1:]
        copies_out = [pltpu.make_async_copy(wgu_ref, wgu_out, out_sems.at[0]),
                      pltpu.make_async_copy(wd_ref, wd_out, out_sems.at[1])]

        @pl.when(pl.program_id(0) == 0)
        def _():
            _load_weight_as_bf16(wgu_in.at[layer], gu_stage, in_sems, wgu_ref)
            _load_weight_as_bf16(wd_in.at[layer], d_stage, in_sems, wd_ref)
            for cp in copies_out:
                cp.start()

    x = x_ref[...]
    for o_ref, w_ref in pres:
        x = x + _dot(o_ref[...], w_ref[...])
    xn = _rmsnorm(x, g_ref[...]).astype(BF16)
    d_ff = wd_ref.shape[0]
    acc = jnp.zeros_like(x)
    for lo, hi in _ffn_chunks(d_ff):
        gate = _dot(xn, wgu_ref[:, lo:hi])
        up = _dot(xn, wgu_ref[:, d_ff + lo:d_ff + hi])
        act = (jax.nn.silu(gate) * up).astype(BF16)
        acc = acc + _dot(act, wd_ref[lo:hi, :])
    y = x + 0.5 * acc
    if with_final:
        y = _rmsnorm(y, gf_ref[...])
    out_ref[...] = y

    if layer is not None:
        @pl.when(pl.program_id(0) == pl.num_programs(0) - 1)
        def _():
            for cp in copies_out:
                cp.wait()


def _ffn(x, pres, g, wgu, wd, g_final=None, layer=None):
    t, d = x.shape
    tm = min(ROW_TILE, t)
    row = lambda w: pl.BlockSpec((tm, w), lambda i: (i, 0))
    hbm = pl.BlockSpec(memory_space=pl.ANY)
    args, specs = [x], [row(d)]
    for o, w in pres:
        args += [o, w]
        specs += [row(o.shape[1]), _const_spec(w)]
    args += [g, wgu, wd]
    specs += [_const_spec(g)] + ([_const_spec(wgu), _const_spec(wd)] if layer is None else [hbm, hbm])
    if g_final is not None:
        args.append(g_final)
        specs.append(_const_spec(g_final))
    out_specs, out_shape, scratch = row(d), jax.ShapeDtypeStruct((t, d), F32), []
    if layer is not None:
        gu_shape, d_shape = wgu.shape[1:], wd.shape[1:]
        chunks = 8
        out_specs = [out_specs, hbm, hbm]
        out_shape = [out_shape, jax.ShapeDtypeStruct(gu_shape, BF16), jax.ShapeDtypeStruct(d_shape, BF16)]
        scratch = [pltpu.VMEM(gu_shape, BF16), pltpu.VMEM(d_shape, BF16),
                   pltpu.VMEM((2, gu_shape[0] // chunks, gu_shape[1]), F32),
                   pltpu.VMEM((2, d_shape[0] // chunks, d_shape[1]), F32),
                   pltpu.SemaphoreType.DMA((2,)), pltpu.SemaphoreType.DMA((2,))]
    return pl.pallas_call(
        functools.partial(_ffn_kernel, len(pres), g_final is not None, layer),
        grid=(t // tm,),
        in_specs=specs,
        out_specs=out_specs,
        out_shape=out_shape,
        scratch_shapes=scratch,
        compiler_params=_params(("arbitrary",)),
        name="ffn",
    )(*args)


def _store_values_with_ones(out_ref, v):
    low = lax.broadcasted_iota(jnp.int32, (1, LANES), 1) < LANES // 2
    for p in range(v.shape[1] // LANES):
        vp = v[:, p * LANES:(p + 1) * LANES]
        out_ref[:, 2 * p * LANES:(2 * p + 1) * LANES] = jnp.where(low, vp, 1.0).astype(out_ref.dtype)
        out_ref[:, (2 * p + 1) * LANES:(2 * p + 2) * LANES] = jnp.where(low, 1.0, vp).astype(out_ref.dtype)


def _merge_head_pair(acc_even, acc_odd):
    half = LANES // 2
    low = lax.broadcasted_iota(jnp.int32, (1, LANES), 1) < half
    den = pltpu.roll(jnp.where(low, acc_odd, acc_even), half, 1)
    return jnp.where(low, acc_even, acc_odd) / den


def _eproj_kernel(x_ref, g_ref, w_ref, aq, ak, av, bq, bk, bv, akf, avf, bkf, bvf):
    hn = _rmsnorm(x_ref[...], g_ref[...]).astype(BF16)
    t = _dot(hn, w_ref[...])
    w = aq.shape[1]
    parts = [t[:, i * w:(i + 1) * w] for i in range(6)]
    aq[...] = (parts[0] * (A_HEAD_DIM ** -0.5 * LOG2E)).astype(BF16)
    ak[...] = parts[1].astype(BF16)
    _store_values_with_ones(av, parts[2])
    bq[...] = (parts[3] * (B_HEAD_DIM ** -0.5 * LOG2E)).astype(BF16)
    bk[...] = parts[4].astype(BF16)
    bv[...] = parts[5].astype(BF16)
    akf[...] = parts[1].reshape(akf.shape)
    avf[...] = parts[2].reshape(avf.shape)
    bkf[...] = parts[4].reshape(bkf.shape)
    bvf[...] = parts[5].reshape(bvf.shape)


def _eproj(x, g, w, rows_per_batch, a_keep):
    t, d = x.shape
    tm = min(ROW_TILE, t, a_keep)
    assert rows_per_batch % tm == 0 and a_keep % tm == 0
    per, kept = rows_per_batch // tm, a_keep // tm
    wd = w.shape[1] // 6
    row = lambda n: pl.BlockSpec((tm, n), lambda i: (i, 0))
    a_map = lambda i: ((i // per) * kept + jnp.maximum(i % per - (per - kept), 0), 0, 0)
    a_f32 = pl.BlockSpec((tm, A_HEADS, A_HEAD_DIM), a_map)
    b_f32 = pl.BlockSpec((tm, B_HEADS, 2 * B_HEAD_DIM), lambda i: (i, 0, 0))
    widths = [wd, wd, 2 * wd, wd, wd, wd]
    shapes = ([jax.ShapeDtypeStruct((t, n), BF16) for n in widths]
              + [jax.ShapeDtypeStruct((t // per * kept, A_HEADS, A_HEAD_DIM), F32)] * 2
              + [jax.ShapeDtypeStruct((t, B_HEADS, 2 * B_HEAD_DIM), F32)] * 2)
    return pl.pallas_call(
        _eproj_kernel,
        grid=(t // tm,),
        in_specs=[row(d), _const_spec(g), _const_spec(w)],
        out_specs=[row(n) for n in widths] + [a_f32, a_f32, b_f32, b_f32],
        out_shape=shapes,
        compiler_params=_params(("arbitrary",)),
        name="eproj",
    )(x, g, w)


def _cproj_kernel(x_ref, g_ref, win_ref, gq_ref, gkv_ref, wq_ref, cq_ref, sq_ref, ck_ref, sk_ref,
                  q_out, ckvf_out, ckvb_out, kr_out, krf_out):
    hn = _rmsnorm(x_ref[...], g_ref[...]).astype(BF16)
    t = _dot(hn, win_ref[...])
    cqn = _rmsnorm(t[:, :C_Q_LORA], gq_ref[...]).astype(BF16)
    ckv = _rmsnorm(t[:, C_Q_LORA:C_Q_LORA + C_KV_LORA], gkv_ref[...])
    ckvf_out[...] = ckv
    ckvb_out[...] = ckv.astype(BF16)
    tg = t[:, C_Q_LORA + C_KV_LORA:]
    kr = tg * ck_ref[...] + pltpu.roll(tg, LANES - C_ROPE, 1) * sk_ref[...]
    kr_out[...] = kr
    krf_out[...] = kr[:, C_NOPE:C_NOPE + C_ROPE]
    q = _dot(cqn, wq_ref[...])
    cq, sq = cq_ref[...], sq_ref[...]
    for h in range(C_HEADS):
        qh = q[:, h * LANES:(h + 1) * LANES]
        qf = qh * cq + pltpu.roll(qh, LANES - C_ROPE, 1) * sq
        q_out[:, h * LANES:(h + 1) * LANES] = qf.astype(BF16)


def _cproj(x, g, win, gq, gkv, wq, tabs):
    t, d = x.shape
    tm = min(ROW_TILE, t)
    nper = tabs[0].shape[0] // tm
    row = lambda n: pl.BlockSpec((tm, n), lambda i: (i, 0))
    tab = pl.BlockSpec((tm, LANES), lambda i: (i % nper, 0))
    shapes = [jax.ShapeDtypeStruct((t, C_HEADS * LANES), BF16),
              jax.ShapeDtypeStruct((t, C_KV_LORA), F32),
              jax.ShapeDtypeStruct((t, C_KV_LORA), BF16),
              jax.ShapeDtypeStruct((t, LANES), F32),
              jax.ShapeDtypeStruct((t, C_ROPE), F32)]
    return pl.pallas_call(
        _cproj_kernel,
        grid=(t // tm,),
        in_specs=[row(d), _const_spec(g), _const_spec(win), _const_spec(gq), _const_spec(gkv),
                  _const_spec(wq), tab, tab, tab, tab],
        out_specs=[row(C_HEADS * LANES), row(C_KV_LORA), row(C_KV_LORA), row(LANES), row(C_ROPE)],
        out_shape=shapes,
        compiler_params=_params(("arbitrary",)),
        name="cproj",
    )(x, g, win, gq, gkv, wq, *tabs)


def _cexpand_kernel(ckv_ref, kr_ref, wk_ref, wv_ref, k_out, v_out):
    ckv = ckv_ref[...]
    kd = _dot(ckv, wk_ref[...])
    vd = _dot(ckv, wv_ref[...])
    kr = kr_ref[...]
    high = (lax.broadcasted_iota(jnp.int32, (1, LANES), 1) >= C_V).astype(F32)
    for h in range(C_HEADS):
        sl = slice(h * LANES, (h + 1) * LANES)
        k_out[:, sl] = (kd[:, sl] + kr).astype(BF16)
        v_out[:, sl] = (vd[:, sl] + (high if h % 2 == 0 else 1.0 - high)).astype(BF16)


def _cexpand(ckv, kr, wk, wv, tm):
    t = ckv.shape[0]
    row = lambda n: pl.BlockSpec((tm, n), lambda i: (i, 0))
    return pl.pallas_call(
        _cexpand_kernel,
        grid=(t // tm,),
        in_specs=[row(C_KV_LORA), row(LANES), _const_spec(wk), _const_spec(wv)],
        out_specs=[row(C_HEADS * LANES), row(C_HEADS * LANES)],
        out_shape=[jax.ShapeDtypeStruct((t, C_HEADS * LANES), BF16)] * 2,
        compiler_params=_params(("arbitrary",)),
        name="cexpand",
    )(ckv, kr, wk, wv)


def _bias_kernel(n_rows, shift, layout, idx_ref, tab_ref, out_ref):
    idx = idx_ref[...]
    n_heads = out_ref.shape[0]
    bq = out_ref.shape[1] // len(layout)
    bk = out_ref.shape[2] // len(layout[0])
    for h in range(n_heads):
        base = tab_ref[h, shift] if shift is not None else 0.0

        def body(r, acc):
            return jnp.where(idx == r, (tab_ref[h, r] - base) * LOG2E, acc)

        vals = lax.fori_loop(0, n_rows, body, jnp.full(idx.shape, NEG, F32))
        for a, row in enumerate(layout):
            for t, u in enumerate(row):
                out_ref[h, a * bq:(a + 1) * bq, t * bk:(t + 1) * bk] = vals[u * bq:(u + 1) * bq, :]


def _bias_expand(idx, table, shift=None, layout=((0,),)):
    n_heads, n_rows = table.shape
    n_blocks = 1 + max(max(row) for row in layout)
    out_shape = (n_heads, idx.shape[0] // n_blocks * len(layout), idx.shape[1] * len(layout[0]))
    return pl.pallas_call(
        functools.partial(_bias_kernel, n_rows, shift, layout),
        in_specs=[pl.BlockSpec(idx.shape, lambda: (0, 0)),
                  pl.BlockSpec(memory_space=pltpu.SMEM)],
        out_specs=pl.BlockSpec(out_shape, lambda: (0, 0, 0)),
        out_shape=jax.ShapeDtypeStruct(out_shape, F32),
        compiler_params=pltpu.CompilerParams(vmem_limit_bytes=VMEM_LIMIT),
        name="bias_expand",
    )(idx, table)


def _unique_blocks(idx, bq, bk):
    q, k = idx.shape
    blocks = idx.reshape(q // bq, bq, k // bk, bk).transpose(0, 2, 1, 3).reshape(-1, bq, bk)
    uniq, inv = np.unique(blocks, axis=0, return_inverse=True)
    layout = tuple(tuple(int(u) for u in row) for row in inv.reshape(q // bq, k // bk))
    return jnp.asarray(uniq.reshape(-1, bk), jnp.int32), layout


def _a_attend(q, kwin, vwin, bias_ref, valid, out_ref, s_ref):
    low = lax.broadcasted_iota(jnp.int32, (1, LANES), 1) < A_HEAD_DIM
    for h in range(A_HEADS):
        sl = slice(h // 2 * LANES, (h // 2 + 1) * LANES)
        qp = q[:, sl]
        qm = jnp.where(low if h % 2 == 0 else jnp.logical_not(low), qp, jnp.zeros_like(qp))
        s = _dot_nt(qm, kwin[:, sl]) + bias_ref[h]
        if valid is not None:
            s = jnp.where(valid, s, NEG)
        s_ref[h] = s - jnp.max(s, axis=-1, keepdims=True)
    accs = []
    for h in range(A_HEADS):
        accs.append(_dot(jnp.exp2(s_ref[h]).astype(BF16), vwin[:, h * LANES:(h + 1) * LANES]))
        if h % 2 == 1:
            out_ref[0, :, h // 2 * LANES:(h // 2 + 1) * LANES] = _merge_head_pair(*accs).astype(BF16)
            accs = []


def _a_prompt_kernel(q_ref, k_ref, v_ref, bias_ref, out_ref, s_ref):
    tq = q_ref.shape[1]
    win = bias_ref.shape[2]
    n_tiles = win // tq
    j = pl.program_id(1)

    def run(check_positions):
        ks, vs = [], []
        for t in range(n_tiles):
            start = pl.multiple_of(jnp.maximum(j - (n_tiles - 1 - t), 0) * tq, tq)
            ks.append(k_ref[0, pl.ds(start, tq), :])
            vs.append(v_ref[0, pl.ds(start, tq), :])
        valid = None
        if check_positions:
            valid = lax.broadcasted_iota(jnp.int32, (1, win), 1) + (j - (n_tiles - 1)) * tq >= 0
        _a_attend(q_ref[0], jnp.concatenate(ks, axis=0), jnp.concatenate(vs, axis=0), bias_ref, valid,
                  out_ref, s_ref)

    @pl.when(j < n_tiles - 1)
    def _():
        run(True)

    @pl.when(j >= n_tiles - 1)
    def _():
        run(False)


def _a_prompt(q, k, v, bias):
    b, s, w = q.shape
    tq = bias.shape[1]
    kv = lambda a: pl.BlockSpec((1, s, a.shape[2]), lambda i, j: (i, 0, 0))
    return pl.pallas_call(
        _a_prompt_kernel,
        grid=(b, s // tq),
        in_specs=[pl.BlockSpec((1, tq, w), lambda i, j: (i, j, 0)), kv(k), kv(v), _const_spec(bias)],
        out_specs=pl.BlockSpec((1, tq, w), lambda i, j: (i, j, 0)),
        out_shape=jax.ShapeDtypeStruct((b, s, w), BF16),
        scratch_shapes=[pltpu.VMEM(bias.shape, F32)],
        compiler_params=_params(("arbitrary", "arbitrary")),
        name="a_prompt",
    )(q, k, v, bias)


def _a_sample_kernel(q_ref, kc_ref, vc_ref, kn_ref, vn_ref, bias_ref, out_ref):
    q = q_ref[0]
    tq, n_c = q.shape[0], kc_ref.shape[1]
    kc = kc_ref[0].astype(BF16)
    vc = vc_ref[0]
    low = lax.broadcasted_iota(jnp.int32, (1, LANES), 1) < A_HEAD_DIM
    for p in range(A_HEADS // 2):
        sl = slice(p * LANES, (p + 1) * LANES)
        qp, vcp = q[:, sl], vc[:, sl]
        accs = []
        for sub in range(2):
            h = 2 * p + sub
            qm = jnp.where(low if sub == 0 else jnp.logical_not(low), qp, jnp.zeros_like(qp))
            s_c = _dot_nt(qm, kc[:, sl]) + bias_ref[h, :, :n_c]
            s_n = _dot_nt(qm, kn_ref[0, :, sl]) + bias_ref[h, :, n_c:n_c + tq]
            m = jnp.maximum(jnp.max(s_c, axis=-1, keepdims=True), jnp.max(s_n, axis=-1, keepdims=True))
            vc_ones = jnp.where(low, vcp, 1.0) if sub == 0 else jnp.where(low, 1.0, vcp)
            accs.append(_dot(jnp.exp2(s_c - m).astype(BF16), vc_ones.astype(BF16))
                        + _dot(jnp.exp2(s_n - m).astype(BF16), vn_ref[0, :, h * LANES:(h + 1) * LANES]))
        out_ref[0, :, sl] = _merge_head_pair(*accs).astype(BF16)


def _a_sample(q, k_cache, v_cache, k_new, v_new, bias):
    b, tq, w = q.shape
    blk = lambda a: pl.BlockSpec((1,) + a.shape[1:], lambda i: (i, 0, 0))
    return pl.pallas_call(
        _a_sample_kernel,
        grid=(b,),
        in_specs=[blk(q), blk(k_cache), blk(v_cache), blk(k_new), blk(v_new), _const_spec(bias)],
        out_specs=blk(q),
        out_shape=jax.ShapeDtypeStruct((b, tq, w), BF16),
        compiler_params=_params(("arbitrary",)),
        name="a_sample",
    )(q, k_cache, v_cache, k_new, v_new, bias)


def _b_lambda(lq1, lk1, lq2, lk2, lam_init):
    s1 = jnp.sum(lq1[...] * lk1[...], axis=-1, keepdims=True)
    s2 = jnp.sum(lq2[...] * lk2[...], axis=-1, keepdims=True)
    return jnp.exp(s1) - jnp.exp(s2) + lam_init


def _b_stack_queries(q):
    lane = lax.broadcasted_iota(jnp.int32, (1, LANES), 1)
    low = lane < B_HEAD_DIM
    qs = []
    for h in range(B_HEADS):
        qh = q[:, h * LANES:(h + 1) * LANES]
        zero = jnp.zeros_like(qh)
        qs.append(jnp.concatenate([jnp.where(low, qh, zero), jnp.where(low, zero, qh)], axis=0))
    return qs


def _b_finish(o, lam, g, lam_init, tq):
    ob = o[:tq] - lam * o[tq:]
    return _rmsnorm(ob, g) * (1.0 - lam_init)


def _b_prompt_kernel(lam_init, q_ref, k_ref, v_ref, bias_ref, lq1, lk1, lq2, lk2, g_ref, out_ref,
                     qs_ref, m_ref, alpha_ref, l_ref, acc_ref, s_ref):
    tq = q_ref.shape[1]
    tk = bias_ref.shape[3]
    i = pl.program_id(1)
    for h, qh in enumerate(_b_stack_queries(q_ref[0])):
        qs_ref[h] = qh

    def tile(j, bias_sel, first, n_tiles=1):
        width = n_tiles * tk
        lane_tiles = range(width // LANES)
        start = pl.multiple_of(j * tk, tk)
        for h in range(B_HEADS):
            sl = slice(h * LANES, (h + 1) * LANES)
            s = _dot_nt(qs_ref[h], k_ref[0, pl.ds(start, width), sl])
            parts = [s[:, c * LANES:(c + 1) * LANES] for c in lane_tiles]
            if bias_sel is not None:
                parts = [jnp.concatenate([pt[:tq] + bias_ref[bias_sel, h, :, c * LANES:(c + 1) * LANES],
                                          pt[tq:] + bias_ref[bias_sel, h, :, c * LANES:(c + 1) * LANES]], axis=0)
                         for c, pt in enumerate(parts)]
            mx = parts[0]
            for pt in parts[1:]:
                mx = jnp.maximum(mx, pt)
            row_max = jnp.max(mx, axis=-1, keepdims=True)
            if first:
                m_new = jnp.broadcast_to(row_max, (2 * tq, LANES))
            else:
                m_new = jnp.maximum(m_ref[h], row_max)
                alpha_ref[h] = jnp.exp2(m_ref[h] - m_new)
            m_ref[h] = m_new
            for c, pt in enumerate(parts):
                s_ref[h, :, c * LANES:(c + 1) * LANES] = pt
        for h in range(B_HEADS):
            sl = slice(h * LANES, (h + 1) * LANES)
            m_new = m_ref[h]
            ps = [jnp.exp2(s_ref[h, :, c * LANES:(c + 1) * LANES] - m_new) for c in lane_tiles]
            psum = ps[0]
            for pv in ps[1:]:
                psum = psum + pv
            pv = _dot(jnp.concatenate(ps, axis=1).astype(BF16), v_ref[0, pl.ds(start, width), sl])
            if first:
                l_ref[h] = psum
                acc_ref[h] = pv
            else:
                alpha = alpha_ref[h]
                l_ref[h] = alpha * l_ref[h] + psum
                acc_ref[h] = alpha * acc_ref[h] + pv

    tile(i, 1, True)

    @pl.when(i >= 1)
    def _():
        tile(i - 1, 0, False)

    n_far = jnp.maximum(i - 1, 0)

    def far_pair(p, carry):
        tile(2 * p, None, False, 2)
        return carry

    lax.fori_loop(0, jnp.right_shift(n_far, 1), far_pair, 0)

    @pl.when(jnp.bitwise_and(n_far, 1) == 1)
    def _():
        tile(n_far - 1, None, False)

    lam = _b_lambda(lq1, lk1, lq2, lk2, lam_init)

    group = 2

    def finish(g, carry):
        for u in range(group):
            h = g * group + u
            l = jnp.sum(l_ref[h], axis=-1, keepdims=True)
            bn = _b_finish(acc_ref[h] / l, lam, g_ref[...], lam_init, tq)
            out_ref[0, :, pl.ds(pl.multiple_of(h * LANES, LANES), LANES)] = bn.astype(BF16)
        return carry

    lax.fori_loop(0, B_HEADS // group, finish, 0)


def _b_prompt(q, k, v, bias, lams, g, lam_init):
    b, s, w = q.shape
    tq = bias.shape[2]
    kv = pl.BlockSpec((1, s, w), lambda i, j: (i, 0, 0))
    scratch = pltpu.VMEM((B_HEADS, 2 * tq, LANES), F32)
    return pl.pallas_call(
        functools.partial(_b_prompt_kernel, lam_init),
        grid=(b, s // tq),
        in_specs=[pl.BlockSpec((1, tq, w), lambda i, j: (i, j, 0)), kv, kv, _const_spec(bias)]
                 + [_const_spec(x) for x in lams] + [_const_spec(g)],
        out_specs=pl.BlockSpec((1, tq, w), lambda i, j: (i, j, 0)),
        out_shape=jax.ShapeDtypeStruct((b, s, w), BF16),
        scratch_shapes=[pltpu.VMEM((B_HEADS, 2 * tq, LANES), BF16), scratch, scratch, scratch, scratch,
                        pltpu.VMEM((B_HEADS, 2 * tq, 2 * tq), F32)],
        compiler_params=_params(("arbitrary", "arbitrary")),
        name="b_prompt",
    )(q, k, v, bias, *lams, g)


def _b_sample_kernel(lam_init, q_ref, kc_ref, vc_ref, kn_ref, vn_ref, bias_ref, lq1, lk1, lq2, lk2, g_ref,
                     out_ref):
    tq, n_c = q_ref.shape[1], kc_ref.shape[1]
    qs = _b_stack_queries(q_ref[0])
    kc, vc = kc_ref[0].astype(BF16), vc_ref[0].astype(BF16)
    lam = _b_lambda(lq1, lk1, lq2, lk2, lam_init)
    both = lambda bias: jnp.concatenate([bias, bias], axis=0)
    for h in range(B_HEADS):
        sl = slice(h * LANES, (h + 1) * LANES)
        s_c = _dot_nt(qs[h], kc[:, sl]) + both(bias_ref[h, :, :n_c])
        s_n = _dot_nt(qs[h], kn_ref[0, :, sl]) + both(bias_ref[h, :, n_c:n_c + tq])
        m = jnp.maximum(jnp.max(s_c, axis=-1, keepdims=True), jnp.max(s_n, axis=-1, keepdims=True))
        e_c, e_n = jnp.exp2(s_c - m), jnp.exp2(s_n - m)
        l = jnp.sum(e_c, axis=-1, keepdims=True) + jnp.sum(e_n, axis=-1, keepdims=True)
        o = (_dot(e_c.astype(BF16), vc[:, sl]) + _dot(e_n.astype(BF16), vn_ref[0, :, sl])) / l
        bn = _b_finish(o, lam, g_ref[...], lam_init, tq)
        out_ref[0, :, sl] = bn.astype(BF16)


def _b_sample(q, k_cache, v_cache, k_new, v_new, bias, lams, g, lam_init):
    b, tq, w = q.shape
    blk = lambda a: pl.BlockSpec((1,) + a.shape[1:], lambda i: (i, 0, 0))
    return pl.pallas_call(
        functools.partial(_b_sample_kernel, lam_init),
        grid=(b,),
        in_specs=[blk(q), blk(k_cache), blk(v_cache), blk(k_new), blk(v_new), _const_spec(bias)]
                 + [_const_spec(x) for x in lams] + [_const_spec(g)],
        out_specs=blk(q),
        out_shape=jax.ShapeDtypeStruct((b, tq, w), BF16),
        compiler_params=_params(("arbitrary",)),
        name="b_sample",
    )(q, k_cache, v_cache, k_new, v_new, bias, *lams, g)


def _c_prompt_kernel(q_ref, k_ref, v_ref, out_ref, m_ref, alpha_ref, acc_ref, mask_ref, s_ref):
    tq = q_ref.shape[1]
    tk = tq
    n_lane_tiles = tk // LANES
    i = pl.program_id(1)

    @pl.when((pl.program_id(0) == 0) & (i == 0))
    def _():
        shift = CHUNK.bit_length() - 1
        rc = jnp.right_shift(lax.broadcasted_iota(jnp.int32, (tq, tk), 0), shift)
        kc = jnp.right_shift(lax.broadcasted_iota(jnp.int32, (tq, tk), 1), shift)
        mask_ref[...] = jnp.where(kc <= rc, 0.0, NEG)

    def tile(j, first):
        start = pl.multiple_of(j * tk, tk)
        for h in range(C_HEADS):
            sl = slice(h * LANES, (h + 1) * LANES)
            s = _dot_nt(q_ref[0, :, sl], k_ref[0, pl.ds(start, tk), sl])
            parts = [s[:, c * LANES:(c + 1) * LANES] for c in range(n_lane_tiles)]
            if first:
                parts = [pt + mask_ref[:, c * LANES:(c + 1) * LANES] for c, pt in enumerate(parts)]
            mx = parts[0]
            for pt in parts[1:]:
                mx = jnp.maximum(mx, pt)
            row_max = jnp.max(mx, axis=-1, keepdims=True)
            if first:
                m_new = jnp.broadcast_to(row_max, (tq, LANES))
            else:
                m_new = jnp.maximum(m_ref[h], row_max)
                alpha_ref[h] = jnp.exp2(m_ref[h] - m_new)
            m_ref[h] = m_new
            for c, pt in enumerate(parts):
                s_ref[h, :, c * LANES:(c + 1) * LANES] = pt
        for h in range(C_HEADS):
            sl = slice(h * LANES, (h + 1) * LANES)
            m_new = m_ref[h]
            p = jnp.concatenate([jnp.exp2(s_ref[h, :, c * LANES:(c + 1) * LANES] - m_new)
                                 for c in range(n_lane_tiles)], axis=1).astype(BF16)
            pv = _dot(p, v_ref[0, pl.ds(start, tk), sl])
            acc_ref[h] = pv if first else alpha_ref[h] * acc_ref[h] + pv

    tile(i, True)

    def full(j, carry):
        tile(j, False)
        return carry

    lax.fori_loop(0, i, full, 0)

    group = 4

    def finish(g, carry):
        for u in range(group):
            p = g * group + u
            merged = _merge_head_pair(acc_ref[2 * p], acc_ref[2 * p + 1])
            out_ref[0, :, pl.ds(pl.multiple_of(p * LANES, LANES), LANES)] = merged.astype(BF16)
        return carry

    lax.fori_loop(0, C_HEADS // 2 // group, finish, 0)


def _c_prompt(q, k, v):
    b, s, w = q.shape
    tq = ATT_TILE
    scratch = pltpu.VMEM((C_HEADS, tq, LANES), F32)
    kv = pl.BlockSpec((1, s, w), lambda i, j: (i, 0, 0))
    return pl.pallas_call(
        _c_prompt_kernel,
        grid=(b, s // tq),
        in_specs=[pl.BlockSpec((1, tq, w), lambda i, j: (i, j, 0)), kv, kv],
        out_specs=pl.BlockSpec((1, tq, C_HEADS * C_V), lambda i, j: (i, j, 0)),
        out_shape=jax.ShapeDtypeStruct((b, s, C_HEADS * C_V), BF16),
        scratch_shapes=[scratch, scratch, scratch, pltpu.VMEM((tq, tq), F32),
                        pltpu.VMEM((C_HEADS, tq, tq), F32)],
        compiler_params=_params(("arbitrary", "arbitrary")),
        name="c_prompt",
    )(q, k, v)


def _c_sample_kernel(q_ref, ckv_c_ref, kr_c_ref, ckv_n_ref, kr_n_ref, wk_ref, wv_ref, mask_ref, out_ref):
    tq, n_c = q_ref.shape[1], ckv_c_ref.shape[1]
    q = q_ref[0]
    heads = [q[:, h * LANES:(h + 1) * LANES] for h in range(C_HEADS)]
    qr = jnp.concatenate(heads, axis=0)
    qa = jnp.concatenate([_dot(qh, wk_ref[h]) for h, qh in enumerate(heads)], axis=0).astype(BF16)
    ckv_c, ckv_n = ckv_c_ref[0].astype(BF16), ckv_n_ref[0]
    kr_c, kr_n = kr_c_ref[0].astype(BF16), kr_n_ref[0].astype(BF16)
    s_c = _dot_nt(qa, ckv_c) + _dot_nt(qr, kr_c) + mask_ref[:, :n_c]
    s_n = _dot_nt(qa, ckv_n) + _dot_nt(qr, kr_n) + mask_ref[:, n_c:n_c + tq]
    m = jnp.maximum(jnp.max(s_c, axis=-1, keepdims=True), jnp.max(s_n, axis=-1, keepdims=True))
    e_c, e_n = jnp.exp2(s_c - m), jnp.exp2(s_n - m)
    l = jnp.sum(e_c, axis=-1, keepdims=True) + jnp.sum(e_n, axis=-1, keepdims=True)
    o = ((_dot(e_c.astype(BF16), ckv_c) + _dot(e_n.astype(BF16), ckv_n)) / l).astype(BF16)
    for p in range(C_HEADS // 2):
        rows = lambda h: o[h * tq:(h + 1) * tq]
        pair = _dot(rows(2 * p), wv_ref[2 * p]) + _dot(rows(2 * p + 1), wv_ref[2 * p + 1])
        out_ref[0, :, p * LANES:(p + 1) * LANES] = pair.astype(BF16)


def _c_sample(q, ckv_cache, kr_cache, ckv_new, kr_new, wk, wv, mask):
    b, tq, _ = q.shape
    blk = lambda a: pl.BlockSpec((1,) + a.shape[1:], lambda i: (i, 0, 0))
    return pl.pallas_call(
        _c_sample_kernel,
        grid=(b,),
        in_specs=[blk(q), blk(ckv_cache), blk(kr_cache), blk(ckv_new), blk(kr_new),
                  _const_spec(wk), _const_spec(wv), _const_spec(mask)],
        out_specs=pl.BlockSpec((1, tq, C_HEADS * C_V), lambda i: (i, 0, 0)),
        out_shape=jax.ShapeDtypeStruct((b, tq, C_HEADS * C_V), BF16),
        compiler_params=_params(("arbitrary",)),
        name="c_sample",
    )(q, ckv_cache, kr_cache, ckv_new, kr_new, wk, wv, mask)


def _t5_bucket(rel):
    nb = T5_BUCKETS // 2
    max_exact = nb // 2
    ret = jnp.where(rel > 0, nb, 0)
    n = jnp.abs(rel)
    n_f = jnp.maximum(n, 1).astype(F32)
    large = max_exact + (jnp.log(n_f / max_exact) / math.log(T5_MAX_DIST / max_exact)
                         * (nb - max_exact)).astype(jnp.int32)
    large = jnp.minimum(large, nb - 1)
    return ret + jnp.where(n < max_exact, n, large)


def _a_index(q_pos, k_pos, k_real):
    rel = np.clip(q_pos[:, None] - k_pos[None, :], -A_REL_CLIP, A_REL_CLIP) + A_REL_CLIP
    qc, kc = q_pos[:, None] // CHUNK, k_pos[None, :] // CHUNK
    ok = k_real[None, :] & (kc <= qc) & (kc >= qc - A_PAST_CHUNKS)
    return np.where(ok, rel, -1).astype(np.int32)


def _b_index(q_pos, k_pos, k_real):
    idx = _t5_bucket(k_pos[None, :] - q_pos[:, None])
    ok = k_real[None, :] & ((k_pos[None, :] // CHUNK) <= (q_pos[:, None] // CHUNK))
    return jnp.where(ok, idx, -1).astype(jnp.int32)


def _rope_tables(pos, scale):
    half = C_ROPE // 2
    inv = ROPE_THETA ** (-jnp.arange(half, dtype=F32) / half)
    ang = pos.astype(F32)[:, None] * inv[None, :]
    cos = jnp.concatenate([jnp.cos(ang)] * 2, axis=-1)
    sin = jnp.concatenate([jnp.sin(ang)] * 2, axis=-1)
    n = pos.shape[0]
    z_nope = jnp.zeros((n, C_NOPE), F32)
    z_tail = jnp.zeros((n, LANES - C_NOPE - C_ROPE), F32)
    cq = jnp.concatenate([jnp.full((n, C_NOPE), scale, F32), cos * scale, z_tail], axis=-1)
    sq = jnp.concatenate([z_nope, sin * scale, z_tail], axis=-1)
    ck = jnp.concatenate([z_nope, cos, z_tail], axis=-1)
    sk = jnp.concatenate([z_nope, sin, z_tail], axis=-1)
    return cq, sq, ck, sk


def _rot_cols(w):
    half = w.shape[-1] // 2
    return jnp.concatenate([-w[..., half:], w[..., :half]], axis=-1)


def kernel(x_prompt, x_sample, cache_a_k, cache_a_v, cache_b_k, cache_b_v, cache_c_kv, cache_c_kr,
           t5_bias, ffn1_norm, ffn1_w_gu, ffn1_w_down, mix_norm, ffn2_norm, ffn2_w_gu, ffn2_w_down,
           e_w_in, a_rel_bias, b_lambda_q1, b_lambda_k1, b_lambda_q2, b_lambda_k2, b_subln, e_w_out,
           c_w_in, c_q_norm, c_kv_norm, c_w_q_up, c_w_kv_up, c_w_out, final_norm):
    batch, seq, d = x_prompt.shape
    dec_batch, t_new, _ = x_sample.shape
    past = cache_b_k.shape[2]
    n_cache_a = cache_a_k.shape[2]
    a_w = A_HEADS * A_HEAD_DIM
    b_w = B_HEADS * 2 * B_HEAD_DIM
    tq = ATT_TILE
    a_pad = A_PAST_CHUNKS * CHUNK
    a_keep = min(a_pad, seq)
    assert tq + 1 >= T5_MAX_DIST and tq % CHUNK == 0 and a_pad % tq == 0
    far_bucket = T5_BUCKETS // 2 - 1
    lam_init = 0.8 - 0.6 * math.exp(-0.3 * 0)
    c_scale = (C_NOPE + C_ROPE) ** -0.5 * math.log2(math.e)
    row2 = lambda v: v.reshape(1, -1)

    e_in = e_w_in[0].astype(BF16)
    e_out_a, e_out_b = e_w_out[0, :a_w].astype(BF16), e_w_out[0, a_w:].astype(BF16)
    w_in = c_w_in[0]
    w_kr = w_in[:, C_Q_LORA + C_KV_LORA:]
    c_in = jnp.concatenate([w_in[:, :C_Q_LORA + C_KV_LORA], jnp.zeros((d, C_NOPE), F32),
                            w_kr, _rot_cols(w_kr)], axis=-1).astype(BF16)
    wq = c_w_q_up[0].reshape(C_Q_LORA, C_HEADS, C_NOPE + C_ROPE)
    wq_rope = wq[..., C_NOPE:]
    c_q = jnp.concatenate([wq[..., :C_NOPE], wq_rope, _rot_cols(wq_rope)], axis=-1)
    c_q = c_q.reshape(C_Q_LORA, C_HEADS * LANES).astype(BF16)
    wkv = c_w_kv_up[0].reshape(C_KV_LORA, C_HEADS, C_NOPE + C_V)
    c_k = jnp.concatenate([wkv[..., :C_NOPE], jnp.zeros_like(wkv[..., C_NOPE:])], axis=-1)
    c_k = c_k.reshape(C_KV_LORA, C_HEADS * LANES).astype(BF16)
    wv_pairs = wkv[..., C_NOPE:].reshape(C_KV_LORA, C_HEADS // 2, 2, C_V)
    z_v = jnp.zeros_like(wv_pairs[:, :, 0])
    c_v = jnp.stack([jnp.concatenate([wv_pairs[:, :, 0], z_v], axis=-1),
                     jnp.concatenate([z_v, wv_pairs[:, :, 1]], axis=-1)], axis=2)
    c_v = c_v.reshape(C_KV_LORA, C_HEADS * LANES).astype(BF16)
    c_k_abs = jnp.pad(jnp.transpose(wkv[..., :C_NOPE], (1, 2, 0)), ((0, 0), (0, LANES - C_NOPE), (0, 0)))
    c_k_abs = c_k_abs.astype(BF16)
    c_v_abs = jnp.transpose(c_v.reshape(C_KV_LORA, C_HEADS, LANES), (1, 0, 2))
    c_out_w = c_w_out[0].astype(BF16)
    lams = [row2(b_lambda_q1[0]), row2(b_lambda_k1[0]), row2(b_lambda_q2[0]), row2(b_lambda_k2[0])]
    subln = row2(b_subln[0])

    pos_s = past + jnp.arange(t_new)
    r = jnp.arange(tq)
    assert A_TILE % CHUNK == 0 and a_pad % A_TILE == 0
    a_idx_p, a_lay_p = _unique_blocks(
        _a_index(a_pad + np.arange(A_TILE), np.arange(a_pad + A_TILE), np.ones((a_pad + A_TILE,), bool)),
        CHUNK, LANES)
    ka = n_cache_a + t_new
    ka_pad = -(-ka // LANES) * LANES
    pos_s_np = past + np.arange(t_new)
    a_kpos = np.concatenate([past - n_cache_a + np.arange(n_cache_a), pos_s_np,
                             np.zeros((ka_pad - ka,), np.int64)])
    a_real = np.arange(ka_pad) < ka
    a_idx_s, a_lay_s = _unique_blocks(_a_index(pos_s_np, a_kpos, a_real & (a_kpos >= 0)), t_new, LANES)
    b_idx_p = jnp.stack([_b_index(tq + r, r, jnp.ones((tq,), bool)),
                         _b_index(r, r, jnp.ones((tq,), bool))])
    kb = past + t_new
    kb_pad = -(-kb // LANES) * LANES
    b_kpos = jnp.arange(kb_pad)
    b_real = b_kpos < kb
    b_idx_s = _b_index(pos_s, b_kpos, b_real)
    c_mask_s = jnp.where(b_real[None, :] & ((b_kpos[None, :] // CHUNK) <= (pos_s[:, None] // CHUNK)),
                         0.0, NEG).astype(F32)

    a_bias_p = _bias_expand(a_idx_p, a_rel_bias[0], None, a_lay_p)
    a_bias_s = _bias_expand(a_idx_s, a_rel_bias[0], None, a_lay_s)
    t5_t = t5_bias.T
    b_bias_p = jnp.stack([_bias_expand(b_idx_p[0], t5_t, far_bucket),
                          _bias_expand(b_idx_p[1], t5_t, far_bucket)])
    b_bias_s = _bias_expand(b_idx_s, t5_t, far_bucket)

    xp = x_prompt.reshape(batch * seq, d)
    xs = x_sample.reshape(dec_batch * t_new, d)
    g1, gm, g2 = row2(ffn1_norm[0]), row2(mix_norm[0]), row2(ffn2_norm[0])

    xp, wgu, wd = _ffn(xp, [], g1, ffn1_w_gu, ffn1_w_down, layer=0)
    xs = _ffn(xs, [], g1, wgu, wd)

    aq, ak, av, bq, bk, bv, akf, avf, bkf, bvf = _eproj(xp, gm, e_in, seq, a_keep)
    to3 = lambda t, n: t.reshape(n, -1, t.shape[-1])
    a_out = _a_prompt(to3(aq, batch), to3(ak, batch), to3(av, batch), a_bias_p)
    b_out = _b_prompt(to3(bq, batch), to3(bk, batch), to3(bv, batch), b_bias_p, lams, subln, lam_init)
    p_a_k = akf.reshape(1, batch, a_keep, A_HEADS, A_HEAD_DIM)
    p_a_v = avf.reshape(1, batch, a_keep, A_HEADS, A_HEAD_DIM)
    p_b_k = bkf.reshape(1, batch, seq, B_HEADS, 2 * B_HEAD_DIM)
    p_b_v = bvf.reshape(1, batch, seq, B_HEADS, 2 * B_HEAD_DIM)
    xp, wgu, wd = _ffn(xp, [(a_out.reshape(-1, a_w), e_out_a), (b_out.reshape(-1, b_w), e_out_b)],
                       g2, ffn2_w_gu, ffn2_w_down, layer=0)

    aq, ak, av, bq, bk, bv, akf, avf, bkf, bvf = _eproj(xs, gm, e_in, dec_batch * t_new, dec_batch * t_new)
    flat = lambda cache: cache.reshape(dec_batch, cache.shape[1], -1)
    a_out = _a_sample(to3(aq, dec_batch), flat(cache_a_k[0]), flat(cache_a_v[0]), to3(ak, dec_batch),
                      to3(av, dec_batch), a_bias_s)
    b_out = _b_sample(to3(bq, dec_batch), flat(cache_b_k[0]), flat(cache_b_v[0]), to3(bk, dec_batch),
                      to3(bv, dec_batch), b_bias_s, lams, subln, lam_init)
    s_a_k = akf.reshape(1, dec_batch, t_new, A_HEADS, A_HEAD_DIM)
    s_a_v = avf.reshape(1, dec_batch, t_new, A_HEADS, A_HEAD_DIM)
    s_b_k = bkf.reshape(1, dec_batch, t_new, B_HEADS, 2 * B_HEAD_DIM)
    s_b_v = bvf.reshape(1, dec_batch, t_new, B_HEADS, 2 * B_HEAD_DIM)
    xs = _ffn(xs, [(a_out.reshape(-1, a_w), e_out_a), (b_out.reshape(-1, b_w), e_out_b)], g2, wgu, wd)

    g1, gm, g2 = row2(ffn1_norm[1]), row2(mix_norm[1]), row2(ffn2_norm[1])
    gq, gkv, gfin = row2(c_q_norm[0]), row2(c_kv_norm[0]), row2(final_norm)
    xp, wgu, wd = _ffn(xp, [], g1, ffn1_w_gu, ffn1_w_down, layer=1)
    xs = _ffn(xs, [], g1, wgu, wd)

    q, ckvf, ckvb, kr, krf = _cproj(xp, gm, c_in, gq, gkv, c_q, _rope_tables(jnp.arange(seq), c_scale))
    k, v = _cexpand(ckvb, kr, c_k, c_v, min(ROW_TILE, batch * seq))
    c_out = _c_prompt(to3(q, batch), to3(k, batch), to3(v, batch))
    p_c_kv = ckvf.reshape(1, batch, seq, C_KV_LORA)
    p_c_kr = krf.reshape(1, batch, seq, C_ROPE)
    y_prompt, wgu, wd = _ffn(xp, [(c_out.reshape(-1, C_HEADS * C_V), c_out_w)], g2, ffn2_w_gu, ffn2_w_down,
                             gfin, layer=1)

    q, ckvf, ckvb, kr, krf = _cproj(xs, gm, c_in, gq, gkv, c_q,
                                    _rope_tables(jnp.tile(pos_s, dec_batch), c_scale))
    kr_cache = jnp.pad(cache_c_kr[0], ((0, 0), (0, 0), (C_NOPE, LANES - C_NOPE - C_ROPE)))
    c_out = _c_sample(to3(q, dec_batch), cache_c_kv[0], kr_cache, to3(ckvb, dec_batch), to3(kr, dec_batch),
                      c_k_abs, c_v_abs, jnp.tile(c_mask_s, (C_HEADS, 1)))
    s_c_kv = ckvf.reshape(1, dec_batch, t_new, C_KV_LORA)
    s_c_kr = krf.reshape(1, dec_batch, t_new, C_ROPE)
    y_sample = _ffn(xs, [(c_out.reshape(-1, C_HEADS * C_V), c_out_w)], g2, wgu, wd, gfin)

    return (y_prompt.reshape(batch, seq, d), y_sample.reshape(dec_batch, t_new, d),
            p_a_k, p_a_v, p_b_k, p_b_v, p_c_kv, p_c_kr,
            s_a_k, s_a_v, s_b_k, s_b_v, s_c_kv, s_c_kr)
```

```python
import functools
import math

import jax
import jax.numpy as jnp
import numpy as np
from jax import lax
from jax.experimental import pallas as pl
from jax.experimental.pallas import tpu as pltpu

F32 = jnp.float32
BF16 = jnp.bfloat16

EPS = 1e-6
NEG = -1e30
LOG2E = math.log2(math.e)
CHUNK = 64
A_HEADS = 8
A_HEAD_DIM = 64
A_PAST_CHUNKS = 8
A_REL_CLIP = 64
B_HEADS = 4
B_HEAD_DIM = 64
T5_BUCKETS = 32
T5_MAX_DIST = 128
C_HEADS = 16
C_NOPE = 64
C_ROPE = 32
C_V = 64
C_Q_LORA = 384
C_KV_LORA = 256
ROPE_THETA = 10000.0

LANES = 128
VMEM_LIMIT = 56 * 1024 * 1024
ROW_TILE = 512
ATT_TILE = 256
A_TILE = 256
TOKEN_SUBTILES = 2
MXU_WIDTH = 256
FFN_CHUNKS = 2


def _ffn_chunks(d_ff):
    tiles = d_ff // MXU_WIDTH
    assert tiles * MXU_WIDTH == d_ff
    per = -(-tiles // FFN_CHUNKS)
    edges = [min(c * per, tiles) * MXU_WIDTH for c in range(FFN_CHUNKS + 1)]
    return [(lo, hi) for lo, hi in zip(edges[:-1], edges[1:]) if hi > lo]


def _rmsnorm(x, g):
    return x * lax.rsqrt(jnp.mean(x * x, axis=-1, keepdims=True) + EPS) * g


def _dot(a, b):
    return jnp.dot(a, b, preferred_element_type=F32)


def _dot_nt(a, b):
    return lax.dot_general(a, b, (((1,), (1,)), ((), ())), preferred_element_type=F32)


def _const_spec(a):
    nd = a.ndim
    return pl.BlockSpec(a.shape, lambda *_: (0,) * nd, pipeline_mode=pl.Buffered(1))


def _params(sem, flags=None):
    return pltpu.CompilerParams(dimension_semantics=sem, vmem_limit_bytes=VMEM_LIMIT, flags=flags)


def _weight_chunk_copy(src, stage, sems, c, slot):
    rows = stage.shape[1]
    return pltpu.make_async_copy(src.at[pl.ds(c * rows, rows), :], stage.at[slot], sems.at[slot])


def _load_weight_as_bf16(src, stage, sems, dst):
    rows = stage.shape[1]
    n = src.shape[0] // rows
    _weight_chunk_copy(src, stage, sems, 0, 0).start()
    for c in range(n):
        slot = c % 2
        if c + 1 < n:
            _weight_chunk_copy(src, stage, sems, c + 1, 1 - slot).start()
        _weight_chunk_copy(src, stage, sems, c, slot).wait()
        dst[c * rows:(c + 1) * rows, :] = stage[slot].astype(BF16)


def _ffn_kernel(n_pre, with_final, layer, *refs):
    x_ref = refs[0]
    pres = [(refs[1 + 2 * i], refs[2 + 2 * i]) for i in range(n_pre)]
    idx = 1 + 2 * n_pre
    g_ref, wgu_in, wd_in = refs[idx:idx + 3]
    idx += 3
    gf_ref = refs[idx] if with_final else None
    idx += with_final
    out_ref = refs[idx]
    if layer is None:
        wgu_ref, wd_ref = wgu_in, wd_in
    else:
        wgu_out, wd_out, wgu_ref, wd_ref, gu_stage, d_stage, in_sems, out_sems = refs[idx + 1:]
        copies_out = [pltpu.make_async_copy(wgu_ref, wgu_out, out_sems.at[0]),
                      pltpu.make_async_copy(wd_ref, wd_out, out_sems.at[1])]

        @pl.when(pl.program_id(0) == 0)
        def _():
            _load_weight_as_bf16(wgu_in.at[layer], gu_stage, in_sems, wgu_ref)
            _load_weight_as_bf16(wd_in.at[layer], d_stage, in_sems, wd_ref)
            for cp in copies_out:
                cp.start()

    x = x_ref[...]
    for o_ref, w_ref in pres:
        x = x + _dot(o_ref[...], w_ref[...])
    xn = _rmsnorm(x, g_ref[...]).astype(BF16)
    d_ff = wd_ref.shape[0]
    acc = jnp.zeros_like(x)
    for lo, hi in _ffn_chunks(d_ff):
        gate = _dot(xn, wgu_ref[:, lo:hi])
        up = _dot(xn, wgu_ref[:, d_ff + lo:d_ff + hi])
        act = (jax.nn.silu(gate) * up).astype(BF16)
        acc = acc + _dot(act, wd_ref[lo:hi, :])
    y = x + 0.5 * acc
    if with_final:
        y = _rmsnorm(y, gf_ref[...])
    out_ref[...] = y

    if layer is not None:
        @pl.when(pl.program_id(0) == pl.num_programs(0) - 1)
        def _():
            for cp in copies_out:
                cp.wait()


def _ffn(x, pres, g, wgu, wd, g_final=None, layer=None):
    t, d = x.shape
    tm = min(ROW_TILE, t)
    row = lambda w: pl.BlockSpec((tm, w), lambda i: (i, 0))
    hbm = pl.BlockSpec(memory_space=pl.ANY)
    args, specs = [x], [row(d)]
    for o, w in pres:
        args += [o, w]
        specs += [row(o.shape[1]), _const_spec(w)]
    args += [g, wgu, wd]
    specs += [_const_spec(g)] + ([_const_spec(wgu), _const_spec(wd)] if layer is None else [hbm, hbm])
    if g_final is not None:
        args.append(g_final)
        specs.append(_const_spec(g_final))
    out_specs, out_shape, scratch = row(d), jax.ShapeDtypeStruct((t, d), F32), []
    if layer is not None:
        gu_shape, d_shape = wgu.shape[1:], wd.shape[1:]
        chunks = 8
        out_specs = [out_specs, hbm, hbm]
        out_shape = [out_shape, jax.ShapeDtypeStruct(gu_shape, BF16), jax.ShapeDtypeStruct(d_shape, BF16)]
        scratch = [pltpu.VMEM(gu_shape, BF16), pltpu.VMEM(d_shape, BF16),
                   pltpu.VMEM((2, gu_shape[0] // chunks, gu_shape[1]), F32),
                   pltpu.VMEM((2, d_shape[0] // chunks, d_shape[1]), F32),
                   pltpu.SemaphoreType.DMA((2,)), pltpu.SemaphoreType.DMA((2,))]
    return pl.pallas_call(
        functools.partial(_ffn_kernel, len(pres), g_final is not None, layer),
        grid=(t // tm,),
        in_specs=specs,
        out_specs=out_specs,
        out_shape=out_shape,
        scratch_shapes=scratch,
        compiler_params=_params(("arbitrary",)),
        name="ffn",
    )(*args)


def _store_values_with_ones(out_ref, v):
    low = lax.broadcasted_iota(jnp.int32, (1, LANES), 1) < LANES // 2
    for p in range(v.shape[1] // LANES):
        vp = v[:, p * LANES:(p + 1) * LANES]
        out_ref[:, 2 * p * LANES:(2 * p + 1) * LANES] = jnp.where(low, vp, 1.0).astype(out_ref.dtype)
        out_ref[:, (2 * p + 1) * LANES:(2 * p + 2) * LANES] = jnp.where(low, 1.0, vp).astype(out_ref.dtype)


def _merge_head_pair(acc_even, acc_odd):
    half = LANES // 2
    low = lax.broadcasted_iota(jnp.int32, (1, LANES), 1) < half
    den = pltpu.roll(jnp.where(low, acc_odd, acc_even), half, 1)
    return jnp.where(low, acc_even, acc_odd) / den


def _eproj_kernel(x_ref, g_ref, w_ref, aq, ak, av, bq, bk, bv, akf, avf, bkf, bvf):
    hn = _rmsnorm(x_ref[...], g_ref[...]).astype(BF16)
    t = _dot(hn, w_ref[...])
    w = aq.shape[1]
    parts = [t[:, i * w:(i + 1) * w] for i in range(6)]
    aq[...] = (parts[0] * (A_HEAD_DIM ** -0.5 * LOG2E)).astype(BF16)
    ak[...] = parts[1].astype(BF16)
    _store_values_with_ones(av, parts[2])
    bq[...] = (parts[3] * (B_HEAD_DIM ** -0.5 * LOG2E)).astype(BF16)
    bk[...] = parts[4].astype(BF16)
    bv[...] = parts[5].astype(BF16)
    akf[...] = parts[1].reshape(akf.shape)
    avf[...] = parts[2].reshape(avf.shape)
    bkf[...] = parts[4].reshape(bkf.shape)
    bvf[...] = parts[5].reshape(bvf.shape)


def _eproj(x, g, w, rows_per_batch, a_keep):
    t, d = x.shape
    tm = min(ROW_TILE, t, a_keep)
    assert rows_per_batch % tm == 0 and a_keep % tm == 0
    per, kept = rows_per_batch // tm, a_keep // tm
    wd = w.shape[1] // 6
    row = lambda n: pl.BlockSpec((tm, n), lambda i: (i, 0))
    a_map = lambda i: ((i // per) * kept + jnp.maximum(i % per - (per - kept), 0), 0, 0)
    a_f32 = pl.BlockSpec((tm, A_HEADS, A_HEAD_DIM), a_map)
    b_f32 = pl.BlockSpec((tm, B_HEADS, 2 * B_HEAD_DIM), lambda i: (i, 0, 0))
    widths = [wd, wd, 2 * wd, wd, wd, wd]
    shapes = ([jax.ShapeDtypeStruct((t, n), BF16) for n in widths]
              + [jax.ShapeDtypeStruct((t // per * kept, A_HEADS, A_HEAD_DIM), F32)] * 2
              + [jax.ShapeDtypeStruct((t, B_HEADS, 2 * B_HEAD_DIM), F32)] * 2)
    return pl.pallas_call(
        _eproj_kernel,
        grid=(t // tm,),
        in_specs=[row(d), _const_spec(g), _const_spec(w)],
        out_specs=[row(n) for n in widths] + [a_f32, a_f32, b_f32, b_f32],
        out_shape=shapes,
        compiler_params=_params(("arbitrary",)),
        name="eproj",
    )(x, g, w)


def _cproj_kernel(x_ref, g_ref, win_ref, gq_ref, gkv_ref, wq_ref, cq_ref, sq_ref, ck_ref, sk_ref,
                  q_out, ckvf_out, ckvb_out, kr_out, krf_out):
    tm = x_ref.shape[0]
    sub = tm // TOKEN_SUBTILES if tm % (TOKEN_SUBTILES * 16) == 0 else tm
    for r0 in range(0, tm, sub):
        rows = slice(r0, r0 + sub)
        hn = _rmsnorm(x_ref[rows, :], g_ref[...]).astype(BF16)
        t = _dot(hn, win_ref[...])
        cqn = _rmsnorm(t[:, :C_Q_LORA], gq_ref[...]).astype(BF16)
        ckv = _rmsnorm(t[:, C_Q_LORA:C_Q_LORA + C_KV_LORA], gkv_ref[...])
        ckvf_out[rows, :] = ckv
        ckvb_out[rows, :] = ckv.astype(BF16)
        tg = t[:, C_Q_LORA + C_KV_LORA:]
        kr = tg * ck_ref[rows, :] + pltpu.roll(tg, LANES - C_ROPE, 1) * sk_ref[rows, :]
        kr_out[rows, :] = kr
        krf_out[rows, :] = kr[:, C_NOPE:C_NOPE + C_ROPE]
        q = _dot(cqn, wq_ref[...])
        cq, sq = cq_ref[rows, :], sq_ref[rows, :]
        for h in range(C_HEADS):
            qh = q[:, h * LANES:(h + 1) * LANES]
            qf = qh * cq + pltpu.roll(qh, LANES - C_ROPE, 1) * sq
            q_out[rows, h * LANES:(h + 1) * LANES] = qf.astype(BF16)


def _cproj(x, g, win, gq, gkv, wq, tabs):
    t, d = x.shape
    tm = min(ROW_TILE, t)
    nper = tabs[0].shape[0] // tm
    row = lambda n: pl.BlockSpec((tm, n), lambda i: (i, 0))
    tab = pl.BlockSpec((tm, LANES), lambda i: (i % nper, 0))
    shapes = [jax.ShapeDtypeStruct((t, C_HEADS * LANES), BF16),
              jax.ShapeDtypeStruct((t, C_KV_LORA), F32),
              jax.ShapeDtypeStruct((t, C_KV_LORA), BF16),
              jax.ShapeDtypeStruct((t, LANES), F32),
              jax.ShapeDtypeStruct((t, C_ROPE), F32)]
    return pl.pallas_call(
        _cproj_kernel,
        grid=(t // tm,),
        in_specs=[row(d), _const_spec(g), _const_spec(win), _const_spec(gq), _const_spec(gkv),
                  _const_spec(wq), tab, tab, tab, tab],
        out_specs=[row(C_HEADS * LANES), row(C_KV_LORA), row(C_KV_LORA), row(LANES), row(C_ROPE)],
        out_shape=shapes,
        compiler_params=_params(("arbitrary",)),
        name="cproj",
    )(x, g, win, gq, gkv, wq, *tabs)


def _bias_kernel(n_rows, shift, layout, idx_ref, tab_ref, out_ref):
    idx = idx_ref[...]
    n_heads = out_ref.shape[0]
    bq = out_ref.shape[1] // len(layout)
    bk = out_ref.shape[2] // len(layout[0])
    for h in range(n_heads):
        base = tab_ref[h, shift] if shift is not None else 0.0

        def body(r, acc):
            return jnp.where(idx == r, (tab_ref[h, r] - base) * LOG2E, acc)

        vals = lax.fori_loop(0, n_rows, body, jnp.full(idx.shape, NEG, F32))
        for a, row in enumerate(layout):
            for t, u in enumerate(row):
                out_ref[h, a * bq:(a + 1) * bq, t * bk:(t + 1) * bk] = vals[u * bq:(u + 1) * bq, :]


def _bias_expand(idx, table, shift=None, layout=((0,),)):
    n_heads, n_rows = table.shape
    n_blocks = 1 + max(max(row) for row in layout)
    out_shape = (n_heads, idx.shape[0] // n_blocks * len(layout), idx.shape[1] * len(layout[0]))
    return pl.pallas_call(
        functools.partial(_bias_kernel, n_rows, shift, layout),
        in_specs=[pl.BlockSpec(idx.shape, lambda: (0, 0)),
                  pl.BlockSpec(memory_space=pltpu.SMEM)],
        out_specs=pl.BlockSpec(out_shape, lambda: (0, 0, 0)),
        out_shape=jax.ShapeDtypeStruct(out_shape, F32),
        compiler_params=pltpu.CompilerParams(vmem_limit_bytes=VMEM_LIMIT),
        name="bias_expand",
    )(idx, table)


def _unique_blocks(idx, bq, bk):
    q, k = idx.shape
    blocks = idx.reshape(q // bq, bq, k // bk, bk).transpose(0, 2, 1, 3).reshape(-1, bq, bk)
    uniq, inv = np.unique(blocks, axis=0, return_inverse=True)
    layout = tuple(tuple(int(u) for u in row) for row in inv.reshape(q // bq, k // bk))
    return jnp.asarray(uniq.reshape(-1, bk), jnp.int32), layout


def _a_attend(q, kwin, vwin, bias_ref, valid, out_ref, s_ref):
    low = lax.broadcasted_iota(jnp.int32, (1, LANES), 1) < A_HEAD_DIM
    for h in range(A_HEADS):
        sl = slice(h // 2 * LANES, (h // 2 + 1) * LANES)
        qp = q[:, sl]
        qm = jnp.where(low if h % 2 == 0 else jnp.logical_not(low), qp, jnp.zeros_like(qp))
        s = _dot_nt(qm, kwin[:, sl]) + bias_ref[h]
        if valid is not None:
            s = jnp.where(valid, s, NEG)
        s_ref[h] = s - jnp.max(s, axis=-1, keepdims=True)
    accs = []
    for h in range(A_HEADS):
        accs.append(_dot(jnp.exp2(s_ref[h]).astype(BF16), vwin[:, h * LANES:(h + 1) * LANES]))
        if h % 2 == 1:
            out_ref[0, :, h // 2 * LANES:(h // 2 + 1) * LANES] = _merge_head_pair(*accs).astype(BF16)
            accs = []


def _a_prompt_kernel(q_ref, k_ref, v_ref, bias_ref, out_ref, s_ref):
    tq = q_ref.shape[1]
    win = bias_ref.shape[2]
    n_tiles = win // tq
    j = pl.program_id(1)

    def run(check_positions):
        ks, vs = [], []
        for t in range(n_tiles):
            start = pl.multiple_of(jnp.maximum(j - (n_tiles - 1 - t), 0) * tq, tq)
            ks.append(k_ref[0, pl.ds(start, tq), :])
            vs.append(v_ref[0, pl.ds(start, tq), :])
        valid = None
        if check_positions:
            valid = lax.broadcasted_iota(jnp.int32, (1, win), 1) + (j - (n_tiles - 1)) * tq >= 0
        _a_attend(q_ref[0], jnp.concatenate(ks, axis=0), jnp.concatenate(vs, axis=0), bias_ref, valid,
                  out_ref, s_ref)

    @pl.when(j < n_tiles - 1)
    def _():
        run(True)

    @pl.when(j >= n_tiles - 1)
    def _():
        run(False)


def _a_prompt(q, k, v, bias):
    b, s, w = q.shape
    tq = bias.shape[1]
    kv = lambda a: pl.BlockSpec((1, s, a.shape[2]), lambda i, j: (i, 0, 0))
    return pl.pallas_call(
        _a_prompt_kernel,
        grid=(b, s // tq),
        in_specs=[pl.BlockSpec((1, tq, w), lambda i, j: (i, j, 0)), kv(k), kv(v), _const_spec(bias)],
        out_specs=pl.BlockSpec((1, tq, w), lambda i, j: (i, j, 0)),
        out_shape=jax.ShapeDtypeStruct((b, s, w), BF16),
        scratch_shapes=[pltpu.VMEM(bias.shape, F32)],
        compiler_params=_params(("arbitrary", "arbitrary")),
        name="a_prompt",
    )(q, k, v, bias)


def _a_sample_kernel(q_ref, kc_ref, vc_ref, kn_ref, vn_ref, bias_ref, out_ref):
    q = q_ref[0]
    tq, n_c = q.shape[0], kc_ref.shape[1]
    kc = kc_ref[0].reshape(n_c, A_HEADS * A_HEAD_DIM).astype(BF16)
    vc = vc_ref[0].reshape(n_c, A_HEADS * A_HEAD_DIM)
    low = lax.broadcasted_iota(jnp.int32, (1, LANES), 1) < A_HEAD_DIM
    for p in range(A_HEADS // 2):
        sl = slice(p * LANES, (p + 1) * LANES)
        qp, vcp = q[:, sl], vc[:, sl]
        accs = []
        for sub in range(2):
            h = 2 * p + sub
            qm = jnp.where(low if sub == 0 else jnp.logical_not(low), qp, jnp.zeros_like(qp))
            s_c = _dot_nt(qm, kc[:, sl]) + bias_ref[h, :, :n_c]
            s_n = _dot_nt(qm, kn_ref[0, :, sl]) + bias_ref[h, :, n_c:n_c + tq]
            m = jnp.maximum(jnp.max(s_c, axis=-1, keepdims=True), jnp.max(s_n, axis=-1, keepdims=True))
            vc_ones = jnp.where(low, vcp, 1.0) if sub == 0 else jnp.where(low, 1.0, vcp)
            accs.append(_dot(jnp.exp2(s_c - m).astype(BF16), vc_ones.astype(BF16))
                        + _dot(jnp.exp2(s_n - m).astype(BF16), vn_ref[0, :, h * LANES:(h + 1) * LANES]))
        out_ref[0, :, sl] = _merge_head_pair(*accs).astype(BF16)


def _a_sample(q, k_cache, v_cache, k_new, v_new, bias):
    b, tq, w = q.shape
    blk = lambda a: pl.BlockSpec((1,) + a.shape[1:], lambda i: (i,) + (0,) * (a.ndim - 1))
    return pl.pallas_call(
        _a_sample_kernel,
        grid=(b,),
        in_specs=[blk(q), blk(k_cache), blk(v_cache), blk(k_new), blk(v_new), _const_spec(bias)],
        out_specs=blk(q),
        out_shape=jax.ShapeDtypeStruct((b, tq, w), BF16),
        compiler_params=_params(("arbitrary",)),
        name="a_sample",
    )(q, k_cache, v_cache, k_new, v_new, bias)


def _b_lambda(lq1, lk1, lq2, lk2, lam_init):
    s1 = jnp.sum(lq1[...] * lk1[...], axis=-1, keepdims=True)
    s2 = jnp.sum(lq2[...] * lk2[...], axis=-1, keepdims=True)
    return jnp.exp(s1) - jnp.exp(s2) + lam_init


def _b_stack_queries(q):
    lane = lax.broadcasted_iota(jnp.int32, (1, LANES), 1)
    low = lane < B_HEAD_DIM
    qs = []
    for h in range(B_HEADS):
        qh = q[:, h * LANES:(h + 1) * LANES]
        zero = jnp.zeros_like(qh)
        qs.append(jnp.concatenate([jnp.where(low, qh, zero), jnp.where(low, zero, qh)], axis=0))
    return qs


def _b_finish(o, lam, g, lam_init, tq):
    ob = o[:tq] - lam * o[tq:]
    return _rmsnorm(ob, g) * (1.0 - lam_init)


def _b_prompt_kernel(lam_init, q_ref, k_ref, v_ref, bias_ref, lq1, lk1, lq2, lk2, g_ref, out_ref,
                     qs_ref, m_ref, alpha_ref, l_ref, acc_ref, s_ref):
    tq = q_ref.shape[1]
    tk = bias_ref.shape[3]
    i = pl.program_id(1)
    for h, qh in enumerate(_b_stack_queries(q_ref[0])):
        qs_ref[h] = qh

    def tile(j, bias_sel, first, n_tiles=1):
        width = n_tiles * tk
        lane_tiles = range(width // LANES)
        start = pl.multiple_of(j * tk, tk)
        for h in range(B_HEADS):
            sl = slice(h * LANES, (h + 1) * LANES)
            s = _dot_nt(qs_ref[h], k_ref[0, pl.ds(start, width), sl])
            parts = [s[:, c * LANES:(c + 1) * LANES] for c in lane_tiles]
            if bias_sel is not None:
                parts = [jnp.concatenate([pt[:tq] + bias_ref[bias_sel, h, :, c * LANES:(c + 1) * LANES],
                                          pt[tq:] + bias_ref[bias_sel, h, :, c * LANES:(c + 1) * LANES]], axis=0)
                         for c, pt in enumerate(parts)]
            mx = parts[0]
            for pt in parts[1:]:
                mx = jnp.maximum(mx, pt)
            row_max = jnp.max(mx, axis=-1, keepdims=True)
            if first:
                m_new = jnp.broadcast_to(row_max, (2 * tq, LANES))
            else:
                m_new = jnp.maximum(m_ref[h], row_max)
                alpha_ref[h] = jnp.exp2(m_ref[h] - m_new)
            m_ref[h] = m_new
            for c, pt in enumerate(parts):
                s_ref[h, :, c * LANES:(c + 1) * LANES] = pt
        for h in range(B_HEADS):
            sl = slice(h * LANES, (h + 1) * LANES)
            m_new = m_ref[h]
            ps = [jnp.exp2(s_ref[h, :, c * LANES:(c + 1) * LANES] - m_new) for c in lane_tiles]
            psum = ps[0]
            for pv in ps[1:]:
                psum = psum + pv
            pv = _dot(jnp.concatenate(ps, axis=1).astype(BF16), v_ref[0, pl.ds(start, width), sl])
            if first:
                l_ref[h] = psum
                acc_ref[h] = pv
            else:
                alpha = alpha_ref[h]
                l_ref[h] = alpha * l_ref[h] + psum
                acc_ref[h] = alpha * acc_ref[h] + pv

    tile(i, 1, True)

    @pl.when(i >= 1)
    def _():
        tile(i - 1, 0, False)

    n_far = jnp.maximum(i - 1, 0)

    def far_pair(p, carry):
        tile(2 * p, None, False, 2)
        return carry

    lax.fori_loop(0, jnp.right_shift(n_far, 1), far_pair, 0)

    @pl.when(jnp.bitwise_and(n_far, 1) == 1)
    def _():
        tile(n_far - 1, None, False)

    lam = _b_lambda(lq1, lk1, lq2, lk2, lam_init)

    group = 2

    def finish(g, carry):
        for u in range(group):
            h = g * group + u
            l = jnp.sum(l_ref[h], axis=-1, keepdims=True)
            bn = _b_finish(acc_ref[h] / l, lam, g_ref[...], lam_init, tq)
            out_ref[0, :, pl.ds(pl.multiple_of(h * LANES, LANES), LANES)] = bn.astype(BF16)
        return carry

    lax.fori_loop(0, B_HEADS // group, finish, 0)


def _b_prompt(q, k, v, bias, lams, g, lam_init):
    b, s, w = q.shape
    tq = bias.shape[2]
    kv = pl.BlockSpec((1, s, w), lambda i, j: (i, 0, 0))
    scratch = pltpu.VMEM((B_HEADS, 2 * tq, LANES), F32)
    return pl.pallas_call(
        functools.partial(_b_prompt_kernel, lam_init),
        grid=(b, s // tq),
        in_specs=[pl.BlockSpec((1, tq, w), lambda i, j: (i, j, 0)), kv, kv, _const_spec(bias)]
                 + [_const_spec(x) for x in lams] + [_const_spec(g)],
        out_specs=pl.BlockSpec((1, tq, w), lambda i, j: (i, j, 0)),
        out_shape=jax.ShapeDtypeStruct((b, s, w), BF16),
        scratch_shapes=[pltpu.VMEM((B_HEADS, 2 * tq, LANES), BF16), scratch, scratch, scratch, scratch,
                        pltpu.VMEM((B_HEADS, 2 * tq, 2 * tq), F32)],
        compiler_params=_params(("arbitrary", "arbitrary")),
        name="b_prompt",
    )(q, k, v, bias, *lams, g)


def _b_sample_kernel(lam_init, q_ref, kc_ref, vc_ref, kn_ref, vn_ref, bias_ref, lq1, lk1, lq2, lk2, g_ref,
                     out_ref):
    tq, n_c = q_ref.shape[1], kc_ref.shape[1]
    qs = _b_stack_queries(q_ref[0])
    lam = _b_lambda(lq1, lk1, lq2, lk2, lam_init)
    both = lambda bias: jnp.concatenate([bias, bias], axis=0)
    for h in range(B_HEADS):
        sl = slice(h * LANES, (h + 1) * LANES)
        kc, vc = kc_ref[0, :, h, :].astype(BF16), vc_ref[0, :, h, :].astype(BF16)
        s_c = _dot_nt(qs[h], kc) + both(bias_ref[h, :, :n_c])
        s_n = _dot_nt(qs[h], kn_ref[0, :, sl]) + both(bias_ref[h, :, n_c:n_c + tq])
        m = jnp.maximum(jnp.max(s_c, axis=-1, keepdims=True), jnp.max(s_n, axis=-1, keepdims=True))
        e_c, e_n = jnp.exp2(s_c - m), jnp.exp2(s_n - m)
        l = jnp.sum(e_c, axis=-1, keepdims=True) + jnp.sum(e_n, axis=-1, keepdims=True)
        o = (_dot(e_c.astype(BF16), vc) + _dot(e_n.astype(BF16), vn_ref[0, :, sl])) / l
        bn = _b_finish(o, lam, g_ref[...], lam_init, tq)
        out_ref[0, :, sl] = bn.astype(BF16)


def _b_sample(q, k_cache, v_cache, k_new, v_new, bias, lams, g, lam_init):
    b, tq, w = q.shape
    blk = lambda a: pl.BlockSpec((1,) + a.shape[1:], lambda i: (i,) + (0,) * (a.ndim - 1))
    return pl.pallas_call(
        functools.partial(_b_sample_kernel, lam_init),
        grid=(b,),
        in_specs=[blk(q), blk(k_cache), blk(v_cache), blk(k_new), blk(v_new), _const_spec(bias)]
                 + [_const_spec(x) for x in lams] + [_const_spec(g)],
        out_specs=blk(q),
        out_shape=jax.ShapeDtypeStruct((b, tq, w), BF16),
        compiler_params=_params(("arbitrary",)),
        name="b_sample",
    )(q, k_cache, v_cache, k_new, v_new, bias, *lams, g)


def _c_prompt_kernel(q_ref, ckv_ref, kr_ref, wk_ref, wv_ref, out_ref, k_ref, v_ref, m_ref, alpha_ref, acc_ref,
                     mask_ref, s_ref):
    tq = q_ref.shape[1]
    tk = tq
    n_lane_tiles = tk // LANES
    i = pl.program_id(1)

    @pl.when((pl.program_id(0) == 0) & (i == 0))
    def _():
        shift = CHUNK.bit_length() - 1
        rc = jnp.right_shift(lax.broadcasted_iota(jnp.int32, (tq, tk), 0), shift)
        kc = jnp.right_shift(lax.broadcasted_iota(jnp.int32, (tq, tk), 1), shift)
        mask_ref[...] = jnp.where(kc <= rc, 0.0, NEG)

    @pl.when(i == 0)
    def _():
        high = (lax.broadcasted_iota(jnp.int32, (1, LANES), 1) >= C_V).astype(F32)
        rows_per_chunk = 2 * tk

        def expand(c, carry):
            rows = pl.ds(pl.multiple_of(c * rows_per_chunk, rows_per_chunk), rows_per_chunk)
            ckv = ckv_ref[0, rows, :]
            kd = _dot(ckv, wk_ref[...])
            vd = _dot(ckv, wv_ref[...])
            kr = kr_ref[0, rows, :]
            for h in range(C_HEADS):
                sl = slice(h * LANES, (h + 1) * LANES)
                k_ref[rows, sl] = (kd[:, sl] + kr).astype(BF16)
                v_ref[rows, sl] = (vd[:, sl] + (high if h % 2 == 0 else 1.0 - high)).astype(BF16)
            return carry

        lax.fori_loop(0, k_ref.shape[0] // rows_per_chunk, expand, 0)

    def tile(j, first):
        start = pl.multiple_of(j * tk, tk)
        for h in range(C_HEADS):
            sl = slice(h * LANES, (h + 1) * LANES)
            s = _dot_nt(q_ref[0, :, sl], k_ref[pl.ds(start, tk), sl])
            parts = [s[:, c * LANES:(c + 1) * LANES] for c in range(n_lane_tiles)]
            if first:
                parts = [pt + mask_ref[:, c * LANES:(c + 1) * LANES] for c, pt in enumerate(parts)]
            mx = parts[0]
            for pt in parts[1:]:
                mx = jnp.maximum(mx, pt)
            row_max = jnp.max(mx, axis=-1, keepdims=True)
            if first:
                m_new = jnp.broadcast_to(row_max, (tq, LANES))
            else:
                m_new = jnp.maximum(m_ref[h], row_max)
                alpha_ref[h] = jnp.exp2(m_ref[h] - m_new)
            m_ref[h] = m_new
            for c, pt in enumerate(parts):
                s_ref[h, :, c * LANES:(c + 1) * LANES] = pt
        for h in range(C_HEADS):
            sl = slice(h * LANES, (h + 1) * LANES)
            m_new = m_ref[h]
            p = jnp.concatenate([jnp.exp2(s_ref[h, :, c * LANES:(c + 1) * LANES] - m_new)
                                 for c in range(n_lane_tiles)], axis=1).astype(BF16)
            pv = _dot(p, v_ref[pl.ds(start, tk), sl])
            acc_ref[h] = pv if first else alpha_ref[h] * acc_ref[h] + pv

    tile(i, True)

    def full(j, carry):
        tile(j, False)
        return carry

    lax.fori_loop(0, i, full, 0)

    group = 4

    def finish(g, carry):
        for u in range(group):
            p = g * group + u
            merged = _merge_head_pair(acc_ref[2 * p], acc_ref[2 * p + 1])
            out_ref[0, :, pl.ds(pl.multiple_of(p * LANES, LANES), LANES)] = merged.astype(BF16)
        return carry

    lax.fori_loop(0, C_HEADS // 2 // group, finish, 0)


def _c_prompt(q, ckv, kr, wk, wv):
    b, s, w = q.shape
    tq = ATT_TILE
    scratch = pltpu.VMEM((C_HEADS, tq, LANES), F32)
    per_batch = lambda a: pl.BlockSpec((1, s, a.shape[2]), lambda i, j: (i, 0, 0))
    return pl.pallas_call(
        _c_prompt_kernel,
        grid=(b, s // tq),
        in_specs=[pl.BlockSpec((1, tq, w), lambda i, j: (i, j, 0)), per_batch(ckv), per_batch(kr),
                  _const_spec(wk), _const_spec(wv)],
        out_specs=pl.BlockSpec((1, tq, C_HEADS * C_V), lambda i, j: (i, j, 0)),
        out_shape=jax.ShapeDtypeStruct((b, s, C_HEADS * C_V), BF16),
        scratch_shapes=[pltpu.VMEM((s, w), BF16), pltpu.VMEM((s, w), BF16),
                        scratch, scratch, scratch, pltpu.VMEM((tq, tq), F32),
                        pltpu.VMEM((C_HEADS, tq, tq), F32)],
        compiler_params=_params(("arbitrary", "arbitrary")),
        name="c_prompt",
    )(q, ckv, kr, wk, wv)


def _c_sample_kernel(q_ref, ckv_c_ref, kr_c_ref, ckv_n_ref, kr_n_ref, wk_ref, wv_ref, mask_ref, out_ref):
    tq, n_c = q_ref.shape[1], ckv_c_ref.shape[1]
    q = q_ref[0]
    heads = [q[:, h * LANES:(h + 1) * LANES] for h in range(C_HEADS)]
    qr = jnp.concatenate(heads, axis=0)
    qa = jnp.concatenate([_dot(qh, wk_ref[h]) for h, qh in enumerate(heads)], axis=0).astype(BF16)
    ckv_c, ckv_n = ckv_c_ref[0].astype(BF16), ckv_n_ref[0]
    kr_c, kr_n = kr_c_ref[0].astype(BF16), kr_n_ref[0].astype(BF16)
    s_c = _dot_nt(qa, ckv_c) + _dot_nt(qr, kr_c) + mask_ref[:, :n_c]
    s_n = _dot_nt(qa, ckv_n) + _dot_nt(qr, kr_n) + mask_ref[:, n_c:n_c + tq]
    m = jnp.maximum(jnp.max(s_c, axis=-1, keepdims=True), jnp.max(s_n, axis=-1, keepdims=True))
    e_c, e_n = jnp.exp2(s_c - m), jnp.exp2(s_n - m)
    l = jnp.sum(e_c, axis=-1, keepdims=True) + jnp.sum(e_n, axis=-1, keepdims=True)
    o = ((_dot(e_c.astype(BF16), ckv_c) + _dot(e_n.astype(BF16), ckv_n)) / l).astype(BF16)
    for p in range(C_HEADS // 2):
        rows = lambda h: o[h * tq:(h + 1) * tq]
        pair = _dot(rows(2 * p), wv_ref[2 * p]) + _dot(rows(2 * p + 1), wv_ref[2 * p + 1])
        out_ref[0, :, p * LANES:(p + 1) * LANES] = pair.astype(BF16)


def _c_sample(q, ckv_cache, kr_cache, ckv_new, kr_new, wk, wv, mask):
    b, tq, _ = q.shape
    blk = lambda a: pl.BlockSpec((1,) + a.shape[1:], lambda i: (i,) + (0,) * (a.ndim - 1))
    return pl.pallas_call(
        _c_sample_kernel,
        grid=(b,),
        in_specs=[blk(q), blk(ckv_cache), blk(kr_cache), blk(ckv_new), blk(kr_new),
                  _const_spec(wk), _const_spec(wv), _const_spec(mask)],
        out_specs=pl.BlockSpec((1, tq, C_HEADS * C_V), lambda i: (i, 0, 0)),
        out_shape=jax.ShapeDtypeStruct((b, tq, C_HEADS * C_V), BF16),
        compiler_params=_params(("arbitrary",)),
        name="c_sample",
    )(q, ckv_cache, kr_cache, ckv_new, kr_new, wk, wv, mask)


def _t5_bucket(rel):
    nb = T5_BUCKETS // 2
    max_exact = nb // 2
    ret = jnp.where(rel > 0, nb, 0)
    n = jnp.abs(rel)
    n_f = jnp.maximum(n, 1).astype(F32)
    large = max_exact + (jnp.log(n_f / max_exact) / math.log(T5_MAX_DIST / max_exact)
                         * (nb - max_exact)).astype(jnp.int32)
    large = jnp.minimum(large, nb - 1)
    return ret + jnp.where(n < max_exact, n, large)


def _a_index(q_pos, k_pos, k_real):
    rel = np.clip(q_pos[:, None] - k_pos[None, :], -A_REL_CLIP, A_REL_CLIP) + A_REL_CLIP
    qc, kc = q_pos[:, None] // CHUNK, k_pos[None, :] // CHUNK
    ok = k_real[None, :] & (kc <= qc) & (kc >= qc - A_PAST_CHUNKS)
    return np.where(ok, rel, -1).astype(np.int32)


def _b_index(q_pos, k_pos, k_real):
    idx = _t5_bucket(k_pos[None, :] - q_pos[:, None])
    ok = k_real[None, :] & ((k_pos[None, :] // CHUNK) <= (q_pos[:, None] // CHUNK))
    return jnp.where(ok, idx, -1).astype(jnp.int32)


def _rope_tables(pos, scale):
    half = C_ROPE // 2
    inv = ROPE_THETA ** (-jnp.arange(half, dtype=F32) / half)
    ang = pos.astype(F32)[:, None] * inv[None, :]
    cos = jnp.concatenate([jnp.cos(ang)] * 2, axis=-1)
    sin = jnp.concatenate([jnp.sin(ang)] * 2, axis=-1)
    n = pos.shape[0]
    z_nope = jnp.zeros((n, C_NOPE), F32)
    z_tail = jnp.zeros((n, LANES - C_NOPE - C_ROPE), F32)
    cq = jnp.concatenate([jnp.full((n, C_NOPE), scale, F32), cos * scale, z_tail], axis=-1)
    sq = jnp.concatenate([z_nope, sin * scale, z_tail], axis=-1)
    ck = jnp.concatenate([z_nope, cos, z_tail], axis=-1)
    sk = jnp.concatenate([z_nope, sin, z_tail], axis=-1)
    return cq, sq, ck, sk


def _rot_cols(w):
    half = w.shape[-1] // 2
    return jnp.concatenate([-w[..., half:], w[..., :half]], axis=-1)


def kernel(x_prompt, x_sample, cache_a_k, cache_a_v, cache_b_k, cache_b_v, cache_c_kv, cache_c_kr,
           t5_bias, ffn1_norm, ffn1_w_gu, ffn1_w_down, mix_norm, ffn2_norm, ffn2_w_gu, ffn2_w_down,
           e_w_in, a_rel_bias, b_lambda_q1, b_lambda_k1, b_lambda_q2, b_lambda_k2, b_subln, e_w_out,
           c_w_in, c_q_norm, c_kv_norm, c_w_q_up, c_w_kv_up, c_w_out, final_norm):
    batch, seq, d = x_prompt.shape
    dec_batch, t_new, _ = x_sample.shape
    past = cache_b_k.shape[2]
    n_cache_a = cache_a_k.shape[2]
    a_w = A_HEADS * A_HEAD_DIM
    b_w = B_HEADS * 2 * B_HEAD_DIM
    tq = ATT_TILE
    a_pad = A_PAST_CHUNKS * CHUNK
    a_keep = min(a_pad, seq)
    assert tq + 1 >= T5_MAX_DIST and tq % CHUNK == 0 and a_pad % tq == 0
    far_bucket = T5_BUCKETS // 2 - 1
    lam_init = 0.8 - 0.6 * math.exp(-0.3 * 0)
    c_scale = (C_NOPE + C_ROPE) ** -0.5 * math.log2(math.e)
    row2 = lambda v: v.reshape(1, -1)

    e_in = e_w_in[0].astype(BF16)
    e_out_a, e_out_b = e_w_out[0, :a_w].astype(BF16), e_w_out[0, a_w:].astype(BF16)
    w_in = c_w_in[0]
    w_kr = w_in[:, C_Q_LORA + C_KV_LORA:]
    c_in = jnp.concatenate([w_in[:, :C_Q_LORA + C_KV_LORA], jnp.zeros((d, C_NOPE), F32),
                            w_kr, _rot_cols(w_kr)], axis=-1).astype(BF16)
    wq = c_w_q_up[0].reshape(C_Q_LORA, C_HEADS, C_NOPE + C_ROPE)
    wq_rope = wq[..., C_NOPE:]
    c_q = jnp.concatenate([wq[..., :C_NOPE], wq_rope, _rot_cols(wq_rope)], axis=-1)
    c_q = c_q.reshape(C_Q_LORA, C_HEADS * LANES).astype(BF16)
    wkv = c_w_kv_up[0].reshape(C_KV_LORA, C_HEADS, C_NOPE + C_V)
    c_k = jnp.concatenate([wkv[..., :C_NOPE], jnp.zeros_like(wkv[..., C_NOPE:])], axis=-1)
    c_k = c_k.reshape(C_KV_LORA, C_HEADS * LANES).astype(BF16)
    wv_pairs = wkv[..., C_NOPE:].reshape(C_KV_LORA, C_HEADS // 2, 2, C_V)
    z_v = jnp.zeros_like(wv_pairs[:, :, 0])
    c_v = jnp.stack([jnp.concatenate([wv_pairs[:, :, 0], z_v], axis=-1),
                     jnp.concatenate([z_v, wv_pairs[:, :, 1]], axis=-1)], axis=2)
    c_v = c_v.reshape(C_KV_LORA, C_HEADS * LANES).astype(BF16)
    c_k_abs = jnp.pad(jnp.transpose(wkv[..., :C_NOPE], (1, 2, 0)), ((0, 0), (0, LANES - C_NOPE), (0, 0)))
    c_k_abs = c_k_abs.astype(BF16)
    c_v_abs = jnp.transpose(c_v.reshape(C_KV_LORA, C_HEADS, LANES), (1, 0, 2))
    c_out_w = c_w_out[0].astype(BF16)
    lams = [row2(b_lambda_q1[0]), row2(b_lambda_k1[0]), row2(b_lambda_q2[0]), row2(b_lambda_k2[0])]
    subln = row2(b_subln[0])

    pos_s = past + jnp.arange(t_new)
    r = jnp.arange(tq)
    assert A_TILE % CHUNK == 0 and a_pad % A_TILE == 0
    a_idx_p, a_lay_p = _unique_blocks(
        _a_index(a_pad + np.arange(A_TILE), np.arange(a_pad + A_TILE), np.ones((a_pad + A_TILE,), bool)),
        CHUNK, LANES)
    ka = n_cache_a + t_new
    ka_pad = -(-ka // LANES) * LANES
    pos_s_np = past + np.arange(t_new)
    a_kpos = np.concatenate([past - n_cache_a + np.arange(n_cache_a), pos_s_np,
                             np.zeros((ka_pad - ka,), np.int64)])
    a_real = np.arange(ka_pad) < ka
    a_idx_s, a_lay_s = _unique_blocks(_a_index(pos_s_np, a_kpos, a_real & (a_kpos >= 0)), t_new, LANES)
    b_idx_p = jnp.stack([_b_index(tq + r, r, jnp.ones((tq,), bool)),
                         _b_index(r, r, jnp.ones((tq,), bool))])
    kb = past + t_new
    kb_pad = -(-kb // LANES) * LANES
    b_kpos = jnp.arange(kb_pad)
    b_real = b_kpos < kb
    b_idx_s = _b_index(pos_s, b_kpos, b_real)
    c_mask_s = jnp.where(b_real[None, :] & ((b_kpos[None, :] // CHUNK) <= (pos_s[:, None] // CHUNK)),
                         0.0, NEG).astype(F32)

    a_bias_p = _bias_expand(a_idx_p, a_rel_bias[0], None, a_lay_p)
    a_bias_s = _bias_expand(a_idx_s, a_rel_bias[0], None, a_lay_s)
    t5_t = t5_bias.T
    b_bias_p = jnp.stack([_bias_expand(b_idx_p[0], t5_t, far_bucket),
                          _bias_expand(b_idx_p[1], t5_t, far_bucket)])
    b_bias_s = _bias_expand(b_idx_s, t5_t, far_bucket)

    xp = x_prompt.reshape(batch * seq, d)
    xs = x_sample.reshape(dec_batch * t_new, d)
    g1, gm, g2 = row2(ffn1_norm[0]), row2(mix_norm[0]), row2(ffn2_norm[0])

    xp, wgu, wd = _ffn(xp, [], g1, ffn1_w_gu, ffn1_w_down, layer=0)
    xs = _ffn(xs, [], g1, wgu, wd)

    aq, ak, av, bq, bk, bv, akf, avf, bkf, bvf = _eproj(xp, gm, e_in, seq, a_keep)
    to3 = lambda t, n: t.reshape(n, -1, t.shape[-1])
    a_out = _a_prompt(to3(aq, batch), to3(ak, batch), to3(av, batch), a_bias_p)
    b_out = _b_prompt(to3(bq, batch), to3(bk, batch), to3(bv, batch), b_bias_p, lams, subln, lam_init)
    p_a_k = akf.reshape(1, batch, a_keep, A_HEADS, A_HEAD_DIM)
    p_a_v = avf.reshape(1, batch, a_keep, A_HEADS, A_HEAD_DIM)
    p_b_k = bkf.reshape(1, batch, seq, B_HEADS, 2 * B_HEAD_DIM)
    p_b_v = bvf.reshape(1, batch, seq, B_HEADS, 2 * B_HEAD_DIM)
    xp, wgu, wd = _ffn(xp, [(a_out.reshape(-1, a_w), e_out_a), (b_out.reshape(-1, b_w), e_out_b)],
                       g2, ffn2_w_gu, ffn2_w_down, layer=0)

    aq, ak, av, bq, bk, bv, akf, avf, bkf, bvf = _eproj(xs, gm, e_in, dec_batch * t_new, dec_batch * t_new)
    a_out = _a_sample(to3(aq, dec_batch), cache_a_k[0], cache_a_v[0], to3(ak, dec_batch), to3(av, dec_batch),
                      a_bias_s)
    b_out = _b_sample(to3(bq, dec_batch), cache_b_k[0], cache_b_v[0], to3(bk, dec_batch), to3(bv, dec_batch),
                      b_bias_s, lams, subln, lam_init)
    s_a_k = akf.reshape(1, dec_batch, t_new, A_HEADS, A_HEAD_DIM)
    s_a_v = avf.reshape(1, dec_batch, t_new, A_HEADS, A_HEAD_DIM)
    s_b_k = bkf.reshape(1, dec_batch, t_new, B_HEADS, 2 * B_HEAD_DIM)
    s_b_v = bvf.reshape(1, dec_batch, t_new, B_HEADS, 2 * B_HEAD_DIM)
    xs = _ffn(xs, [(a_out.reshape(-1, a_w), e_out_a), (b_out.reshape(-1, b_w), e_out_b)], g2, wgu, wd)

    g1, gm, g2 = row2(ffn1_norm[1]), row2(mix_norm[1]), row2(ffn2_norm[1])
    gq, gkv, gfin = row2(c_q_norm[0]), row2(c_kv_norm[0]), row2(final_norm)
    xp, wgu, wd = _ffn(xp, [], g1, ffn1_w_gu, ffn1_w_down, layer=1)
    xs = _ffn(xs, [], g1, wgu, wd)

    q, ckvf, ckvb, kr, krf = _cproj(xp, gm, c_in, gq, gkv, c_q, _rope_tables(jnp.arange(seq), c_scale))
    c_out = _c_prompt(to3(q, batch), to3(ckvb, batch), to3(kr, batch), c_k, c_v)
    p_c_kv = ckvf.reshape(1, batch, seq, C_KV_LORA)
    p_c_kr = krf.reshape(1, batch, seq, C_ROPE)
    y_prompt, wgu, wd = _ffn(xp, [(c_out.reshape(-1, C_HEADS * C_V), c_out_w)], g2, ffn2_w_gu, ffn2_w_down,
                             gfin, layer=1)

    q, ckvf, ckvb, kr, krf = _cproj(xs, gm, c_in, gq, gkv, c_q,
                                    _rope_tables(jnp.tile(pos_s, dec_batch), c_scale))
    kr_cache = jnp.pad(cache_c_kr[0], ((0, 0), (0, 0), (C_NOPE, LANES - C_NOPE - C_ROPE)))
    c_out = _c_sample(to3(q, dec_batch), cache_c_kv[0], kr_cache, to3(ckvb, dec_batch), to3(kr, dec_batch),
                      c_k_abs, c_v_abs, jnp.tile(c_mask_s, (C_HEADS, 1)))
    s_c_kv = ckvf.reshape(1, dec_batch, t_new, C_KV_LORA)
    s_c_kr = krf.reshape(1, dec_batch, t_new, C_ROPE)
    y_sample = _ffn(xs, [(c_out.reshape(-1, C_HEADS * C_V), c_out_w)], g2, wgu, wd, gfin)

    return (y_prompt.reshape(batch, seq, d), y_sample.reshape(dec_batch, t_new, d),
            p_a_k, p_a_v, p_b_k, p_b_v, p_c_kv, p_c_kr,
            s_a_k, s_a_v, s_b_k, s_b_v, s_c_kv, s_c_kr)
```

```python
import functools
import math

import jax
import jax.numpy as jnp
import numpy as np
from jax import lax
from jax.experimental import pallas as pl
from jax.experimental.pallas import tpu as pltpu

F32 = jnp.float32
BF16 = jnp.bfloat16

EPS = 1e-6
NEG = -1e30
LOG2E = math.log2(math.e)
CHUNK = 64
A_HEADS = 8
A_HEAD_DIM = 64
A_PAST_CHUNKS = 8
A_REL_CLIP = 64
B_HEADS = 4
B_HEAD_DIM = 64
T5_BUCKETS = 32
T5_MAX_DIST = 128
C_HEADS = 16
C_NOPE = 64
C_ROPE = 32
C_V = 64
C_Q_LORA = 384
C_KV_LORA = 256
ROPE_THETA = 10000.0

LANES = 128
VMEM_LIMIT = 56 * 1024 * 1024
ROW_TILE = 512
ATT_TILE = 256
A_TILE = 256
TOKEN_SUBTILES = 2
MXU_WIDTH = 256
FFN_CHUNKS = 2


def _ffn_chunks(d_ff):
    tiles = d_ff // MXU_WIDTH
    assert tiles * MXU_WIDTH == d_ff
    per = -(-tiles // FFN_CHUNKS)
    edges = [min(c * per, tiles) * MXU_WIDTH for c in range(FFN_CHUNKS + 1)]
    return [(lo, hi) for lo, hi in zip(edges[:-1], edges[1:]) if hi > lo]


def _rmsnorm(x, g):
    return x * lax.rsqrt(jnp.mean(x * x, axis=-1, keepdims=True) + EPS) * g


def _dot(a, b):
    return jnp.dot(a, b, preferred_element_type=F32)


def _dot_nt(a, b):
    return lax.dot_general(a, b, (((1,), (1,)), ((), ())), preferred_element_type=F32)


def _const_spec(a):
    nd = a.ndim
    return pl.BlockSpec(a.shape, lambda *_: (0,) * nd, pipeline_mode=pl.Buffered(1))


def _params(sem, flags=None):
    return pltpu.CompilerParams(dimension_semantics=sem, vmem_limit_bytes=VMEM_LIMIT, flags=flags)


def _weight_chunk_copy(src, stage, sems, c, slot):
    rows = stage.shape[1]
    return pltpu.make_async_copy(src.at[pl.ds(c * rows, rows), :], stage.at[slot], sems.at[slot])


def _load_weight_as_bf16(src, stage, sems, dst):
    rows = stage.shape[1]
    n = src.shape[0] // rows
    _weight_chunk_copy(src, stage, sems, 0, 0).start()
    for c in range(n):
        slot = c % 2
        if c + 1 < n:
            _weight_chunk_copy(src, stage, sems, c + 1, 1 - slot).start()
        _weight_chunk_copy(src, stage, sems, c, slot).wait()
        dst[c * rows:(c + 1) * rows, :] = stage[slot].astype(BF16)


def _ffn_kernel(n_pre, with_final, layer, *refs):
    x_ref = refs[0]
    pres = [(refs[1 + 2 * i], refs[2 + 2 * i]) for i in range(n_pre)]
    idx = 1 + 2 * n_pre
    g_ref, wgu_in, wd_in = refs[idx:idx + 3]
    idx += 3
    gf_ref = refs[idx] if with_final else None
    idx += with_final
    out_ref = refs[idx]
    if layer is None:
        wgu_ref, wd_ref = wgu_in, wd_in
    else:
        wgu_out, wd_out, wgu_ref, wd_ref, gu_stage, d_stage, in_sems, out_sems = refs[idx + 1:]
        copies_out = [pltpu.make_async_copy(wgu_ref, wgu_out, out_sems.at[0]),
                      pltpu.make_async_copy(wd_ref, wd_out, out_sems.at[1])]

        @pl.when(pl.program_id(0) == 0)
        def _():
            _load_weight_as_bf16(wgu_in.at[layer], gu_stage, in_sems, wgu_ref)
            _load_weight_as_bf16(wd_in.at[layer], d_stage, in_sems, wd_ref)
            for cp in copies_out:
                cp.start()

    x = x_ref[...]
    for o_ref, w_ref in pres:
        x = x + _dot(o_ref[...], w_ref[...])
    xn = _rmsnorm(x, g_ref[...]).astype(BF16)
    d_ff = wd_ref.shape[0]
    acc = jnp.zeros_like(x)
    for lo, hi in _ffn_chunks(d_ff):
        gate = _dot(xn, wgu_ref[:, lo:hi])
        up = _dot(xn, wgu_ref[:, d_ff + lo:d_ff + hi])
        act = (jax.nn.silu(gate) * up).astype(BF16)
        acc = acc + _dot(act, wd_ref[lo:hi, :])
    y = x + 0.5 * acc
    if with_final:
        y = _rmsnorm(y, gf_ref[...])
    out_ref[...] = y

    if layer is not None:
        @pl.when(pl.program_id(0) == pl.num_programs(0) - 1)
        def _():
            for cp in copies_out:
                cp.wait()


def _ffn(x, pres, g, wgu, wd, g_final=None, layer=None):
    t, d = x.shape
    tm = min(ROW_TILE, t)
    row = lambda w: pl.BlockSpec((tm, w), lambda i: (i, 0))
    hbm = pl.BlockSpec(memory_space=pl.ANY)
    args, specs = [x], [row(d)]
    for o, w in pres:
        args += [o, w]
        specs += [row(o.shape[1]), _const_spec(w)]
    args += [g, wgu, wd]
    specs += [_const_spec(g)] + ([_const_spec(wgu), _const_spec(wd)] if layer is None else [hbm, hbm])
    if g_final is not None:
        args.append(g_final)
        specs.append(_const_spec(g_final))
    out_specs, out_shape, scratch = row(d), jax.ShapeDtypeStruct((t, d), F32), []
    if layer is not None:
        gu_shape, d_shape = wgu.shape[1:], wd.shape[1:]
        chunks = 8
        out_specs = [out_specs, hbm, hbm]
        out_shape = [out_shape, jax.ShapeDtypeStruct(gu_shape, BF16), jax.ShapeDtypeStruct(d_shape, BF16)]
        scratch = [pltpu.VMEM(gu_shape, BF16), pltpu.VMEM(d_shape, BF16),
                   pltpu.VMEM((2, gu_shape[0] // chunks, gu_shape[1]), F32),
                   pltpu.VMEM((2, d_shape[0] // chunks, d_shape[1]), F32),
                   pltpu.SemaphoreType.DMA((2,)), pltpu.SemaphoreType.DMA((2,))]
    return pl.pallas_call(
        functools.partial(_ffn_kernel, len(pres), g_final is not None, layer),
        grid=(t // tm,),
        in_specs=specs,
        out_specs=out_specs,
        out_shape=out_shape,
        scratch_shapes=scratch,
        compiler_params=_params(("arbitrary",)),
        name="ffn",
    )(*args)


def _store_values_with_ones(out_ref, v):
    low = lax.broadcasted_iota(jnp.int32, (1, LANES), 1) < LANES // 2
    for p in range(v.shape[1] // LANES):
        vp = v[:, p * LANES:(p + 1) * LANES]
        out_ref[:, 2 * p * LANES:(2 * p + 1) * LANES] = jnp.where(low, vp, 1.0).astype(out_ref.dtype)
        out_ref[:, (2 * p + 1) * LANES:(2 * p + 2) * LANES] = jnp.where(low, 1.0, vp).astype(out_ref.dtype)


def _merge_head_pair(acc_even, acc_odd):
    half = LANES // 2
    low = lax.broadcasted_iota(jnp.int32, (1, LANES), 1) < half
    den = pltpu.roll(jnp.where(low, acc_odd, acc_even), half, 1)
    return jnp.where(low, acc_even, acc_odd) / den


def _eproj_kernel(x_ref, g_ref, w_ref, aq, ak, av, bq, bk, bv, akf, avf, bkf, bvf):
    hn = _rmsnorm(x_ref[...], g_ref[...]).astype(BF16)
    t = _dot(hn, w_ref[...])
    w = aq.shape[1]
    parts = [t[:, i * w:(i + 1) * w] for i in range(6)]
    aq[...] = (parts[0] * (A_HEAD_DIM ** -0.5 * LOG2E)).astype(BF16)
    ak[...] = parts[1].astype(BF16)
    _store_values_with_ones(av, parts[2])
    bq[...] = (parts[3] * (B_HEAD_DIM ** -0.5 * LOG2E)).astype(BF16)
    bk[...] = parts[4].astype(BF16)
    ones = jnp.ones((parts[5].shape[0], LANES), BF16)
    for h in range(B_HEADS):
        bv[:, 2 * h * LANES:(2 * h + 1) * LANES] = parts[5][:, h * LANES:(h + 1) * LANES].astype(BF16)
        bv[:, (2 * h + 1) * LANES:(2 * h + 2) * LANES] = ones
    akf[...] = parts[1].reshape(akf.shape)
    avf[...] = parts[2].reshape(avf.shape)
    bkf[...] = parts[4].reshape(bkf.shape)
    bvf[...] = parts[5].reshape(bvf.shape)


def _eproj(x, g, w, rows_per_batch, a_keep):
    t, d = x.shape
    tm = min(ROW_TILE, t, a_keep)
    assert rows_per_batch % tm == 0 and a_keep % tm == 0
    per, kept = rows_per_batch // tm, a_keep // tm
    wd = w.shape[1] // 6
    row = lambda n: pl.BlockSpec((tm, n), lambda i: (i, 0))
    a_map = lambda i: ((i // per) * kept + jnp.maximum(i % per - (per - kept), 0), 0, 0)
    a_f32 = pl.BlockSpec((tm, A_HEADS, A_HEAD_DIM), a_map)
    b_f32 = pl.BlockSpec((tm, B_HEADS, 2 * B_HEAD_DIM), lambda i: (i, 0, 0))
    widths = [wd, wd, 2 * wd, wd, wd, 2 * wd]
    shapes = ([jax.ShapeDtypeStruct((t, n), BF16) for n in widths]
              + [jax.ShapeDtypeStruct((t // per * kept, A_HEADS, A_HEAD_DIM), F32)] * 2
              + [jax.ShapeDtypeStruct((t, B_HEADS, 2 * B_HEAD_DIM), F32)] * 2)
    return pl.pallas_call(
        _eproj_kernel,
        grid=(t // tm,),
        in_specs=[row(d), _const_spec(g), _const_spec(w)],
        out_specs=[row(n) for n in widths] + [a_f32, a_f32, b_f32, b_f32],
        out_shape=shapes,
        compiler_params=_params(("arbitrary",)),
        name="eproj",
    )(x, g, w)


def _cproj_kernel(x_ref, g_ref, win_ref, gq_ref, gkv_ref, wq_ref, cq_ref, sq_ref, ck_ref, sk_ref,
                  q_out, ckvf_out, ckvb_out, kr_out, krf_out):
    tm = x_ref.shape[0]
    sub = tm // TOKEN_SUBTILES if tm % (TOKEN_SUBTILES * 16) == 0 else tm
    for r0 in range(0, tm, sub):
        rows = slice(r0, r0 + sub)
        hn = _rmsnorm(x_ref[rows, :], g_ref[...]).astype(BF16)
        t = _dot(hn, win_ref[...])
        cqn = _rmsnorm(t[:, :C_Q_LORA], gq_ref[...]).astype(BF16)
        ckv = _rmsnorm(t[:, C_Q_LORA:C_Q_LORA + C_KV_LORA], gkv_ref[...])
        ckvf_out[rows, :] = ckv
        ckvb_out[rows, :] = ckv.astype(BF16)
        tg = t[:, C_Q_LORA + C_KV_LORA:]
        kr = tg * ck_ref[rows, :] + pltpu.roll(tg, LANES - C_ROPE, 1) * sk_ref[rows, :]
        kr_out[rows, :] = kr
        krf_out[rows, :] = kr[:, C_NOPE:C_NOPE + C_ROPE]
        q = _dot(cqn, wq_ref[...])
        cq, sq = cq_ref[rows, :], sq_ref[rows, :]
        for h in range(C_HEADS):
            qh = q[:, h * LANES:(h + 1) * LANES]
            qf = qh * cq + pltpu.roll(qh, LANES - C_ROPE, 1) * sq
            q_out[rows, h * LANES:(h + 1) * LANES] = qf.astype(BF16)


def _cproj(x, g, win, gq, gkv, wq, tabs):
    t, d = x.shape
    tm = min(ROW_TILE, t)
    nper = tabs[0].shape[0] // tm
    row = lambda n: pl.BlockSpec((tm, n), lambda i: (i, 0))
    tab = pl.BlockSpec((tm, LANES), lambda i: (i % nper, 0))
    shapes = [jax.ShapeDtypeStruct((t, C_HEADS * LANES), BF16),
              jax.ShapeDtypeStruct((t, C_KV_LORA), F32),
              jax.ShapeDtypeStruct((t, C_KV_LORA), BF16),
              jax.ShapeDtypeStruct((t, LANES), F32),
              jax.ShapeDtypeStruct((t, C_ROPE), F32)]
    return pl.pallas_call(
        _cproj_kernel,
        grid=(t // tm,),
        in_specs=[row(d), _const_spec(g), _const_spec(win), _const_spec(gq), _const_spec(gkv),
                  _const_spec(wq), tab, tab, tab, tab],
        out_specs=[row(C_HEADS * LANES), row(C_KV_LORA), row(C_KV_LORA), row(LANES), row(C_ROPE)],
        out_shape=shapes,
        compiler_params=_params(("arbitrary",)),
        name="cproj",
    )(x, g, win, gq, gkv, wq, *tabs)


def _bias_kernel(n_rows, shift, layout, idx_ref, tab_ref, out_ref):
    idx = idx_ref[...]
    n_heads = out_ref.shape[0]
    bq = out_ref.shape[1] // len(layout)
    bk = out_ref.shape[2] // len(layout[0])
    for h in range(n_heads):
        base = tab_ref[h, shift] if shift is not None else 0.0

        def body(r, acc):
            return jnp.where(idx == r, (tab_ref[h, r] - base) * LOG2E, acc)

        vals = lax.fori_loop(0, n_rows, body, jnp.full(idx.shape, NEG, F32))
        for a, row in enumerate(layout):
            for t, u in enumerate(row):
                out_ref[h, a * bq:(a + 1) * bq, t * bk:(t + 1) * bk] = vals[u * bq:(u + 1) * bq, :]


def _bias_expand(idx, table, shift=None, layout=((0,),)):
    n_heads, n_rows = table.shape
    n_blocks = 1 + max(max(row) for row in layout)
    out_shape = (n_heads, idx.shape[0] // n_blocks * len(layout), idx.shape[1] * len(layout[0]))
    return pl.pallas_call(
        functools.partial(_bias_kernel, n_rows, shift, layout),
        in_specs=[pl.BlockSpec(idx.shape, lambda: (0, 0)),
                  pl.BlockSpec(memory_space=pltpu.SMEM)],
        out_specs=pl.BlockSpec(out_shape, lambda: (0, 0, 0)),
        out_shape=jax.ShapeDtypeStruct(out_shape, F32),
        compiler_params=pltpu.CompilerParams(vmem_limit_bytes=VMEM_LIMIT),
        name="bias_expand",
    )(idx, table)


def _unique_blocks(idx, bq, bk):
    q, k = idx.shape
    blocks = idx.reshape(q // bq, bq, k // bk, bk).transpose(0, 2, 1, 3).reshape(-1, bq, bk)
    uniq, inv = np.unique(blocks, axis=0, return_inverse=True)
    layout = tuple(tuple(int(u) for u in row) for row in inv.reshape(q // bq, k // bk))
    return jnp.asarray(uniq.reshape(-1, bk), jnp.int32), layout


def _a_attend(q, kwin, vwin, bias_ref, valid, out_ref, s_ref):
    low = lax.broadcasted_iota(jnp.int32, (1, LANES), 1) < A_HEAD_DIM
    for h in range(A_HEADS):
        sl = slice(h // 2 * LANES, (h // 2 + 1) * LANES)
        qp = q[:, sl]
        qm = jnp.where(low if h % 2 == 0 else jnp.logical_not(low), qp, jnp.zeros_like(qp))
        s = _dot_nt(qm, kwin[:, sl]) + bias_ref[h]
        if valid is not None:
            s = jnp.where(valid, s, NEG)
        s_ref[h] = s - jnp.max(s, axis=-1, keepdims=True)
    accs = []
    for h in range(A_HEADS):
        accs.append(_dot(jnp.exp2(s_ref[h]).astype(BF16), vwin[:, h * LANES:(h + 1) * LANES]))
        if h % 2 == 1:
            out_ref[0, :, h // 2 * LANES:(h // 2 + 1) * LANES] = _merge_head_pair(*accs).astype(BF16)
            accs = []


def _a_prompt_kernel(q_ref, k_ref, v_ref, bias_ref, out_ref, s_ref):
    tq = q_ref.shape[1]
    win = bias_ref.shape[2]
    n_tiles = win // tq
    j = pl.program_id(1)

    def run(check_positions):
        ks, vs = [], []
        for t in range(n_tiles):
            start = pl.multiple_of(jnp.maximum(j - (n_tiles - 1 - t), 0) * tq, tq)
            ks.append(k_ref[0, pl.ds(start, tq), :])
            vs.append(v_ref[0, pl.ds(start, tq), :])
        valid = None
        if check_positions:
            valid = lax.broadcasted_iota(jnp.int32, (1, win), 1) + (j - (n_tiles - 1)) * tq >= 0
        _a_attend(q_ref[0], jnp.concatenate(ks, axis=0), jnp.concatenate(vs, axis=0), bias_ref, valid,
                  out_ref, s_ref)

    @pl.when(j < n_tiles - 1)
    def _():
        run(True)

    @pl.when(j >= n_tiles - 1)
    def _():
        run(False)


def _a_prompt(q, k, v, bias):
    b, s, w = q.shape
    tq = bias.shape[1]
    kv = lambda a: pl.BlockSpec((1, s, a.shape[2]), lambda i, j: (i, 0, 0))
    return pl.pallas_call(
        _a_prompt_kernel,
        grid=(b, s // tq),
        in_specs=[pl.BlockSpec((1, tq, w), lambda i, j: (i, j, 0)), kv(k), kv(v), _const_spec(bias)],
        out_specs=pl.BlockSpec((1, tq, w), lambda i, j: (i, j, 0)),
        out_shape=jax.ShapeDtypeStruct((b, s, w), BF16),
        scratch_shapes=[pltpu.VMEM(bias.shape, F32)],
        compiler_params=_params(("arbitrary", "arbitrary")),
        name="a_prompt",
    )(q, k, v, bias)


def _a_sample_kernel(q_ref, kc_ref, vc_ref, kn_ref, vn_ref, bias_ref, out_ref):
    q = q_ref[0]
    tq, n_c = q.shape[0], kc_ref.shape[1]
    kc = kc_ref[0].reshape(n_c, A_HEADS * A_HEAD_DIM).astype(BF16)
    vc = vc_ref[0].reshape(n_c, A_HEADS * A_HEAD_DIM)
    low = lax.broadcasted_iota(jnp.int32, (1, LANES), 1) < A_HEAD_DIM
    for p in range(A_HEADS // 2):
        sl = slice(p * LANES, (p + 1) * LANES)
        qp, vcp = q[:, sl], vc[:, sl]
        accs = []
        for sub in range(2):
            h = 2 * p + sub
            qm = jnp.where(low if sub == 0 else jnp.logical_not(low), qp, jnp.zeros_like(qp))
            s_c = _dot_nt(qm, kc[:, sl]) + bias_ref[h, :, :n_c]
            s_n = _dot_nt(qm, kn_ref[0, :, sl]) + bias_ref[h, :, n_c:n_c + tq]
            m = jnp.maximum(jnp.max(s_c, axis=-1, keepdims=True), jnp.max(s_n, axis=-1, keepdims=True))
            vc_ones = jnp.where(low, vcp, 1.0) if sub == 0 else jnp.where(low, 1.0, vcp)
            accs.append(_dot(jnp.exp2(s_c - m).astype(BF16), vc_ones.astype(BF16))
                        + _dot(jnp.exp2(s_n - m).astype(BF16), vn_ref[0, :, h * LANES:(h + 1) * LANES]))
        out_ref[0, :, sl] = _merge_head_pair(*accs).astype(BF16)


def _a_sample(q, k_cache, v_cache, k_new, v_new, bias):
    b, tq, w = q.shape
    blk = lambda a: pl.BlockSpec((1,) + a.shape[1:], lambda i: (i,) + (0,) * (a.ndim - 1))
    return pl.pallas_call(
        _a_sample_kernel,
        grid=(b,),
        in_specs=[blk(q), blk(k_cache), blk(v_cache), blk(k_new), blk(v_new), _const_spec(bias)],
        out_specs=blk(q),
        out_shape=jax.ShapeDtypeStruct((b, tq, w), BF16),
        compiler_params=_params(("arbitrary",)),
        name="a_sample",
    )(q, k_cache, v_cache, k_new, v_new, bias)


def _b_lambda(lq1, lk1, lq2, lk2, lam_init):
    s1 = jnp.sum(lq1[...] * lk1[...], axis=-1, keepdims=True)
    s2 = jnp.sum(lq2[...] * lk2[...], axis=-1, keepdims=True)
    return jnp.exp(s1) - jnp.exp(s2) + lam_init


def _b_stack_queries(q):
    lane = lax.broadcasted_iota(jnp.int32, (1, LANES), 1)
    low = lane < B_HEAD_DIM
    qs = []
    for h in range(B_HEADS):
        qh = q[:, h * LANES:(h + 1) * LANES]
        zero = jnp.zeros_like(qh)
        qs.append(jnp.concatenate([jnp.where(low, qh, zero), jnp.where(low, zero, qh)], axis=0))
    return qs


def _b_finish(o, lam, g, lam_init, tq):
    ob = o[:tq] - lam * o[tq:]
    return _rmsnorm(ob, g) * (1.0 - lam_init)


def _b_prompt_kernel(lam_init, q_ref, k_ref, v_ref, bias_ref, lq1, lk1, lq2, lk2, g_ref, out_ref,
                     qs_ref, m_ref, alpha_ref, acc_ref, s_ref):
    tq = q_ref.shape[1]
    tk = bias_ref.shape[3]
    i = pl.program_id(1)
    for h, qh in enumerate(_b_stack_queries(q_ref[0])):
        qs_ref[h] = qh

    lane_tiles = range(tk // LANES)
    lam = _b_lambda(lq1, lk1, lq2, lk2, lam_init)

    def scores(j, bias_sel, first):
        start = pl.multiple_of(j * tk, tk)
        for h in range(B_HEADS):
            sl = slice(h * LANES, (h + 1) * LANES)
            s = _dot_nt(qs_ref[h], k_ref[0, pl.ds(start, tk), sl])
            parts = [s[:, c * LANES:(c + 1) * LANES] for c in lane_tiles]
            if bias_sel is not None:
                parts = [jnp.concatenate([pt[:tq] + bias_ref[bias_sel, h, :, c * LANES:(c + 1) * LANES],
                                          pt[tq:] + bias_ref[bias_sel, h, :, c * LANES:(c + 1) * LANES]], axis=0)
                         for c, pt in enumerate(parts)]
            mx = parts[0]
            for pt in parts[1:]:
                mx = jnp.maximum(mx, pt)
            row_max = jnp.max(mx, axis=-1, keepdims=True)
            if first:
                m_new = jnp.broadcast_to(row_max, (2 * tq, LANES))
                alpha_ref[h] = jnp.zeros((2 * tq, LANES), F32)
            else:
                m_new = jnp.maximum(m_ref[h], row_max)
                alpha_ref[h] = jnp.exp2(m_ref[h] - m_new)
            m_ref[h] = m_new
            for c, pt in enumerate(parts):
                s_ref[h, :, c * LANES:(c + 1) * LANES] = pt

    def attend(j, last):
        start = pl.multiple_of(j * tk, tk)
        for h in range(B_HEADS):
            m_new = m_ref[h]
            p = jnp.concatenate([jnp.exp2(s_ref[h, :, c * LANES:(c + 1) * LANES] - m_new) for c in lane_tiles],
                                axis=1).astype(BF16)
            alpha = alpha_ref[h]
            pv = _dot(p, v_ref[0, pl.ds(start, tk), 2 * h * LANES:2 * (h + 1) * LANES])
            num = alpha * acc_ref[h, :, :LANES] + pv[:, :LANES]
            den = alpha * acc_ref[h, :, LANES:] + pv[:, LANES:]
            if last:
                bn = _b_finish(num / den, lam, g_ref[...], lam_init, tq)
                out_ref[0, :, h * LANES:(h + 1) * LANES] = bn.astype(BF16)
            else:
                acc_ref[h, :, :LANES] = num
                acc_ref[h, :, LANES:] = den

    acc_ref[...] = jnp.zeros(acc_ref.shape, F32)
    scores(i, 1, True)

    @pl.when(i >= 1)
    def _():
        attend(i, False)
        scores(i - 1, 0, False)

    def trip(t, carry):
        attend(i - t, False)
        scores(i - 1 - t, None, False)
        return carry

    lax.fori_loop(1, i, trip, 0)
    attend(0, True)


def _b_prompt(q, k, v, bias, lams, g, lam_init):
    b, s, w = q.shape
    tq = bias.shape[2]
    kv = lambda a: pl.BlockSpec((1, s, a.shape[2]), lambda i, j: (i, 0, 0))
    scratch = pltpu.VMEM((B_HEADS, 2 * tq, LANES), F32)
    return pl.pallas_call(
        functools.partial(_b_prompt_kernel, lam_init),
        grid=(b, s // tq),
        in_specs=[pl.BlockSpec((1, tq, w), lambda i, j: (i, j, 0)), kv(k), kv(v), _const_spec(bias)]
                 + [_const_spec(x) for x in lams] + [_const_spec(g)],
        out_specs=pl.BlockSpec((1, tq, w), lambda i, j: (i, j, 0)),
        out_shape=jax.ShapeDtypeStruct((b, s, w), BF16),
        scratch_shapes=[pltpu.VMEM((B_HEADS, 2 * tq, LANES), BF16), scratch, scratch,
                        pltpu.VMEM((B_HEADS, 2 * tq, 2 * LANES), F32), pltpu.VMEM((B_HEADS, 2 * tq, tq), F32)],
        compiler_params=_params(("arbitrary", "arbitrary")),
        name="b_prompt",
    )(q, k, v, bias, *lams, g)


def _b_sample_kernel(lam_init, q_ref, kc_ref, vc_ref, kn_ref, vn_ref, bias_ref, lq1, lk1, lq2, lk2, g_ref,
                     out_ref):
    tq, n_c = q_ref.shape[1], kc_ref.shape[1]
    qs = _b_stack_queries(q_ref[0])
    lam = _b_lambda(lq1, lk1, lq2, lk2, lam_init)
    both = lambda bias: jnp.concatenate([bias, bias], axis=0)
    for h in range(B_HEADS):
        sl = slice(h * LANES, (h + 1) * LANES)
        kc, vc = kc_ref[0, :, h, :].astype(BF16), vc_ref[0, :, h, :].astype(BF16)
        s_c = _dot_nt(qs[h], kc) + both(bias_ref[h, :, :n_c])
        s_n = _dot_nt(qs[h], kn_ref[0, :, sl]) + both(bias_ref[h, :, n_c:n_c + tq])
        m = jnp.maximum(jnp.max(s_c, axis=-1, keepdims=True), jnp.max(s_n, axis=-1, keepdims=True))
        e_c, e_n = jnp.exp2(s_c - m), jnp.exp2(s_n - m)
        l = jnp.sum(e_c, axis=-1, keepdims=True) + jnp.sum(e_n, axis=-1, keepdims=True)
        o = (_dot(e_c.astype(BF16), vc) + _dot(e_n.astype(BF16), vn_ref[0, :, 2 * h * LANES:(2 * h + 1) * LANES])) / l
        bn = _b_finish(o, lam, g_ref[...], lam_init, tq)
        out_ref[0, :, sl] = bn.astype(BF16)


def _b_sample(q, k_cache, v_cache, k_new, v_new, bias, lams, g, lam_init):
    b, tq, w = q.shape
    blk = lambda a: pl.BlockSpec((1,) + a.shape[1:], lambda i: (i,) + (0,) * (a.ndim - 1))
    return pl.pallas_call(
        functools.partial(_b_sample_kernel, lam_init),
        grid=(b,),
        in_specs=[blk(q), blk(k_cache), blk(v_cache), blk(k_new), blk(v_new), _const_spec(bias)]
                 + [_const_spec(x) for x in lams] + [_const_spec(g)],
        out_specs=blk(q),
        out_shape=jax.ShapeDtypeStruct((b, tq, w), BF16),
        compiler_params=_params(("arbitrary",)),
        name="b_sample",
    )(q, k_cache, v_cache, k_new, v_new, bias, *lams, g)


def _c_prompt_kernel(q_ref, ckv_ref, kr_ref, wk_ref, wv_ref, out_ref, k_ref, v_ref, m_ref, alpha_ref, acc_ref,
                     mask_ref, s_ref):
    tq = q_ref.shape[1]
    tk = tq
    n_lane_tiles = tk // LANES
    i = pl.program_id(1)

    @pl.when((pl.program_id(0) == 0) & (i == 0))
    def _():
        shift = CHUNK.bit_length() - 1
        rc = jnp.right_shift(lax.broadcasted_iota(jnp.int32, (tq, tk), 0), shift)
        kc = jnp.right_shift(lax.broadcasted_iota(jnp.int32, (tq, tk), 1), shift)
        mask_ref[...] = jnp.where(kc <= rc, 0.0, NEG)

    @pl.when(i == 0)
    def _():
        high = (lax.broadcasted_iota(jnp.int32, (1, LANES), 1) >= C_V).astype(F32)
        rows_per_chunk = 2 * tk

        def expand(c, carry):
            rows = pl.ds(pl.multiple_of(c * rows_per_chunk, rows_per_chunk), rows_per_chunk)
            ckv = ckv_ref[0, rows, :]
            kd = _dot(ckv, wk_ref[...])
            vd = _dot(ckv, wv_ref[...])
            kr = kr_ref[0, rows, :]
            for h in range(C_HEADS):
                sl = slice(h * LANES, (h + 1) * LANES)
                k_ref[rows, sl] = (kd[:, sl] + kr).astype(BF16)
                v_ref[rows, sl] = (vd[:, sl] + (high if h % 2 == 0 else 1.0 - high)).astype(BF16)
            return carry

        lax.fori_loop(0, k_ref.shape[0] // rows_per_chunk, expand, 0)

    def scores(j, first):
        start = pl.multiple_of(j * tk, tk)
        for h in range(C_HEADS):
            sl = slice(h * LANES, (h + 1) * LANES)
            s = _dot_nt(q_ref[0, :, sl], k_ref[pl.ds(start, tk), sl])
            parts = [s[:, c * LANES:(c + 1) * LANES] for c in range(n_lane_tiles)]
            if first:
                parts = [pt + mask_ref[:, c * LANES:(c + 1) * LANES] for c, pt in enumerate(parts)]
            mx = parts[0]
            for pt in parts[1:]:
                mx = jnp.maximum(mx, pt)
            row_max = jnp.max(mx, axis=-1, keepdims=True)
            if first:
                m_new = jnp.broadcast_to(row_max, (tq, LANES))
                alpha_ref[h] = jnp.zeros((tq, LANES), F32)
            else:
                m_new = jnp.maximum(m_ref[h], row_max)
                alpha_ref[h] = jnp.exp2(m_ref[h] - m_new)
            m_ref[h] = m_new
            for c, pt in enumerate(parts):
                s_ref[h, :, c * LANES:(c + 1) * LANES] = pt

    def attend(j, last):
        start = pl.multiple_of(j * tk, tk)
        accs = []
        for h in range(C_HEADS):
            sl = slice(h * LANES, (h + 1) * LANES)
            m_new = m_ref[h]
            p = jnp.concatenate([jnp.exp2(s_ref[h, :, c * LANES:(c + 1) * LANES] - m_new)
                                 for c in range(n_lane_tiles)], axis=1).astype(BF16)
            acc = alpha_ref[h] * acc_ref[h] + _dot(p, v_ref[pl.ds(start, tk), sl])
            if not last:
                acc_ref[h] = acc
            elif h % 2 == 0:
                accs = [acc]
            else:
                out_ref[0, :, h // 2 * LANES:(h // 2 + 1) * LANES] = _merge_head_pair(accs[0], acc).astype(BF16)

    acc_ref[...] = jnp.zeros(acc_ref.shape, F32)
    scores(i, True)

    def full(j, carry):
        attend(jnp.where(j == 0, i, j - 1), False)
        scores(j, False)
        return carry

    lax.fori_loop(0, i, full, 0)
    attend(jnp.maximum(i - 1, 0), True)


def _c_prompt(q, ckv, kr, wk, wv):
    b, s, w = q.shape
    tq = ATT_TILE
    scratch = pltpu.VMEM((C_HEADS, tq, LANES), F32)
    per_batch = lambda a: pl.BlockSpec((1, s, a.shape[2]), lambda i, j: (i, 0, 0))
    return pl.pallas_call(
        _c_prompt_kernel,
        grid=(b, s // tq),
        in_specs=[pl.BlockSpec((1, tq, w), lambda i, j: (i, j, 0)), per_batch(ckv), per_batch(kr),
                  _const_spec(wk), _const_spec(wv)],
        out_specs=pl.BlockSpec((1, tq, C_HEADS * C_V), lambda i, j: (i, j, 0)),
        out_shape=jax.ShapeDtypeStruct((b, s, C_HEADS * C_V), BF16),
        scratch_shapes=[pltpu.VMEM((s, w), BF16), pltpu.VMEM((s, w), BF16),
                        scratch, scratch, scratch, pltpu.VMEM((tq, tq), F32),
                        pltpu.VMEM((C_HEADS, tq, tq), F32)],
        compiler_params=_params(("arbitrary", "arbitrary")),
        name="c_prompt",
    )(q, ckv, kr, wk, wv)


def _c_sample_kernel(q_ref, ckv_c_ref, kr_c_ref, ckv_n_ref, kr_n_ref, wk_ref, wv_ref, mask_ref, out_ref):
    tq, n_c = q_ref.shape[1], ckv_c_ref.shape[1]
    q = q_ref[0]
    heads = [q[:, h * LANES:(h + 1) * LANES] for h in range(C_HEADS)]
    qr = jnp.concatenate(heads, axis=0)
    qa = jnp.concatenate([_dot(qh, wk_ref[h]) for h, qh in enumerate(heads)], axis=0).astype(BF16)
    ckv_c, ckv_n = ckv_c_ref[0].astype(BF16), ckv_n_ref[0]
    kr_c, kr_n = kr_c_ref[0].astype(BF16), kr_n_ref[0].astype(BF16)
    s_c = _dot_nt(qa, ckv_c) + _dot_nt(qr, kr_c) + mask_ref[:, :n_c]
    s_n = _dot_nt(qa, ckv_n) + _dot_nt(qr, kr_n) + mask_ref[:, n_c:n_c + tq]
    m = jnp.maximum(jnp.max(s_c, axis=-1, keepdims=True), jnp.max(s_n, axis=-1, keepdims=True))
    e_c, e_n = jnp.exp2(s_c - m), jnp.exp2(s_n - m)
    l = jnp.sum(e_c, axis=-1, keepdims=True) + jnp.sum(e_n, axis=-1, keepdims=True)
    o = ((_dot(e_c.astype(BF16), ckv_c) + _dot(e_n.astype(BF16), ckv_n)) / l).astype(BF16)
    for p in range(C_HEADS // 2):
        rows = lambda h: o[h * tq:(h + 1) * tq]
        pair = _dot(rows(2 * p), wv_ref[2 * p]) + _dot(rows(2 * p + 1), wv_ref[2 * p + 1])
        out_ref[0, :, p * LANES:(p + 1) * LANES] = pair.astype(BF16)


def _c_sample(q, ckv_cache, kr_cache, ckv_new, kr_new, wk, wv, mask):
    b, tq, _ = q.shape
    blk = lambda a: pl.BlockSpec((1,) + a.shape[1:], lambda i: (i,) + (0,) * (a.ndim - 1))
    return pl.pallas_call(
        _c_sample_kernel,
        grid=(b,),
        in_specs=[blk(q), blk(ckv_cache), blk(kr_cache), blk(ckv_new), blk(kr_new),
                  _const_spec(wk), _const_spec(wv), _const_spec(mask)],
        out_specs=pl.BlockSpec((1, tq, C_HEADS * C_V), lambda i: (i, 0, 0)),
        out_shape=jax.ShapeDtypeStruct((b, tq, C_HEADS * C_V), BF16),
        compiler_params=_params(("arbitrary",)),
        name="c_sample",
    )(q, ckv_cache, kr_cache, ckv_new, kr_new, wk, wv, mask)


def _t5_bucket(rel):
    nb = T5_BUCKETS // 2
    max_exact = nb // 2
    ret = jnp.where(rel > 0, nb, 0)
    n = jnp.abs(rel)
    n_f = jnp.maximum(n, 1).astype(F32)
    large = max_exact + (jnp.log(n_f / max_exact) / math.log(T5_MAX_DIST / max_exact)
                         * (nb - max_exact)).astype(jnp.int32)
    large = jnp.minimum(large, nb - 1)
    return ret + jnp.where(n < max_exact, n, large)


def _a_index(q_pos, k_pos, k_real):
    rel = np.clip(q_pos[:, None] - k_pos[None, :], -A_REL_CLIP, A_REL_CLIP) + A_REL_CLIP
    qc, kc = q_pos[:, None] // CHUNK, k_pos[None, :] // CHUNK
    ok = k_real[None, :] & (kc <= qc) & (kc >= qc - A_PAST_CHUNKS)
    return np.where(ok, rel, -1).astype(np.int32)


def _b_index(q_pos, k_pos, k_real):
    idx = _t5_bucket(k_pos[None, :] - q_pos[:, None])
    ok = k_real[None, :] & ((k_pos[None, :] // CHUNK) <= (q_pos[:, None] // CHUNK))
    return jnp.where(ok, idx, -1).astype(jnp.int32)


def _rope_tables(pos, scale):
    half = C_ROPE // 2
    inv = ROPE_THETA ** (-jnp.arange(half, dtype=F32) / half)
    ang = pos.astype(F32)[:, None] * inv[None, :]
    cos = jnp.concatenate([jnp.cos(ang)] * 2, axis=-1)
    sin = jnp.concatenate([jnp.sin(ang)] * 2, axis=-1)
    n = pos.shape[0]
    z_nope = jnp.zeros((n, C_NOPE), F32)
    z_tail = jnp.zeros((n, LANES - C_NOPE - C_ROPE), F32)
    cq = jnp.concatenate([jnp.full((n, C_NOPE), scale, F32), cos * scale, z_tail], axis=-1)
    sq = jnp.concatenate([z_nope, sin * scale, z_tail], axis=-1)
    ck = jnp.concatenate([z_nope, cos, z_tail], axis=-1)
    sk = jnp.concatenate([z_nope, sin, z_tail], axis=-1)
    return cq, sq, ck, sk


def _rot_cols(w):
    half = w.shape[-1] // 2
    return jnp.concatenate([-w[..., half:], w[..., :half]], axis=-1)


def kernel(x_prompt, x_sample, cache_a_k, cache_a_v, cache_b_k, cache_b_v, cache_c_kv, cache_c_kr,
           t5_bias, ffn1_norm, ffn1_w_gu, ffn1_w_down, mix_norm, ffn2_norm, ffn2_w_gu, ffn2_w_down,
           e_w_in, a_rel_bias, b_lambda_q1, b_lambda_k1, b_lambda_q2, b_lambda_k2, b_subln, e_w_out,
           c_w_in, c_q_norm, c_kv_norm, c_w_q_up, c_w_kv_up, c_w_out, final_norm):
    batch, seq, d = x_prompt.shape
    dec_batch, t_new, _ = x_sample.shape
    past = cache_b_k.shape[2]
    n_cache_a = cache_a_k.shape[2]
    a_w = A_HEADS * A_HEAD_DIM
    b_w = B_HEADS * 2 * B_HEAD_DIM
    tq = ATT_TILE
    a_pad = A_PAST_CHUNKS * CHUNK
    a_keep = min(a_pad, seq)
    assert tq + 1 >= T5_MAX_DIST and tq % CHUNK == 0 and a_pad % tq == 0
    far_bucket = T5_BUCKETS // 2 - 1
    lam_init = 0.8 - 0.6 * math.exp(-0.3 * 0)
    c_scale = (C_NOPE + C_ROPE) ** -0.5 * math.log2(math.e)
    row2 = lambda v: v.reshape(1, -1)

    e_in = e_w_in[0].astype(BF16)
    e_out_a, e_out_b = e_w_out[0, :a_w].astype(BF16), e_w_out[0, a_w:].astype(BF16)
    w_in = c_w_in[0]
    w_kr = w_in[:, C_Q_LORA + C_KV_LORA:]
    c_in = jnp.concatenate([w_in[:, :C_Q_LORA + C_KV_LORA], jnp.zeros((d, C_NOPE), F32),
                            w_kr, _rot_cols(w_kr)], axis=-1).astype(BF16)
    wq = c_w_q_up[0].reshape(C_Q_LORA, C_HEADS, C_NOPE + C_ROPE)
    wq_rope = wq[..., C_NOPE:]
    c_q = jnp.concatenate([wq[..., :C_NOPE], wq_rope, _rot_cols(wq_rope)], axis=-1)
    c_q = c_q.reshape(C_Q_LORA, C_HEADS * LANES).astype(BF16)
    wkv = c_w_kv_up[0].reshape(C_KV_LORA, C_HEADS, C_NOPE + C_V)
    c_k = jnp.concatenate([wkv[..., :C_NOPE], jnp.zeros_like(wkv[..., C_NOPE:])], axis=-1)
    c_k = c_k.reshape(C_KV_LORA, C_HEADS * LANES).astype(BF16)
    wv_pairs = wkv[..., C_NOPE:].reshape(C_KV_LORA, C_HEADS // 2, 2, C_V)
    z_v = jnp.zeros_like(wv_pairs[:, :, 0])
    c_v = jnp.stack([jnp.concatenate([wv_pairs[:, :, 0], z_v], axis=-1),
                     jnp.concatenate([z_v, wv_pairs[:, :, 1]], axis=-1)], axis=2)
    c_v = c_v.reshape(C_KV_LORA, C_HEADS * LANES).astype(BF16)
    c_k_abs = jnp.pad(jnp.transpose(wkv[..., :C_NOPE], (1, 2, 0)), ((0, 0), (0, LANES - C_NOPE), (0, 0)))
    c_k_abs = c_k_abs.astype(BF16)
    c_v_abs = jnp.transpose(c_v.reshape(C_KV_LORA, C_HEADS, LANES), (1, 0, 2))
    c_out_w = c_w_out[0].astype(BF16)
    lams = [row2(b_lambda_q1[0]), row2(b_lambda_k1[0]), row2(b_lambda_q2[0]), row2(b_lambda_k2[0])]
    subln = row2(b_subln[0])

    pos_s = past + jnp.arange(t_new)
    r = jnp.arange(tq)
    assert A_TILE % CHUNK == 0 and a_pad % A_TILE == 0
    a_idx_p, a_lay_p = _unique_blocks(
        _a_index(a_pad + np.arange(A_TILE), np.arange(a_pad + A_TILE), np.ones((a_pad + A_TILE,), bool)),
        CHUNK, LANES)
    ka = n_cache_a + t_new
    ka_pad = -(-ka // LANES) * LANES
    pos_s_np = past + np.arange(t_new)
    a_kpos = np.concatenate([past - n_cache_a + np.arange(n_cache_a), pos_s_np,
                             np.zeros((ka_pad - ka,), np.int64)])
    a_real = np.arange(ka_pad) < ka
    a_idx_s, a_lay_s = _unique_blocks(_a_index(pos_s_np, a_kpos, a_real & (a_kpos >= 0)), t_new, LANES)
    b_idx_p = jnp.stack([_b_index(tq + r, r, jnp.ones((tq,), bool)),
                         _b_index(r, r, jnp.ones((tq,), bool))])
    kb = past + t_new
    kb_pad = -(-kb // LANES) * LANES
    b_kpos = jnp.arange(kb_pad)
    b_real = b_kpos < kb
    b_idx_s = _b_index(pos_s, b_kpos, b_real)
    c_mask_s = jnp.where(b_real[None, :] & ((b_kpos[None, :] // CHUNK) <= (pos_s[:, None] // CHUNK)),
                         0.0, NEG).astype(F32)

    a_bias_p = _bias_expand(a_idx_p, a_rel_bias[0], None, a_lay_p)
    a_bias_s = _bias_expand(a_idx_s, a_rel_bias[0], None, a_lay_s)
    t5_t = t5_bias.T
    b_bias_p = jnp.stack([_bias_expand(b_idx_p[0], t5_t, far_bucket),
                          _bias_expand(b_idx_p[1], t5_t, far_bucket)])
    b_bias_s = _bias_expand(b_idx_s, t5_t, far_bucket)

    xp = x_prompt.reshape(batch * seq, d)
    xs = x_sample.reshape(dec_batch * t_new, d)
    g1, gm, g2 = row2(ffn1_norm[0]), row2(mix_norm[0]), row2(ffn2_norm[0])

    xp, wgu, wd = _ffn(xp, [], g1, ffn1_w_gu, ffn1_w_down, layer=0)
    xs = _ffn(xs, [], g1, wgu, wd)

    aq, ak, av, bq, bk, bv, akf, avf, bkf, bvf = _eproj(xp, gm, e_in, seq, a_keep)
    to3 = lambda t, n: t.reshape(n, -1, t.shape[-1])
    a_out = _a_prompt(to3(aq, batch), to3(ak, batch), to3(av, batch), a_bias_p)
    b_out = _b_prompt(to3(bq, batch), to3(bk, batch), to3(bv, batch), b_bias_p, lams, subln, lam_init)
    p_a_k = akf.reshape(1, batch, a_keep, A_HEADS, A_HEAD_DIM)
    p_a_v = avf.reshape(1, batch, a_keep, A_HEADS, A_HEAD_DIM)
    p_b_k = bkf.reshape(1, batch, seq, B_HEADS, 2 * B_HEAD_DIM)
    p_b_v = bvf.reshape(1, batch, seq, B_HEADS, 2 * B_HEAD_DIM)
    xp, wgu, wd = _ffn(xp, [(a_out.reshape(-1, a_w), e_out_a), (b_out.reshape(-1, b_w), e_out_b)],
                       g2, ffn2_w_gu, ffn2_w_down, layer=0)

    aq, ak, av, bq, bk, bv, akf, avf, bkf, bvf = _eproj(xs, gm, e_in, dec_batch * t_new, dec_batch * t_new)
    a_out = _a_sample(to3(aq, dec_batch), cache_a_k[0], cache_a_v[0], to3(ak, dec_batch), to3(av, dec_batch),
                      a_bias_s)
    b_out = _b_sample(to3(bq, dec_batch), cache_b_k[0], cache_b_v[0], to3(bk, dec_batch), to3(bv, dec_batch),
                      b_bias_s, lams, subln, lam_init)
    s_a_k = akf.reshape(1, dec_batch, t_new, A_HEADS, A_HEAD_DIM)
    s_a_v = avf.reshape(1, dec_batch, t_new, A_HEADS, A_HEAD_DIM)
    s_b_k = bkf.reshape(1, dec_batch, t_new, B_HEADS, 2 * B_HEAD_DIM)
    s_b_v = bvf.reshape(1, dec_batch, t_new, B_HEADS, 2 * B_HEAD_DIM)
    xs = _ffn(xs, [(a_out.reshape(-1, a_w), e_out_a), (b_out.reshape(-1, b_w), e_out_b)], g2, wgu, wd)

    g1, gm, g2 = row2(ffn1_norm[1]), row2(mix_norm[1]), row2(ffn2_norm[1])
    gq, gkv, gfin = row2(c_q_norm[0]), row2(c_kv_norm[0]), row2(final_norm)
    xp, wgu, wd = _ffn(xp, [], g1, ffn1_w_gu, ffn1_w_down, layer=1)
    xs = _ffn(xs, [], g1, wgu, wd)

    q, ckvf, ckvb, kr, krf = _cproj(xp, gm, c_in, gq, gkv, c_q, _rope_tables(jnp.arange(seq), c_scale))
    c_out = _c_prompt(to3(q, batch), to3(ckvb, batch), to3(kr, batch), c_k, c_v)
    p_c_kv = ckvf.reshape(1, batch, seq, C_KV_LORA)
    p_c_kr = krf.reshape(1, batch, seq, C_ROPE)
    y_prompt, wgu, wd = _ffn(xp, [(c_out.reshape(-1, C_HEADS * C_V), c_out_w)], g2, ffn2_w_gu, ffn2_w_down,
                             gfin, layer=1)

    q, ckvf, ckvb, kr, krf = _cproj(xs, gm, c_in, gq, gkv, c_q,
                                    _rope_tables(jnp.tile(pos_s, dec_batch), c_scale))
    kr_cache = jnp.pad(cache_c_kr[0], ((0, 0), (0, 0), (C_NOPE, LANES - C_NOPE - C_ROPE)))
    c_out = _c_sample(to3(q, dec_batch), cache_c_kv[0], kr_cache, to3(ckvb, dec_batch), to3(kr, dec_batch),
                      c_k_abs, c_v_abs, jnp.tile(c_mask_s, (C_HEADS, 1)))
    s_c_kv = ckvf.reshape(1, dec_batch, t_new, C_KV_LORA)
    s_c_kr = krf.reshape(1, dec_batch, t_new, C_ROPE)
    y_sample = _ffn(xs, [(c_out.reshape(-1, C_HEADS * C_V), c_out_w)], g2, wgu, wd, gfin)

    return (y_prompt.reshape(batch, seq, d), y_sample.reshape(dec_batch, t_new, d),
            p_a_k, p_a_v, p_b_k, p_b_v, p_c_kv, p_c_kr,
            s_a_k, s_a_v, s_b_k, s_b_v, s_c_kv, s_c_kr)
```

```python
import functools
import math

import jax
import jax.numpy as jnp
import numpy as np
from jax import lax
from jax.experimental import pallas as pl
from jax.experimental.pallas import tpu as pltpu

F32 = jnp.float32
BF16 = jnp.bfloat16

EPS = 1e-6
NEG = -1e30
LOG2E = math.log2(math.e)
CHUNK = 64
A_HEADS = 8
A_HEAD_DIM = 64
A_PAST_CHUNKS = 8
A_REL_CLIP = 64
B_HEADS = 4
B_HEAD_DIM = 64
T5_BUCKETS = 32
T5_MAX_DIST = 128
C_HEADS = 16
C_NOPE = 64
C_ROPE = 32
C_V = 64
C_Q_LORA = 384
C_KV_LORA = 256
ROPE_THETA = 10000.0

LANES = 128
VMEM_LIMIT = 56 * 1024 * 1024
ROW_TILE = 512
ATT_TILE = 256
A_TILE = 256
TOKEN_SUBTILES = 2
MXU_WIDTH = 256
FFN_CHUNKS = 2


def _ffn_chunks(d_ff):
    tiles = d_ff // MXU_WIDTH
    assert tiles * MXU_WIDTH == d_ff
    per = -(-tiles // FFN_CHUNKS)
    edges = [min(c * per, tiles) * MXU_WIDTH for c in range(FFN_CHUNKS + 1)]
    return [(lo, hi) for lo, hi in zip(edges[:-1], edges[1:]) if hi > lo]


def _rmsnorm(x, g):
    return x * lax.rsqrt(jnp.mean(x * x, axis=-1, keepdims=True) + EPS) * g


def _dot(a, b):
    return jnp.dot(a, b, preferred_element_type=F32)


def _dot_nt(a, b):
    return lax.dot_general(a, b, (((1,), (1,)), ((), ())), preferred_element_type=F32)


def _const_spec(a):
    nd = a.ndim
    return pl.BlockSpec(a.shape, lambda *_: (0,) * nd, pipeline_mode=pl.Buffered(1))


def _params(sem, flags=None):
    return pltpu.CompilerParams(dimension_semantics=sem, vmem_limit_bytes=VMEM_LIMIT, flags=flags)


def _weight_chunk_copy(src, stage, sems, c, slot):
    rows = stage.shape[1]
    return pltpu.make_async_copy(src.at[pl.ds(c * rows, rows), :], stage.at[slot], sems.at[slot])


def _load_weight_as_bf16(src, stage, sems, dst):
    rows = stage.shape[1]
    n = src.shape[0] // rows
    _weight_chunk_copy(src, stage, sems, 0, 0).start()
    for c in range(n):
        slot = c % 2
        if c + 1 < n:
            _weight_chunk_copy(src, stage, sems, c + 1, 1 - slot).start()
        _weight_chunk_copy(src, stage, sems, c, slot).wait()
        dst[c * rows:(c + 1) * rows, :] = stage[slot].astype(BF16)


def _ffn_kernel(n_pre, with_final, layer, *refs):
    x_ref = refs[0]
    pres = [(refs[1 + 2 * i], refs[2 + 2 * i]) for i in range(n_pre)]
    idx = 1 + 2 * n_pre
    g_ref, wgu_in, wd_in = refs[idx:idx + 3]
    idx += 3
    gf_ref = refs[idx] if with_final else None
    idx += with_final
    out_ref = refs[idx]
    if layer is None:
        wgu_ref, wd_ref = wgu_in, wd_in
    else:
        wgu_out, wd_out, wgu_ref, wd_ref, gu_stage, d_stage, in_sems, out_sems = refs[idx + 1:]
        copies_out = [pltpu.make_async_copy(wgu_ref, wgu_out, out_sems.at[0]),
                      pltpu.make_async_copy(wd_ref, wd_out, out_sems.at[1])]

        @pl.when(pl.program_id(0) == 0)
        def _():
            _load_weight_as_bf16(wgu_in.at[layer], gu_stage, in_sems, wgu_ref)
            _load_weight_as_bf16(wd_in.at[layer], d_stage, in_sems, wd_ref)
            for cp in copies_out:
                cp.start()

    x = x_ref[...]
    for o_ref, w_ref in pres:
        x = x + _dot(o_ref[...], w_ref[...])
    xn = _rmsnorm(x, g_ref[...]).astype(BF16)
    d_ff = wd_ref.shape[0]
    acc = jnp.zeros_like(x)
    for lo, hi in _ffn_chunks(d_ff):
        gate = _dot(xn, wgu_ref[:, lo:hi])
        up = _dot(xn, wgu_ref[:, d_ff + lo:d_ff + hi])
        act = (jax.nn.silu(gate) * up).astype(BF16)
        acc = acc + _dot(act, wd_ref[lo:hi, :])
    y = x + 0.5 * acc
    if with_final:
        y = _rmsnorm(y, gf_ref[...])
    out_ref[...] = y

    if layer is not None:
        @pl.when(pl.program_id(0) == pl.num_programs(0) - 1)
        def _():
            for cp in copies_out:
                cp.wait()


def _ffn(x, pres, g, wgu, wd, g_final=None, layer=None):
    t, d = x.shape
    tm = min(ROW_TILE, t)
    row = lambda w: pl.BlockSpec((tm, w), lambda i: (i, 0))
    hbm = pl.BlockSpec(memory_space=pl.ANY)
    args, specs = [x], [row(d)]
    for o, w in pres:
        args += [o, w]
        specs += [row(o.shape[1]), _const_spec(w)]
    args += [g, wgu, wd]
    specs += [_const_spec(g)] + ([_const_spec(wgu), _const_spec(wd)] if layer is None else [hbm, hbm])
    if g_final is not None:
        args.append(g_final)
        specs.append(_const_spec(g_final))
    out_specs, out_shape, scratch = row(d), jax.ShapeDtypeStruct((t, d), F32), []
    if layer is not None:
        gu_shape, d_shape = wgu.shape[1:], wd.shape[1:]
        chunks = 8
        out_specs = [out_specs, hbm, hbm]
        out_shape = [out_shape, jax.ShapeDtypeStruct(gu_shape, BF16), jax.ShapeDtypeStruct(d_shape, BF16)]
        scratch = [pltpu.VMEM(gu_shape, BF16), pltpu.VMEM(d_shape, BF16),
                   pltpu.VMEM((2, gu_shape[0] // chunks, gu_shape[1]), F32),
                   pltpu.VMEM((2, d_shape[0] // chunks, d_shape[1]), F32),
                   pltpu.SemaphoreType.DMA((2,)), pltpu.SemaphoreType.DMA((2,))]
    return pl.pallas_call(
        functools.partial(_ffn_kernel, len(pres), g_final is not None, layer),
        grid=(t // tm,),
        in_specs=specs,
        out_specs=out_specs,
        out_shape=out_shape,
        scratch_shapes=scratch,
        compiler_params=_params(("arbitrary",)),
        name="ffn",
    )(*args)


def _store_values_with_ones(out_ref, v):
    low = lax.broadcasted_iota(jnp.int32, (1, LANES), 1) < LANES // 2
    for p in range(v.shape[1] // LANES):
        vp = v[:, p * LANES:(p + 1) * LANES]
        out_ref[:, 2 * p * LANES:(2 * p + 1) * LANES] = jnp.where(low, vp, 1.0).astype(out_ref.dtype)
        out_ref[:, (2 * p + 1) * LANES:(2 * p + 2) * LANES] = jnp.where(low, 1.0, vp).astype(out_ref.dtype)


def _merge_head_pair(acc_even, acc_odd):
    half = LANES // 2
    low = lax.broadcasted_iota(jnp.int32, (1, LANES), 1) < half
    den = pltpu.roll(jnp.where(low, acc_odd, acc_even), half, 1)
    return jnp.where(low, acc_even, acc_odd) / den


def _eproj_kernel(x_ref, g_ref, w_ref, aq, ak, av, bq, bk, bv, akf, avf, bkf, bvf):
    hn = _rmsnorm(x_ref[...], g_ref[...]).astype(BF16)
    t = _dot(hn, w_ref[...])
    w = aq.shape[1]
    parts = [t[:, i * w:(i + 1) * w] for i in range(6)]
    aq[...] = (parts[0] * (A_HEAD_DIM ** -0.5 * LOG2E)).astype(BF16)
    ak[...] = parts[1].astype(BF16)
    _store_values_with_ones(av, parts[2])
    bq[...] = (parts[3] * (B_HEAD_DIM ** -0.5 * LOG2E)).astype(BF16)
    bk[...] = parts[4].astype(BF16)
    ones = jnp.ones((parts[5].shape[0], LANES), BF16)
    for h in range(B_HEADS):
        bv[:, 2 * h * LANES:(2 * h + 1) * LANES] = parts[5][:, h * LANES:(h + 1) * LANES].astype(BF16)
        bv[:, (2 * h + 1) * LANES:(2 * h + 2) * LANES] = ones
    akf[...] = parts[1].reshape(akf.shape)
    avf[...] = parts[2].reshape(avf.shape)
    bkf[...] = parts[4].reshape(bkf.shape)
    bvf[...] = parts[5].reshape(bvf.shape)


def _eproj(x, g, w, rows_per_batch, a_keep):
    t, d = x.shape
    tm = min(ROW_TILE, t, a_keep)
    assert rows_per_batch % tm == 0 and a_keep % tm == 0
    per, kept = rows_per_batch // tm, a_keep // tm
    wd = w.shape[1] // 6
    row = lambda n: pl.BlockSpec((tm, n), lambda i: (i, 0))
    a_map = lambda i: ((i // per) * kept + jnp.maximum(i % per - (per - kept), 0), 0, 0)
    a_f32 = pl.BlockSpec((tm, A_HEADS, A_HEAD_DIM), a_map)
    b_f32 = pl.BlockSpec((tm, B_HEADS, 2 * B_HEAD_DIM), lambda i: (i, 0, 0))
    widths = [wd, wd, 2 * wd, wd, wd, 2 * wd]
    shapes = ([jax.ShapeDtypeStruct((t, n), BF16) for n in widths]
              + [jax.ShapeDtypeStruct((t // per * kept, A_HEADS, A_HEAD_DIM), F32)] * 2
              + [jax.ShapeDtypeStruct((t, B_HEADS, 2 * B_HEAD_DIM), F32)] * 2)
    return pl.pallas_call(
        _eproj_kernel,
        grid=(t // tm,),
        in_specs=[row(d), _const_spec(g), _const_spec(w)],
        out_specs=[row(n) for n in widths] + [a_f32, a_f32, b_f32, b_f32],
        out_shape=shapes,
        compiler_params=_params(("arbitrary",)),
        name="eproj",
    )(x, g, w)


def _cproj_kernel(x_ref, g_ref, win_ref, gq_ref, gkv_ref, wq_ref, cq_ref, sq_ref, ck_ref, sk_ref,
                  q_out, ckvf_out, ckvb_out, kr_out, krf_out):
    tm = x_ref.shape[0]
    sub = tm // TOKEN_SUBTILES if tm % (TOKEN_SUBTILES * 16) == 0 else tm
    for r0 in range(0, tm, sub):
        rows = slice(r0, r0 + sub)
        hn = _rmsnorm(x_ref[rows, :], g_ref[...]).astype(BF16)
        t = _dot(hn, win_ref[...])
        cqn = _rmsnorm(t[:, :C_Q_LORA], gq_ref[...]).astype(BF16)
        ckv = _rmsnorm(t[:, C_Q_LORA:C_Q_LORA + C_KV_LORA], gkv_ref[...])
        ckvf_out[rows, :] = ckv
        ckvb_out[rows, :] = ckv.astype(BF16)
        tg = t[:, C_Q_LORA + C_KV_LORA:]
        kr = tg * ck_ref[rows, :] + pltpu.roll(tg, LANES - C_ROPE, 1) * sk_ref[rows, :]
        kr_out[rows, :] = kr
        krf_out[rows, :] = kr[:, C_NOPE:C_NOPE + C_ROPE]
        q = _dot(cqn, wq_ref[...])
        cq, sq = cq_ref[rows, :], sq_ref[rows, :]
        for h in range(C_HEADS):
            qh = q[:, h * LANES:(h + 1) * LANES]
            qf = qh * cq + pltpu.roll(qh, LANES - C_ROPE, 1) * sq
            q_out[rows, h * LANES:(h + 1) * LANES] = qf.astype(BF16)


def _cproj(x, g, win, gq, gkv, wq, tabs):
    t, d = x.shape
    tm = min(ROW_TILE, t)
    nper = tabs[0].shape[0] // tm
    row = lambda n: pl.BlockSpec((tm, n), lambda i: (i, 0))
    tab = pl.BlockSpec((tm, LANES), lambda i: (i % nper, 0))
    shapes = [jax.ShapeDtypeStruct((t, C_HEADS * LANES), BF16),
              jax.ShapeDtypeStruct((t, C_KV_LORA), F32),
              jax.ShapeDtypeStruct((t, C_KV_LORA), BF16),
              jax.ShapeDtypeStruct((t, LANES), F32),
              jax.ShapeDtypeStruct((t, C_ROPE), F32)]
    return pl.pallas_call(
        _cproj_kernel,
        grid=(t // tm,),
        in_specs=[row(d), _const_spec(g), _const_spec(win), _const_spec(gq), _const_spec(gkv),
                  _const_spec(wq), tab, tab, tab, tab],
        out_specs=[row(C_HEADS * LANES), row(C_KV_LORA), row(C_KV_LORA), row(LANES), row(C_ROPE)],
        out_shape=shapes,
        compiler_params=_params(("arbitrary",)),
        name="cproj",
    )(x, g, win, gq, gkv, wq, *tabs)


def _bias_kernel(n_rows, shift, layout, idx_ref, tab_ref, out_ref):
    idx = idx_ref[...]
    n_heads = out_ref.shape[0]
    bq = out_ref.shape[1] // len(layout)
    bk = out_ref.shape[2] // len(layout[0])
    for h in range(n_heads):
        base = tab_ref[h, shift] if shift is not None else 0.0

        def body(r, acc):
            return jnp.where(idx == r, (tab_ref[h, r] - base) * LOG2E, acc)

        vals = lax.fori_loop(0, n_rows, body, jnp.full(idx.shape, NEG, F32))
        for a, row in enumerate(layout):
            for t, u in enumerate(row):
                out_ref[h, a * bq:(a + 1) * bq, t * bk:(t + 1) * bk] = vals[u * bq:(u + 1) * bq, :]


def _bias_expand(idx, table, shift=None, layout=((0,),)):
    n_heads, n_rows = table.shape
    n_blocks = 1 + max(max(row) for row in layout)
    out_shape = (n_heads, idx.shape[0] // n_blocks * len(layout), idx.shape[1] * len(layout[0]))
    return pl.pallas_call(
        functools.partial(_bias_kernel, n_rows, shift, layout),
        in_specs=[pl.BlockSpec(idx.shape, lambda: (0, 0)),
                  pl.BlockSpec(memory_space=pltpu.SMEM)],
        out_specs=pl.BlockSpec(out_shape, lambda: (0, 0, 0)),
        out_shape=jax.ShapeDtypeStruct(out_shape, F32),
        compiler_params=pltpu.CompilerParams(vmem_limit_bytes=VMEM_LIMIT),
        name="bias_expand",
    )(idx, table)


def _unique_blocks(idx, bq, bk):
    q, k = idx.shape
    blocks = idx.reshape(q // bq, bq, k // bk, bk).transpose(0, 2, 1, 3).reshape(-1, bq, bk)
    uniq, inv = np.unique(blocks, axis=0, return_inverse=True)
    layout = tuple(tuple(int(u) for u in row) for row in inv.reshape(q // bq, k // bk))
    return jnp.asarray(uniq.reshape(-1, bk), jnp.int32), layout


def _a_attend(q, kwin, vwin, bias_ref, valid, out_ref, s_ref):
    low = lax.broadcasted_iota(jnp.int32, (1, LANES), 1) < A_HEAD_DIM
    for h in range(A_HEADS):
        sl = slice(h // 2 * LANES, (h // 2 + 1) * LANES)
        qp = q[:, sl]
        qm = jnp.where(low if h % 2 == 0 else jnp.logical_not(low), qp, jnp.zeros_like(qp))
        s = _dot_nt(qm, kwin[:, sl]) + bias_ref[h]
        if valid is not None:
            s = jnp.where(valid, s, NEG)
        s_ref[h] = s - jnp.max(s, axis=-1, keepdims=True)
    accs = []
    for h in range(A_HEADS):
        accs.append(_dot(jnp.exp2(s_ref[h]).astype(BF16), vwin[:, h * LANES:(h + 1) * LANES]))
        if h % 2 == 1:
            out_ref[0, :, h // 2 * LANES:(h // 2 + 1) * LANES] = _merge_head_pair(*accs).astype(BF16)
            accs = []


def _a_prompt_kernel(q_ref, k_ref, v_ref, bias_ref, out_ref, s_ref):
    tq = q_ref.shape[1]
    win = bias_ref.shape[2]
    n_tiles = win // tq
    j = pl.program_id(1)

    def run(check_positions):
        ks, vs = [], []
        for t in range(n_tiles):
            start = pl.multiple_of(jnp.maximum(j - (n_tiles - 1 - t), 0) * tq, tq)
            ks.append(k_ref[0, pl.ds(start, tq), :])
            vs.append(v_ref[0, pl.ds(start, tq), :])
        valid = None
        if check_positions:
            valid = lax.broadcasted_iota(jnp.int32, (1, win), 1) + (j - (n_tiles - 1)) * tq >= 0
        _a_attend(q_ref[0], jnp.concatenate(ks, axis=0), jnp.concatenate(vs, axis=0), bias_ref, valid,
                  out_ref, s_ref)

    @pl.when(j < n_tiles - 1)
    def _():
        run(True)

    @pl.when(j >= n_tiles - 1)
    def _():
        run(False)


def _a_prompt(q, k, v, bias):
    b, s, w = q.shape
    tq = bias.shape[1]
    kv = lambda a: pl.BlockSpec((1, s, a.shape[2]), lambda i, j: (i, 0, 0))
    return pl.pallas_call(
        _a_prompt_kernel,
        grid=(b, s // tq),
        in_specs=[pl.BlockSpec((1, tq, w), lambda i, j: (i, j, 0)), kv(k), kv(v), _const_spec(bias)],
        out_specs=pl.BlockSpec((1, tq, w), lambda i, j: (i, j, 0)),
        out_shape=jax.ShapeDtypeStruct((b, s, w), BF16),
        scratch_shapes=[pltpu.VMEM(bias.shape, F32)],
        compiler_params=_params(("arbitrary", "arbitrary")),
        name="a_prompt",
    )(q, k, v, bias)


def _a_sample_kernel(q_ref, kct_ref, vct_ref, kn_ref, vn_ref, bias_ref, out_ref):
    q = q_ref[0]
    tq, n_c = q.shape[0], kct_ref.shape[2]
    low = lax.broadcasted_iota(jnp.int32, (1, LANES), 1) < A_HEAD_DIM
    for p in range(A_HEADS // 2):
        sl = slice(p * LANES, (p + 1) * LANES)
        qp = q[:, sl]
        kct, vct = kct_ref[0, sl, :].astype(BF16), vct_ref[0, sl, :].astype(BF16)
        outs = []
        for sub in range(2):
            h = 2 * p + sub
            qm = jnp.where(low if sub == 0 else jnp.logical_not(low), qp, jnp.zeros_like(qp))
            s_c = _dot(qm, kct) + bias_ref[h, :, :n_c]
            s_n = _dot_nt(qm, kn_ref[0, :, sl]) + bias_ref[h, :, n_c:n_c + tq]
            m = jnp.maximum(jnp.max(s_c, axis=-1, keepdims=True), jnp.max(s_n, axis=-1, keepdims=True))
            e_c, e_n = jnp.exp2(s_c - m), jnp.exp2(s_n - m)
            l = jnp.sum(e_c, axis=-1, keepdims=True) + jnp.sum(e_n, axis=-1, keepdims=True)
            acc = _dot_nt(e_c.astype(BF16), vct) + _dot(e_n.astype(BF16), vn_ref[0, :, h * LANES:(h + 1) * LANES])
            outs.append(acc / l)
        out_ref[0, :, sl] = jnp.where(low, outs[0], outs[1]).astype(BF16)


def _a_sample(q, k_cache, v_cache, k_new, v_new, bias):
    b, tq, w = q.shape
    blk = lambda a: pl.BlockSpec((1,) + a.shape[1:], lambda i: (i,) + (0,) * (a.ndim - 1))
    return pl.pallas_call(
        _a_sample_kernel,
        grid=(b,),
        in_specs=[blk(q), blk(k_cache), blk(v_cache), blk(k_new), blk(v_new), _const_spec(bias)],
        out_specs=blk(q),
        out_shape=jax.ShapeDtypeStruct((b, tq, w), BF16),
        compiler_params=_params(("arbitrary",)),
        name="a_sample",
    )(q, k_cache, v_cache, k_new, v_new, bias)


def _b_lambda(lq1, lk1, lq2, lk2, lam_init):
    s1 = jnp.sum(lq1[...] * lk1[...], axis=-1, keepdims=True)
    s2 = jnp.sum(lq2[...] * lk2[...], axis=-1, keepdims=True)
    return jnp.exp(s1) - jnp.exp(s2) + lam_init


def _b_stack_queries(q):
    lane = lax.broadcasted_iota(jnp.int32, (1, LANES), 1)
    low = lane < B_HEAD_DIM
    qs = []
    for h in range(B_HEADS):
        qh = q[:, h * LANES:(h + 1) * LANES]
        zero = jnp.zeros_like(qh)
        qs.append(jnp.concatenate([jnp.where(low, qh, zero), jnp.where(low, zero, qh)], axis=0))
    return qs


def _b_finish(o, lam, g, lam_init, tq):
    ob = o[:tq] - lam * o[tq:]
    return _rmsnorm(ob, g) * (1.0 - lam_init)


def _b_prompt_kernel(lam_init, q_ref, k_ref, v_ref, bias_ref, lq1, lk1, lq2, lk2, g_ref, out_ref,
                     qs_ref, m_ref, alpha_ref, acc_ref, s_ref):
    tq = q_ref.shape[1]
    tk = bias_ref.shape[3]
    i = pl.program_id(1)
    for h, qh in enumerate(_b_stack_queries(q_ref[0])):
        qs_ref[h] = qh

    lane_tiles = range(tk // LANES)
    lam = _b_lambda(lq1, lk1, lq2, lk2, lam_init)

    def scores(j, bias_sel, first):
        start = pl.multiple_of(j * tk, tk)
        for h in range(B_HEADS):
            sl = slice(h * LANES, (h + 1) * LANES)
            s = _dot_nt(qs_ref[h], k_ref[0, pl.ds(start, tk), sl])
            parts = [s[:, c * LANES:(c + 1) * LANES] for c in lane_tiles]
            if bias_sel is not None:
                parts = [jnp.concatenate([pt[:tq] + bias_ref[bias_sel, h, :, c * LANES:(c + 1) * LANES],
                                          pt[tq:] + bias_ref[bias_sel, h, :, c * LANES:(c + 1) * LANES]], axis=0)
                         for c, pt in enumerate(parts)]
            mx = parts[0]
            for pt in parts[1:]:
                mx = jnp.maximum(mx, pt)
            row_max = jnp.max(mx, axis=-1, keepdims=True)
            if first:
                m_new = jnp.broadcast_to(row_max, (2 * tq, LANES))
                alpha_ref[h] = jnp.zeros((2 * tq, LANES), F32)
            else:
                m_new = jnp.maximum(m_ref[h], row_max)
                alpha_ref[h] = jnp.exp2(m_ref[h] - m_new)
            m_ref[h] = m_new
            for c, pt in enumerate(parts):
                s_ref[h, :, c * LANES:(c + 1) * LANES] = pt

    def attend(j, last):
        start = pl.multiple_of(j * tk, tk)
        for h in range(B_HEADS):
            m_new = m_ref[h]
            p = jnp.concatenate([jnp.exp2(s_ref[h, :, c * LANES:(c + 1) * LANES] - m_new) for c in lane_tiles],
                                axis=1).astype(BF16)
            alpha = alpha_ref[h]
            pv = _dot(p, v_ref[0, pl.ds(start, tk), 2 * h * LANES:2 * (h + 1) * LANES])
            num = alpha * acc_ref[h, :, :LANES] + pv[:, :LANES]
            den = alpha * acc_ref[h, :, LANES:] + pv[:, LANES:]
            if last:
                bn = _b_finish(num / den, lam, g_ref[...], lam_init, tq)
                out_ref[0, :, h * LANES:(h + 1) * LANES] = bn.astype(BF16)
            else:
                acc_ref[h, :, :LANES] = num
                acc_ref[h, :, LANES:] = den

    acc_ref[...] = jnp.zeros(acc_ref.shape, F32)
    scores(i, 1, True)

    @pl.when(i >= 1)
    def _():
        attend(i, False)
        scores(i - 1, 0, False)

    def trip(t, carry):
        attend(i - t, False)
        scores(i - 1 - t, None, False)
        return carry

    lax.fori_loop(1, i, trip, 0)
    attend(0, True)


def _b_prompt(q, k, v, bias, lams, g, lam_init):
    b, s, w = q.shape
    tq = bias.shape[2]
    kv = lambda a: pl.BlockSpec((1, s, a.shape[2]), lambda i, j: (i, 0, 0))
    scratch = pltpu.VMEM((B_HEADS, 2 * tq, LANES), F32)
    return pl.pallas_call(
        functools.partial(_b_prompt_kernel, lam_init),
        grid=(b, s // tq),
        in_specs=[pl.BlockSpec((1, tq, w), lambda i, j: (i, j, 0)), kv(k), kv(v), _const_spec(bias)]
                 + [_const_spec(x) for x in lams] + [_const_spec(g)],
        out_specs=pl.BlockSpec((1, tq, w), lambda i, j: (i, j, 0)),
        out_shape=jax.ShapeDtypeStruct((b, s, w), BF16),
        scratch_shapes=[pltpu.VMEM((B_HEADS, 2 * tq, LANES), BF16), scratch, scratch,
                        pltpu.VMEM((B_HEADS, 2 * tq, 2 * LANES), F32), pltpu.VMEM((B_HEADS, 2 * tq, tq), F32)],
        compiler_params=_params(("arbitrary", "arbitrary")),
        name="b_prompt",
    )(q, k, v, bias, *lams, g)


def _b_sample_kernel(lam_init, q_ref, kc_ref, vc_ref, kn_ref, vn_ref, bias_ref, lq1, lk1, lq2, lk2, g_ref,
                     out_ref):
    tq, n_c = q_ref.shape[1], kc_ref.shape[1] // B_HEADS
    qs = _b_stack_queries(q_ref[0])
    lam = _b_lambda(lq1, lk1, lq2, lk2, lam_init)
    both = lambda bias: jnp.concatenate([bias, bias], axis=0)
    for h in range(B_HEADS):
        sl = slice(h * LANES, (h + 1) * LANES)
        head_rows = pl.ds(h, n_c, stride=B_HEADS)
        kc, vc = kc_ref[0, head_rows, :].astype(BF16), vc_ref[0, head_rows, :].astype(BF16)
        s_c = _dot_nt(qs[h], kc) + both(bias_ref[h, :, :n_c])
        s_n = _dot_nt(qs[h], kn_ref[0, :, sl]) + both(bias_ref[h, :, n_c:n_c + tq])
        m = jnp.maximum(jnp.max(s_c, axis=-1, keepdims=True), jnp.max(s_n, axis=-1, keepdims=True))
        e_c, e_n = jnp.exp2(s_c - m), jnp.exp2(s_n - m)
        l = jnp.sum(e_c, axis=-1, keepdims=True) + jnp.sum(e_n, axis=-1, keepdims=True)
        o = (_dot(e_c.astype(BF16), vc) + _dot(e_n.astype(BF16), vn_ref[0, :, 2 * h * LANES:(2 * h + 1) * LANES])) / l
        bn = _b_finish(o, lam, g_ref[...], lam_init, tq)
        out_ref[0, :, sl] = bn.astype(BF16)


def _b_sample(q, k_cache, v_cache, k_new, v_new, bias, lams, g, lam_init):
    b, tq, w = q.shape
    blk = lambda a: pl.BlockSpec((1,) + a.shape[1:], lambda i: (i,) + (0,) * (a.ndim - 1))
    return pl.pallas_call(
        functools.partial(_b_sample_kernel, lam_init),
        grid=(b,),
        in_specs=[blk(q), blk(k_cache), blk(v_cache), blk(k_new), blk(v_new), _const_spec(bias)]
                 + [_const_spec(x) for x in lams] + [_const_spec(g)],
        out_specs=blk(q),
        out_shape=jax.ShapeDtypeStruct((b, tq, w), BF16),
        compiler_params=_params(("arbitrary",)),
        name="b_sample",
    )(q, k_cache, v_cache, k_new, v_new, bias, *lams, g)


def _c_prompt_kernel(q_ref, ckv_ref, kr_ref, wk_ref, wv_ref, out_ref, k_ref, v_ref, m_ref, alpha_ref, acc_ref,
                     mask_ref, s_ref):
    tq = q_ref.shape[1]
    tk = tq
    n_lane_tiles = tk // LANES
    i = pl.program_id(1)

    @pl.when((pl.program_id(0) == 0) & (i == 0))
    def _():
        shift = CHUNK.bit_length() - 1
        rc = jnp.right_shift(lax.broadcasted_iota(jnp.int32, (tq, tk), 0), shift)
        kc = jnp.right_shift(lax.broadcasted_iota(jnp.int32, (tq, tk), 1), shift)
        mask_ref[...] = jnp.where(kc <= rc, 0.0, NEG)

    @pl.when(i == 0)
    def _():
        high = (lax.broadcasted_iota(jnp.int32, (1, LANES), 1) >= C_V).astype(F32)
        rows_per_chunk = 2 * tk

        def expand(c, carry):
            rows = pl.ds(pl.multiple_of(c * rows_per_chunk, rows_per_chunk), rows_per_chunk)
            ckv = ckv_ref[0, rows, :]
            kd = _dot(ckv, wk_ref[...])
            vd = _dot(ckv, wv_ref[...])
            kr = kr_ref[0, rows, :]
            for h in range(C_HEADS):
                sl = slice(h * LANES, (h + 1) * LANES)
                k_ref[rows, sl] = (kd[:, sl] + kr).astype(BF16)
                v_ref[rows, sl] = (vd[:, sl] + (high if h % 2 == 0 else 1.0 - high)).astype(BF16)
            return carry

        lax.fori_loop(0, k_ref.shape[0] // rows_per_chunk, expand, 0)

    def scores(j, first):
        start = pl.multiple_of(j * tk, tk)
        for h in range(C_HEADS):
            sl = slice(h * LANES, (h + 1) * LANES)
            s = _dot_nt(q_ref[0, :, sl], k_ref[pl.ds(start, tk), sl])
            parts = [s[:, c * LANES:(c + 1) * LANES] for c in range(n_lane_tiles)]
            if first:
                parts = [pt + mask_ref[:, c * LANES:(c + 1) * LANES] for c, pt in enumerate(parts)]
            mx = parts[0]
            for pt in parts[1:]:
                mx = jnp.maximum(mx, pt)
            row_max = jnp.max(mx, axis=-1, keepdims=True)
            if first:
                m_new = jnp.broadcast_to(row_max, (tq, LANES))
                alpha_ref[h] = jnp.zeros((tq, LANES), F32)
            else:
                m_new = jnp.maximum(m_ref[h], row_max)
                alpha_ref[h] = jnp.exp2(m_ref[h] - m_new)
            m_ref[h] = m_new
            for c, pt in enumerate(parts):
                s_ref[h, :, c * LANES:(c + 1) * LANES] = pt

    def attend(j, last):
        start = pl.multiple_of(j * tk, tk)
        accs = []
        for h in range(C_HEADS):
            sl = slice(h * LANES, (h + 1) * LANES)
            m_new = m_ref[h]
            p = jnp.concatenate([jnp.exp2(s_ref[h, :, c * LANES:(c + 1) * LANES] - m_new)
                                 for c in range(n_lane_tiles)], axis=1).astype(BF16)
            acc = alpha_ref[h] * acc_ref[h] + _dot(p, v_ref[pl.ds(start, tk), sl])
            if not last:
                acc_ref[h] = acc
            elif h % 2 == 0:
                accs = [acc]
            else:
                out_ref[0, :, h // 2 * LANES:(h // 2 + 1) * LANES] = _merge_head_pair(accs[0], acc).astype(BF16)

    acc_ref[...] = jnp.zeros(acc_ref.shape, F32)
    scores(i, True)

    def full(j, carry):
        attend(jnp.where(j == 0, i, j - 1), False)
        scores(j, False)
        return carry

    lax.fori_loop(0, i, full, 0)
    attend(jnp.maximum(i - 1, 0), True)


def _c_prompt(q, ckv, kr, wk, wv):
    b, s, w = q.shape
    tq = ATT_TILE
    scratch = pltpu.VMEM((C_HEADS, tq, LANES), F32)
    per_batch = lambda a: pl.BlockSpec((1, s, a.shape[2]), lambda i, j: (i, 0, 0))
    return pl.pallas_call(
        _c_prompt_kernel,
        grid=(b, s // tq),
        in_specs=[pl.BlockSpec((1, tq, w), lambda i, j: (i, j, 0)), per_batch(ckv), per_batch(kr),
                  _const_spec(wk), _const_spec(wv)],
        out_specs=pl.BlockSpec((1, tq, C_HEADS * C_V), lambda i, j: (i, j, 0)),
        out_shape=jax.ShapeDtypeStruct((b, s, C_HEADS * C_V), BF16),
        scratch_shapes=[pltpu.VMEM((s, w), BF16), pltpu.VMEM((s, w), BF16),
                        scratch, scratch, scratch, pltpu.VMEM((tq, tq), F32),
                        pltpu.VMEM((C_HEADS, tq, tq), F32)],
        compiler_params=_params(("arbitrary", "arbitrary")),
        name="c_prompt",
    )(q, ckv, kr, wk, wv)


def _c_sample_kernel(q_ref, ckv_c_ref, kr_c_ref, ckv_n_ref, kr_n_ref, wk_ref, wv_ref, mask_ref, out_ref):
    tq, n_c = q_ref.shape[1], ckv_c_ref.shape[1]
    q = q_ref[0]
    heads = [q[:, h * LANES:(h + 1) * LANES] for h in range(C_HEADS)]
    qr = jnp.concatenate(heads, axis=0)
    qa = jnp.concatenate([_dot(qh, wk_ref[h]) for h, qh in enumerate(heads)], axis=0).astype(BF16)
    ckv_c, ckv_n = ckv_c_ref[0].astype(BF16), ckv_n_ref[0]
    kr_c, kr_n = kr_c_ref[0].astype(BF16), kr_n_ref[0].astype(BF16)
    s_c = _dot_nt(qa, ckv_c) + _dot_nt(qr, kr_c) + mask_ref[:, :n_c]
    s_n = _dot_nt(qa, ckv_n) + _dot_nt(qr, kr_n) + mask_ref[:, n_c:n_c + tq]
    m = jnp.maximum(jnp.max(s_c, axis=-1, keepdims=True), jnp.max(s_n, axis=-1, keepdims=True))
    e_c, e_n = jnp.exp2(s_c - m), jnp.exp2(s_n - m)
    l = jnp.sum(e_c, axis=-1, keepdims=True) + jnp.sum(e_n, axis=-1, keepdims=True)
    o = ((_dot(e_c.astype(BF16), ckv_c) + _dot(e_n.astype(BF16), ckv_n)) / l).astype(BF16)
    for p in range(C_HEADS // 2):
        rows = lambda h: o[h * tq:(h + 1) * tq]
        pair = _dot(rows(2 * p), wv_ref[2 * p]) + _dot(rows(2 * p + 1), wv_ref[2 * p + 1])
        out_ref[0, :, p * LANES:(p + 1) * LANES] = pair.astype(BF16)


def _c_sample(q, ckv_cache, kr_cache, ckv_new, kr_new, wk, wv, mask):
    b, tq, _ = q.shape
    blk = lambda a: pl.BlockSpec((1,) + a.shape[1:], lambda i: (i,) + (0,) * (a.ndim - 1))
    return pl.pallas_call(
        _c_sample_kernel,
        grid=(b,),
        in_specs=[blk(q), blk(ckv_cache), blk(kr_cache), blk(ckv_new), blk(kr_new),
                  _const_spec(wk), _const_spec(wv), _const_spec(mask)],
        out_specs=pl.BlockSpec((1, tq, C_HEADS * C_V), lambda i: (i, 0, 0)),
        out_shape=jax.ShapeDtypeStruct((b, tq, C_HEADS * C_V), BF16),
        compiler_params=_params(("arbitrary",)),
        name="c_sample",
    )(q, ckv_cache, kr_cache, ckv_new, kr_new, wk, wv, mask)


def _t5_bucket(rel):
    nb = T5_BUCKETS // 2
    max_exact = nb // 2
    ret = jnp.where(rel > 0, nb, 0)
    n = jnp.abs(rel)
    n_f = jnp.maximum(n, 1).astype(F32)
    large = max_exact + (jnp.log(n_f / max_exact) / math.log(T5_MAX_DIST / max_exact)
                         * (nb - max_exact)).astype(jnp.int32)
    large = jnp.minimum(large, nb - 1)
    return ret + jnp.where(n < max_exact, n, large)


def _a_index(q_pos, k_pos, k_real):
    rel = np.clip(q_pos[:, None] - k_pos[None, :], -A_REL_CLIP, A_REL_CLIP) + A_REL_CLIP
    qc, kc = q_pos[:, None] // CHUNK, k_pos[None, :] // CHUNK
    ok = k_real[None, :] & (kc <= qc) & (kc >= qc - A_PAST_CHUNKS)
    return np.where(ok, rel, -1).astype(np.int32)


def _b_index(q_pos, k_pos, k_real):
    idx = _t5_bucket(k_pos[None, :] - q_pos[:, None])
    ok = k_real[None, :] & ((k_pos[None, :] // CHUNK) <= (q_pos[:, None] // CHUNK))
    return jnp.where(ok, idx, -1).astype(jnp.int32)


def _rope_tables(pos, scale):
    half = C_ROPE // 2
    inv = ROPE_THETA ** (-jnp.arange(half, dtype=F32) / half)
    ang = pos.astype(F32)[:, None] * inv[None, :]
    cos = jnp.concatenate([jnp.cos(ang)] * 2, axis=-1)
    sin = jnp.concatenate([jnp.sin(ang)] * 2, axis=-1)
    n = pos.shape[0]
    z_nope = jnp.zeros((n, C_NOPE), F32)
    z_tail = jnp.zeros((n, LANES - C_NOPE - C_ROPE), F32)
    cq = jnp.concatenate([jnp.full((n, C_NOPE), scale, F32), cos * scale, z_tail], axis=-1)
    sq = jnp.concatenate([z_nope, sin * scale, z_tail], axis=-1)
    ck = jnp.concatenate([z_nope, cos, z_tail], axis=-1)
    sk = jnp.concatenate([z_nope, sin, z_tail], axis=-1)
    return cq, sq, ck, sk


def _rot_cols(w):
    half = w.shape[-1] // 2
    return jnp.concatenate([-w[..., half:], w[..., :half]], axis=-1)


def kernel(x_prompt, x_sample, cache_a_k, cache_a_v, cache_b_k, cache_b_v, cache_c_kv, cache_c_kr,
           t5_bias, ffn1_norm, ffn1_w_gu, ffn1_w_down, mix_norm, ffn2_norm, ffn2_w_gu, ffn2_w_down,
           e_w_in, a_rel_bias, b_lambda_q1, b_lambda_k1, b_lambda_q2, b_lambda_k2, b_subln, e_w_out,
           c_w_in, c_q_norm, c_kv_norm, c_w_q_up, c_w_kv_up, c_w_out, final_norm):
    batch, seq, d = x_prompt.shape
    dec_batch, t_new, _ = x_sample.shape
    past = cache_b_k.shape[2]
    n_cache_a = cache_a_k.shape[2]
    a_w = A_HEADS * A_HEAD_DIM
    b_w = B_HEADS * 2 * B_HEAD_DIM
    tq = ATT_TILE
    a_pad = A_PAST_CHUNKS * CHUNK
    a_keep = min(a_pad, seq)
    assert tq + 1 >= T5_MAX_DIST and tq % CHUNK == 0 and a_pad % tq == 0
    far_bucket = T5_BUCKETS // 2 - 1
    lam_init = 0.8 - 0.6 * math.exp(-0.3 * 0)
    c_scale = (C_NOPE + C_ROPE) ** -0.5 * math.log2(math.e)
    row2 = lambda v: v.reshape(1, -1)

    e_in = e_w_in[0].astype(BF16)
    e_out_a, e_out_b = e_w_out[0, :a_w].astype(BF16), e_w_out[0, a_w:].astype(BF16)
    w_in = c_w_in[0]
    w_kr = w_in[:, C_Q_LORA + C_KV_LORA:]
    c_in = jnp.concatenate([w_in[:, :C_Q_LORA + C_KV_LORA], jnp.zeros((d, C_NOPE), F32),
                            w_kr, _rot_cols(w_kr)], axis=-1).astype(BF16)
    wq = c_w_q_up[0].reshape(C_Q_LORA, C_HEADS, C_NOPE + C_ROPE)
    wq_rope = wq[..., C_NOPE:]
    c_q = jnp.concatenate([wq[..., :C_NOPE], wq_rope, _rot_cols(wq_rope)], axis=-1)
    c_q = c_q.reshape(C_Q_LORA, C_HEADS * LANES).astype(BF16)
    wkv = c_w_kv_up[0].reshape(C_KV_LORA, C_HEADS, C_NOPE + C_V)
    c_k = jnp.concatenate([wkv[..., :C_NOPE], jnp.zeros_like(wkv[..., C_NOPE:])], axis=-1)
    c_k = c_k.reshape(C_KV_LORA, C_HEADS * LANES).astype(BF16)
    wv_pairs = wkv[..., C_NOPE:].reshape(C_KV_LORA, C_HEADS // 2, 2, C_V)
    z_v = jnp.zeros_like(wv_pairs[:, :, 0])
    c_v = jnp.stack([jnp.concatenate([wv_pairs[:, :, 0], z_v], axis=-1),
                     jnp.concatenate([z_v, wv_pairs[:, :, 1]], axis=-1)], axis=2)
    c_v = c_v.reshape(C_KV_LORA, C_HEADS * LANES).astype(BF16)
    c_k_abs = jnp.pad(jnp.transpose(wkv[..., :C_NOPE], (1, 2, 0)), ((0, 0), (0, LANES - C_NOPE), (0, 0)))
    c_k_abs = c_k_abs.astype(BF16)
    c_v_abs = jnp.transpose(c_v.reshape(C_KV_LORA, C_HEADS, LANES), (1, 0, 2))
    c_out_w = c_w_out[0].astype(BF16)
    lams = [row2(b_lambda_q1[0]), row2(b_lambda_k1[0]), row2(b_lambda_q2[0]), row2(b_lambda_k2[0])]
    subln = row2(b_subln[0])

    pos_s = past + jnp.arange(t_new)
    r = jnp.arange(tq)
    assert A_TILE % CHUNK == 0 and a_pad % A_TILE == 0
    a_idx_p, a_lay_p = _unique_blocks(
        _a_index(a_pad + np.arange(A_TILE), np.arange(a_pad + A_TILE), np.ones((a_pad + A_TILE,), bool)),
        CHUNK, LANES)
    ka = n_cache_a + t_new
    ka_pad = -(-ka // LANES) * LANES
    pos_s_np = past + np.arange(t_new)
    a_kpos = np.concatenate([past - n_cache_a + np.arange(n_cache_a), pos_s_np,
                             np.zeros((ka_pad - ka,), np.int64)])
    a_real = np.arange(ka_pad) < ka
    a_idx_s, a_lay_s = _unique_blocks(_a_index(pos_s_np, a_kpos, a_real & (a_kpos >= 0)), t_new, LANES)
    b_idx_p = jnp.stack([_b_index(tq + r, r, jnp.ones((tq,), bool)),
                         _b_index(r, r, jnp.ones((tq,), bool))])
    kb = past + t_new
    kb_pad = -(-kb // LANES) * LANES
    b_kpos = jnp.arange(kb_pad)
    b_real = b_kpos < kb
    b_idx_s = _b_index(pos_s, b_kpos, b_real)
    c_mask_s = jnp.where(b_real[None, :] & ((b_kpos[None, :] // CHUNK) <= (pos_s[:, None] // CHUNK)),
                         0.0, NEG).astype(F32)

    a_bias_p = _bias_expand(a_idx_p, a_rel_bias[0], None, a_lay_p)
    a_bias_s = _bias_expand(a_idx_s, a_rel_bias[0], None, a_lay_s)
    t5_t = t5_bias.T
    b_bias_p = jnp.stack([_bias_expand(b_idx_p[0], t5_t, far_bucket),
                          _bias_expand(b_idx_p[1], t5_t, far_bucket)])
    b_bias_s = _bias_expand(b_idx_s, t5_t, far_bucket)

    xp = x_prompt.reshape(batch * seq, d)
    xs = x_sample.reshape(dec_batch * t_new, d)
    g1, gm, g2 = row2(ffn1_norm[0]), row2(mix_norm[0]), row2(ffn2_norm[0])

    xp, wgu, wd = _ffn(xp, [], g1, ffn1_w_gu, ffn1_w_down, layer=0)
    xs = _ffn(xs, [], g1, wgu, wd)

    aq, ak, av, bq, bk, bv, akf, avf, bkf, bvf = _eproj(xp, gm, e_in, seq, a_keep)
    to3 = lambda t, n: t.reshape(n, -1, t.shape[-1])
    a_out = _a_prompt(to3(aq, batch), to3(ak, batch), to3(av, batch), a_bias_p)
    b_out = _b_prompt(to3(bq, batch), to3(bk, batch), to3(bv, batch), b_bias_p, lams, subln, lam_init)
    p_a_k = akf.reshape(1, batch, a_keep, A_HEADS, A_HEAD_DIM)
    p_a_v = avf.reshape(1, batch, a_keep, A_HEADS, A_HEAD_DIM)
    p_b_k = bkf.reshape(1, batch, seq, B_HEADS, 2 * B_HEAD_DIM)
    p_b_v = bvf.reshape(1, batch, seq, B_HEADS, 2 * B_HEAD_DIM)
    xp, wgu, wd = _ffn(xp, [(a_out.reshape(-1, a_w), e_out_a), (b_out.reshape(-1, b_w), e_out_b)],
                       g2, ffn2_w_gu, ffn2_w_down, layer=0)

    aq, ak, av, bq, bk, bv, akf, avf, bkf, bvf = _eproj(xs, gm, e_in, dec_batch * t_new, dec_batch * t_new)
    dims_major = lambda c: jnp.transpose(c, (0, 2, 3, 1)).reshape(dec_batch, a_w, n_cache_a)
    a_out = _a_sample(to3(aq, dec_batch), dims_major(cache_a_k[0]), dims_major(cache_a_v[0]), to3(ak, dec_batch),
                      to3(av, dec_batch), a_bias_s)
    pos_head_rows = lambda c: c.reshape(dec_batch, past * B_HEADS, 2 * B_HEAD_DIM)
    b_out = _b_sample(to3(bq, dec_batch), pos_head_rows(cache_b_k[0]), pos_head_rows(cache_b_v[0]),
                      to3(bk, dec_batch), to3(bv, dec_batch), b_bias_s, lams, subln, lam_init)
    s_a_k = akf.reshape(1, dec_batch, t_new, A_HEADS, A_HEAD_DIM)
    s_a_v = avf.reshape(1, dec_batch, t_new, A_HEADS, A_HEAD_DIM)
    s_b_k = bkf.reshape(1, dec_batch, t_new, B_HEADS, 2 * B_HEAD_DIM)
    s_b_v = bvf.reshape(1, dec_batch, t_new, B_HEADS, 2 * B_HEAD_DIM)
    xs = _ffn(xs, [(a_out.reshape(-1, a_w), e_out_a), (b_out.reshape(-1, b_w), e_out_b)], g2, wgu, wd)

    g1, gm, g2 = row2(ffn1_norm[1]), row2(mix_norm[1]), row2(ffn2_norm[1])
    gq, gkv, gfin = row2(c_q_norm[0]), row2(c_kv_norm[0]), row2(final_norm)
    xp, wgu, wd = _ffn(xp, [], g1, ffn1_w_gu, ffn1_w_down, layer=1)
    xs = _ffn(xs, [], g1, wgu, wd)

    q, ckvf, ckvb, kr, krf = _cproj(xp, gm, c_in, gq, gkv, c_q, _rope_tables(jnp.arange(seq), c_scale))
    c_out = _c_prompt(to3(q, batch), to3(ckvb, batch), to3(kr, batch), c_k, c_v)
    p_c_kv = ckvf.reshape(1, batch, seq, C_KV_LORA)
    p_c_kr = krf.reshape(1, batch, seq, C_ROPE)
    y_prompt, wgu, wd = _ffn(xp, [(c_out.reshape(-1, C_HEADS * C_V), c_out_w)], g2, ffn2_w_gu, ffn2_w_down,
                             gfin, layer=1)

    q, ckvf, ckvb, kr, krf = _cproj(xs, gm, c_in, gq, gkv, c_q,
                                    _rope_tables(jnp.tile(pos_s, dec_batch), c_scale))
    kr_cache = jnp.pad(cache_c_kr[0], ((0, 0), (0, 0), (C_NOPE, LANES - C_NOPE - C_ROPE)))
    c_out = _c_sample(to3(q, dec_batch), cache_c_kv[0], kr_cache, to3(ckvb, dec_batch), to3(kr, dec_batch),
                      c_k_abs, c_v_abs, jnp.tile(c_mask_s, (C_HEADS, 1)))
    s_c_kv = ckvf.reshape(1, dec_batch, t_new, C_KV_LORA)
    s_c_kr = krf.reshape(1, dec_batch, t_new, C_ROPE)
    y_sample = _ffn(xs, [(c_out.reshape(-1, C_HEADS * C_V), c_out_w)], g2, wgu, wd, gfin)

    return (y_prompt.reshape(batch, seq, d), y_sample.reshape(dec_batch, t_new, d),
            p_a_k, p_a_v, p_b_k, p_b_v, p_c_kv, p_c_kr,
            s_a_k, s_a_v, s_b_k, s_b_v, s_c_kv, s_c_kr)
```

```python
import functools
import math

import jax
import jax.numpy as jnp
import numpy as np
from jax import lax
from jax.experimental import pallas as pl
from jax.experimental.pallas import tpu as pltpu

F32 = jnp.float32
BF16 = jnp.bfloat16

EPS = 1e-6
NEG = -1e30
LOG2E = math.log2(math.e)
CHUNK = 64
A_HEADS = 8
A_HEAD_DIM = 64
A_PAST_CHUNKS = 8
A_REL_CLIP = 64
B_HEADS = 4
B_HEAD_DIM = 64
T5_BUCKETS = 32
T5_MAX_DIST = 128
C_HEADS = 16
C_NOPE = 64
C_ROPE = 32
C_V = 64
C_Q_LORA = 384
C_KV_LORA = 256
ROPE_THETA = 10000.0

LANES = 128
VMEM_LIMIT = 56 * 1024 * 1024
ROW_TILE = 512
ATT_TILE = 256
A_TILE = 256
TOKEN_SUBTILES = 2
MXU_WIDTH = 256
FFN_CHUNKS = 2


def _ffn_chunks(d_ff):
    tiles = d_ff // MXU_WIDTH
    assert tiles * MXU_WIDTH == d_ff
    per = -(-tiles // FFN_CHUNKS)
    edges = [min(c * per, tiles) * MXU_WIDTH for c in range(FFN_CHUNKS + 1)]
    return [(lo, hi) for lo, hi in zip(edges[:-1], edges[1:]) if hi > lo]


def _rmsnorm(x, g):
    return x * lax.rsqrt(jnp.mean(x * x, axis=-1, keepdims=True) + EPS) * g


def _dot(a, b):
    return jnp.dot(a, b, preferred_element_type=F32)


def _dot_nt(a, b):
    return lax.dot_general(a, b, (((1,), (1,)), ((), ())), preferred_element_type=F32)


def _const_spec(a):
    nd = a.ndim
    return pl.BlockSpec(a.shape, lambda *_: (0,) * nd, pipeline_mode=pl.Buffered(1))


def _params(sem, flags=None):
    return pltpu.CompilerParams(dimension_semantics=sem, vmem_limit_bytes=VMEM_LIMIT, flags=flags)


def _weight_chunk_copy(src, stage, sems, c, slot):
    rows = stage.shape[1]
    return pltpu.make_async_copy(src.at[pl.ds(c * rows, rows), :], stage.at[slot], sems.at[slot])


def _load_weight_as_bf16(src, stage, sems, dst):
    rows = stage.shape[1]
    n = src.shape[0] // rows
    _weight_chunk_copy(src, stage, sems, 0, 0).start()
    for c in range(n):
        slot = c % 2
        if c + 1 < n:
            _weight_chunk_copy(src, stage, sems, c + 1, 1 - slot).start()
        _weight_chunk_copy(src, stage, sems, c, slot).wait()
        dst[c * rows:(c + 1) * rows, :] = stage[slot].astype(BF16)


def _ffn_kernel(n_pre, with_final, layer, *refs):
    group = 1 + n_pre
    groups = [refs[:group], refs[group:2 * group]]
    idx = 2 * group
    pre_w = refs[idx:idx + n_pre]
    idx += n_pre
    g_ref, wgu_in, wd_in = refs[idx:idx + 3]
    idx += 3
    gf_ref = refs[idx] if with_final else None
    idx += with_final
    outs = refs[idx:idx + 2]
    wgu_ref, wd_ref, gu_stage, d_stage, sems = refs[idx + 2:]
    step, last = pl.program_id(0), pl.num_programs(0) - 1

    @pl.when(step == 0)
    def _():
        _load_weight_as_bf16(wgu_in.at[layer], gu_stage, sems, wgu_ref)
        _load_weight_as_bf16(wd_in.at[layer], d_stage, sems, wd_ref)

    def run(x_ref, o_refs, out_ref):
        x = x_ref[...]
        for o_ref, w_ref in zip(o_refs, pre_w):
            x = x + _dot(o_ref[...], w_ref[...])
        xn = _rmsnorm(x, g_ref[...]).astype(BF16)
        d_ff = wd_ref.shape[0]
        acc = jnp.zeros_like(x)
        for lo, hi in _ffn_chunks(d_ff):
            gate = _dot(xn, wgu_ref[:, lo:hi])
            up = _dot(xn, wgu_ref[:, d_ff + lo:d_ff + hi])
            act = (jax.nn.silu(gate) * up).astype(BF16)
            acc = acc + _dot(act, wd_ref[lo:hi, :])
        y = x + 0.5 * acc
        if with_final:
            y = _rmsnorm(y, gf_ref[...])
        out_ref[...] = y

    @pl.when(step < last)
    def _():
        run(groups[0][0], groups[0][1:], outs[0])

    @pl.when(step == last)
    def _():
        run(groups[1][0], groups[1][1:], outs[1])


def _ffn(x_prompt, pres_prompt, x_sample, pres_sample, pre_weights, g, wgu, wd, layer, g_final=None):
    t, d = x_prompt.shape
    tm = min(ROW_TILE, t)
    n = t // tm
    row = lambda w: pl.BlockSpec((tm, w), lambda i: (jnp.minimum(i, n - 1), 0))
    whole = lambda a: pl.BlockSpec(a.shape, lambda i: (0, 0))
    hbm = pl.BlockSpec(memory_space=pl.ANY)
    args = [x_prompt, *pres_prompt, x_sample, *pres_sample, *pre_weights, g, wgu, wd]
    specs = ([row(d)] + [row(o.shape[1]) for o in pres_prompt] + [whole(x_sample)] + [whole(o) for o in pres_sample]
             + [_const_spec(w) for w in pre_weights] + [_const_spec(g), hbm, hbm])
    if g_final is not None:
        args.append(g_final)
        specs.append(_const_spec(g_final))
    gu_shape, d_shape = wgu.shape[1:], wd.shape[1:]
    chunks = 8
    return pl.pallas_call(
        functools.partial(_ffn_kernel, len(pre_weights), g_final is not None, layer),
        grid=(n + 1,),
        in_specs=specs,
        out_specs=[row(d), whole(x_sample)],
        out_shape=[jax.ShapeDtypeStruct((t, d), F32), jax.ShapeDtypeStruct(x_sample.shape, F32)],
        scratch_shapes=[pltpu.VMEM(gu_shape, BF16), pltpu.VMEM(d_shape, BF16),
                        pltpu.VMEM((2, gu_shape[0] // chunks, gu_shape[1]), F32),
                        pltpu.VMEM((2, d_shape[0] // chunks, d_shape[1]), F32),
                        pltpu.SemaphoreType.DMA((2,))],
        compiler_params=_params(("arbitrary",)),
        name="ffn",
    )(*args)


def _store_values_with_ones(out_ref, v):
    low = lax.broadcasted_iota(jnp.int32, (1, LANES), 1) < LANES // 2
    for p in range(v.shape[1] // LANES):
        vp = v[:, p * LANES:(p + 1) * LANES]
        out_ref[:, 2 * p * LANES:(2 * p + 1) * LANES] = jnp.where(low, vp, 1.0).astype(out_ref.dtype)
        out_ref[:, (2 * p + 1) * LANES:(2 * p + 2) * LANES] = jnp.where(low, 1.0, vp).astype(out_ref.dtype)


def _merge_head_pair(acc_even, acc_odd):
    half = LANES // 2
    low = lax.broadcasted_iota(jnp.int32, (1, LANES), 1) < half
    den = pltpu.roll(jnp.where(low, acc_odd, acc_even), half, 1)
    return jnp.where(low, acc_even, acc_odd) / den


def _eproj_kernel(x_ref, g_ref, w_ref, aq, ak, av, bq, bk, bv, akf, avf, bkf, bvf):
    tm = x_ref.shape[0]
    sub = tm // TOKEN_SUBTILES if tm % (TOKEN_SUBTILES * 16) == 0 else tm
    w = aq.shape[1]
    for r0 in range(0, tm, sub):
        rows = slice(r0, r0 + sub)
        hn = _rmsnorm(x_ref[rows, :], g_ref[...]).astype(BF16)
        t = _dot(hn, w_ref[...])
        parts = [t[:, i * w:(i + 1) * w] for i in range(6)]
        aq[rows, :] = (parts[0] * (A_HEAD_DIM ** -0.5 * LOG2E)).astype(BF16)
        ak[rows, :] = parts[1].astype(BF16)
        _store_values_with_ones(av.at[rows, :], parts[2])
        bq[rows, :] = (parts[3] * (B_HEAD_DIM ** -0.5 * LOG2E)).astype(BF16)
        bk[rows, :] = parts[4].astype(BF16)
        ones = jnp.ones((sub, LANES), BF16)
        for h in range(B_HEADS):
            bv[rows, 2 * h * LANES:(2 * h + 1) * LANES] = parts[5][:, h * LANES:(h + 1) * LANES].astype(BF16)
            bv[rows, (2 * h + 1) * LANES:(2 * h + 2) * LANES] = ones
        akf[rows] = parts[1].reshape((sub,) + akf.shape[1:])
        avf[rows] = parts[2].reshape((sub,) + avf.shape[1:])
        bkf[rows] = parts[4].reshape((sub,) + bkf.shape[1:])
        bvf[rows] = parts[5].reshape((sub,) + bvf.shape[1:])


def _eproj(x, g, w, rows_per_batch, a_keep):
    t, d = x.shape
    tm = min(ROW_TILE, t, a_keep)
    assert rows_per_batch % tm == 0 and a_keep % tm == 0
    per, kept = rows_per_batch // tm, a_keep // tm
    wd = w.shape[1] // 6
    row = lambda n: pl.BlockSpec((tm, n), lambda i: (i, 0))
    a_map = lambda i: ((i // per) * kept + jnp.maximum(i % per - (per - kept), 0), 0, 0)
    a_f32 = pl.BlockSpec((tm, A_HEADS, A_HEAD_DIM), a_map)
    b_f32 = pl.BlockSpec((tm, B_HEADS, 2 * B_HEAD_DIM), lambda i: (i, 0, 0))
    widths = [wd, wd, 2 * wd, wd, wd, 2 * wd]
    shapes = ([jax.ShapeDtypeStruct((t, n), BF16) for n in widths]
              + [jax.ShapeDtypeStruct((t // per * kept, A_HEADS, A_HEAD_DIM), F32)] * 2
              + [jax.ShapeDtypeStruct((t, B_HEADS, 2 * B_HEAD_DIM), F32)] * 2)
    return pl.pallas_call(
        _eproj_kernel,
        grid=(t // tm,),
        in_specs=[row(d), _const_spec(g), _const_spec(w)],
        out_specs=[row(n) for n in widths] + [a_f32, a_f32, b_f32, b_f32],
        out_shape=shapes,
        compiler_params=_params(("arbitrary",)),
        name="eproj",
    )(x, g, w)


def _cproj_kernel(x_ref, g_ref, win_ref, gq_ref, gkv_ref, wq_ref, cq_ref, sq_ref, ck_ref, sk_ref,
                  q_out, ckvf_out, ckvb_out, kr_out, krf_out):
    tm = x_ref.shape[0]
    sub = tm // TOKEN_SUBTILES if tm % (TOKEN_SUBTILES * 16) == 0 else tm
    for r0 in range(0, tm, sub):
        rows = slice(r0, r0 + sub)
        hn = _rmsnorm(x_ref[rows, :], g_ref[...]).astype(BF16)
        t = _dot(hn, win_ref[...])
        cqn = _rmsnorm(t[:, :C_Q_LORA], gq_ref[...]).astype(BF16)
        ckv = _rmsnorm(t[:, C_Q_LORA:C_Q_LORA + C_KV_LORA], gkv_ref[...])
        ckvf_out[rows, :] = ckv
        ckvb_out[rows, :] = ckv.astype(BF16)
        tg = t[:, C_Q_LORA + C_KV_LORA:]
        kr = tg * ck_ref[rows, :] + pltpu.roll(tg, LANES - C_ROPE, 1) * sk_ref[rows, :]
        kr_out[rows, :] = kr
        krf_out[rows, :] = kr[:, C_NOPE:C_NOPE + C_ROPE]
        q = _dot(cqn, wq_ref[...])
        cq, sq = cq_ref[rows, :], sq_ref[rows, :]
        for h in range(C_HEADS):
            qh = q[:, h * LANES:(h + 1) * LANES]
            qf = qh * cq + pltpu.roll(qh, LANES - C_ROPE, 1) * sq
            q_out[rows, h * LANES:(h + 1) * LANES] = qf.astype(BF16)


def _cproj(x, g, win, gq, gkv, wq, tabs):
    t, d = x.shape
    tm = min(ROW_TILE, t)
    nper = tabs[0].shape[0] // tm
    row = lambda n: pl.BlockSpec((tm, n), lambda i: (i, 0))
    tab = pl.BlockSpec((tm, LANES), lambda i: (i % nper, 0))
    shapes = [jax.ShapeDtypeStruct((t, C_HEADS * LANES), BF16),
              jax.ShapeDtypeStruct((t, C_KV_LORA), F32),
              jax.ShapeDtypeStruct((t, C_KV_LORA), BF16),
              jax.ShapeDtypeStruct((t, LANES), F32),
              jax.ShapeDtypeStruct((t, C_ROPE), F32)]
    return pl.pallas_call(
        _cproj_kernel,
        grid=(t // tm,),
        in_specs=[row(d), _const_spec(g), _const_spec(win), _const_spec(gq), _const_spec(gkv),
                  _const_spec(wq), tab, tab, tab, tab],
        out_specs=[row(C_HEADS * LANES), row(C_KV_LORA), row(C_KV_LORA), row(LANES), row(C_ROPE)],
        out_shape=shapes,
        compiler_params=_params(("arbitrary",)),
        name="cproj",
    )(x, g, win, gq, gkv, wq, *tabs)


def _bias_kernel(n_rows, shift, layout, idx_ref, tab_ref, out_ref):
    n_heads = out_ref.shape[0]
    bq = out_ref.shape[1] // len(layout)
    bk = out_ref.shape[2] // len(layout[0])
    block_vregs = max(1, bq * bk // (8 * LANES))
    group = max(1, min(n_heads, 32 // block_vregs))
    for u in range(idx_ref.shape[0] // bq):
        idx = idx_ref[u * bq:(u + 1) * bq, :]
        for h0 in range(0, n_heads, group):
            heads = list(range(h0, min(h0 + group, n_heads)))
            bases = [tab_ref[h, shift] if shift is not None else 0.0 for h in heads]

            def body(r, accs):
                hit = idx == r
                return tuple(jnp.where(hit, (tab_ref[h, r] - b) * LOG2E, acc)
                             for h, b, acc in zip(heads, bases, accs))

            vals = lax.fori_loop(0, n_rows, body, tuple(jnp.full(idx.shape, NEG, F32) for _ in heads))
            for h, val in zip(heads, vals):
                for a, row in enumerate(layout):
                    for t, uu in enumerate(row):
                        if uu == u:
                            out_ref[h, a * bq:(a + 1) * bq, t * bk:(t + 1) * bk] = val


def _bias_expand(idx, table, shift=None, layout=((0,),)):
    n_heads, n_rows = table.shape
    n_blocks = 1 + max(max(row) for row in layout)
    out_shape = (n_heads, idx.shape[0] // n_blocks * len(layout), idx.shape[1] * len(layout[0]))
    return pl.pallas_call(
        functools.partial(_bias_kernel, n_rows, shift, layout),
        in_specs=[pl.BlockSpec(idx.shape, lambda: (0, 0)),
                  pl.BlockSpec(memory_space=pltpu.SMEM)],
        out_specs=pl.BlockSpec(out_shape, lambda: (0, 0, 0)),
        out_shape=jax.ShapeDtypeStruct(out_shape, F32),
        compiler_params=pltpu.CompilerParams(vmem_limit_bytes=VMEM_LIMIT),
        name="bias_expand",
    )(idx, table)


def _unique_blocks(idx, bq, bk):
    q, k = idx.shape
    blocks = idx.reshape(q // bq, bq, k // bk, bk).transpose(0, 2, 1, 3).reshape(-1, bq, bk)
    uniq, inv = np.unique(blocks, axis=0, return_inverse=True)
    layout = tuple(tuple(int(u) for u in row) for row in inv.reshape(q // bq, k // bk))
    return jnp.asarray(uniq.reshape(-1, bk), jnp.int32), layout


def _a_attend(q, kwin, vwin, bias_ref, valid, out_ref, s_ref):
    low = lax.broadcasted_iota(jnp.int32, (1, LANES), 1) < A_HEAD_DIM
    for h in range(A_HEADS):
        sl = slice(h // 2 * LANES, (h // 2 + 1) * LANES)
        qp = q[:, sl]
        qm = jnp.where(low if h % 2 == 0 else jnp.logical_not(low), qp, jnp.zeros_like(qp))
        s = _dot_nt(qm, kwin[:, sl]) + bias_ref[h]
        if valid is not None:
            s = jnp.where(valid, s, NEG)
        s_ref[h] = s - jnp.max(s, axis=-1, keepdims=True)
    accs = []
    for h in range(A_HEADS):
        accs.append(_dot(jnp.exp2(s_ref[h]).astype(BF16), vwin[:, h * LANES:(h + 1) * LANES]))
        if h % 2 == 1:
            out_ref[0, :, h // 2 * LANES:(h // 2 + 1) * LANES] = _merge_head_pair(*accs).astype(BF16)
            accs = []


def _a_prompt_kernel(q_ref, k_ref, v_ref, bias_ref, out_ref, s_ref):
    tq = q_ref.shape[1]
    win = bias_ref.shape[2]
    n_tiles = win // tq
    j = pl.program_id(1)

    def run(check_positions):
        ks, vs = [], []
        for t in range(n_tiles):
            start = pl.multiple_of(jnp.maximum(j - (n_tiles - 1 - t), 0) * tq, tq)
            ks.append(k_ref[0, pl.ds(start, tq), :])
            vs.append(v_ref[0, pl.ds(start, tq), :])
        valid = None
        if check_positions:
            valid = lax.broadcasted_iota(jnp.int32, (1, win), 1) + (j - (n_tiles - 1)) * tq >= 0
        _a_attend(q_ref[0], jnp.concatenate(ks, axis=0), jnp.concatenate(vs, axis=0), bias_ref, valid,
                  out_ref, s_ref)

    @pl.when(j < n_tiles - 1)
    def _():
        run(True)

    @pl.when(j >= n_tiles - 1)
    def _():
        run(False)


def _a_prompt(q, k, v, bias):
    b, s, w = q.shape
    tq = bias.shape[1]
    kv = lambda a: pl.BlockSpec((1, s, a.shape[2]), lambda i, j: (i, 0, 0))
    return pl.pallas_call(
        _a_prompt_kernel,
        grid=(b, s // tq),
        in_specs=[pl.BlockSpec((1, tq, w), lambda i, j: (i, j, 0)), kv(k), kv(v), _const_spec(bias)],
        out_specs=pl.BlockSpec((1, tq, w), lambda i, j: (i, j, 0)),
        out_shape=jax.ShapeDtypeStruct((b, s, w), BF16),
        scratch_shapes=[pltpu.VMEM(bias.shape, F32)],
        compiler_params=_params(("arbitrary", "arbitrary")),
        name="a_prompt",
    )(q, k, v, bias)


def _a_sample_kernel(q_ref, kct_ref, vct_ref, kn_ref, vn_ref, bias_ref, out_ref):
    q = q_ref[0]
    tq, n_c = q.shape[0], kct_ref.shape[2]
    low = lax.broadcasted_iota(jnp.int32, (1, LANES), 1) < A_HEAD_DIM
    for p in range(A_HEADS // 2):
        sl = slice(p * LANES, (p + 1) * LANES)
        qp = q[:, sl]
        kct, vct = kct_ref[0, sl, :].astype(BF16), vct_ref[0, sl, :].astype(BF16)
        outs = []
        for sub in range(2):
            h = 2 * p + sub
            qm = jnp.where(low if sub == 0 else jnp.logical_not(low), qp, jnp.zeros_like(qp))
            s_c = _dot(qm, kct) + bias_ref[h, :, :n_c]
            s_n = _dot_nt(qm, kn_ref[0, :, sl]) + bias_ref[h, :, n_c:n_c + tq]
            m = jnp.maximum(jnp.max(s_c, axis=-1, keepdims=True), jnp.max(s_n, axis=-1, keepdims=True))
            e_c, e_n = jnp.exp2(s_c - m), jnp.exp2(s_n - m)
            l = jnp.sum(e_c, axis=-1, keepdims=True) + jnp.sum(e_n, axis=-1, keepdims=True)
            acc = _dot_nt(e_c.astype(BF16), vct) + _dot(e_n.astype(BF16), vn_ref[0, :, h * LANES:(h + 1) * LANES])
            outs.append(acc / l)
        out_ref[0, :, sl] = jnp.where(low, outs[0], outs[1]).astype(BF16)


def _a_sample(q, k_cache, v_cache, k_new, v_new, bias):
    b, tq, w = q.shape
    blk = lambda a: pl.BlockSpec((1,) + a.shape[1:], lambda i: (i,) + (0,) * (a.ndim - 1))
    return pl.pallas_call(
        _a_sample_kernel,
        grid=(b,),
        in_specs=[blk(q), blk(k_cache), blk(v_cache), blk(k_new), blk(v_new), _const_spec(bias)],
        out_specs=blk(q),
        out_shape=jax.ShapeDtypeStruct((b, tq, w), BF16),
        compiler_params=_params(("arbitrary",)),
        name="a_sample",
    )(q, k_cache, v_cache, k_new, v_new, bias)


def _b_lambda(lq1, lk1, lq2, lk2, lam_init):
    s1 = jnp.sum(lq1[...] * lk1[...], axis=-1, keepdims=True)
    s2 = jnp.sum(lq2[...] * lk2[...], axis=-1, keepdims=True)
    return jnp.exp(s1) - jnp.exp(s2) + lam_init


def _b_stack_queries(q):
    lane = lax.broadcasted_iota(jnp.int32, (1, LANES), 1)
    low = lane < B_HEAD_DIM
    qs = []
    for h in range(B_HEADS):
        qh = q[:, h * LANES:(h + 1) * LANES]
        zero = jnp.zeros_like(qh)
        qs.append(jnp.concatenate([jnp.where(low, qh, zero), jnp.where(low, zero, qh)], axis=0))
    return qs


def _b_finish(o, lam, g, lam_init, tq):
    ob = o[:tq] - lam * o[tq:]
    return _rmsnorm(ob, g) * (1.0 - lam_init)


def _b_prompt_kernel(lam_init, q_ref, k_ref, v_ref, bias_ref, lq1, lk1, lq2, lk2, g_ref, out_ref,
                     qs_ref, m_ref, alpha_ref, acc_ref, s_ref):
    tq = q_ref.shape[1]
    tk = bias_ref.shape[3]
    i = pl.program_id(1)
    for h, qh in enumerate(_b_stack_queries(q_ref[0])):
        qs_ref[h] = qh

    lane_tiles = range(tk // LANES)
    lam = _b_lambda(lq1, lk1, lq2, lk2, lam_init)

    def scores(j, bias_sel, first):
        start = pl.multiple_of(j * tk, tk)
        for h in range(B_HEADS):
            sl = slice(h * LANES, (h + 1) * LANES)
            s = _dot_nt(qs_ref[h], k_ref[0, pl.ds(start, tk), sl])
            parts = [s[:, c * LANES:(c + 1) * LANES] for c in lane_tiles]
            if bias_sel is not None:
                parts = [jnp.concatenate([pt[:tq] + bias_ref[bias_sel, h, :, c * LANES:(c + 1) * LANES],
                                          pt[tq:] + bias_ref[bias_sel, h, :, c * LANES:(c + 1) * LANES]], axis=0)
                         for c, pt in enumerate(parts)]
            mx = parts[0]
            for pt in parts[1:]:
                mx = jnp.maximum(mx, pt)
            row_max = jnp.max(mx, axis=-1, keepdims=True)
            if first:
                m_new = jnp.broadcast_to(row_max, (2 * tq, LANES))
                alpha_ref[h] = jnp.zeros((2 * tq, LANES), F32)
            else:
                m_new = jnp.maximum(m_ref[h], row_max)
                alpha_ref[h] = jnp.exp2(m_ref[h] - m_new)
            m_ref[h] = m_new
            for c, pt in enumerate(parts):
                s_ref[h, :, c * LANES:(c + 1) * LANES] = pt

    def attend(j, last):
        start = pl.multiple_of(j * tk, tk)
        for h in range(B_HEADS):
            m_new = m_ref[h]
            p = jnp.concatenate([jnp.exp2(s_ref[h, :, c * LANES:(c + 1) * LANES] - m_new) for c in lane_tiles],
                                axis=1).astype(BF16)
            alpha = alpha_ref[h]
            pv = _dot(p, v_ref[0, pl.ds(start, tk), 2 * h * LANES:2 * (h + 1) * LANES])
            num = alpha * acc_ref[h, :, :LANES] + pv[:, :LANES]
            den = alpha * acc_ref[h, :, LANES:] + pv[:, LANES:]
            if last:
                bn = _b_finish(num / den, lam, g_ref[...], lam_init, tq)
                out_ref[0, :, h * LANES:(h + 1) * LANES] = bn.astype(BF16)
            else:
                acc_ref[h, :, :LANES] = num
                acc_ref[h, :, LANES:] = den

    acc_ref[...] = jnp.zeros(acc_ref.shape, F32)
    scores(i, 1, True)

    @pl.when(i >= 1)
    def _():
        attend(i, False)
        scores(i - 1, 0, False)

    def trip(t, carry):
        attend(i - t, False)
        scores(i - 1 - t, None, False)
        return carry

    lax.fori_loop(1, i, trip, 0)
    attend(0, True)


def _b_prompt(q, k, v, bias, lams, g, lam_init):
    b, s, w = q.shape
    tq = bias.shape[2]
    kv = lambda a: pl.BlockSpec((1, s, a.shape[2]), lambda i, j: (i, 0, 0))
    scratch = pltpu.VMEM((B_HEADS, 2 * tq, LANES), F32)
    return pl.pallas_call(
        functools.partial(_b_prompt_kernel, lam_init),
        grid=(b, s // tq),
        in_specs=[pl.BlockSpec((1, tq, w), lambda i, j: (i, j, 0)), kv(k), kv(v), _const_spec(bias)]
                 + [_const_spec(x) for x in lams] + [_const_spec(g)],
        out_specs=pl.BlockSpec((1, tq, w), lambda i, j: (i, j, 0)),
        out_shape=jax.ShapeDtypeStruct((b, s, w), BF16),
        scratch_shapes=[pltpu.VMEM((B_HEADS, 2 * tq, LANES), BF16), scratch, scratch,
                        pltpu.VMEM((B_HEADS, 2 * tq, 2 * LANES), F32), pltpu.VMEM((B_HEADS, 2 * tq, tq), F32)],
        compiler_params=_params(("arbitrary", "arbitrary")),
        name="b_prompt",
    )(q, k, v, bias, *lams, g)


def _b_sample_kernel(lam_init, q_ref, kc_ref, vc_ref, kn_ref, vn_ref, bias_ref, lq1, lk1, lq2, lk2, g_ref,
                     out_ref):
    tq, n_c = q_ref.shape[1], kc_ref.shape[1] // B_HEADS
    qs = _b_stack_queries(q_ref[0])
    lam = _b_lambda(lq1, lk1, lq2, lk2, lam_init)
    both = lambda bias: jnp.concatenate([bias, bias], axis=0)
    for h in range(B_HEADS):
        sl = slice(h * LANES, (h + 1) * LANES)
        head_rows = pl.ds(h, n_c, stride=B_HEADS)
        kc, vc = kc_ref[0, head_rows, :].astype(BF16), vc_ref[0, head_rows, :].astype(BF16)
        s_c = _dot_nt(qs[h], kc) + both(bias_ref[h, :, :n_c])
        s_n = _dot_nt(qs[h], kn_ref[0, :, sl]) + both(bias_ref[h, :, n_c:n_c + tq])
        m = jnp.maximum(jnp.max(s_c, axis=-1, keepdims=True), jnp.max(s_n, axis=-1, keepdims=True))
        e_c, e_n = jnp.exp2(s_c - m), jnp.exp2(s_n - m)
        l = jnp.sum(e_c, axis=-1, keepdims=True) + jnp.sum(e_n, axis=-1, keepdims=True)
        o = (_dot(e_c.astype(BF16), vc) + _dot(e_n.astype(BF16), vn_ref[0, :, 2 * h * LANES:(2 * h + 1) * LANES])) / l
        bn = _b_finish(o, lam, g_ref[...], lam_init, tq)
        out_ref[0, :, sl] = bn.astype(BF16)


def _b_sample(q, k_cache, v_cache, k_new, v_new, bias, lams, g, lam_init):
    b, tq, w = q.shape
    blk = lambda a: pl.BlockSpec((1,) + a.shape[1:], lambda i: (i,) + (0,) * (a.ndim - 1))
    return pl.pallas_call(
        functools.partial(_b_sample_kernel, lam_init),
        grid=(b,),
        in_specs=[blk(q), blk(k_cache), blk(v_cache), blk(k_new), blk(v_new), _const_spec(bias)]
                 + [_const_spec(x) for x in lams] + [_const_spec(g)],
        out_specs=blk(q),
        out_shape=jax.ShapeDtypeStruct((b, tq, w), BF16),
        compiler_params=_params(("arbitrary",)),
        name="b_sample",
    )(q, k_cache, v_cache, k_new, v_new, bias, *lams, g)


def _c_prompt_kernel(q_ref, ckv_ref, kr_ref, wk_ref, wv_ref, out_ref, k_ref, v_ref, m_ref, alpha_ref, acc_ref,
                     mask_ref, s_ref):
    tq = q_ref.shape[1]
    tk = tq
    n_lane_tiles = tk // LANES
    i = pl.program_id(1)

    @pl.when((pl.program_id(0) == 0) & (i == 0))
    def _():
        shift = CHUNK.bit_length() - 1
        rc = jnp.right_shift(lax.broadcasted_iota(jnp.int32, (tq, tk), 0), shift)
        kc = jnp.right_shift(lax.broadcasted_iota(jnp.int32, (tq, tk), 1), shift)
        mask_ref[...] = jnp.where(kc <= rc, 0.0, NEG)

    @pl.when(i == 0)
    def _():
        high = (lax.broadcasted_iota(jnp.int32, (1, LANES), 1) >= C_V).astype(F32)
        rows_per_chunk = 2 * tk

        def expand(c, carry):
            rows = pl.ds(pl.multiple_of(c * rows_per_chunk, rows_per_chunk), rows_per_chunk)
            ckv = ckv_ref[0, rows, :]
            kd = _dot(ckv, wk_ref[...])
            vd = _dot(ckv, wv_ref[...])
            kr = kr_ref[0, rows, :]
            for h in range(C_HEADS):
                sl = slice(h * LANES, (h + 1) * LANES)
                k_ref[rows, sl] = (kd[:, sl] + kr).astype(BF16)
                v_ref[rows, sl] = (vd[:, sl] + (high if h % 2 == 0 else 1.0 - high)).astype(BF16)
            return carry

        lax.fori_loop(0, k_ref.shape[0] // rows_per_chunk, expand, 0)

    def scores(j, first):
        start = pl.multiple_of(j * tk, tk)
        for h in range(C_HEADS):
            sl = slice(h * LANES, (h + 1) * LANES)
            s = _dot_nt(q_ref[0, :, sl], k_ref[pl.ds(start, tk), sl])
            parts = [s[:, c * LANES:(c + 1) * LANES] for c in range(n_lane_tiles)]
            if first:
                parts = [pt + mask_ref[:, c * LANES:(c + 1) * LANES] for c, pt in enumerate(parts)]
            mx = parts[0]
            for pt in parts[1:]:
                mx = jnp.maximum(mx, pt)
            row_max = jnp.max(mx, axis=-1, keepdims=True)
            if first:
                m_new = jnp.broadcast_to(row_max, (tq, LANES))
                alpha_ref[h] = jnp.zeros((tq, LANES), F32)
            else:
                m_new = jnp.maximum(m_ref[h], row_max)
                alpha_ref[h] = jnp.exp2(m_ref[h] - m_new)
            m_ref[h] = m_new
            for c, pt in enumerate(parts):
                s_ref[h, :, c * LANES:(c + 1) * LANES] = pt

    def attend(j, last):
        start = pl.multiple_of(j * tk, tk)
        accs = []
        for h in range(C_HEADS):
            sl = slice(h * LANES, (h + 1) * LANES)
            m_new = m_ref[h]
            p = jnp.concatenate([jnp.exp2(s_ref[h, :, c * LANES:(c + 1) * LANES] - m_new)
                                 for c in range(n_lane_tiles)], axis=1).astype(BF16)
            acc = alpha_ref[h] * acc_ref[h] + _dot(p, v_ref[pl.ds(start, tk), sl])
            if not last:
                acc_ref[h] = acc
            elif h % 2 == 0:
                accs = [acc]
            else:
                out_ref[0, :, h // 2 * LANES:(h // 2 + 1) * LANES] = _merge_head_pair(accs[0], acc).astype(BF16)

    acc_ref[...] = jnp.zeros(acc_ref.shape, F32)
    scores(i, True)

    def full(j, carry):
        attend(jnp.where(j == 0, i, j - 1), False)
        scores(j, False)
        return carry

    lax.fori_loop(0, i, full, 0)
    attend(jnp.maximum(i - 1, 0), True)


def _c_prompt(q, ckv, kr, wk, wv):
    b, s, w = q.shape
    tq = ATT_TILE
    scratch = pltpu.VMEM((C_HEADS, tq, LANES), F32)
    per_batch = lambda a: pl.BlockSpec((1, s, a.shape[2]), lambda i, j: (i, 0, 0))
    return pl.pallas_call(
        _c_prompt_kernel,
        grid=(b, s // tq),
        in_specs=[pl.BlockSpec((1, tq, w), lambda i, j: (i, j, 0)), per_batch(ckv), per_batch(kr),
                  _const_spec(wk), _const_spec(wv)],
        out_specs=pl.BlockSpec((1, tq, C_HEADS * C_V), lambda i, j: (i, j, 0)),
        out_shape=jax.ShapeDtypeStruct((b, s, C_HEADS * C_V), BF16),
        scratch_shapes=[pltpu.VMEM((s, w), BF16), pltpu.VMEM((s, w), BF16),
                        scratch, scratch, scratch, pltpu.VMEM((tq, tq), F32),
                        pltpu.VMEM((C_HEADS, tq, tq), F32)],
        compiler_params=_params(("arbitrary", "arbitrary")),
        name="c_prompt",
    )(q, ckv, kr, wk, wv)


def _c_sample_kernel(q_ref, ckv_c_ref, kr_c_ref, ckv_n_ref, kr_n_ref, wk_ref, wv_ref, mask_ref, out_ref):
    tq, n_c = q_ref.shape[1], ckv_c_ref.shape[1]
    q = q_ref[0]
    heads = [q[:, h * LANES:(h + 1) * LANES] for h in range(C_HEADS)]
    qr = jnp.concatenate(heads, axis=0)
    qa = jnp.concatenate([_dot(qh, wk_ref[h]) for h, qh in enumerate(heads)], axis=0).astype(BF16)
    ckv_c, ckv_n = ckv_c_ref[0].astype(BF16), ckv_n_ref[0]
    kr_c, kr_n = kr_c_ref[0].astype(BF16), kr_n_ref[0].astype(BF16)
    s_c = _dot_nt(qa, ckv_c) + _dot_nt(qr, kr_c) + mask_ref[:, :n_c]
    s_n = _dot_nt(qa, ckv_n) + _dot_nt(qr, kr_n) + mask_ref[:, n_c:n_c + tq]
    m = jnp.maximum(jnp.max(s_c, axis=-1, keepdims=True), jnp.max(s_n, axis=-1, keepdims=True))
    e_c, e_n = jnp.exp2(s_c - m), jnp.exp2(s_n - m)
    l = jnp.sum(e_c, axis=-1, keepdims=True) + jnp.sum(e_n, axis=-1, keepdims=True)
    o = ((_dot(e_c.astype(BF16), ckv_c) + _dot(e_n.astype(BF16), ckv_n)) / l).astype(BF16)
    for p in range(C_HEADS // 2):
        rows = lambda h: o[h * tq:(h + 1) * tq]
        pair = _dot(rows(2 * p), wv_ref[2 * p]) + _dot(rows(2 * p + 1), wv_ref[2 * p + 1])
        out_ref[0, :, p * LANES:(p + 1) * LANES] = pair.astype(BF16)


def _c_sample(q, ckv_cache, kr_cache, ckv_new, kr_new, wk, wv, mask):
    b, tq, _ = q.shape
    blk = lambda a: pl.BlockSpec((1,) + a.shape[1:], lambda i: (i,) + (0,) * (a.ndim - 1))
    return pl.pallas_call(
        _c_sample_kernel,
        grid=(b,),
        in_specs=[blk(q), blk(ckv_cache), blk(kr_cache), blk(ckv_new), blk(kr_new),
                  _const_spec(wk), _const_spec(wv), _const_spec(mask)],
        out_specs=pl.BlockSpec((1, tq, C_HEADS * C_V), lambda i: (i, 0, 0)),
        out_shape=jax.ShapeDtypeStruct((b, tq, C_HEADS * C_V), BF16),
        compiler_params=_params(("arbitrary",)),
        name="c_sample",
    )(q, ckv_cache, kr_cache, ckv_new, kr_new, wk, wv, mask)


def _t5_bucket(rel):
    nb = T5_BUCKETS // 2
    max_exact = nb // 2
    ret = jnp.where(rel > 0, nb, 0)
    n = jnp.abs(rel)
    n_f = jnp.maximum(n, 1).astype(F32)
    large = max_exact + (jnp.log(n_f / max_exact) / math.log(T5_MAX_DIST / max_exact)
                         * (nb - max_exact)).astype(jnp.int32)
    large = jnp.minimum(large, nb - 1)
    return ret + jnp.where(n < max_exact, n, large)


def _a_index(q_pos, k_pos, k_real):
    rel = np.clip(q_pos[:, None] - k_pos[None, :], -A_REL_CLIP, A_REL_CLIP) + A_REL_CLIP
    qc, kc = q_pos[:, None] // CHUNK, k_pos[None, :] // CHUNK
    ok = k_real[None, :] & (kc <= qc) & (kc >= qc - A_PAST_CHUNKS)
    return np.where(ok, rel, -1).astype(np.int32)


def _b_index(q_pos, k_pos, k_real):
    idx = _t5_bucket(k_pos[None, :] - q_pos[:, None])
    ok = k_real[None, :] & ((k_pos[None, :] // CHUNK) <= (q_pos[:, None] // CHUNK))
    return jnp.where(ok, idx, -1).astype(jnp.int32)


def _rope_tables(pos, scale):
    half = C_ROPE // 2
    inv = ROPE_THETA ** (-jnp.arange(half, dtype=F32) / half)
    ang = pos.astype(F32)[:, None] * inv[None, :]
    cos = jnp.concatenate([jnp.cos(ang)] * 2, axis=-1)
    sin = jnp.concatenate([jnp.sin(ang)] * 2, axis=-1)
    n = pos.shape[0]
    z_nope = jnp.zeros((n, C_NOPE), F32)
    z_tail = jnp.zeros((n, LANES - C_NOPE - C_ROPE), F32)
    cq = jnp.concatenate([jnp.full((n, C_NOPE), scale, F32), cos * scale, z_tail], axis=-1)
    sq = jnp.concatenate([z_nope, sin * scale, z_tail], axis=-1)
    ck = jnp.concatenate([z_nope, cos, z_tail], axis=-1)
    sk = jnp.concatenate([z_nope, sin, z_tail], axis=-1)
    return cq, sq, ck, sk


def _rot_cols(w):
    half = w.shape[-1] // 2
    return jnp.concatenate([-w[..., half:], w[..., :half]], axis=-1)


def kernel(x_prompt, x_sample, cache_a_k, cache_a_v, cache_b_k, cache_b_v, cache_c_kv, cache_c_kr,
           t5_bias, ffn1_norm, ffn1_w_gu, ffn1_w_down, mix_norm, ffn2_norm, ffn2_w_gu, ffn2_w_down,
           e_w_in, a_rel_bias, b_lambda_q1, b_lambda_k1, b_lambda_q2, b_lambda_k2, b_subln, e_w_out,
           c_w_in, c_q_norm, c_kv_norm, c_w_q_up, c_w_kv_up, c_w_out, final_norm):
    batch, seq, d = x_prompt.shape
    dec_batch, t_new, _ = x_sample.shape
    past = cache_b_k.shape[2]
    n_cache_a = cache_a_k.shape[2]
    a_w = A_HEADS * A_HEAD_DIM
    b_w = B_HEADS * 2 * B_HEAD_DIM
    tq = ATT_TILE
    a_pad = A_PAST_CHUNKS * CHUNK
    a_keep = min(a_pad, seq)
    assert tq + 1 >= T5_MAX_DIST and tq % CHUNK == 0 and a_pad % tq == 0
    far_bucket = T5_BUCKETS // 2 - 1
    lam_init = 0.8 - 0.6 * math.exp(-0.3 * 0)
    c_scale = (C_NOPE + C_ROPE) ** -0.5 * math.log2(math.e)
    row2 = lambda v: v.reshape(1, -1)

    e_in = e_w_in[0].astype(BF16)
    e_out_a, e_out_b = e_w_out[0, :a_w].astype(BF16), e_w_out[0, a_w:].astype(BF16)
    w_in = c_w_in[0]
    w_kr = w_in[:, C_Q_LORA + C_KV_LORA:]
    c_in = jnp.concatenate([w_in[:, :C_Q_LORA + C_KV_LORA], jnp.zeros((d, C_NOPE), F32),
                            w_kr, _rot_cols(w_kr)], axis=-1).astype(BF16)
    wq = c_w_q_up[0].reshape(C_Q_LORA, C_HEADS, C_NOPE + C_ROPE)
    wq_rope = wq[..., C_NOPE:]
    c_q = jnp.concatenate([wq[..., :C_NOPE], wq_rope, _rot_cols(wq_rope)], axis=-1)
    c_q = c_q.reshape(C_Q_LORA, C_HEADS * LANES).astype(BF16)
    wkv = c_w_kv_up[0].reshape(C_KV_LORA, C_HEADS, C_NOPE + C_V)
    c_k = jnp.concatenate([wkv[..., :C_NOPE], jnp.zeros_like(wkv[..., C_NOPE:])], axis=-1)
    c_k = c_k.reshape(C_KV_LORA, C_HEADS * LANES).astype(BF16)
    wv_pairs = wkv[..., C_NOPE:].reshape(C_KV_LORA, C_HEADS // 2, 2, C_V)
    z_v = jnp.zeros_like(wv_pairs[:, :, 0])
    c_v = jnp.stack([jnp.concatenate([wv_pairs[:, :, 0], z_v], axis=-1),
                     jnp.concatenate([z_v, wv_pairs[:, :, 1]], axis=-1)], axis=2)
    c_v = c_v.reshape(C_KV_LORA, C_HEADS * LANES).astype(BF16)
    c_k_abs = jnp.pad(jnp.transpose(wkv[..., :C_NOPE], (1, 2, 0)), ((0, 0), (0, LANES - C_NOPE), (0, 0)))
    c_k_abs = c_k_abs.astype(BF16)
    c_v_abs = jnp.transpose(c_v.reshape(C_KV_LORA, C_HEADS, LANES), (1, 0, 2))
    c_out_w = c_w_out[0].astype(BF16)
    lams = [row2(b_lambda_q1[0]), row2(b_lambda_k1[0]), row2(b_lambda_q2[0]), row2(b_lambda_k2[0])]
    subln = row2(b_subln[0])

    pos_s = past + jnp.arange(t_new)
    r = jnp.arange(tq)
    assert A_TILE % CHUNK == 0 and a_pad % A_TILE == 0
    a_idx_p, a_lay_p = _unique_blocks(
        _a_index(a_pad + np.arange(A_TILE), np.arange(a_pad + A_TILE), np.ones((a_pad + A_TILE,), bool)),
        CHUNK, LANES)
    ka = n_cache_a + t_new
    ka_pad = -(-ka // LANES) * LANES
    pos_s_np = past + np.arange(t_new)
    a_kpos = np.concatenate([past - n_cache_a + np.arange(n_cache_a), pos_s_np,
                             np.zeros((ka_pad - ka,), np.int64)])
    a_real = np.arange(ka_pad) < ka
    a_idx_s, a_lay_s = _unique_blocks(_a_index(pos_s_np, a_kpos, a_real & (a_kpos >= 0)), t_new, LANES)
    b_idx_p = jnp.stack([_b_index(tq + r, r, jnp.ones((tq,), bool)),
                         _b_index(r, r, jnp.ones((tq,), bool))])
    kb = past + t_new
    kb_pad = -(-kb // LANES) * LANES
    b_kpos = jnp.arange(kb_pad)
    b_real = b_kpos < kb
    b_idx_s = _b_index(pos_s, b_kpos, b_real)
    c_mask_s = jnp.where(b_real[None, :] & ((b_kpos[None, :] // CHUNK) <= (pos_s[:, None] // CHUNK)),
                         0.0, NEG).astype(F32)

    a_bias_p = _bias_expand(a_idx_p, a_rel_bias[0], None, a_lay_p)
    a_bias_s = _bias_expand(a_idx_s, a_rel_bias[0], None, a_lay_s)
    t5_t = t5_bias.T
    b_bias_p = jnp.stack([_bias_expand(b_idx_p[0], t5_t, far_bucket),
                          _bias_expand(b_idx_p[1], t5_t, far_bucket)])
    b_bias_s = _bias_expand(b_idx_s, t5_t, far_bucket)

    xp = x_prompt.reshape(batch * seq, d)
    xs = x_sample.reshape(dec_batch * t_new, d)
    g1, gm, g2 = row2(ffn1_norm[0]), row2(mix_norm[0]), row2(ffn2_norm[0])

    xp, xs = _ffn(xp, [], xs, [], [], g1, ffn1_w_gu, ffn1_w_down, 0)

    aq, ak, av, bq, bk, bv, akf, avf, bkf, bvf = _eproj(xp, gm, e_in, seq, a_keep)
    to3 = lambda t, n: t.reshape(n, -1, t.shape[-1])
    a_out_p = _a_prompt(to3(aq, batch), to3(ak, batch), to3(av, batch), a_bias_p)
    b_out_p = _b_prompt(to3(bq, batch), to3(bk, batch), to3(bv, batch), b_bias_p, lams, subln, lam_init)
    p_a_k = akf.reshape(1, batch, a_keep, A_HEADS, A_HEAD_DIM)
    p_a_v = avf.reshape(1, batch, a_keep, A_HEADS, A_HEAD_DIM)
    p_b_k = bkf.reshape(1, batch, seq, B_HEADS, 2 * B_HEAD_DIM)
    p_b_v = bvf.reshape(1, batch, seq, B_HEADS, 2 * B_HEAD_DIM)

    aq, ak, av, bq, bk, bv, akf, avf, bkf, bvf = _eproj(xs, gm, e_in, dec_batch * t_new, dec_batch * t_new)
    dims_major = lambda c: jnp.transpose(c, (0, 2, 3, 1)).reshape(dec_batch, a_w, n_cache_a)
    a_out_s = _a_sample(to3(aq, dec_batch), dims_major(cache_a_k[0]), dims_major(cache_a_v[0]),
                        to3(ak, dec_batch), to3(av, dec_batch), a_bias_s)
    pos_head_rows = lambda c: c.reshape(dec_batch, past * B_HEADS, 2 * B_HEAD_DIM)
    b_out_s = _b_sample(to3(bq, dec_batch), pos_head_rows(cache_b_k[0]), pos_head_rows(cache_b_v[0]),
                        to3(bk, dec_batch), to3(bv, dec_batch), b_bias_s, lams, subln, lam_init)
    s_a_k = akf.reshape(1, dec_batch, t_new, A_HEADS, A_HEAD_DIM)
    s_a_v = avf.reshape(1, dec_batch, t_new, A_HEADS, A_HEAD_DIM)
    s_b_k = bkf.reshape(1, dec_batch, t_new, B_HEADS, 2 * B_HEAD_DIM)
    s_b_v = bvf.reshape(1, dec_batch, t_new, B_HEADS, 2 * B_HEAD_DIM)
    flat = lambda t: t.reshape(-1, t.shape[-1])
    xp, xs = _ffn(xp, [flat(a_out_p), flat(b_out_p)], xs, [flat(a_out_s), flat(b_out_s)], [e_out_a, e_out_b],
                  g2, ffn2_w_gu, ffn2_w_down, 0)

    g1, gm, g2 = row2(ffn1_norm[1]), row2(mix_norm[1]), row2(ffn2_norm[1])
    gq, gkv, gfin = row2(c_q_norm[0]), row2(c_kv_norm[0]), row2(final_norm)
    xp, xs = _ffn(xp, [], xs, [], [], g1, ffn1_w_gu, ffn1_w_down, 1)

    q, ckvf, ckvb, kr, krf = _cproj(xp, gm, c_in, gq, gkv, c_q, _rope_tables(jnp.arange(seq), c_scale))
    c_out_p = _c_prompt(to3(q, batch), to3(ckvb, batch), to3(kr, batch), c_k, c_v)
    p_c_kv = ckvf.reshape(1, batch, seq, C_KV_LORA)
    p_c_kr = krf.reshape(1, batch, seq, C_ROPE)

    q, ckvf, ckvb, kr, krf = _cproj(xs, gm, c_in, gq, gkv, c_q,
                                    _rope_tables(jnp.tile(pos_s, dec_batch), c_scale))
    kr_cache = jnp.pad(cache_c_kr[0], ((0, 0), (0, 0), (C_NOPE, LANES - C_NOPE - C_ROPE)))
    c_out_s = _c_sample(to3(q, dec_batch), cache_c_kv[0], kr_cache, to3(ckvb, dec_batch), to3(kr, dec_batch),
                        c_k_abs, c_v_abs, jnp.tile(c_mask_s, (C_HEADS, 1)))
    s_c_kv = ckvf.reshape(1, dec_batch, t_new, C_KV_LORA)
    s_c_kr = krf.reshape(1, dec_batch, t_new, C_ROPE)
    y_prompt, y_sample = _ffn(xp, [flat(c_out_p)], xs, [flat(c_out_s)], [c_out_w], g2, ffn2_w_gu, ffn2_w_down,
                              1, gfin)

    return (y_prompt.reshape(batch, seq, d), y_sample.reshape(dec_batch, t_new, d),
            p_a_k, p_a_v, p_b_k, p_b_v, p_c_kv, p_c_kr,
            s_a_k, s_a_v, s_b_k, s_b_v, s_c_kv, s_c_kr)
```

```python
import functools
import math

import jax
import jax.numpy as jnp
import numpy as np
from jax import lax
from jax.experimental import pallas as pl
from jax.experimental.pallas import tpu as pltpu

F32 = jnp.float32
BF16 = jnp.bfloat16

EPS = 1e-6
NEG = -1e30
LOG2E = math.log2(math.e)
CHUNK = 64
A_HEADS = 8
A_HEAD_DIM = 64
A_PAST_CHUNKS = 8
A_REL_CLIP = 64
B_HEADS = 4
B_HEAD_DIM = 64
T5_BUCKETS = 32
T5_MAX_DIST = 128
C_HEADS = 16
C_NOPE = 64
C_ROPE = 32
C_V = 64
C_Q_LORA = 384
C_KV_LORA = 256
ROPE_THETA = 10000.0

LANES = 128
VMEM_LIMIT = 56 * 1024 * 1024
ROW_TILE = 512
ATT_TILE = 256
A_TILE = 256
BIAS_LOOP_UNROLL = 8
TOKEN_SUBTILES = 2
MXU_WIDTH = 256
FFN_CHUNKS = 2


def _ffn_chunks(d_ff):
    tiles = d_ff // MXU_WIDTH
    assert tiles * MXU_WIDTH == d_ff
    per = -(-tiles // FFN_CHUNKS)
    edges = [min(c * per, tiles) * MXU_WIDTH for c in range(FFN_CHUNKS + 1)]
    return [(lo, hi) for lo, hi in zip(edges[:-1], edges[1:]) if hi > lo]


def _rmsnorm(x, g):
    return x * lax.rsqrt(jnp.mean(x * x, axis=-1, keepdims=True) + EPS) * g


def _dot(a, b):
    return jnp.dot(a, b, preferred_element_type=F32)


def _dot_nt(a, b):
    return lax.dot_general(a, b, (((1,), (1,)), ((), ())), preferred_element_type=F32)


def _const_spec(a):
    nd = a.ndim
    return pl.BlockSpec(a.shape, lambda *_: (0,) * nd, pipeline_mode=pl.Buffered(1))


def _params(sem, flags=None):
    return pltpu.CompilerParams(dimension_semantics=sem, vmem_limit_bytes=VMEM_LIMIT, flags=flags)


def _weight_chunk_copy(src, stage, sems, c, slot):
    rows = stage.shape[1]
    return pltpu.make_async_copy(src.at[pl.ds(c * rows, rows), :], stage.at[slot], sems.at[slot])


def _load_weight_as_bf16(src, stage, sems, dst):
    rows = stage.shape[1]
    n = src.shape[0] // rows
    _weight_chunk_copy(src, stage, sems, 0, 0).start()
    for c in range(n):
        slot = c % 2
        if c + 1 < n:
            _weight_chunk_copy(src, stage, sems, c + 1, 1 - slot).start()
        _weight_chunk_copy(src, stage, sems, c, slot).wait()
        dst[c * rows:(c + 1) * rows, :] = stage[slot].astype(BF16)


def _ffn_kernel(n_pre, with_final, layer, *refs):
    group = 1 + n_pre
    groups = [refs[:group], refs[group:2 * group]]
    idx = 2 * group
    pre_w = refs[idx:idx + n_pre]
    idx += n_pre
    g_ref, wgu_in, wd_in = refs[idx:idx + 3]
    idx += 3
    gf_ref = refs[idx] if with_final else None
    idx += with_final
    outs = refs[idx:idx + 2]
    wgu_ref, wd_ref, gu_stage, d_stage, sems = refs[idx + 2:]
    step, last = pl.program_id(0), pl.num_programs(0) - 1

    @pl.when(step == 0)
    def _():
        _load_weight_as_bf16(wgu_in.at[layer], gu_stage, sems, wgu_ref)
        _load_weight_as_bf16(wd_in.at[layer], d_stage, sems, wd_ref)

    def run(x_ref, o_refs, out_ref):
        x = x_ref[...]
        for o_ref, w_ref in zip(o_refs, pre_w):
            x = x + _dot(o_ref[...], w_ref[...])
        xn = _rmsnorm(x, g_ref[...]).astype(BF16)
        d_ff = wd_ref.shape[0]
        acc = jnp.zeros_like(x)
        for lo, hi in _ffn_chunks(d_ff):
            gate = _dot(xn, wgu_ref[:, lo:hi])
            up = _dot(xn, wgu_ref[:, d_ff + lo:d_ff + hi])
            act = (jax.nn.silu(gate) * up).astype(BF16)
            acc = acc + _dot(act, wd_ref[lo:hi, :])
        y = x + 0.5 * acc
        if with_final:
            y = _rmsnorm(y, gf_ref[...])
        out_ref[...] = y

    @pl.when(step < last)
    def _():
        run(groups[0][0], groups[0][1:], outs[0])

    @pl.when(step == last)
    def _():
        run(groups[1][0], groups[1][1:], outs[1])


def _ffn(x_prompt, pres_prompt, x_sample, pres_sample, pre_weights, g, wgu, wd, layer, g_final=None):
    t, d = x_prompt.shape
    tm = min(ROW_TILE, t)
    n = t // tm
    row = lambda w: pl.BlockSpec((tm, w), lambda i: (jnp.minimum(i, n - 1), 0))
    whole = lambda a: pl.BlockSpec(a.shape, lambda i: (0, 0))
    hbm = pl.BlockSpec(memory_space=pl.ANY)
    args = [x_prompt, *pres_prompt, x_sample, *pres_sample, *pre_weights, g, wgu, wd]
    specs = ([row(d)] + [row(o.shape[1]) for o in pres_prompt] + [whole(x_sample)] + [whole(o) for o in pres_sample]
             + [_const_spec(w) for w in pre_weights] + [_const_spec(g), hbm, hbm])
    if g_final is not None:
        args.append(g_final)
        specs.append(_const_spec(g_final))
    gu_shape, d_shape = wgu.shape[1:], wd.shape[1:]
    chunks = 8
    return pl.pallas_call(
        functools.partial(_ffn_kernel, len(pre_weights), g_final is not None, layer),
        grid=(n + 1,),
        in_specs=specs,
        out_specs=[row(d), whole(x_sample)],
        out_shape=[jax.ShapeDtypeStruct((t, d), F32), jax.ShapeDtypeStruct(x_sample.shape, F32)],
        scratch_shapes=[pltpu.VMEM(gu_shape, BF16), pltpu.VMEM(d_shape, BF16),
                        pltpu.VMEM((2, gu_shape[0] // chunks, gu_shape[1]), F32),
                        pltpu.VMEM((2, d_shape[0] // chunks, d_shape[1]), F32),
                        pltpu.SemaphoreType.DMA((2,))],
        compiler_params=_params(("arbitrary",)),
        name="ffn",
    )(*args)


def _store_values_with_ones(out_ref, v):
    low = lax.broadcasted_iota(jnp.int32, (1, LANES), 1) < LANES // 2
    for p in range(v.shape[1] // LANES):
        vp = v[:, p * LANES:(p + 1) * LANES]
        out_ref[:, 2 * p * LANES:(2 * p + 1) * LANES] = jnp.where(low, vp, 1.0).astype(out_ref.dtype)
        out_ref[:, (2 * p + 1) * LANES:(2 * p + 2) * LANES] = jnp.where(low, 1.0, vp).astype(out_ref.dtype)


def _merge_head_pair(acc_even, acc_odd):
    half = LANES // 2
    low = lax.broadcasted_iota(jnp.int32, (1, LANES), 1) < half
    den = pltpu.roll(jnp.where(low, acc_odd, acc_even), half, 1)
    return jnp.where(low, acc_even, acc_odd) / den


def _eproj_kernel(x_ref, g_ref, w_ref, aq, ak, av, bq, bk, bv, akf, avf, bkf, bvf):
    tm = x_ref.shape[0]
    sub = tm // TOKEN_SUBTILES if tm % (TOKEN_SUBTILES * 16) == 0 else tm
    w = aq.shape[1]
    for r0 in range(0, tm, sub):
        rows = slice(r0, r0 + sub)
        hn = _rmsnorm(x_ref[rows, :], g_ref[...]).astype(BF16)
        t = _dot(hn, w_ref[...])
        parts = [t[:, i * w:(i + 1) * w] for i in range(6)]
        aq[rows, :] = (parts[0] * (A_HEAD_DIM ** -0.5 * LOG2E)).astype(BF16)
        ak[rows, :] = parts[1].astype(BF16)
        _store_values_with_ones(av.at[rows, :], parts[2])
        bq[rows, :] = (parts[3] * (B_HEAD_DIM ** -0.5 * LOG2E)).astype(BF16)
        bk[rows, :] = parts[4].astype(BF16)
        ones = jnp.ones((sub, LANES), BF16)
        for h in range(B_HEADS):
            bv[rows, 2 * h * LANES:(2 * h + 1) * LANES] = parts[5][:, h * LANES:(h + 1) * LANES].astype(BF16)
            bv[rows, (2 * h + 1) * LANES:(2 * h + 2) * LANES] = ones
        akf[rows] = parts[1].reshape((sub,) + akf.shape[1:])
        avf[rows] = parts[2].reshape((sub,) + avf.shape[1:])
        bkf[rows] = parts[4].reshape((sub,) + bkf.shape[1:])
        bvf[rows] = parts[5].reshape((sub,) + bvf.shape[1:])


def _eproj(x, g, w, rows_per_batch, a_keep):
    t, d = x.shape
    tm = min(ROW_TILE, t, a_keep)
    assert rows_per_batch % tm == 0 and a_keep % tm == 0
    per, kept = rows_per_batch // tm, a_keep // tm
    wd = w.shape[1] // 6
    row = lambda n: pl.BlockSpec((tm, n), lambda i: (i, 0))
    a_map = lambda i: ((i // per) * kept + jnp.maximum(i % per - (per - kept), 0), 0, 0)
    a_f32 = pl.BlockSpec((tm, A_HEADS, A_HEAD_DIM), a_map)
    b_f32 = pl.BlockSpec((tm, B_HEADS, 2 * B_HEAD_DIM), lambda i: (i, 0, 0))
    widths = [wd, wd, 2 * wd, wd, wd, 2 * wd]
    shapes = ([jax.ShapeDtypeStruct((t, n), BF16) for n in widths]
              + [jax.ShapeDtypeStruct((t // per * kept, A_HEADS, A_HEAD_DIM), F32)] * 2
              + [jax.ShapeDtypeStruct((t, B_HEADS, 2 * B_HEAD_DIM), F32)] * 2)
    return pl.pallas_call(
        _eproj_kernel,
        grid=(t // tm,),
        in_specs=[row(d), _const_spec(g), _const_spec(w)],
        out_specs=[row(n) for n in widths] + [a_f32, a_f32, b_f32, b_f32],
        out_shape=shapes,
        compiler_params=_params(("arbitrary",)),
        name="eproj",
    )(x, g, w)


def _cproj_kernel(x_ref, g_ref, win_ref, gq_ref, gkv_ref, wq_ref, cq_ref, sq_ref, ck_ref, sk_ref,
                  q_out, ckvf_out, ckvb_out, kr_out, krf_out):
    tm = x_ref.shape[0]
    sub = tm // TOKEN_SUBTILES if tm % (TOKEN_SUBTILES * 16) == 0 else tm
    for r0 in range(0, tm, sub):
        rows = slice(r0, r0 + sub)
        hn = _rmsnorm(x_ref[rows, :], g_ref[...]).astype(BF16)
        t = _dot(hn, win_ref[...])
        cqn = _rmsnorm(t[:, :C_Q_LORA], gq_ref[...]).astype(BF16)
        ckv = _rmsnorm(t[:, C_Q_LORA:C_Q_LORA + C_KV_LORA], gkv_ref[...])
        ckvf_out[rows, :] = ckv
        ckvb_out[rows, :] = ckv.astype(BF16)
        tg = t[:, C_Q_LORA + C_KV_LORA:]
        kr = tg * ck_ref[rows, :] + pltpu.roll(tg, LANES - C_ROPE, 1) * sk_ref[rows, :]
        kr_out[rows, :] = kr
        krf_out[rows, :] = kr[:, C_NOPE:C_NOPE + C_ROPE]
        q = _dot(cqn, wq_ref[...])
        cq, sq = cq_ref[rows, :], sq_ref[rows, :]
        for h in range(C_HEADS):
            qh = q[:, h * LANES:(h + 1) * LANES]
            qf = qh * cq + pltpu.roll(qh, LANES - C_ROPE, 1) * sq
            q_out[rows, h * LANES:(h + 1) * LANES] = qf.astype(BF16)


def _cproj(x, g, win, gq, gkv, wq, tabs):
    t, d = x.shape
    tm = min(ROW_TILE, t)
    nper = tabs[0].shape[0] // tm
    row = lambda n: pl.BlockSpec((tm, n), lambda i: (i, 0))
    tab = pl.BlockSpec((tm, LANES), lambda i: (i % nper, 0))
    shapes = [jax.ShapeDtypeStruct((t, C_HEADS * LANES), BF16),
              jax.ShapeDtypeStruct((t, C_KV_LORA), F32),
              jax.ShapeDtypeStruct((t, C_KV_LORA), BF16),
              jax.ShapeDtypeStruct((t, LANES), F32),
              jax.ShapeDtypeStruct((t, C_ROPE), F32)]
    return pl.pallas_call(
        _cproj_kernel,
        grid=(t // tm,),
        in_specs=[row(d), _const_spec(g), _const_spec(win), _const_spec(gq), _const_spec(gkv),
                  _const_spec(wq), tab, tab, tab, tab],
        out_specs=[row(C_HEADS * LANES), row(C_KV_LORA), row(C_KV_LORA), row(LANES), row(C_ROPE)],
        out_shape=shapes,
        compiler_params=_params(("arbitrary",)),
        name="cproj",
    )(x, g, win, gq, gkv, wq, *tabs)


def _bias_kernel(n_rows, shift, layout, idx_ref, tab_ref, out_ref):
    n_heads = out_ref.shape[0]
    bq = out_ref.shape[1] // len(layout)
    bk = out_ref.shape[2] // len(layout[0])
    block_vregs = max(1, bq * bk // (8 * LANES))
    group = max(1, min(n_heads, 32 // block_vregs))
    for u in range(idx_ref.shape[0] // bq):
        idx = idx_ref[u * bq:(u + 1) * bq, :]
        for h0 in range(0, n_heads, group):
            heads = list(range(h0, min(h0 + group, n_heads)))
            bases = [tab_ref[h, shift] if shift is not None else 0.0 for h in heads]

            def body(r, accs):
                hit = idx == r
                return tuple(jnp.where(hit, (tab_ref[h, r] - b) * LOG2E, acc)
                             for h, b, acc in zip(heads, bases, accs))

            vals = lax.fori_loop(0, n_rows, body, tuple(jnp.full(idx.shape, NEG, F32) for _ in heads),
                                 unroll=BIAS_LOOP_UNROLL)
            for h, val in zip(heads, vals):
                for a, row in enumerate(layout):
                    for t, uu in enumerate(row):
                        if uu == u:
                            out_ref[h, a * bq:(a + 1) * bq, t * bk:(t + 1) * bk] = val


def _bias_expand(idx, table, shift=None, layout=((0,),)):
    n_heads, n_rows = table.shape
    n_blocks = 1 + max(max(row) for row in layout)
    out_shape = (n_heads, idx.shape[0] // n_blocks * len(layout), idx.shape[1] * len(layout[0]))
    return pl.pallas_call(
        functools.partial(_bias_kernel, n_rows, shift, layout),
        in_specs=[pl.BlockSpec(idx.shape, lambda: (0, 0)),
                  pl.BlockSpec(memory_space=pltpu.SMEM)],
        out_specs=pl.BlockSpec(out_shape, lambda: (0, 0, 0)),
        out_shape=jax.ShapeDtypeStruct(out_shape, F32),
        compiler_params=pltpu.CompilerParams(vmem_limit_bytes=VMEM_LIMIT),
        name="bias_expand",
    )(idx, table)


def _unique_blocks(idx, bq, bk):
    q, k = idx.shape
    blocks = idx.reshape(q // bq, bq, k // bk, bk).transpose(0, 2, 1, 3).reshape(-1, bq, bk)
    uniq, inv = np.unique(blocks, axis=0, return_inverse=True)
    layout = tuple(tuple(int(u) for u in row) for row in inv.reshape(q // bq, k // bk))
    return jnp.asarray(uniq.reshape(-1, bk), jnp.int32), layout


def _a_attend(q, kwin, vwin, bias_ref, valid, out_ref, s_ref):
    low = lax.broadcasted_iota(jnp.int32, (1, LANES), 1) < A_HEAD_DIM
    for h in range(A_HEADS):
        sl = slice(h // 2 * LANES, (h // 2 + 1) * LANES)
        qp = q[:, sl]
        qm = jnp.where(low if h % 2 == 0 else jnp.logical_not(low), qp, jnp.zeros_like(qp))
        s = _dot_nt(qm, kwin[:, sl]) + bias_ref[h]
        if valid is not None:
            s = jnp.where(valid, s, NEG)
        s_ref[h] = s - jnp.max(s, axis=-1, keepdims=True)
    accs = []
    for h in range(A_HEADS):
        accs.append(_dot(jnp.exp2(s_ref[h]).astype(BF16), vwin[:, h * LANES:(h + 1) * LANES]))
        if h % 2 == 1:
            out_ref[0, :, h // 2 * LANES:(h // 2 + 1) * LANES] = _merge_head_pair(*accs).astype(BF16)
            accs = []


def _a_prompt_kernel(q_ref, k_ref, v_ref, bias_ref, out_ref, s_ref):
    tq = q_ref.shape[1]
    win = bias_ref.shape[2]
    n_tiles = win // tq
    j = pl.program_id(1)

    def run(check_positions):
        ks, vs = [], []
        for t in range(n_tiles):
            start = pl.multiple_of(jnp.maximum(j - (n_tiles - 1 - t), 0) * tq, tq)
            ks.append(k_ref[0, pl.ds(start, tq), :])
            vs.append(v_ref[0, pl.ds(start, tq), :])
        valid = None
        if check_positions:
            valid = lax.broadcasted_iota(jnp.int32, (1, win), 1) + (j - (n_tiles - 1)) * tq >= 0
        _a_attend(q_ref[0], jnp.concatenate(ks, axis=0), jnp.concatenate(vs, axis=0), bias_ref, valid,
                  out_ref, s_ref)

    @pl.when(j < n_tiles - 1)
    def _():
        run(True)

    @pl.when(j >= n_tiles - 1)
    def _():
        run(False)


def _a_prompt(q, k, v, bias):
    b, s, w = q.shape
    tq = bias.shape[1]
    kv = lambda a: pl.BlockSpec((1, s, a.shape[2]), lambda i, j: (i, 0, 0))
    return pl.pallas_call(
        _a_prompt_kernel,
        grid=(b, s // tq),
        in_specs=[pl.BlockSpec((1, tq, w), lambda i, j: (i, j, 0)), kv(k), kv(v), _const_spec(bias)],
        out_specs=pl.BlockSpec((1, tq, w), lambda i, j: (i, j, 0)),
        out_shape=jax.ShapeDtypeStruct((b, s, w), BF16),
        scratch_shapes=[pltpu.VMEM(bias.shape, F32)],
        compiler_params=_params(("arbitrary", "arbitrary")),
        name="a_prompt",
    )(q, k, v, bias)


def _a_sample_kernel(q_ref, kct_ref, vct_ref, kn_ref, vn_ref, bias_ref, out_ref):
    q = q_ref[0]
    tq, n_c = q.shape[0], kct_ref.shape[2]
    low = lax.broadcasted_iota(jnp.int32, (1, LANES), 1) < A_HEAD_DIM
    for p in range(A_HEADS // 2):
        sl = slice(p * LANES, (p + 1) * LANES)
        qp = q[:, sl]
        kct, vct = kct_ref[0, sl, :].astype(BF16), vct_ref[0, sl, :].astype(BF16)
        outs = []
        for sub in range(2):
            h = 2 * p + sub
            qm = jnp.where(low if sub == 0 else jnp.logical_not(low), qp, jnp.zeros_like(qp))
            s_c = _dot(qm, kct) + bias_ref[h, :, :n_c]
            s_n = _dot_nt(qm, kn_ref[0, :, sl]) + bias_ref[h, :, n_c:n_c + tq]
            m = jnp.maximum(jnp.max(s_c, axis=-1, keepdims=True), jnp.max(s_n, axis=-1, keepdims=True))
            e_c, e_n = jnp.exp2(s_c - m), jnp.exp2(s_n - m)
            l = jnp.sum(e_c, axis=-1, keepdims=True) + jnp.sum(e_n, axis=-1, keepdims=True)
            acc = _dot_nt(e_c.astype(BF16), vct) + _dot(e_n.astype(BF16), vn_ref[0, :, h * LANES:(h + 1) * LANES])
            outs.append(acc / l)
        out_ref[0, :, sl] = jnp.where(low, outs[0], outs[1]).astype(BF16)


def _a_sample(q, k_cache, v_cache, k_new, v_new, bias):
    b, tq, w = q.shape
    blk = lambda a: pl.BlockSpec((1,) + a.shape[1:], lambda i: (i,) + (0,) * (a.ndim - 1))
    return pl.pallas_call(
        _a_sample_kernel,
        grid=(b,),
        in_specs=[blk(q), blk(k_cache), blk(v_cache), blk(k_new), blk(v_new), _const_spec(bias)],
        out_specs=blk(q),
        out_shape=jax.ShapeDtypeStruct((b, tq, w), BF16),
        compiler_params=_params(("arbitrary",)),
        name="a_sample",
    )(q, k_cache, v_cache, k_new, v_new, bias)


def _b_lambda(lq1, lk1, lq2, lk2, lam_init):
    s1 = jnp.sum(lq1[...] * lk1[...], axis=-1, keepdims=True)
    s2 = jnp.sum(lq2[...] * lk2[...], axis=-1, keepdims=True)
    return jnp.exp(s1) - jnp.exp(s2) + lam_init


def _b_stack_queries(q):
    lane = lax.broadcasted_iota(jnp.int32, (1, LANES), 1)
    low = lane < B_HEAD_DIM
    qs = []
    for h in range(B_HEADS):
        qh = q[:, h * LANES:(h + 1) * LANES]
        zero = jnp.zeros_like(qh)
        qs.append(jnp.concatenate([jnp.where(low, qh, zero), jnp.where(low, zero, qh)], axis=0))
    return qs


def _b_finish(o, lam, g, lam_init, tq):
    ob = o[:tq] - lam * o[tq:]
    return _rmsnorm(ob, g) * (1.0 - lam_init)


def _b_prompt_kernel(lam_init, q_ref, k_ref, v_ref, bias_ref, lq1, lk1, lq2, lk2, g_ref, out_ref,
                     qs_ref, m_ref, alpha_ref, acc_ref, s_ref):
    tq = q_ref.shape[1]
    tk = bias_ref.shape[3]
    i = pl.program_id(1)
    for h, qh in enumerate(_b_stack_queries(q_ref[0])):
        qs_ref[h] = qh

    lane_tiles = range(tk // LANES)
    lam = _b_lambda(lq1, lk1, lq2, lk2, lam_init)

    def scores(j, bias_sel, first):
        start = pl.multiple_of(j * tk, tk)
        for h in range(B_HEADS):
            sl = slice(h * LANES, (h + 1) * LANES)
            s = _dot_nt(qs_ref[h], k_ref[0, pl.ds(start, tk), sl])
            parts = [s[:, c * LANES:(c + 1) * LANES] for c in lane_tiles]
            if bias_sel is not None:
                parts = [jnp.concatenate([pt[:tq] + bias_ref[bias_sel, h, :, c * LANES:(c + 1) * LANES],
                                          pt[tq:] + bias_ref[bias_sel, h, :, c * LANES:(c + 1) * LANES]], axis=0)
                         for c, pt in enumerate(parts)]
            mx = parts[0]
            for pt in parts[1:]:
                mx = jnp.maximum(mx, pt)
            row_max = jnp.max(mx, axis=-1, keepdims=True)
            if first:
                m_new = jnp.broadcast_to(row_max, (2 * tq, LANES))
                alpha_ref[h] = jnp.zeros((2 * tq, LANES), F32)
            else:
                m_new = jnp.maximum(m_ref[h], row_max)
                alpha_ref[h] = jnp.exp2(m_ref[h] - m_new)
            m_ref[h] = m_new
            for c, pt in enumerate(parts):
                s_ref[h, :, c * LANES:(c + 1) * LANES] = pt

    def attend(j, last):
        start = pl.multiple_of(j * tk, tk)
        for h in range(B_HEADS):
            m_new = m_ref[h]
            p = jnp.concatenate([jnp.exp2(s_ref[h, :, c * LANES:(c + 1) * LANES] - m_new) for c in lane_tiles],
                                axis=1).astype(BF16)
            alpha = alpha_ref[h]
            pv = _dot(p, v_ref[0, pl.ds(start, tk), 2 * h * LANES:2 * (h + 1) * LANES])
            num = alpha * acc_ref[h, :, :LANES] + pv[:, :LANES]
            den = alpha * acc_ref[h, :, LANES:] + pv[:, LANES:]
            if last:
                bn = _b_finish(num / den, lam, g_ref[...], lam_init, tq)
                out_ref[0, :, h * LANES:(h + 1) * LANES] = bn.astype(BF16)
            else:
                acc_ref[h, :, :LANES] = num
                acc_ref[h, :, LANES:] = den

    acc_ref[...] = jnp.zeros(acc_ref.shape, F32)
    scores(i, 1, True)

    @pl.when(i >= 1)
    def _():
        attend(i, False)
        scores(i - 1, 0, False)

    def trip(t, carry):
        attend(i - t, False)
        scores(i - 1 - t, None, False)
        return carry

    lax.fori_loop(1, i, trip, 0)
    attend(0, True)


def _b_prompt(q, k, v, bias, lams, g, lam_init):
    b, s, w = q.shape
    tq = bias.shape[2]
    kv = lambda a: pl.BlockSpec((1, s, a.shape[2]), lambda i, j: (i, 0, 0))
    scratch = pltpu.VMEM((B_HEADS, 2 * tq, LANES), F32)
    return pl.pallas_call(
        functools.partial(_b_prompt_kernel, lam_init),
        grid=(b, s // tq),
        in_specs=[pl.BlockSpec((1, tq, w), lambda i, j: (i, j, 0)), kv(k), kv(v), _const_spec(bias)]
                 + [_const_spec(x) for x in lams] + [_const_spec(g)],
        out_specs=pl.BlockSpec((1, tq, w), lambda i, j: (i, j, 0)),
        out_shape=jax.ShapeDtypeStruct((b, s, w), BF16),
        scratch_shapes=[pltpu.VMEM((B_HEADS, 2 * tq, LANES), BF16), scratch, scratch,
                        pltpu.VMEM((B_HEADS, 2 * tq, 2 * LANES), F32), pltpu.VMEM((B_HEADS, 2 * tq, tq), F32)],
        compiler_params=_params(("arbitrary", "arbitrary")),
        name="b_prompt",
    )(q, k, v, bias, *lams, g)


def _b_sample_kernel(lam_init, q_ref, kc_ref, vc_ref, kn_ref, vn_ref, bias_ref, lq1, lk1, lq2, lk2, g_ref,
                     out_ref):
    tq, n_c = q_ref.shape[1], kc_ref.shape[1] // B_HEADS
    qs = _b_stack_queries(q_ref[0])
    lam = _b_lambda(lq1, lk1, lq2, lk2, lam_init)
    both = lambda bias: jnp.concatenate([bias, bias], axis=0)
    for h in range(B_HEADS):
        sl = slice(h * LANES, (h + 1) * LANES)
        head_rows = pl.ds(h, n_c, stride=B_HEADS)
        kc, vc = kc_ref[0, head_rows, :].astype(BF16), vc_ref[0, head_rows, :].astype(BF16)
        s_c = _dot_nt(qs[h], kc) + both(bias_ref[h, :, :n_c])
        s_n = _dot_nt(qs[h], kn_ref[0, :, sl]) + both(bias_ref[h, :, n_c:n_c + tq])
        m = jnp.maximum(jnp.max(s_c, axis=-1, keepdims=True), jnp.max(s_n, axis=-1, keepdims=True))
        e_c, e_n = jnp.exp2(s_c - m), jnp.exp2(s_n - m)
        l = jnp.sum(e_c, axis=-1, keepdims=True) + jnp.sum(e_n, axis=-1, keepdims=True)
        o = (_dot(e_c.astype(BF16), vc) + _dot(e_n.astype(BF16), vn_ref[0, :, 2 * h * LANES:(2 * h + 1) * LANES])) / l
        bn = _b_finish(o, lam, g_ref[...], lam_init, tq)
        out_ref[0, :, sl] = bn.astype(BF16)


def _b_sample(q, k_cache, v_cache, k_new, v_new, bias, lams, g, lam_init):
    b, tq, w = q.shape
    blk = lambda a: pl.BlockSpec((1,) + a.shape[1:], lambda i: (i,) + (0,) * (a.ndim - 1))
    return pl.pallas_call(
        functools.partial(_b_sample_kernel, lam_init),
        grid=(b,),
        in_specs=[blk(q), blk(k_cache), blk(v_cache), blk(k_new), blk(v_new), _const_spec(bias)]
                 + [_const_spec(x) for x in lams] + [_const_spec(g)],
        out_specs=blk(q),
        out_shape=jax.ShapeDtypeStruct((b, tq, w), BF16),
        compiler_params=_params(("arbitrary",)),
        name="b_sample",
    )(q, k_cache, v_cache, k_new, v_new, bias, *lams, g)


def _c_prompt_kernel(q_ref, ckv_ref, kr_ref, wk_ref, wv_ref, out_ref, k_ref, v_ref, m_ref, alpha_ref, acc_ref,
                     mask_ref, s_ref):
    tq = q_ref.shape[1]
    tk = tq
    n_lane_tiles = tk // LANES
    i = pl.program_id(1)

    @pl.when((pl.program_id(0) == 0) & (i == 0))
    def _():
        shift = CHUNK.bit_length() - 1
        rc = jnp.right_shift(lax.broadcasted_iota(jnp.int32, (tq, tk), 0), shift)
        kc = jnp.right_shift(lax.broadcasted_iota(jnp.int32, (tq, tk), 1), shift)
        mask_ref[...] = jnp.where(kc <= rc, 0.0, NEG)

    @pl.when(i == 0)
    def _():
        high = (lax.broadcasted_iota(jnp.int32, (1, LANES), 1) >= C_V).astype(F32)
        rows_per_chunk = 2 * tk

        def expand(c, carry):
            rows = pl.ds(pl.multiple_of(c * rows_per_chunk, rows_per_chunk), rows_per_chunk)
            ckv = ckv_ref[0, rows, :]
            kd = _dot(ckv, wk_ref[...])
            vd = _dot(ckv, wv_ref[...])
            kr = kr_ref[0, rows, :]
            for h in range(C_HEADS):
                sl = slice(h * LANES, (h + 1) * LANES)
                k_ref[rows, sl] = (kd[:, sl] + kr).astype(BF16)
                v_ref[rows, sl] = (vd[:, sl] + (high if h % 2 == 0 else 1.0 - high)).astype(BF16)
            return carry

        lax.fori_loop(0, k_ref.shape[0] // rows_per_chunk, expand, 0)

    def scores(j, first):
        start = pl.multiple_of(j * tk, tk)
        for h in range(C_HEADS):
            sl = slice(h * LANES, (h + 1) * LANES)
            s = _dot_nt(q_ref[0, :, sl], k_ref[pl.ds(start, tk), sl])
            parts = [s[:, c * LANES:(c + 1) * LANES] for c in range(n_lane_tiles)]
            if first:
                parts = [pt + mask_ref[:, c * LANES:(c + 1) * LANES] for c, pt in enumerate(parts)]
            mx = parts[0]
            for pt in parts[1:]:
                mx = jnp.maximum(mx, pt)
            row_max = jnp.max(mx, axis=-1, keepdims=True)
            if first:
                m_new = jnp.broadcast_to(row_max, (tq, LANES))
                alpha_ref[h] = jnp.zeros((tq, LANES), F32)
            else:
                m_new = jnp.maximum(m_ref[h], row_max)
                alpha_ref[h] = jnp.exp2(m_ref[h] - m_new)
            m_ref[h] = m_new
            for c, pt in enumerate(parts):
                s_ref[h, :, c * LANES:(c + 1) * LANES] = pt

    def attend(j, last):
        start = pl.multiple_of(j * tk, tk)
        accs = []
        for h in range(C_HEADS):
            sl = slice(h * LANES, (h + 1) * LANES)
            m_new = m_ref[h]
            p = jnp.concatenate([jnp.exp2(s_ref[h, :, c * LANES:(c + 1) * LANES] - m_new)
                                 for c in range(n_lane_tiles)], axis=1).astype(BF16)
            acc = alpha_ref[h] * acc_ref[h] + _dot(p, v_ref[pl.ds(start, tk), sl])
            if not last:
                acc_ref[h] = acc
            elif h % 2 == 0:
                accs = [acc]
            else:
                out_ref[0, :, h // 2 * LANES:(h // 2 + 1) * LANES] = _merge_head_pair(accs[0], acc).astype(BF16)

    acc_ref[...] = jnp.zeros(acc_ref.shape, F32)
    scores(i, True)

    def full(j, carry):
        attend(jnp.where(j == 0, i, j - 1), False)
        scores(j, False)
        return carry

    lax.fori_loop(0, i, full, 0)
    attend(jnp.maximum(i - 1, 0), True)


def _c_prompt(q, ckv, kr, wk, wv):
    b, s, w = q.shape
    tq = ATT_TILE
    scratch = pltpu.VMEM((C_HEADS, tq, LANES), F32)
    per_batch = lambda a: pl.BlockSpec((1, s, a.shape[2]), lambda i, j: (i, 0, 0))
    return pl.pallas_call(
        _c_prompt_kernel,
        grid=(b, s // tq),
        in_specs=[pl.BlockSpec((1, tq, w), lambda i, j: (i, j, 0)), per_batch(ckv), per_batch(kr),
                  _const_spec(wk), _const_spec(wv)],
        out_specs=pl.BlockSpec((1, tq, C_HEADS * C_V), lambda i, j: (i, j, 0)),
        out_shape=jax.ShapeDtypeStruct((b, s, C_HEADS * C_V), BF16),
        scratch_shapes=[pltpu.VMEM((s, w), BF16), pltpu.VMEM((s, w), BF16),
                        scratch, scratch, scratch, pltpu.VMEM((tq, tq), F32),
                        pltpu.VMEM((C_HEADS, tq, tq), F32)],
        compiler_params=_params(("arbitrary", "arbitrary")),
        name="c_prompt",
    )(q, ckv, kr, wk, wv)


def _c_sample_kernel(q_ref, ckv_c_ref, kr_c_ref, ckv_n_ref, kr_n_ref, wk_ref, wv_ref, mask_ref, out_ref):
    tq, n_c = q_ref.shape[1], ckv_c_ref.shape[1]
    q = q_ref[0]
    heads = [q[:, h * LANES:(h + 1) * LANES] for h in range(C_HEADS)]
    qr = jnp.concatenate(heads, axis=0)
    qa = jnp.concatenate([_dot(qh, wk_ref[h]) for h, qh in enumerate(heads)], axis=0).astype(BF16)
    ckv_c, ckv_n = ckv_c_ref[0].astype(BF16), ckv_n_ref[0]
    kr_n = kr_n_ref[0].astype(BF16)
    kr_ct = kr_c_ref[0].astype(BF16)
    kr_ct = jnp.concatenate([jnp.zeros((C_NOPE, n_c), BF16), kr_ct,
                             jnp.zeros((LANES - C_NOPE - C_ROPE, n_c), BF16)], axis=0)
    s_c = _dot_nt(qa, ckv_c) + _dot(qr, kr_ct) + mask_ref[:, :n_c]
    s_n = _dot_nt(qa, ckv_n) + _dot_nt(qr, kr_n) + mask_ref[:, n_c:n_c + tq]
    m = jnp.maximum(jnp.max(s_c, axis=-1, keepdims=True), jnp.max(s_n, axis=-1, keepdims=True))
    e_c, e_n = jnp.exp2(s_c - m), jnp.exp2(s_n - m)
    l = jnp.sum(e_c, axis=-1, keepdims=True) + jnp.sum(e_n, axis=-1, keepdims=True)
    o = ((_dot(e_c.astype(BF16), ckv_c) + _dot(e_n.astype(BF16), ckv_n)) / l).astype(BF16)
    for p in range(C_HEADS // 2):
        rows = lambda h: o[h * tq:(h + 1) * tq]
        pair = _dot(rows(2 * p), wv_ref[2 * p]) + _dot(rows(2 * p + 1), wv_ref[2 * p + 1])
        out_ref[0, :, p * LANES:(p + 1) * LANES] = pair.astype(BF16)


def _c_sample(q, ckv_cache, kr_cache, ckv_new, kr_new, wk, wv, mask):
    b, tq, _ = q.shape
    blk = lambda a: pl.BlockSpec((1,) + a.shape[1:], lambda i: (i,) + (0,) * (a.ndim - 1))
    return pl.pallas_call(
        _c_sample_kernel,
        grid=(b,),
        in_specs=[blk(q), blk(ckv_cache), blk(kr_cache), blk(ckv_new), blk(kr_new),
                  _const_spec(wk), _const_spec(wv), _const_spec(mask)],
        out_specs=pl.BlockSpec((1, tq, C_HEADS * C_V), lambda i: (i, 0, 0)),
        out_shape=jax.ShapeDtypeStruct((b, tq, C_HEADS * C_V), BF16),
        compiler_params=_params(("arbitrary",)),
        name="c_sample",
    )(q, ckv_cache, kr_cache, ckv_new, kr_new, wk, wv, mask)


def _t5_bucket(rel):
    nb = T5_BUCKETS // 2
    max_exact = nb // 2
    ret = jnp.where(rel > 0, nb, 0)
    n = jnp.abs(rel)
    n_f = jnp.maximum(n, 1).astype(F32)
    large = max_exact + (jnp.log(n_f / max_exact) / math.log(T5_MAX_DIST / max_exact)
                         * (nb - max_exact)).astype(jnp.int32)
    large = jnp.minimum(large, nb - 1)
    return ret + jnp.where(n < max_exact, n, large)


def _a_index(q_pos, k_pos, k_real):
    rel = np.clip(q_pos[:, None] - k_pos[None, :], -A_REL_CLIP, A_REL_CLIP) + A_REL_CLIP
    qc, kc = q_pos[:, None] // CHUNK, k_pos[None, :] // CHUNK
    ok = k_real[None, :] & (kc <= qc) & (kc >= qc - A_PAST_CHUNKS)
    return np.where(ok, rel, -1).astype(np.int32)


def _b_index(q_pos, k_pos, k_real):
    idx = _t5_bucket(k_pos[None, :] - q_pos[:, None])
    ok = k_real[None, :] & ((k_pos[None, :] // CHUNK) <= (q_pos[:, None] // CHUNK))
    return jnp.where(ok, idx, -1).astype(jnp.int32)


def _rope_tables(pos, scale):
    half = C_ROPE // 2
    inv = ROPE_THETA ** (-jnp.arange(half, dtype=F32) / half)
    ang = pos.astype(F32)[:, None] * inv[None, :]
    cos = jnp.concatenate([jnp.cos(ang)] * 2, axis=-1)
    sin = jnp.concatenate([jnp.sin(ang)] * 2, axis=-1)
    n = pos.shape[0]
    z_nope = jnp.zeros((n, C_NOPE), F32)
    z_tail = jnp.zeros((n, LANES - C_NOPE - C_ROPE), F32)
    cq = jnp.concatenate([jnp.full((n, C_NOPE), scale, F32), cos * scale, z_tail], axis=-1)
    sq = jnp.concatenate([z_nope, sin * scale, z_tail], axis=-1)
    ck = jnp.concatenate([z_nope, cos, z_tail], axis=-1)
    sk = jnp.concatenate([z_nope, sin, z_tail], axis=-1)
    return cq, sq, ck, sk


def _rot_cols(w):
    half = w.shape[-1] // 2
    return jnp.concatenate([-w[..., half:], w[..., :half]], axis=-1)


def kernel(x_prompt, x_sample, cache_a_k, cache_a_v, cache_b_k, cache_b_v, cache_c_kv, cache_c_kr,
           t5_bias, ffn1_norm, ffn1_w_gu, ffn1_w_down, mix_norm, ffn2_norm, ffn2_w_gu, ffn2_w_down,
           e_w_in, a_rel_bias, b_lambda_q1, b_lambda_k1, b_lambda_q2, b_lambda_k2, b_subln, e_w_out,
           c_w_in, c_q_norm, c_kv_norm, c_w_q_up, c_w_kv_up, c_w_out, final_norm):
    batch, seq, d = x_prompt.shape
    dec_batch, t_new, _ = x_sample.shape
    past = cache_b_k.shape[2]
    n_cache_a = cache_a_k.shape[2]
    a_w = A_HEADS * A_HEAD_DIM
    b_w = B_HEADS * 2 * B_HEAD_DIM
    tq = ATT_TILE
    a_pad = A_PAST_CHUNKS * CHUNK
    a_keep = min(a_pad, seq)
    assert tq + 1 >= T5_MAX_DIST and tq % CHUNK == 0 and a_pad % tq == 0
    far_bucket = T5_BUCKETS // 2 - 1
    lam_init = 0.8 - 0.6 * math.exp(-0.3 * 0)
    c_scale = (C_NOPE + C_ROPE) ** -0.5 * math.log2(math.e)
    row2 = lambda v: v.reshape(1, -1)

    e_in = e_w_in[0].astype(BF16)
    e_out_a, e_out_b = e_w_out[0, :a_w].astype(BF16), e_w_out[0, a_w:].astype(BF16)
    w_in = c_w_in[0]
    w_kr = w_in[:, C_Q_LORA + C_KV_LORA:]
    c_in = jnp.concatenate([w_in[:, :C_Q_LORA + C_KV_LORA], jnp.zeros((d, C_NOPE), F32),
                            w_kr, _rot_cols(w_kr)], axis=-1).astype(BF16)
    wq = c_w_q_up[0].reshape(C_Q_LORA, C_HEADS, C_NOPE + C_ROPE)
    wq_rope = wq[..., C_NOPE:]
    c_q = jnp.concatenate([wq[..., :C_NOPE], wq_rope, _rot_cols(wq_rope)], axis=-1)
    c_q = c_q.reshape(C_Q_LORA, C_HEADS * LANES).astype(BF16)
    wkv = c_w_kv_up[0].reshape(C_KV_LORA, C_HEADS, C_NOPE + C_V)
    c_k = jnp.concatenate([wkv[..., :C_NOPE], jnp.zeros_like(wkv[..., C_NOPE:])], axis=-1)
    c_k = c_k.reshape(C_KV_LORA, C_HEADS * LANES).astype(BF16)
    wv_pairs = wkv[..., C_NOPE:].reshape(C_KV_LORA, C_HEADS // 2, 2, C_V)
    z_v = jnp.zeros_like(wv_pairs[:, :, 0])
    c_v = jnp.stack([jnp.concatenate([wv_pairs[:, :, 0], z_v], axis=-1),
                     jnp.concatenate([z_v, wv_pairs[:, :, 1]], axis=-1)], axis=2)
    c_v = c_v.reshape(C_KV_LORA, C_HEADS * LANES).astype(BF16)
    c_k_abs = jnp.pad(jnp.transpose(wkv[..., :C_NOPE], (1, 2, 0)), ((0, 0), (0, LANES - C_NOPE), (0, 0)))
    c_k_abs = c_k_abs.astype(BF16)
    c_v_abs = jnp.transpose(c_v.reshape(C_KV_LORA, C_HEADS, LANES), (1, 0, 2))
    c_out_w = c_w_out[0].astype(BF16)
    lams = [row2(b_lambda_q1[0]), row2(b_lambda_k1[0]), row2(b_lambda_q2[0]), row2(b_lambda_k2[0])]
    subln = row2(b_subln[0])

    pos_s = past + jnp.arange(t_new)
    r = jnp.arange(tq)
    assert A_TILE % CHUNK == 0 and a_pad % A_TILE == 0
    a_idx_p, a_lay_p = _unique_blocks(
        _a_index(a_pad + np.arange(A_TILE), np.arange(a_pad + A_TILE), np.ones((a_pad + A_TILE,), bool)),
        CHUNK, LANES)
    ka = n_cache_a + t_new
    ka_pad = -(-ka // LANES) * LANES
    pos_s_np = past + np.arange(t_new)
    a_kpos = np.concatenate([past - n_cache_a + np.arange(n_cache_a), pos_s_np,
                             np.zeros((ka_pad - ka,), np.int64)])
    a_real = np.arange(ka_pad) < ka
    a_idx_s, a_lay_s = _unique_blocks(_a_index(pos_s_np, a_kpos, a_real & (a_kpos >= 0)), t_new, LANES)
    b_idx_p = jnp.stack([_b_index(tq + r, r, jnp.ones((tq,), bool)),
                         _b_index(r, r, jnp.ones((tq,), bool))])
    kb = past + t_new
    kb_pad = -(-kb // LANES) * LANES
    b_kpos = jnp.arange(kb_pad)
    b_real = b_kpos < kb
    b_idx_s = _b_index(pos_s, b_kpos, b_real)
    c_mask_s = jnp.where(b_real[None, :] & ((b_kpos[None, :] // CHUNK) <= (pos_s[:, None] // CHUNK)),
                         0.0, NEG).astype(F32)

    a_bias_p = _bias_expand(a_idx_p, a_rel_bias[0], None, a_lay_p)
    a_bias_s = _bias_expand(a_idx_s, a_rel_bias[0], None, a_lay_s)
    t5_t = t5_bias.T
    b_bias_p = jnp.stack([_bias_expand(b_idx_p[0], t5_t, far_bucket),
                          _bias_expand(b_idx_p[1], t5_t, far_bucket)])
    b_bias_s = _bias_expand(b_idx_s, t5_t, far_bucket)

    xp = x_prompt.reshape(batch * seq, d)
    xs = x_sample.reshape(dec_batch * t_new, d)
    g1, gm, g2 = row2(ffn1_norm[0]), row2(mix_norm[0]), row2(ffn2_norm[0])

    xp, xs = _ffn(xp, [], xs, [], [], g1, ffn1_w_gu, ffn1_w_down, 0)

    aq, ak, av, bq, bk, bv, akf, avf, bkf, bvf = _eproj(xp, gm, e_in, seq, a_keep)
    to3 = lambda t, n: t.reshape(n, -1, t.shape[-1])
    a_out_p = _a_prompt(to3(aq, batch), to3(ak, batch), to3(av, batch), a_bias_p)
    b_out_p = _b_prompt(to3(bq, batch), to3(bk, batch), to3(bv, batch), b_bias_p, lams, subln, lam_init)
    p_a_k = akf.reshape(1, batch, a_keep, A_HEADS, A_HEAD_DIM)
    p_a_v = avf.reshape(1, batch, a_keep, A_HEADS, A_HEAD_DIM)
    p_b_k = bkf.reshape(1, batch, seq, B_HEADS, 2 * B_HEAD_DIM)
    p_b_v = bvf.reshape(1, batch, seq, B_HEADS, 2 * B_HEAD_DIM)

    aq, ak, av, bq, bk, bv, akf, avf, bkf, bvf = _eproj(xs, gm, e_in, dec_batch * t_new, dec_batch * t_new)
    dims_major = lambda c: jnp.transpose(c, (0, 2, 3, 1)).reshape(dec_batch, a_w, n_cache_a)
    a_out_s = _a_sample(to3(aq, dec_batch), dims_major(cache_a_k[0]), dims_major(cache_a_v[0]),
                        to3(ak, dec_batch), to3(av, dec_batch), a_bias_s)
    pos_head_rows = lambda c: c.reshape(dec_batch, past * B_HEADS, 2 * B_HEAD_DIM)
    b_out_s = _b_sample(to3(bq, dec_batch), pos_head_rows(cache_b_k[0]), pos_head_rows(cache_b_v[0]),
                        to3(bk, dec_batch), to3(bv, dec_batch), b_bias_s, lams, subln, lam_init)
    s_a_k = akf.reshape(1, dec_batch, t_new, A_HEADS, A_HEAD_DIM)
    s_a_v = avf.reshape(1, dec_batch, t_new, A_HEADS, A_HEAD_DIM)
    s_b_k = bkf.reshape(1, dec_batch, t_new, B_HEADS, 2 * B_HEAD_DIM)
    s_b_v = bvf.reshape(1, dec_batch, t_new, B_HEADS, 2 * B_HEAD_DIM)
    flat = lambda t: t.reshape(-1, t.shape[-1])
    xp, xs = _ffn(xp, [flat(a_out_p), flat(b_out_p)], xs, [flat(a_out_s), flat(b_out_s)], [e_out_a, e_out_b],
                  g2, ffn2_w_gu, ffn2_w_down, 0)

    g1, gm, g2 = row2(ffn1_norm[1]), row2(mix_norm[1]), row2(ffn2_norm[1])
    gq, gkv, gfin = row2(c_q_norm[0]), row2(c_kv_norm[0]), row2(final_norm)
    xp, xs = _ffn(xp, [], xs, [], [], g1, ffn1_w_gu, ffn1_w_down, 1)

    q, ckvf, ckvb, kr, krf = _cproj(xp, gm, c_in, gq, gkv, c_q, _rope_tables(jnp.arange(seq), c_scale))
    c_out_p = _c_prompt(to3(q, batch), to3(ckvb, batch), to3(kr, batch), c_k, c_v)
    p_c_kv = ckvf.reshape(1, batch, seq, C_KV_LORA)
    p_c_kr = krf.reshape(1, batch, seq, C_ROPE)

    q, ckvf, ckvb, kr, krf = _cproj(xs, gm, c_in, gq, gkv, c_q,
                                    _rope_tables(jnp.tile(pos_s, dec_batch), c_scale))
    kr_cache = jnp.transpose(cache_c_kr[0], (0, 2, 1))
    c_out_s = _c_sample(to3(q, dec_batch), cache_c_kv[0], kr_cache, to3(ckvb, dec_batch), to3(kr, dec_batch),
                        c_k_abs, c_v_abs, jnp.tile(c_mask_s, (C_HEADS, 1)))
    s_c_kv = ckvf.reshape(1, dec_batch, t_new, C_KV_LORA)
    s_c_kr = krf.reshape(1, dec_batch, t_new, C_ROPE)
    y_prompt, y_sample = _ffn(xp, [flat(c_out_p)], xs, [flat(c_out_s)], [c_out_w], g2, ffn2_w_gu, ffn2_w_down,
                              1, gfin)

    return (y_prompt.reshape(batch, seq, d), y_sample.reshape(dec_batch, t_new, d),
            p_a_k, p_a_v, p_b_k, p_b_v, p_c_kv, p_c_kr,
            s_a_k, s_a_v, s_b_k, s_b_v, s_c_kv, s_c_kr)
```

```python
import functools
import math

import jax
import jax.numpy as jnp
import numpy as np
from jax import lax
from jax.experimental import pallas as pl
from jax.experimental.pallas import tpu as pltpu

F32 = jnp.float32
BF16 = jnp.bfloat16

EPS = 1e-6
NEG = -1e30
LOG2E = math.log2(math.e)
CHUNK = 64
A_HEADS = 8
A_HEAD_DIM = 64
A_PAST_CHUNKS = 8
A_REL_CLIP = 64
B_HEADS = 4
B_HEAD_DIM = 64
T5_BUCKETS = 32
T5_MAX_DIST = 128
C_HEADS = 16
C_NOPE = 64
C_ROPE = 32
C_V = 64
C_Q_LORA = 384
C_KV_LORA = 256
ROPE_THETA = 10000.0

LANES = 128
SUBLANES = 8
VMEM_LIMIT = 56 * 1024 * 1024
ROW_TILE = 512
ATT_TILE = 256
A_TILE = 256
BIAS_LOOP_UNROLL = 8
BIAS_GROUP_VREGS = 32
TOKEN_SUBTILES = 2
MIN_SUBTILE_ROWS = 256
WEIGHT_CAST_CHUNKS = 8
MXU_WIDTH = 256
FFN_CHUNKS = 2


def _subtile_rows(tm):
    sub = tm // TOKEN_SUBTILES
    return sub if sub * TOKEN_SUBTILES == tm and sub >= MIN_SUBTILE_ROWS else tm


def _ffn_chunks(d_ff):
    tiles = d_ff // MXU_WIDTH
    assert tiles * MXU_WIDTH == d_ff
    per = -(-tiles // FFN_CHUNKS)
    edges = [min(c * per, tiles) * MXU_WIDTH for c in range(FFN_CHUNKS + 1)]
    return [(lo, hi) for lo, hi in zip(edges[:-1], edges[1:]) if hi > lo]


def _rmsnorm(x, g):
    return x * lax.rsqrt(jnp.mean(x * x, axis=-1, keepdims=True) + EPS) * g


def _dot(a, b):
    return jnp.dot(a, b, preferred_element_type=F32)


def _dot_nt(a, b):
    return lax.dot_general(a, b, (((1,), (1,)), ((), ())), preferred_element_type=F32)


def _const_spec(a):
    nd = a.ndim
    return pl.BlockSpec(a.shape, lambda *_: (0,) * nd, pipeline_mode=pl.Buffered(1))


def _params(sem, flags=None):
    return pltpu.CompilerParams(dimension_semantics=sem, vmem_limit_bytes=VMEM_LIMIT, flags=flags)


def _weight_chunk_copy(src, stage, sems, c, slot):
    rows = stage.shape[1]
    return pltpu.make_async_copy(src.at[pl.ds(c * rows, rows), :], stage.at[slot], sems.at[slot])


def _load_weight_as_bf16(src, stage, sems, dst):
    rows = stage.shape[1]
    n = src.shape[0] // rows
    _weight_chunk_copy(src, stage, sems, 0, 0).start()
    for c in range(n):
        slot = c % 2
        if c + 1 < n:
            _weight_chunk_copy(src, stage, sems, c + 1, 1 - slot).start()
        _weight_chunk_copy(src, stage, sems, c, slot).wait()
        dst[c * rows:(c + 1) * rows, :] = stage[slot].astype(BF16)


def _ffn_kernel(n_pre, with_final, layer, *refs):
    group = 1 + n_pre
    groups = [refs[:group], refs[group:2 * group]]
    idx = 2 * group
    pre_w = refs[idx:idx + n_pre]
    idx += n_pre
    g_ref, wgu_in, wd_in = refs[idx:idx + 3]
    idx += 3
    gf_ref = refs[idx] if with_final else None
    idx += with_final
    outs = refs[idx:idx + 2]
    wgu_ref, wd_ref, gu_stage, d_stage, sems = refs[idx + 2:]
    step, last = pl.program_id(0), pl.num_programs(0) - 1

    @pl.when(step == 0)
    def _():
        _load_weight_as_bf16(wgu_in.at[layer], gu_stage, sems, wgu_ref)
        _load_weight_as_bf16(wd_in.at[layer], d_stage, sems, wd_ref)

    def run(x_ref, o_refs, out_ref):
        x = x_ref[...]
        for o_ref, w_ref in zip(o_refs, pre_w):
            x = x + _dot(o_ref[...], w_ref[...])
        xn = _rmsnorm(x, g_ref[...]).astype(BF16)
        d_ff = wd_ref.shape[0]
        acc = jnp.zeros_like(x)
        for lo, hi in _ffn_chunks(d_ff):
            gate = _dot(xn, wgu_ref[:, lo:hi])
            up = _dot(xn, wgu_ref[:, d_ff + lo:d_ff + hi])
            act = (jax.nn.silu(gate) * up).astype(BF16)
            acc = acc + _dot(act, wd_ref[lo:hi, :])
        y = x + 0.5 * acc
        if with_final:
            y = _rmsnorm(y, gf_ref[...])
        out_ref[...] = y

    @pl.when(step < last)
    def _():
        run(groups[0][0], groups[0][1:], outs[0])

    @pl.when(step == last)
    def _():
        run(groups[1][0], groups[1][1:], outs[1])


def _ffn(x_prompt, pres_prompt, x_sample, pres_sample, pre_weights, g, wgu, wd, layer, g_final=None):
    t, d = x_prompt.shape
    tm = min(ROW_TILE, t)
    n = t // tm
    row = lambda w: pl.BlockSpec((tm, w), lambda i: (jnp.minimum(i, n - 1), 0))
    whole = lambda a: pl.BlockSpec(a.shape, lambda i: (0, 0))
    hbm = pl.BlockSpec(memory_space=pl.ANY)
    args = [x_prompt, *pres_prompt, x_sample, *pres_sample, *pre_weights, g, wgu, wd]
    specs = ([row(d)] + [row(o.shape[1]) for o in pres_prompt] + [whole(x_sample)] + [whole(o) for o in pres_sample]
             + [_const_spec(w) for w in pre_weights] + [_const_spec(g), hbm, hbm])
    if g_final is not None:
        args.append(g_final)
        specs.append(_const_spec(g_final))
    gu_shape, d_shape = wgu.shape[1:], wd.shape[1:]
    chunks = WEIGHT_CAST_CHUNKS
    return pl.pallas_call(
        functools.partial(_ffn_kernel, len(pre_weights), g_final is not None, layer),
        grid=(n + 1,),
        in_specs=specs,
        out_specs=[row(d), whole(x_sample)],
        out_shape=[jax.ShapeDtypeStruct((t, d), F32), jax.ShapeDtypeStruct(x_sample.shape, F32)],
        scratch_shapes=[pltpu.VMEM(gu_shape, BF16), pltpu.VMEM(d_shape, BF16),
                        pltpu.VMEM((2, gu_shape[0] // chunks, gu_shape[1]), F32),
                        pltpu.VMEM((2, d_shape[0] // chunks, d_shape[1]), F32),
                        pltpu.SemaphoreType.DMA((2,))],
        compiler_params=_params(("arbitrary",)),
        name="ffn",
    )(*args)


def _store_values_with_ones(out_ref, v):
    low = lax.broadcasted_iota(jnp.int32, (1, LANES), 1) < LANES // 2
    for p in range(v.shape[1] // LANES):
        vp = v[:, p * LANES:(p + 1) * LANES]
        out_ref[:, 2 * p * LANES:(2 * p + 1) * LANES] = jnp.where(low, vp, 1.0).astype(out_ref.dtype)
        out_ref[:, (2 * p + 1) * LANES:(2 * p + 2) * LANES] = jnp.where(low, 1.0, vp).astype(out_ref.dtype)


def _merge_head_pair(acc_even, acc_odd):
    half = LANES // 2
    low = lax.broadcasted_iota(jnp.int32, (1, LANES), 1) < half
    den = pltpu.roll(jnp.where(low, acc_odd, acc_even), half, 1)
    return jnp.where(low, acc_even, acc_odd) / den


def _eproj_kernel(x_ref, g_ref, w_ref, aq, ak, av, bq, bk, bv, akf, avf, bkf, bvf):
    tm = x_ref.shape[0]
    sub = _subtile_rows(tm)
    w = aq.shape[1]
    for r0 in range(0, tm, sub):
        rows = slice(r0, r0 + sub)
        hn = _rmsnorm(x_ref[rows, :], g_ref[...]).astype(BF16)
        t = _dot(hn, w_ref[...])
        parts = [t[:, i * w:(i + 1) * w] for i in range(6)]
        aq[rows, :] = (parts[0] * (A_HEAD_DIM ** -0.5 * LOG2E)).astype(BF16)
        ak[rows, :] = parts[1].astype(BF16)
        _store_values_with_ones(av.at[rows, :], parts[2])
        bq[rows, :] = (parts[3] * (B_HEAD_DIM ** -0.5 * LOG2E)).astype(BF16)
        bk[rows, :] = parts[4].astype(BF16)
        ones = jnp.ones((sub, LANES), BF16)
        for h in range(B_HEADS):
            bv[rows, 2 * h * LANES:(2 * h + 1) * LANES] = parts[5][:, h * LANES:(h + 1) * LANES].astype(BF16)
            bv[rows, (2 * h + 1) * LANES:(2 * h + 2) * LANES] = ones
        akf[rows] = parts[1].reshape((sub,) + akf.shape[1:])
        avf[rows] = parts[2].reshape((sub,) + avf.shape[1:])
        bkf[rows] = parts[4].reshape((sub,) + bkf.shape[1:])
        bvf[rows] = parts[5].reshape((sub,) + bvf.shape[1:])


def _eproj(x, g, w, rows_per_batch, a_keep):
    t, d = x.shape
    tm = min(ROW_TILE, t, a_keep)
    assert rows_per_batch % tm == 0 and a_keep % tm == 0
    per, kept = rows_per_batch // tm, a_keep // tm
    wd = w.shape[1] // 6
    row = lambda n: pl.BlockSpec((tm, n), lambda i: (i, 0))
    a_map = lambda i: ((i // per) * kept + jnp.maximum(i % per - (per - kept), 0), 0, 0)
    a_f32 = pl.BlockSpec((tm, A_HEADS, A_HEAD_DIM), a_map)
    b_f32 = pl.BlockSpec((tm, B_HEADS, 2 * B_HEAD_DIM), lambda i: (i, 0, 0))
    widths = [wd, wd, 2 * wd, wd, wd, 2 * wd]
    shapes = ([jax.ShapeDtypeStruct((t, n), BF16) for n in widths]
              + [jax.ShapeDtypeStruct((t // per * kept, A_HEADS, A_HEAD_DIM), F32)] * 2
              + [jax.ShapeDtypeStruct((t, B_HEADS, 2 * B_HEAD_DIM), F32)] * 2)
    return pl.pallas_call(
        _eproj_kernel,
        grid=(t // tm,),
        in_specs=[row(d), _const_spec(g), _const_spec(w)],
        out_specs=[row(n) for n in widths] + [a_f32, a_f32, b_f32, b_f32],
        out_shape=shapes,
        compiler_params=_params(("arbitrary",)),
        name="eproj",
    )(x, g, w)


def _cproj_kernel(x_ref, g_ref, win_ref, gq_ref, gkv_ref, wq_ref, cq_ref, sq_ref, ck_ref, sk_ref,
                  q_out, ckvf_out, ckvb_out, kr_out, krf_out):
    tm = x_ref.shape[0]
    sub = _subtile_rows(tm)
    for r0 in range(0, tm, sub):
        rows = slice(r0, r0 + sub)
        hn = _rmsnorm(x_ref[rows, :], g_ref[...]).astype(BF16)
        t = _dot(hn, win_ref[...])
        cqn = _rmsnorm(t[:, :C_Q_LORA], gq_ref[...]).astype(BF16)
        ckv = _rmsnorm(t[:, C_Q_LORA:C_Q_LORA + C_KV_LORA], gkv_ref[...])
        ckvf_out[rows, :] = ckv
        ckvb_out[rows, :] = ckv.astype(BF16)
        tg = t[:, C_Q_LORA + C_KV_LORA:]
        kr = tg * ck_ref[rows, :] + pltpu.roll(tg, LANES - C_ROPE, 1) * sk_ref[rows, :]
        kr_out[rows, :] = kr
        krf_out[rows, :] = kr[:, C_NOPE:C_NOPE + C_ROPE]
        q = _dot(cqn, wq_ref[...])
        cq, sq = cq_ref[rows, :], sq_ref[rows, :]
        for h in range(C_HEADS):
            qh = q[:, h * LANES:(h + 1) * LANES]
            qf = qh * cq + pltpu.roll(qh, LANES - C_ROPE, 1) * sq
            q_out[rows, h * LANES:(h + 1) * LANES] = qf.astype(BF16)


def _cproj(x, g, win, gq, gkv, wq, tabs):
    t, d = x.shape
    tm = min(ROW_TILE, t)
    nper = tabs[0].shape[0] // tm
    row = lambda n: pl.BlockSpec((tm, n), lambda i: (i, 0))
    tab = pl.BlockSpec((tm, LANES), lambda i: (i % nper, 0))
    shapes = [jax.ShapeDtypeStruct((t, C_HEADS * LANES), BF16),
              jax.ShapeDtypeStruct((t, C_KV_LORA), F32),
              jax.ShapeDtypeStruct((t, C_KV_LORA), BF16),
              jax.ShapeDtypeStruct((t, LANES), F32),
              jax.ShapeDtypeStruct((t, C_ROPE), F32)]
    return pl.pallas_call(
        _cproj_kernel,
        grid=(t // tm,),
        in_specs=[row(d), _const_spec(g), _const_spec(win), _const_spec(gq), _const_spec(gkv),
                  _const_spec(wq), tab, tab, tab, tab],
        out_specs=[row(C_HEADS * LANES), row(C_KV_LORA), row(C_KV_LORA), row(LANES), row(C_ROPE)],
        out_shape=shapes,
        compiler_params=_params(("arbitrary",)),
        name="cproj",
    )(x, g, win, gq, gkv, wq, *tabs)


def _bias_kernel(n_rows, shift, layout, idx_ref, tab_ref, out_ref):
    n_heads = out_ref.shape[0]
    bq = out_ref.shape[1] // len(layout)
    bk = out_ref.shape[2] // len(layout[0])
    block_vregs = max(1, bq * bk // (SUBLANES * LANES))
    group = max(1, min(n_heads, BIAS_GROUP_VREGS // block_vregs))
    for u in range(idx_ref.shape[0] // bq):
        idx = idx_ref[u * bq:(u + 1) * bq, :]
        for h0 in range(0, n_heads, group):
            heads = list(range(h0, min(h0 + group, n_heads)))
            bases = [tab_ref[h, shift] if shift is not None else 0.0 for h in heads]

            def body(r, accs):
                hit = idx == r
                return tuple(jnp.where(hit, (tab_ref[h, r] - b) * LOG2E, acc)
                             for h, b, acc in zip(heads, bases, accs))

            vals = lax.fori_loop(0, n_rows, body, tuple(jnp.full(idx.shape, NEG, F32) for _ in heads),
                                 unroll=BIAS_LOOP_UNROLL)
            for h, val in zip(heads, vals):
                for a, row in enumerate(layout):
                    for t, uu in enumerate(row):
                        if uu == u:
                            out_ref[h, a * bq:(a + 1) * bq, t * bk:(t + 1) * bk] = val


def _bias_expand(idx, table, shift=None, layout=((0,),)):
    n_heads, n_rows = table.shape
    n_blocks = 1 + max(max(row) for row in layout)
    out_shape = (n_heads, idx.shape[0] // n_blocks * len(layout), idx.shape[1] * len(layout[0]))
    return pl.pallas_call(
        functools.partial(_bias_kernel, n_rows, shift, layout),
        in_specs=[pl.BlockSpec(idx.shape, lambda: (0, 0)),
                  pl.BlockSpec(memory_space=pltpu.SMEM)],
        out_specs=pl.BlockSpec(out_shape, lambda: (0, 0, 0)),
        out_shape=jax.ShapeDtypeStruct(out_shape, F32),
        compiler_params=pltpu.CompilerParams(vmem_limit_bytes=VMEM_LIMIT),
        name="bias_expand",
    )(idx, table)


def _unique_blocks(idx, bq, bk):
    q, k = idx.shape
    blocks = idx.reshape(q // bq, bq, k // bk, bk).transpose(0, 2, 1, 3).reshape(-1, bq, bk)
    uniq, inv = np.unique(blocks, axis=0, return_inverse=True)
    layout = tuple(tuple(int(u) for u in row) for row in inv.reshape(q // bq, k // bk))
    return jnp.asarray(uniq.reshape(-1, bk), jnp.int32), layout


def _a_attend(q, kwin, vwin, bias_ref, valid, out_ref, s_ref):
    low = lax.broadcasted_iota(jnp.int32, (1, LANES), 1) < A_HEAD_DIM
    for h in range(A_HEADS):
        sl = slice(h // 2 * LANES, (h // 2 + 1) * LANES)
        qp = q[:, sl]
        qm = jnp.where(low if h % 2 == 0 else jnp.logical_not(low), qp, jnp.zeros_like(qp))
        s = _dot_nt(qm, kwin[:, sl]) + bias_ref[h]
        if valid is not None:
            s = jnp.where(valid, s, NEG)
        s_ref[h] = s - jnp.max(s, axis=-1, keepdims=True)
    accs = []
    for h in range(A_HEADS):
        accs.append(_dot(jnp.exp2(s_ref[h]).astype(BF16), vwin[:, h * LANES:(h + 1) * LANES]))
        if h % 2 == 1:
            out_ref[0, :, h // 2 * LANES:(h // 2 + 1) * LANES] = _merge_head_pair(*accs).astype(BF16)
            accs = []


def _a_prompt_kernel(q_ref, k_ref, v_ref, bias_ref, out_ref, s_ref):
    tq = q_ref.shape[1]
    win = bias_ref.shape[2]
    n_tiles = win // tq
    j = pl.program_id(1)

    def run(check_positions):
        ks, vs = [], []
        for t in range(n_tiles):
            start = pl.multiple_of(jnp.maximum(j - (n_tiles - 1 - t), 0) * tq, tq)
            ks.append(k_ref[0, pl.ds(start, tq), :])
            vs.append(v_ref[0, pl.ds(start, tq), :])
        valid = None
        if check_positions:
            valid = lax.broadcasted_iota(jnp.int32, (1, win), 1) + (j - (n_tiles - 1)) * tq >= 0
        _a_attend(q_ref[0], jnp.concatenate(ks, axis=0), jnp.concatenate(vs, axis=0), bias_ref, valid,
                  out_ref, s_ref)

    @pl.when(j < n_tiles - 1)
    def _():
        run(True)

    @pl.when(j >= n_tiles - 1)
    def _():
        run(False)


def _a_prompt(q, k, v, bias):
    b, s, w = q.shape
    tq = bias.shape[1]
    kv = lambda a: pl.BlockSpec((1, s, a.shape[2]), lambda i, j: (i, 0, 0))
    return pl.pallas_call(
        _a_prompt_kernel,
        grid=(b, s // tq),
        in_specs=[pl.BlockSpec((1, tq, w), lambda i, j: (i, j, 0)), kv(k), kv(v), _const_spec(bias)],
        out_specs=pl.BlockSpec((1, tq, w), lambda i, j: (i, j, 0)),
        out_shape=jax.ShapeDtypeStruct((b, s, w), BF16),
        scratch_shapes=[pltpu.VMEM(bias.shape, F32)],
        compiler_params=_params(("arbitrary", "arbitrary")),
        name="a_prompt",
    )(q, k, v, bias)


def _a_sample_kernel(q_ref, kct_ref, vct_ref, kn_ref, vn_ref, bias_ref, out_ref):
    q = q_ref[0]
    tq, n_c = q.shape[0], kct_ref.shape[2]
    low = lax.broadcasted_iota(jnp.int32, (1, LANES), 1) < A_HEAD_DIM
    for p in range(A_HEADS // 2):
        sl = slice(p * LANES, (p + 1) * LANES)
        qp = q[:, sl]
        kct, vct = kct_ref[0, sl, :].astype(BF16), vct_ref[0, sl, :].astype(BF16)
        outs = []
        for sub in range(2):
            h = 2 * p + sub
            qm = jnp.where(low if sub == 0 else jnp.logical_not(low), qp, jnp.zeros_like(qp))
            s_c = _dot(qm, kct) + bias_ref[h, :, :n_c]
            s_n = _dot_nt(qm, kn_ref[0, :, sl]) + bias_ref[h, :, n_c:n_c + tq]
            m = jnp.maximum(jnp.max(s_c, axis=-1, keepdims=True), jnp.max(s_n, axis=-1, keepdims=True))
            e_c, e_n = jnp.exp2(s_c - m), jnp.exp2(s_n - m)
            l = jnp.sum(e_c, axis=-1, keepdims=True) + jnp.sum(e_n, axis=-1, keepdims=True)
            acc = _dot_nt(e_c.astype(BF16), vct) + _dot(e_n.astype(BF16), vn_ref[0, :, h * LANES:(h + 1) * LANES])
            outs.append(acc / l)
        out_ref[0, :, sl] = jnp.where(low, outs[0], outs[1]).astype(BF16)


def _a_sample(q, k_cache, v_cache, k_new, v_new, bias):
    b, tq, w = q.shape
    blk = lambda a: pl.BlockSpec((1,) + a.shape[1:], lambda i: (i,) + (0,) * (a.ndim - 1))
    return pl.pallas_call(
        _a_sample_kernel,
        grid=(b,),
        in_specs=[blk(q), blk(k_cache), blk(v_cache), blk(k_new), blk(v_new), _const_spec(bias)],
        out_specs=blk(q),
        out_shape=jax.ShapeDtypeStruct((b, tq, w), BF16),
        compiler_params=_params(("arbitrary",)),
        name="a_sample",
    )(q, k_cache, v_cache, k_new, v_new, bias)


def _b_lambda(lq1, lk1, lq2, lk2, lam_init):
    s1 = jnp.sum(lq1[...] * lk1[...], axis=-1, keepdims=True)
    s2 = jnp.sum(lq2[...] * lk2[...], axis=-1, keepdims=True)
    return jnp.exp(s1) - jnp.exp(s2) + lam_init


def _b_stack_queries(q):
    lane = lax.broadcasted_iota(jnp.int32, (1, LANES), 1)
    low = lane < B_HEAD_DIM
    qs = []
    for h in range(B_HEADS):
        qh = q[:, h * LANES:(h + 1) * LANES]
        zero = jnp.zeros_like(qh)
        qs.append(jnp.concatenate([jnp.where(low, qh, zero), jnp.where(low, zero, qh)], axis=0))
    return qs


def _b_finish(o, lam, g, lam_init, tq):
    ob = o[:tq] - lam * o[tq:]
    return _rmsnorm(ob, g) * (1.0 - lam_init)


def _b_prompt_kernel(lam_init, q_ref, k_ref, v_ref, bias_ref, lq1, lk1, lq2, lk2, g_ref, out_ref,
                     qs_ref, m_ref, alpha_ref, acc_ref, s_ref):
    tq = q_ref.shape[1]
    tk = bias_ref.shape[3]
    i = pl.program_id(1)
    for h, qh in enumerate(_b_stack_queries(q_ref[0])):
        qs_ref[h] = qh

    lane_tiles = range(tk // LANES)
    lam = _b_lambda(lq1, lk1, lq2, lk2, lam_init)

    def scores(j, bias_sel, first):
        start = pl.multiple_of(j * tk, tk)
        for h in range(B_HEADS):
            sl = slice(h * LANES, (h + 1) * LANES)
            s = _dot_nt(qs_ref[h], k_ref[0, pl.ds(start, tk), sl])
            parts = [s[:, c * LANES:(c + 1) * LANES] for c in lane_tiles]
            if bias_sel is not None:
                parts = [jnp.concatenate([pt[:tq] + bias_ref[bias_sel, h, :, c * LANES:(c + 1) * LANES],
                                          pt[tq:] + bias_ref[bias_sel, h, :, c * LANES:(c + 1) * LANES]], axis=0)
                         for c, pt in enumerate(parts)]
            mx = parts[0]
            for pt in parts[1:]:
                mx = jnp.maximum(mx, pt)
            row_max = jnp.max(mx, axis=-1, keepdims=True)
            if first:
                m_new = jnp.broadcast_to(row_max, (2 * tq, LANES))
                alpha_ref[h] = jnp.zeros((2 * tq, LANES), F32)
            else:
                m_new = jnp.maximum(m_ref[h], row_max)
                alpha_ref[h] = jnp.exp2(m_ref[h] - m_new)
            m_ref[h] = m_new
            for c, pt in enumerate(parts):
                s_ref[h, :, c * LANES:(c + 1) * LANES] = pt

    def attend(j, last):
        start = pl.multiple_of(j * tk, tk)
        for h in range(B_HEADS):
            m_new = m_ref[h]
            p = jnp.concatenate([jnp.exp2(s_ref[h, :, c * LANES:(c + 1) * LANES] - m_new) for c in lane_tiles],
                                axis=1).astype(BF16)
            alpha = alpha_ref[h]
            pv = _dot(p, v_ref[0, pl.ds(start, tk), 2 * h * LANES:2 * (h + 1) * LANES])
            num = alpha * acc_ref[h, :, :LANES] + pv[:, :LANES]
            den = alpha * acc_ref[h, :, LANES:] + pv[:, LANES:]
            if last:
                bn = _b_finish(num / den, lam, g_ref[...], lam_init, tq)
                out_ref[0, :, h * LANES:(h + 1) * LANES] = bn.astype(BF16)
            else:
                acc_ref[h, :, :LANES] = num
                acc_ref[h, :, LANES:] = den

    acc_ref[...] = jnp.zeros(acc_ref.shape, F32)
    scores(i, 1, True)

    @pl.when(i >= 1)
    def _():
        attend(i, False)
        scores(i - 1, 0, False)

    def trip(t, carry):
        attend(i - t, False)
        scores(i - 1 - t, None, False)
        return carry

    lax.fori_loop(1, i, trip, 0)
    attend(0, True)


def _b_prompt(q, k, v, bias, lams, g, lam_init):
    b, s, w = q.shape
    tq = bias.shape[2]
    kv = lambda a: pl.BlockSpec((1, s, a.shape[2]), lambda i, j: (i, 0, 0))
    scratch = pltpu.VMEM((B_HEADS, 2 * tq, LANES), F32)
    return pl.pallas_call(
        functools.partial(_b_prompt_kernel, lam_init),
        grid=(b, s // tq),
        in_specs=[pl.BlockSpec((1, tq, w), lambda i, j: (i, j, 0)), kv(k), kv(v), _const_spec(bias)]
                 + [_const_spec(x) for x in lams] + [_const_spec(g)],
        out_specs=pl.BlockSpec((1, tq, w), lambda i, j: (i, j, 0)),
        out_shape=jax.ShapeDtypeStruct((b, s, w), BF16),
        scratch_shapes=[pltpu.VMEM((B_HEADS, 2 * tq, LANES), BF16), scratch, scratch,
                        pltpu.VMEM((B_HEADS, 2 * tq, 2 * LANES), F32), pltpu.VMEM((B_HEADS, 2 * tq, tq), F32)],
        compiler_params=_params(("arbitrary", "arbitrary")),
        name="b_prompt",
    )(q, k, v, bias, *lams, g)


def _b_sample_kernel(lam_init, q_ref, kc_ref, vc_ref, kn_ref, vn_ref, bias_ref, lq1, lk1, lq2, lk2, g_ref,
                     out_ref):
    tq, n_c = q_ref.shape[1], kc_ref.shape[1] // B_HEADS
    qs = _b_stack_queries(q_ref[0])
    lam = _b_lambda(lq1, lk1, lq2, lk2, lam_init)
    both = lambda bias: jnp.concatenate([bias, bias], axis=0)
    for h in range(B_HEADS):
        sl = slice(h * LANES, (h + 1) * LANES)
        head_rows = pl.ds(h, n_c, stride=B_HEADS)
        kc, vc = kc_ref[0, head_rows, :].astype(BF16), vc_ref[0, head_rows, :].astype(BF16)
        s_c = _dot_nt(qs[h], kc) + both(bias_ref[h, :, :n_c])
        s_n = _dot_nt(qs[h], kn_ref[0, :, sl]) + both(bias_ref[h, :, n_c:n_c + tq])
        m = jnp.maximum(jnp.max(s_c, axis=-1, keepdims=True), jnp.max(s_n, axis=-1, keepdims=True))
        e_c, e_n = jnp.exp2(s_c - m), jnp.exp2(s_n - m)
        l = jnp.sum(e_c, axis=-1, keepdims=True) + jnp.sum(e_n, axis=-1, keepdims=True)
        o = (_dot(e_c.astype(BF16), vc) + _dot(e_n.astype(BF16), vn_ref[0, :, 2 * h * LANES:(2 * h + 1) * LANES])) / l
        bn = _b_finish(o, lam, g_ref[...], lam_init, tq)
        out_ref[0, :, sl] = bn.astype(BF16)


def _b_sample(q, k_cache, v_cache, k_new, v_new, bias, lams, g, lam_init):
    b, tq, w = q.shape
    blk = lambda a: pl.BlockSpec((1,) + a.shape[1:], lambda i: (i,) + (0,) * (a.ndim - 1))
    return pl.pallas_call(
        functools.partial(_b_sample_kernel, lam_init),
        grid=(b,),
        in_specs=[blk(q), blk(k_cache), blk(v_cache), blk(k_new), blk(v_new), _const_spec(bias)]
                 + [_const_spec(x) for x in lams] + [_const_spec(g)],
        out_specs=blk(q),
        out_shape=jax.ShapeDtypeStruct((b, tq, w), BF16),
        compiler_params=_params(("arbitrary",)),
        name="b_sample",
    )(q, k_cache, v_cache, k_new, v_new, bias, *lams, g)


def _c_prompt_kernel(q_ref, ckv_ref, kr_ref, wk_ref, wv_ref, out_ref, k_ref, v_ref, m_ref, alpha_ref, acc_ref,
                     mask_ref, s_ref):
    tq = q_ref.shape[1]
    tk = tq
    n_lane_tiles = tk // LANES
    i = pl.program_id(1)

    @pl.when((pl.program_id(0) == 0) & (i == 0))
    def _():
        shift = CHUNK.bit_length() - 1
        rc = jnp.right_shift(lax.broadcasted_iota(jnp.int32, (tq, tk), 0), shift)
        kc = jnp.right_shift(lax.broadcasted_iota(jnp.int32, (tq, tk), 1), shift)
        mask_ref[...] = jnp.where(kc <= rc, 0.0, NEG)

    @pl.when(i == 0)
    def _():
        high = (lax.broadcasted_iota(jnp.int32, (1, LANES), 1) >= C_V).astype(F32)
        rows_per_chunk = 2 * tk

        def expand(c, carry):
            rows = pl.ds(pl.multiple_of(c * rows_per_chunk, rows_per_chunk), rows_per_chunk)
            ckv = ckv_ref[0, rows, :]
            kd = _dot(ckv, wk_ref[...])
            vd = _dot(ckv, wv_ref[...])
            kr = kr_ref[0, rows, :]
            for h in range(C_HEADS):
                sl = slice(h * LANES, (h + 1) * LANES)
                k_ref[rows, sl] = (kd[:, sl] + kr).astype(BF16)
                v_ref[rows, sl] = (vd[:, sl] + (high if h % 2 == 0 else 1.0 - high)).astype(BF16)
            return carry

        lax.fori_loop(0, k_ref.shape[0] // rows_per_chunk, expand, 0)

    def scores(j, first):
        start = pl.multiple_of(j * tk, tk)
        for h in range(C_HEADS):
            sl = slice(h * LANES, (h + 1) * LANES)
            s = _dot_nt(q_ref[0, :, sl], k_ref[pl.ds(start, tk), sl])
            parts = [s[:, c * LANES:(c + 1) * LANES] for c in range(n_lane_tiles)]
            if first:
                parts = [pt + mask_ref[:, c * LANES:(c + 1) * LANES] for c, pt in enumerate(parts)]
            mx = parts[0]
            for pt in parts[1:]:
                mx = jnp.maximum(mx, pt)
            row_max = jnp.max(mx, axis=-1, keepdims=True)
            if first:
                m_new = jnp.broadcast_to(row_max, (tq, LANES))
                alpha_ref[h] = jnp.zeros((tq, LANES), F32)
            else:
                m_new = jnp.maximum(m_ref[h], row_max)
                alpha_ref[h] = jnp.exp2(m_ref[h] - m_new)
            m_ref[h] = m_new
            for c, pt in enumerate(parts):
                s_ref[h, :, c * LANES:(c + 1) * LANES] = pt

    def attend(j, last):
        start = pl.multiple_of(j * tk, tk)
        accs = []
        for h in range(C_HEADS):
            sl = slice(h * LANES, (h + 1) * LANES)
            m_new = m_ref[h]
            p = jnp.concatenate([jnp.exp2(s_ref[h, :, c * LANES:(c + 1) * LANES] - m_new)
                                 for c in range(n_lane_tiles)], axis=1).astype(BF16)
            acc = alpha_ref[h] * acc_ref[h] + _dot(p, v_ref[pl.ds(start, tk), sl])
            if not last:
                acc_ref[h] = acc
            elif h % 2 == 0:
                accs = [acc]
            else:
                out_ref[0, :, h // 2 * LANES:(h // 2 + 1) * LANES] = _merge_head_pair(accs[0], acc).astype(BF16)

    acc_ref[...] = jnp.zeros(acc_ref.shape, F32)
    scores(i, True)

    def full(j, carry):
        attend(jnp.where(j == 0, i, j - 1), False)
        scores(j, False)
        return carry

    lax.fori_loop(0, i, full, 0)
    attend(jnp.maximum(i - 1, 0), True)


def _c_prompt(q, ckv, kr, wk, wv):
    b, s, w = q.shape
    tq = ATT_TILE
    scratch = pltpu.VMEM((C_HEADS, tq, LANES), F32)
    per_batch = lambda a: pl.BlockSpec((1, s, a.shape[2]), lambda i, j: (i, 0, 0))
    return pl.pallas_call(
        _c_prompt_kernel,
        grid=(b, s // tq),
        in_specs=[pl.BlockSpec((1, tq, w), lambda i, j: (i, j, 0)), per_batch(ckv), per_batch(kr),
                  _const_spec(wk), _const_spec(wv)],
        out_specs=pl.BlockSpec((1, tq, C_HEADS * C_V), lambda i, j: (i, j, 0)),
        out_shape=jax.ShapeDtypeStruct((b, s, C_HEADS * C_V), BF16),
        scratch_shapes=[pltpu.VMEM((s, w), BF16), pltpu.VMEM((s, w), BF16),
                        scratch, scratch, scratch, pltpu.VMEM((tq, tq), F32),
                        pltpu.VMEM((C_HEADS, tq, tq), F32)],
        compiler_params=_params(("arbitrary", "arbitrary")),
        name="c_prompt",
    )(q, ckv, kr, wk, wv)


def _c_sample_kernel(q_ref, ckv_c_ref, kr_c_ref, ckv_n_ref, kr_n_ref, wk_ref, wv_ref, mask_ref, out_ref):
    tq, n_c = q_ref.shape[1], ckv_c_ref.shape[1]
    q = q_ref[0]
    heads = [q[:, h * LANES:(h + 1) * LANES] for h in range(C_HEADS)]
    qr = jnp.concatenate(heads, axis=0)
    qa = jnp.concatenate([_dot(qh, wk_ref[h]) for h, qh in enumerate(heads)], axis=0).astype(BF16)
    ckv_c, ckv_n = ckv_c_ref[0].astype(BF16), ckv_n_ref[0]
    kr_n = kr_n_ref[0].astype(BF16)
    kr_ct = kr_c_ref[0].astype(BF16)
    kr_ct = jnp.concatenate([jnp.zeros((C_NOPE, n_c), BF16), kr_ct,
                             jnp.zeros((LANES - C_NOPE - C_ROPE, n_c), BF16)], axis=0)
    s_c = _dot_nt(qa, ckv_c) + _dot(qr, kr_ct) + mask_ref[:, :n_c]
    s_n = _dot_nt(qa, ckv_n) + _dot_nt(qr, kr_n) + mask_ref[:, n_c:n_c + tq]
    m = jnp.maximum(jnp.max(s_c, axis=-1, keepdims=True), jnp.max(s_n, axis=-1, keepdims=True))
    e_c, e_n = jnp.exp2(s_c - m), jnp.exp2(s_n - m)
    l = jnp.sum(e_c, axis=-1, keepdims=True) + jnp.sum(e_n, axis=-1, keepdims=True)
    o = ((_dot(e_c.astype(BF16), ckv_c) + _dot(e_n.astype(BF16), ckv_n)) / l).astype(BF16)
    for p in range(C_HEADS // 2):
        rows = lambda h: o[h * tq:(h + 1) * tq]
        pair = _dot(rows(2 * p), wv_ref[2 * p]) + _dot(rows(2 * p + 1), wv_ref[2 * p + 1])
        out_ref[0, :, p * LANES:(p + 1) * LANES] = pair.astype(BF16)


def _c_sample(q, ckv_cache, kr_cache, ckv_new, kr_new, wk, wv, mask):
    b, tq, _ = q.shape
    blk = lambda a: pl.BlockSpec((1,) + a.shape[1:], lambda i: (i,) + (0,) * (a.ndim - 1))
    return pl.pallas_call(
        _c_sample_kernel,
        grid=(b,),
        in_specs=[blk(q), blk(ckv_cache), blk(kr_cache), blk(ckv_new), blk(kr_new),
                  _const_spec(wk), _const_spec(wv), _const_spec(mask)],
        out_specs=pl.BlockSpec((1, tq, C_HEADS * C_V), lambda i: (i, 0, 0)),
        out_shape=jax.ShapeDtypeStruct((b, tq, C_HEADS * C_V), BF16),
        compiler_params=_params(("arbitrary",)),
        name="c_sample",
    )(q, ckv_cache, kr_cache, ckv_new, kr_new, wk, wv, mask)


def _t5_bucket(rel):
    nb = T5_BUCKETS // 2
    max_exact = nb // 2
    ret = jnp.where(rel > 0, nb, 0)
    n = jnp.abs(rel)
    n_f = jnp.maximum(n, 1).astype(F32)
    large = max_exact + (jnp.log(n_f / max_exact) / math.log(T5_MAX_DIST / max_exact)
                         * (nb - max_exact)).astype(jnp.int32)
    large = jnp.minimum(large, nb - 1)
    return ret + jnp.where(n < max_exact, n, large)


def _a_index(q_pos, k_pos, k_real):
    rel = np.clip(q_pos[:, None] - k_pos[None, :], -A_REL_CLIP, A_REL_CLIP) + A_REL_CLIP
    qc, kc = q_pos[:, None] // CHUNK, k_pos[None, :] // CHUNK
    ok = k_real[None, :] & (kc <= qc) & (kc >= qc - A_PAST_CHUNKS)
    return np.where(ok, rel, -1).astype(np.int32)


def _b_index(q_pos, k_pos, k_real):
    idx = _t5_bucket(k_pos[None, :] - q_pos[:, None])
    ok = k_real[None, :] & ((k_pos[None, :] // CHUNK) <= (q_pos[:, None] // CHUNK))
    return jnp.where(ok, idx, -1).astype(jnp.int32)


def _rope_tables(pos, scale):
    half = C_ROPE // 2
    inv = ROPE_THETA ** (-jnp.arange(half, dtype=F32) / half)
    ang = pos.astype(F32)[:, None] * inv[None, :]
    cos = jnp.concatenate([jnp.cos(ang)] * 2, axis=-1)
    sin = jnp.concatenate([jnp.sin(ang)] * 2, axis=-1)
    n = pos.shape[0]
    z_nope = jnp.zeros((n, C_NOPE), F32)
    z_tail = jnp.zeros((n, LANES - C_NOPE - C_ROPE), F32)
    cq = jnp.concatenate([jnp.full((n, C_NOPE), scale, F32), cos * scale, z_tail], axis=-1)
    sq = jnp.concatenate([z_nope, sin * scale, z_tail], axis=-1)
    ck = jnp.concatenate([z_nope, cos, z_tail], axis=-1)
    sk = jnp.concatenate([z_nope, sin, z_tail], axis=-1)
    return cq, sq, ck, sk


def _rot_cols(w):
    half = w.shape[-1] // 2
    return jnp.concatenate([-w[..., half:], w[..., :half]], axis=-1)


def kernel(x_prompt, x_sample, cache_a_k, cache_a_v, cache_b_k, cache_b_v, cache_c_kv, cache_c_kr,
           t5_bias, ffn1_norm, ffn1_w_gu, ffn1_w_down, mix_norm, ffn2_norm, ffn2_w_gu, ffn2_w_down,
           e_w_in, a_rel_bias, b_lambda_q1, b_lambda_k1, b_lambda_q2, b_lambda_k2, b_subln, e_w_out,
           c_w_in, c_q_norm, c_kv_norm, c_w_q_up, c_w_kv_up, c_w_out, final_norm):
    batch, seq, d = x_prompt.shape
    dec_batch, t_new, _ = x_sample.shape
    past = cache_b_k.shape[2]
    n_cache_a = cache_a_k.shape[2]
    a_w = A_HEADS * A_HEAD_DIM
    b_w = B_HEADS * 2 * B_HEAD_DIM
    tq = ATT_TILE
    a_pad = A_PAST_CHUNKS * CHUNK
    a_keep = min(a_pad, seq)
    assert tq + 1 >= T5_MAX_DIST and tq % CHUNK == 0 and a_pad % tq == 0
    far_bucket = T5_BUCKETS // 2 - 1
    lam_init = 0.8 - 0.6 * math.exp(-0.3 * 0)
    c_scale = (C_NOPE + C_ROPE) ** -0.5 * math.log2(math.e)
    row2 = lambda v: v.reshape(1, -1)

    e_in = e_w_in[0].astype(BF16)
    e_out_a, e_out_b = e_w_out[0, :a_w].astype(BF16), e_w_out[0, a_w:].astype(BF16)
    w_in = c_w_in[0]
    w_kr = w_in[:, C_Q_LORA + C_KV_LORA:]
    c_in = jnp.concatenate([w_in[:, :C_Q_LORA + C_KV_LORA], jnp.zeros((d, C_NOPE), F32),
                            w_kr, _rot_cols(w_kr)], axis=-1).astype(BF16)
    wq = c_w_q_up[0].reshape(C_Q_LORA, C_HEADS, C_NOPE + C_ROPE)
    wq_rope = wq[..., C_NOPE:]
    c_q = jnp.concatenate([wq[..., :C_NOPE], wq_rope, _rot_cols(wq_rope)], axis=-1)
    c_q = c_q.reshape(C_Q_LORA, C_HEADS * LANES).astype(BF16)
    wkv = c_w_kv_up[0].reshape(C_KV_LORA, C_HEADS, C_NOPE + C_V)
    c_k = jnp.concatenate([wkv[..., :C_NOPE], jnp.zeros_like(wkv[..., C_NOPE:])], axis=-1)
    c_k = c_k.reshape(C_KV_LORA, C_HEADS * LANES).astype(BF16)
    wv_pairs = wkv[..., C_NOPE:].reshape(C_KV_LORA, C_HEADS // 2, 2, C_V)
    z_v = jnp.zeros_like(wv_pairs[:, :, 0])
    c_v = jnp.stack([jnp.concatenate([wv_pairs[:, :, 0], z_v], axis=-1),
                     jnp.concatenate([z_v, wv_pairs[:, :, 1]], axis=-1)], axis=2)
    c_v = c_v.reshape(C_KV_LORA, C_HEADS * LANES).astype(BF16)
    c_k_abs = jnp.pad(jnp.transpose(wkv[..., :C_NOPE], (1, 2, 0)), ((0, 0), (0, LANES - C_NOPE), (0, 0)))
    c_k_abs = c_k_abs.astype(BF16)
    c_v_abs = jnp.transpose(c_v.reshape(C_KV_LORA, C_HEADS, LANES), (1, 0, 2))
    c_out_w = c_w_out[0].astype(BF16)
    lams = [row2(b_lambda_q1[0]), row2(b_lambda_k1[0]), row2(b_lambda_q2[0]), row2(b_lambda_k2[0])]
    subln = row2(b_subln[0])

    pos_s = past + jnp.arange(t_new)
    r = jnp.arange(tq)
    assert A_TILE % CHUNK == 0 and a_pad % A_TILE == 0
    a_idx_p, a_lay_p = _unique_blocks(
        _a_index(a_pad + np.arange(A_TILE), np.arange(a_pad + A_TILE), np.ones((a_pad + A_TILE,), bool)),
        CHUNK, LANES)
    ka = n_cache_a + t_new
    ka_pad = -(-ka // LANES) * LANES
    pos_s_np = past + np.arange(t_new)
    a_kpos = np.concatenate([past - n_cache_a + np.arange(n_cache_a), pos_s_np,
                             np.zeros((ka_pad - ka,), np.int64)])
    a_real = np.arange(ka_pad) < ka
    a_idx_s, a_lay_s = _unique_blocks(_a_index(pos_s_np, a_kpos, a_real & (a_kpos >= 0)), t_new, LANES)
    b_idx_p = jnp.stack([_b_index(tq + r, r, jnp.ones((tq,), bool)),
                         _b_index(r, r, jnp.ones((tq,), bool))])
    kb = past + t_new
    kb_pad = -(-kb // LANES) * LANES
    b_kpos = jnp.arange(kb_pad)
    b_real = b_kpos < kb
    b_idx_s = _b_index(pos_s, b_kpos, b_real)
    c_mask_s = jnp.where(b_real[None, :] & ((b_kpos[None, :] // CHUNK) <= (pos_s[:, None] // CHUNK)),
                         0.0, NEG).astype(F32)

    a_bias_p = _bias_expand(a_idx_p, a_rel_bias[0], None, a_lay_p)
    a_bias_s = _bias_expand(a_idx_s, a_rel_bias[0], None, a_lay_s)
    t5_t = t5_bias.T
    b_bias_p = jnp.stack([_bias_expand(b_idx_p[0], t5_t, far_bucket),
                          _bias_expand(b_idx_p[1], t5_t, far_bucket)])
    b_bias_s = _bias_expand(b_idx_s, t5_t, far_bucket)

    xp = x_prompt.reshape(batch * seq, d)
    xs = x_sample.reshape(dec_batch * t_new, d)
    g1, gm, g2 = row2(ffn1_norm[0]), row2(mix_norm[0]), row2(ffn2_norm[0])

    xp, xs = _ffn(xp, [], xs, [], [], g1, ffn1_w_gu, ffn1_w_down, 0)

    aq, ak, av, bq, bk, bv, akf, avf, bkf, bvf = _eproj(xp, gm, e_in, seq, a_keep)
    to3 = lambda t, n: t.reshape(n, -1, t.shape[-1])
    a_out_p = _a_prompt(to3(aq, batch), to3(ak, batch), to3(av, batch), a_bias_p)
    b_out_p = _b_prompt(to3(bq, batch), to3(bk, batch), to3(bv, batch), b_bias_p, lams, subln, lam_init)
    p_a_k = akf.reshape(1, batch, a_keep, A_HEADS, A_HEAD_DIM)
    p_a_v = avf.reshape(1, batch, a_keep, A_HEADS, A_HEAD_DIM)
    p_b_k = bkf.reshape(1, batch, seq, B_HEADS, 2 * B_HEAD_DIM)
    p_b_v = bvf.reshape(1, batch, seq, B_HEADS, 2 * B_HEAD_DIM)

    aq, ak, av, bq, bk, bv, akf, avf, bkf, bvf = _eproj(xs, gm, e_in, dec_batch * t_new, dec_batch * t_new)
    dims_major = lambda c: jnp.transpose(c, (0, 2, 3, 1)).reshape(dec_batch, a_w, n_cache_a)
    a_out_s = _a_sample(to3(aq, dec_batch), dims_major(cache_a_k[0]), dims_major(cache_a_v[0]),
                        to3(ak, dec_batch), to3(av, dec_batch), a_bias_s)
    pos_head_rows = lambda c: c.reshape(dec_batch, past * B_HEADS, 2 * B_HEAD_DIM)
    b_out_s = _b_sample(to3(bq, dec_batch), pos_head_rows(cache_b_k[0]), pos_head_rows(cache_b_v[0]),
                        to3(bk, dec_batch), to3(bv, dec_batch), b_bias_s, lams, subln, lam_init)
    s_a_k = akf.reshape(1, dec_batch, t_new, A_HEADS, A_HEAD_DIM)
    s_a_v = avf.reshape(1, dec_batch, t_new, A_HEADS, A_HEAD_DIM)
    s_b_k = bkf.reshape(1, dec_batch, t_new, B_HEADS, 2 * B_HEAD_DIM)
    s_b_v = bvf.reshape(1, dec_batch, t_new, B_HEADS, 2 * B_HEAD_DIM)
    flat = lambda t: t.reshape(-1, t.shape[-1])
    xp, xs = _ffn(xp, [flat(a_out_p), flat(b_out_p)], xs, [flat(a_out_s), flat(b_out_s)], [e_out_a, e_out_b],
                  g2, ffn2_w_gu, ffn2_w_down, 0)

    g1, gm, g2 = row2(ffn1_norm[1]), row2(mix_norm[1]), row2(ffn2_norm[1])
    gq, gkv, gfin = row2(c_q_norm[0]), row2(c_kv_norm[0]), row2(final_norm)
    xp, xs = _ffn(xp, [], xs, [], [], g1, ffn1_w_gu, ffn1_w_down, 1)

    q, ckvf, ckvb, kr, krf = _cproj(xp, gm, c_in, gq, gkv, c_q, _rope_tables(jnp.arange(seq), c_scale))
    c_out_p = _c_prompt(to3(q, batch), to3(ckvb, batch), to3(kr, batch), c_k, c_v)
    p_c_kv = ckvf.reshape(1, batch, seq, C_KV_LORA)
    p_c_kr = krf.reshape(1, batch, seq, C_ROPE)

    q, ckvf, ckvb, kr, krf = _cproj(xs, gm, c_in, gq, gkv, c_q,
                                    _rope_tables(jnp.tile(pos_s, dec_batch), c_scale))
    kr_cache = jnp.transpose(cache_c_kr[0], (0, 2, 1))
    c_out_s = _c_sample(to3(q, dec_batch), cache_c_kv[0], kr_cache, to3(ckvb, dec_batch), to3(kr, dec_batch),
                        c_k_abs, c_v_abs, jnp.tile(c_mask_s, (C_HEADS, 1)))
    s_c_kv = ckvf.reshape(1, dec_batch, t_new, C_KV_LORA)
    s_c_kr = krf.reshape(1, dec_batch, t_new, C_ROPE)
    y_prompt, y_sample = _ffn(xp, [flat(c_out_p)], xs, [flat(c_out_s)], [c_out_w], g2, ffn2_w_gu, ffn2_w_down,
                              1, gfin)

    return (y_prompt.reshape(batch, seq, d), y_sample.reshape(dec_batch, t_new, d),
            p_a_k, p_a_v, p_b_k, p_b_v, p_c_kv, p_c_kr,
            s_a_k, s_a_v, s_b_k, s_b_v, s_c_kv, s_c_kr)
```

```python
import functools
import math

import jax
import jax.numpy as jnp
import numpy as np
from jax import lax
from jax.experimental import pallas as pl
from jax.experimental.pallas import tpu as pltpu

F32 = jnp.float32
BF16 = jnp.bfloat16

EPS = 1e-6
NEG = -1e30
LOG2E = math.log2(math.e)
CHUNK = 64
A_HEADS = 8
A_HEAD_DIM = 64
A_PAST_CHUNKS = 8
A_REL_CLIP = 64
B_HEADS = 4
B_HEAD_DIM = 64
T5_BUCKETS = 32
T5_MAX_DIST = 128
C_HEADS = 16
C_NOPE = 64
C_ROPE = 32
C_V = 64
C_Q_LORA = 384
C_KV_LORA = 256
ROPE_THETA = 10000.0

LANES = 128
SUBLANES = 8
VMEM_LIMIT = 56 * 1024 * 1024
ROW_TILE = 512
ATT_TILE = 256
A_TILE = 256
BIAS_LOOP_UNROLL = 8
BIAS_GROUP_VREGS = 32
TOKEN_SUBTILES = 2
MIN_SUBTILE_ROWS = 256
WEIGHT_CAST_CHUNKS = 8
MXU_WIDTH = 256
FFN_CHUNKS = 2


def _subtile_rows(tm):
    sub = tm // TOKEN_SUBTILES
    return sub if sub * TOKEN_SUBTILES == tm and sub >= MIN_SUBTILE_ROWS else tm


def _ffn_chunks(d_ff):
    tiles = d_ff // MXU_WIDTH
    assert tiles * MXU_WIDTH == d_ff
    per = -(-tiles // FFN_CHUNKS)
    edges = [min(c * per, tiles) * MXU_WIDTH for c in range(FFN_CHUNKS + 1)]
    return [(lo, hi) for lo, hi in zip(edges[:-1], edges[1:]) if hi > lo]


def _rmsnorm(x, g):
    return x * lax.rsqrt(jnp.mean(x * x, axis=-1, keepdims=True) + EPS) * g


def _dot(a, b):
    return jnp.dot(a, b, preferred_element_type=F32)


def _dot_nt(a, b):
    return lax.dot_general(a, b, (((1,), (1,)), ((), ())), preferred_element_type=F32)


def _const_spec(a):
    nd = a.ndim
    return pl.BlockSpec(a.shape, lambda *_: (0,) * nd, pipeline_mode=pl.Buffered(1))


def _params(sem, flags=None):
    return pltpu.CompilerParams(dimension_semantics=sem, vmem_limit_bytes=VMEM_LIMIT, flags=flags)


def _weight_chunk_copy(src, stage, sems, c, slot):
    rows = stage.shape[1]
    return pltpu.make_async_copy(src.at[pl.ds(c * rows, rows), :], stage.at[slot], sems.at[slot])


def _load_weight_as_bf16(src, stage, sems, dst):
    rows = stage.shape[1]
    n = src.shape[0] // rows
    _weight_chunk_copy(src, stage, sems, 0, 0).start()
    for c in range(n):
        slot = c % 2
        if c + 1 < n:
            _weight_chunk_copy(src, stage, sems, c + 1, 1 - slot).start()
        _weight_chunk_copy(src, stage, sems, c, slot).wait()
        dst[c * rows:(c + 1) * rows, :] = stage[slot].astype(BF16)


def _ffn_kernel(n_pre, with_final, layer, *refs):
    group = 1 + n_pre
    groups = [refs[:group], refs[group:2 * group]]
    idx = 2 * group
    pre_w = refs[idx:idx + n_pre]
    idx += n_pre
    g_ref, wgu_in, wd_in = refs[idx:idx + 3]
    idx += 3
    gf_ref = refs[idx] if with_final else None
    idx += with_final
    outs = refs[idx:idx + 2]
    wgu_ref, wd_ref, gu_stage, d_stage, sems = refs[idx + 2:]
    step, last = pl.program_id(0), pl.num_programs(0) - 1

    @pl.when(step == 0)
    def _():
        _load_weight_as_bf16(wgu_in.at[layer], gu_stage, sems, wgu_ref)
        _load_weight_as_bf16(wd_in.at[layer], d_stage, sems, wd_ref)

    def run(x_ref, o_refs, out_ref):
        x = x_ref[...]
        for o_ref, w_ref in zip(o_refs, pre_w):
            x = x + _dot(o_ref[...], w_ref[...])
        xn = _rmsnorm(x, g_ref[...]).astype(BF16)
        d_ff = wd_ref.shape[0]
        acc = jnp.zeros_like(x)
        for lo, hi in _ffn_chunks(d_ff):
            gate = _dot(xn, wgu_ref[:, lo:hi])
            up = _dot(xn, wgu_ref[:, d_ff + lo:d_ff + hi])
            act = (jax.nn.silu(gate) * up).astype(BF16)
            acc = acc + _dot(act, wd_ref[lo:hi, :])
        y = x + 0.5 * acc
        if with_final:
            y = _rmsnorm(y, gf_ref[...])
        out_ref[...] = y

    @pl.when(step < last)
    def _():
        run(groups[0][0], groups[0][1:], outs[0])

    @pl.when(step == last)
    def _():
        run(groups[1][0], groups[1][1:], outs[1])


def _ffn(x_prompt, pres_prompt, x_sample, pres_sample, pre_weights, g, wgu, wd, layer, g_final=None):
    t, d = x_prompt.shape
    tm = min(ROW_TILE, t)
    n = t // tm
    row = lambda w: pl.BlockSpec((tm, w), lambda i: (jnp.minimum(i, n - 1), 0))
    whole = lambda a: pl.BlockSpec(a.shape, lambda i: (0, 0))
    hbm = pl.BlockSpec(memory_space=pl.ANY)
    args = [x_prompt, *pres_prompt, x_sample, *pres_sample, *pre_weights, g, wgu, wd]
    specs = ([row(d)] + [row(o.shape[1]) for o in pres_prompt] + [whole(x_sample)] + [whole(o) for o in pres_sample]
             + [_const_spec(w) for w in pre_weights] + [_const_spec(g), hbm, hbm])
    if g_final is not None:
        args.append(g_final)
        specs.append(_const_spec(g_final))
    gu_shape, d_shape = wgu.shape[1:], wd.shape[1:]
    chunks = WEIGHT_CAST_CHUNKS
    return pl.pallas_call(
        functools.partial(_ffn_kernel, len(pre_weights), g_final is not None, layer),
        grid=(n + 1,),
        in_specs=specs,
        out_specs=[row(d), whole(x_sample)],
        out_shape=[jax.ShapeDtypeStruct((t, d), F32), jax.ShapeDtypeStruct(x_sample.shape, F32)],
        scratch_shapes=[pltpu.VMEM(gu_shape, BF16), pltpu.VMEM(d_shape, BF16),
                        pltpu.VMEM((2, gu_shape[0] // chunks, gu_shape[1]), F32),
                        pltpu.VMEM((2, d_shape[0] // chunks, d_shape[1]), F32),
                        pltpu.SemaphoreType.DMA((2,))],
        compiler_params=_params(("arbitrary",)),
        name="ffn",
    )(*args)


def _store_values_with_ones(out_ref, v):
    low = lax.broadcasted_iota(jnp.int32, (1, LANES), 1) < LANES // 2
    for p in range(v.shape[1] // LANES):
        vp = v[:, p * LANES:(p + 1) * LANES]
        out_ref[:, 2 * p * LANES:(2 * p + 1) * LANES] = jnp.where(low, vp, 1.0).astype(out_ref.dtype)
        out_ref[:, (2 * p + 1) * LANES:(2 * p + 2) * LANES] = jnp.where(low, 1.0, vp).astype(out_ref.dtype)


def _merge_head_pair(acc_even, acc_odd):
    half = LANES // 2
    low = lax.broadcasted_iota(jnp.int32, (1, LANES), 1) < half
    den = pltpu.roll(jnp.where(low, acc_odd, acc_even), half, 1)
    return jnp.where(low, acc_even, acc_odd) / den


def _eproj_kernel(x_ref, g_ref, w_ref, aq, ak, av, bq, bk, bv, akf, avf, bkf, bvf):
    tm = x_ref.shape[0]
    sub = _subtile_rows(tm)
    w = aq.shape[1]
    for r0 in range(0, tm, sub):
        rows = slice(r0, r0 + sub)
        hn = _rmsnorm(x_ref[rows, :], g_ref[...]).astype(BF16)
        t = _dot(hn, w_ref[...])
        parts = [t[:, i * w:(i + 1) * w] for i in range(6)]
        aq[rows, :] = (parts[0] * (A_HEAD_DIM ** -0.5 * LOG2E)).astype(BF16)
        ak[rows, :] = parts[1].astype(BF16)
        _store_values_with_ones(av.at[rows, :], parts[2])
        bq[rows, :] = (parts[3] * (B_HEAD_DIM ** -0.5 * LOG2E)).astype(BF16)
        bk[rows, :] = parts[4].astype(BF16)
        ones = jnp.ones((sub, LANES), BF16)
        for h in range(B_HEADS):
            bv[rows, 2 * h * LANES:(2 * h + 1) * LANES] = parts[5][:, h * LANES:(h + 1) * LANES].astype(BF16)
            bv[rows, (2 * h + 1) * LANES:(2 * h + 2) * LANES] = ones
        akf[rows] = parts[1].reshape((sub,) + akf.shape[1:])
        avf[rows] = parts[2].reshape((sub,) + avf.shape[1:])
        bkf[rows] = parts[4].reshape((sub,) + bkf.shape[1:])
        bvf[rows] = parts[5].reshape((sub,) + bvf.shape[1:])


def _eproj(x, g, w, rows_per_batch, a_keep):
    t, d = x.shape
    tm = min(ROW_TILE, t, a_keep)
    assert rows_per_batch % tm == 0 and a_keep % tm == 0
    per, kept = rows_per_batch // tm, a_keep // tm
    wd = w.shape[1] // 6
    row = lambda n: pl.BlockSpec((tm, n), lambda i: (i, 0))
    a_map = lambda i: ((i // per) * kept + jnp.maximum(i % per - (per - kept), 0), 0, 0)
    a_f32 = pl.BlockSpec((tm, A_HEADS, A_HEAD_DIM), a_map)
    b_f32 = pl.BlockSpec((tm, B_HEADS, 2 * B_HEAD_DIM), lambda i: (i, 0, 0))
    widths = [wd, wd, 2 * wd, wd, wd, 2 * wd]
    shapes = ([jax.ShapeDtypeStruct((t, n), BF16) for n in widths]
              + [jax.ShapeDtypeStruct((t // per * kept, A_HEADS, A_HEAD_DIM), F32)] * 2
              + [jax.ShapeDtypeStruct((t, B_HEADS, 2 * B_HEAD_DIM), F32)] * 2)
    return pl.pallas_call(
        _eproj_kernel,
        grid=(t // tm,),
        in_specs=[row(d), _const_spec(g), _const_spec(w)],
        out_specs=[row(n) for n in widths] + [a_f32, a_f32, b_f32, b_f32],
        out_shape=shapes,
        compiler_params=_params(("arbitrary",)),
        name="eproj",
    )(x, g, w)


def _cproj_kernel(x_ref, g_ref, win_ref, gq_ref, gkv_ref, wq_ref, cq_ref, sq_ref, ck_ref, sk_ref,
                  q_out, ckvf_out, ckvb_out, kr_out, krf_out):
    tm = x_ref.shape[0]
    sub = _subtile_rows(tm)
    for r0 in range(0, tm, sub):
        rows = slice(r0, r0 + sub)
        hn = _rmsnorm(x_ref[rows, :], g_ref[...]).astype(BF16)
        t = _dot(hn, win_ref[...])
        cqn = _rmsnorm(t[:, :C_Q_LORA], gq_ref[...]).astype(BF16)
        ckv = _rmsnorm(t[:, C_Q_LORA:C_Q_LORA + C_KV_LORA], gkv_ref[...])
        ckvf_out[rows, :] = ckv
        ckvb_out[rows, :] = ckv.astype(BF16)
        tg = t[:, C_Q_LORA + C_KV_LORA:]
        kr = tg * ck_ref[rows, :] + pltpu.roll(tg, LANES - C_ROPE, 1) * sk_ref[rows, :]
        kr_out[rows, :] = kr
        krf_out[rows, :] = kr[:, C_NOPE:C_NOPE + C_ROPE]
        q = _dot(cqn, wq_ref[...])
        cq, sq = cq_ref[rows, :], sq_ref[rows, :]
        for h in range(C_HEADS):
            qh = q[:, h * LANES:(h + 1) * LANES]
            qf = qh * cq + pltpu.roll(qh, LANES - C_ROPE, 1) * sq
            q_out[rows, h * LANES:(h + 1) * LANES] = qf.astype(BF16)


def _cproj(x, g, win, gq, gkv, wq, tabs):
    t, d = x.shape
    tm = min(ROW_TILE, t)
    nper = tabs[0].shape[0] // tm
    row = lambda n: pl.BlockSpec((tm, n), lambda i: (i, 0))
    tab = pl.BlockSpec((tm, LANES), lambda i: (i % nper, 0))
    shapes = [jax.ShapeDtypeStruct((t, C_HEADS * LANES), BF16),
              jax.ShapeDtypeStruct((t, C_KV_LORA), F32),
              jax.ShapeDtypeStruct((t, C_KV_LORA), BF16),
              jax.ShapeDtypeStruct((t, LANES), F32),
              jax.ShapeDtypeStruct((t, C_ROPE), F32)]
    return pl.pallas_call(
        _cproj_kernel,
        grid=(t // tm,),
        in_specs=[row(d), _const_spec(g), _const_spec(win), _const_spec(gq), _const_spec(gkv),
                  _const_spec(wq), tab, tab, tab, tab],
        out_specs=[row(C_HEADS * LANES), row(C_KV_LORA), row(C_KV_LORA), row(LANES), row(C_ROPE)],
        out_shape=shapes,
        compiler_params=_params(("arbitrary",)),
        name="cproj",
    )(x, g, win, gq, gkv, wq, *tabs)


def _bias_kernel(n_rows, shift, layout, idx_ref, tab_ref, out_ref):
    n_heads = out_ref.shape[0]
    bq = out_ref.shape[1] // len(layout)
    bk = out_ref.shape[2] // len(layout[0])
    block_vregs = max(1, bq * bk // (SUBLANES * LANES))
    group = max(1, min(n_heads, BIAS_GROUP_VREGS // block_vregs))
    for u in range(idx_ref.shape[0] // bq):
        idx = idx_ref[u * bq:(u + 1) * bq, :]
        for h0 in range(0, n_heads, group):
            heads = list(range(h0, min(h0 + group, n_heads)))
            bases = [tab_ref[h, shift] if shift is not None else 0.0 for h in heads]

            def body(r, accs):
                hit = idx == r
                return tuple(jnp.where(hit, (tab_ref[h, r] - b) * LOG2E, acc)
                             for h, b, acc in zip(heads, bases, accs))

            vals = lax.fori_loop(0, n_rows, body, tuple(jnp.full(idx.shape, NEG, F32) for _ in heads),
                                 unroll=BIAS_LOOP_UNROLL)
            for h, val in zip(heads, vals):
                for a, row in enumerate(layout):
                    for t, uu in enumerate(row):
                        if uu == u:
                            out_ref[h, a * bq:(a + 1) * bq, t * bk:(t + 1) * bk] = val


def _bias_expand(idx, table, shift=None, layout=((0,),)):
    n_heads, n_rows = table.shape
    n_blocks = 1 + max(max(row) for row in layout)
    out_shape = (n_heads, idx.shape[0] // n_blocks * len(layout), idx.shape[1] * len(layout[0]))
    return pl.pallas_call(
        functools.partial(_bias_kernel, n_rows, shift, layout),
        in_specs=[pl.BlockSpec(idx.shape, lambda: (0, 0)),
                  pl.BlockSpec(memory_space=pltpu.SMEM)],
        out_specs=pl.BlockSpec(out_shape, lambda: (0, 0, 0)),
        out_shape=jax.ShapeDtypeStruct(out_shape, F32),
        compiler_params=pltpu.CompilerParams(vmem_limit_bytes=VMEM_LIMIT),
        name="bias_expand",
    )(idx, table)


def _unique_blocks(idx, bq, bk):
    q, k = idx.shape
    blocks = idx.reshape(q // bq, bq, k // bk, bk).transpose(0, 2, 1, 3).reshape(-1, bq, bk)
    uniq, inv = np.unique(blocks, axis=0, return_inverse=True)
    layout = tuple(tuple(int(u) for u in row) for row in inv.reshape(q // bq, k // bk))
    return jnp.asarray(uniq.reshape(-1, bk), jnp.int32), layout


def _a_attend(q, kwin, vwin, bias_ref, valid, out_ref, s_ref):
    low = lax.broadcasted_iota(jnp.int32, (1, LANES), 1) < A_HEAD_DIM
    for h in range(A_HEADS):
        sl = slice(h // 2 * LANES, (h // 2 + 1) * LANES)
        qp = q[:, sl]
        qm = jnp.where(low if h % 2 == 0 else jnp.logical_not(low), qp, jnp.zeros_like(qp))
        s = _dot_nt(qm, kwin[:, sl]) + bias_ref[h]
        if valid is not None:
            s = jnp.where(valid, s, NEG)
        s_ref[h] = s - jnp.max(s, axis=-1, keepdims=True)
    accs = []
    for h in range(A_HEADS):
        accs.append(_dot(jnp.exp2(s_ref[h]).astype(BF16), vwin[:, h * LANES:(h + 1) * LANES]))
        if h % 2 == 1:
            out_ref[0, :, h // 2 * LANES:(h // 2 + 1) * LANES] = _merge_head_pair(*accs).astype(BF16)
            accs = []


def _a_prompt_kernel(q_ref, k_ref, v_ref, bias_ref, out_ref, s_ref):
    tq = q_ref.shape[1]
    win = bias_ref.shape[2]
    n_tiles = win // tq
    j = pl.program_id(1)

    def run(check_positions):
        ks, vs = [], []
        for t in range(n_tiles):
            start = pl.multiple_of(jnp.maximum(j - (n_tiles - 1 - t), 0) * tq, tq)
            ks.append(k_ref[0, pl.ds(start, tq), :])
            vs.append(v_ref[0, pl.ds(start, tq), :])
        valid = None
        if check_positions:
            valid = lax.broadcasted_iota(jnp.int32, (1, win), 1) + (j - (n_tiles - 1)) * tq >= 0
        _a_attend(q_ref[0], jnp.concatenate(ks, axis=0), jnp.concatenate(vs, axis=0), bias_ref, valid,
                  out_ref, s_ref)

    @pl.when(j < n_tiles - 1)
    def _():
        run(True)

    @pl.when(j >= n_tiles - 1)
    def _():
        run(False)


def _a_prompt(q, k, v, bias):
    b, s, w = q.shape
    tq = bias.shape[1]
    kv = lambda a: pl.BlockSpec((1, s, a.shape[2]), lambda i, j: (i, 0, 0))
    return pl.pallas_call(
        _a_prompt_kernel,
        grid=(b, s // tq),
        in_specs=[pl.BlockSpec((1, tq, w), lambda i, j: (i, j, 0)), kv(k), kv(v), _const_spec(bias)],
        out_specs=pl.BlockSpec((1, tq, w), lambda i, j: (i, j, 0)),
        out_shape=jax.ShapeDtypeStruct((b, s, w), BF16),
        scratch_shapes=[pltpu.VMEM(bias.shape, F32)],
        compiler_params=_params(("arbitrary", "arbitrary")),
        name="a_prompt",
    )(q, k, v, bias)


def _a_sample_kernel(q_ref, kct_ref, vct_ref, kn_ref, vn_ref, bias_ref, out_ref):
    q = q_ref[0]
    tq, n_c = q.shape[0], kct_ref.shape[2]
    low = lax.broadcasted_iota(jnp.int32, (1, LANES), 1) < A_HEAD_DIM
    for p in range(A_HEADS // 2):
        sl = slice(p * LANES, (p + 1) * LANES)
        qp = q[:, sl]
        kct, vct = kct_ref[0, sl, :].astype(BF16), vct_ref[0, sl, :].astype(BF16)
        outs = []
        for sub in range(2):
            h = 2 * p + sub
            qm = jnp.where(low if sub == 0 else jnp.logical_not(low), qp, jnp.zeros_like(qp))
            s_c = _dot(qm, kct) + bias_ref[h, :, :n_c]
            s_n = _dot_nt(qm, kn_ref[0, :, sl]) + bias_ref[h, :, n_c:n_c + tq]
            m = jnp.maximum(jnp.max(s_c, axis=-1, keepdims=True), jnp.max(s_n, axis=-1, keepdims=True))
            e_c, e_n = jnp.exp2(s_c - m), jnp.exp2(s_n - m)
            l = jnp.sum(e_c, axis=-1, keepdims=True) + jnp.sum(e_n, axis=-1, keepdims=True)
            acc = _dot_nt(e_c.astype(BF16), vct) + _dot(e_n.astype(BF16), vn_ref[0, :, h * LANES:(h + 1) * LANES])
            outs.append(acc / l)
        out_ref[0, :, sl] = jnp.where(low, outs[0], outs[1]).astype(BF16)


def _a_sample(q, k_cache, v_cache, k_new, v_new, bias):
    b, tq, w = q.shape
    blk = lambda a: pl.BlockSpec((1,) + a.shape[1:], lambda i: (i,) + (0,) * (a.ndim - 1))
    return pl.pallas_call(
        _a_sample_kernel,
        grid=(b,),
        in_specs=[blk(q), blk(k_cache), blk(v_cache), blk(k_new), blk(v_new), _const_spec(bias)],
        out_specs=blk(q),
        out_shape=jax.ShapeDtypeStruct((b, tq, w), BF16),
        compiler_params=_params(("arbitrary",)),
        name="a_sample",
    )(q, k_cache, v_cache, k_new, v_new, bias)


def _b_lambda(lq1, lk1, lq2, lk2, lam_init):
    s1 = jnp.sum(lq1[...] * lk1[...], axis=-1, keepdims=True)
    s2 = jnp.sum(lq2[...] * lk2[...], axis=-1, keepdims=True)
    return jnp.exp(s1) - jnp.exp(s2) + lam_init


def _b_stack_queries(q):
    lane = lax.broadcasted_iota(jnp.int32, (1, LANES), 1)
    low = lane < B_HEAD_DIM
    qs = []
    for h in range(B_HEADS):
        qh = q[:, h * LANES:(h + 1) * LANES]
        zero = jnp.zeros_like(qh)
        qs.append(jnp.concatenate([jnp.where(low, qh, zero), jnp.where(low, zero, qh)], axis=0))
    return qs


def _b_finish(o, lam, g, lam_init, tq):
    ob = o[:tq] - lam * o[tq:]
    return _rmsnorm(ob, g) * (1.0 - lam_init)


def _b_prompt_kernel(lam_init, q_ref, k_ref, v_ref, bias_ref, lq1, lk1, lq2, lk2, g_ref, out_ref,
                     qs_ref, m_ref, alpha_ref, acc_ref, s_ref):
    tq = q_ref.shape[1]
    tk = bias_ref.shape[3]
    i = pl.program_id(1)
    for h, qh in enumerate(_b_stack_queries(q_ref[0])):
        qs_ref[h] = qh

    lane_tiles = range(tk // LANES)
    lam = _b_lambda(lq1, lk1, lq2, lk2, lam_init)

    def scores(j, bias_sel, first):
        start = pl.multiple_of(j * tk, tk)
        for h in range(B_HEADS):
            sl = slice(h * LANES, (h + 1) * LANES)
            s = _dot_nt(qs_ref[h], k_ref[0, pl.ds(start, tk), sl])
            parts = [s[:, c * LANES:(c + 1) * LANES] for c in lane_tiles]
            if bias_sel is not None:
                parts = [jnp.concatenate([pt[:tq] + bias_ref[bias_sel, h, :, c * LANES:(c + 1) * LANES],
                                          pt[tq:] + bias_ref[bias_sel, h, :, c * LANES:(c + 1) * LANES]], axis=0)
                         for c, pt in enumerate(parts)]
            mx = parts[0]
            for pt in parts[1:]:
                mx = jnp.maximum(mx, pt)
            row_max = jnp.max(mx, axis=-1, keepdims=True)
            if first:
                m_new = jnp.broadcast_to(row_max, (2 * tq, LANES))
            else:
                m_new = jnp.maximum(m_ref[h], row_max)
                alpha_ref[h] = jnp.exp2(m_ref[h] - m_new)
            m_ref[h] = m_new
            for c, pt in enumerate(parts):
                s_ref[h, :, c * LANES:(c + 1) * LANES] = pt

    def attend(j, first, last):
        start = pl.multiple_of(j * tk, tk)
        for h in range(B_HEADS):
            m_new = m_ref[h]
            p = jnp.concatenate([jnp.exp2(s_ref[h, :, c * LANES:(c + 1) * LANES] - m_new) for c in lane_tiles],
                                axis=1).astype(BF16)
            pv = _dot(p, v_ref[0, pl.ds(start, tk), 2 * h * LANES:2 * (h + 1) * LANES])
            num, den = pv[:, :LANES], pv[:, LANES:]
            if not first:
                alpha = alpha_ref[h]
                num = alpha * acc_ref[h, :, :LANES] + num
                den = alpha * acc_ref[h, :, LANES:] + den
            if last:
                bn = _b_finish(num / den, lam, g_ref[...], lam_init, tq)
                out_ref[0, :, h * LANES:(h + 1) * LANES] = bn.astype(BF16)
            else:
                acc_ref[h, :, :LANES] = num
                acc_ref[h, :, LANES:] = den

    scores(i, 1, True)

    @pl.when(i == 0)
    def _():
        attend(0, True, True)

    @pl.when(i >= 1)
    def _():
        attend(i, True, False)
        scores(i - 1, 0, False)

    def trip(t, carry):
        attend(i - t, False, False)
        scores(i - 1 - t, None, False)
        return carry

    lax.fori_loop(1, i, trip, 0)

    @pl.when(i >= 1)
    def _():
        attend(0, False, True)


def _b_prompt(q, k, v, bias, lams, g, lam_init):
    b, s, w = q.shape
    tq = bias.shape[2]
    kv = lambda a: pl.BlockSpec((1, s, a.shape[2]), lambda i, j: (i, 0, 0))
    scratch = pltpu.VMEM((B_HEADS, 2 * tq, LANES), F32)
    return pl.pallas_call(
        functools.partial(_b_prompt_kernel, lam_init),
        grid=(b, s // tq),
        in_specs=[pl.BlockSpec((1, tq, w), lambda i, j: (i, j, 0)), kv(k), kv(v), _const_spec(bias)]
                 + [_const_spec(x) for x in lams] + [_const_spec(g)],
        out_specs=pl.BlockSpec((1, tq, w), lambda i, j: (i, j, 0)),
        out_shape=jax.ShapeDtypeStruct((b, s, w), BF16),
        scratch_shapes=[pltpu.VMEM((B_HEADS, 2 * tq, LANES), BF16), scratch, scratch,
                        pltpu.VMEM((B_HEADS, 2 * tq, 2 * LANES), F32), pltpu.VMEM((B_HEADS, 2 * tq, tq), F32)],
        compiler_params=_params(("arbitrary", "arbitrary")),
        name="b_prompt",
    )(q, k, v, bias, *lams, g)


def _b_sample_kernel(lam_init, q_ref, kc_ref, vc_ref, kn_ref, vn_ref, bias_ref, lq1, lk1, lq2, lk2, g_ref,
                     out_ref):
    tq, n_c = q_ref.shape[1], kc_ref.shape[1] // B_HEADS
    qs = _b_stack_queries(q_ref[0])
    lam = _b_lambda(lq1, lk1, lq2, lk2, lam_init)
    both = lambda bias: jnp.concatenate([bias, bias], axis=0)
    for h in range(B_HEADS):
        sl = slice(h * LANES, (h + 1) * LANES)
        head_rows = pl.ds(h, n_c, stride=B_HEADS)
        kc, vc = kc_ref[0, head_rows, :].astype(BF16), vc_ref[0, head_rows, :].astype(BF16)
        s_c = _dot_nt(qs[h], kc) + both(bias_ref[h, :, :n_c])
        s_n = _dot_nt(qs[h], kn_ref[0, :, sl]) + both(bias_ref[h, :, n_c:n_c + tq])
        m = jnp.maximum(jnp.max(s_c, axis=-1, keepdims=True), jnp.max(s_n, axis=-1, keepdims=True))
        e_c, e_n = jnp.exp2(s_c - m), jnp.exp2(s_n - m)
        l = jnp.sum(e_c, axis=-1, keepdims=True) + jnp.sum(e_n, axis=-1, keepdims=True)
        o = (_dot(e_c.astype(BF16), vc) + _dot(e_n.astype(BF16), vn_ref[0, :, 2 * h * LANES:(2 * h + 1) * LANES])) / l
        bn = _b_finish(o, lam, g_ref[...], lam_init, tq)
        out_ref[0, :, sl] = bn.astype(BF16)


def _b_sample(q, k_cache, v_cache, k_new, v_new, bias, lams, g, lam_init):
    b, tq, w = q.shape
    blk = lambda a: pl.BlockSpec((1,) + a.shape[1:], lambda i: (i,) + (0,) * (a.ndim - 1))
    return pl.pallas_call(
        functools.partial(_b_sample_kernel, lam_init),
        grid=(b,),
        in_specs=[blk(q), blk(k_cache), blk(v_cache), blk(k_new), blk(v_new), _const_spec(bias)]
                 + [_const_spec(x) for x in lams] + [_const_spec(g)],
        out_specs=blk(q),
        out_shape=jax.ShapeDtypeStruct((b, tq, w), BF16),
        compiler_params=_params(("arbitrary",)),
        name="b_sample",
    )(q, k_cache, v_cache, k_new, v_new, bias, *lams, g)


def _c_prompt_kernel(q_ref, ckv_ref, kr_ref, wk_ref, wv_ref, out_ref, k_ref, v_ref, m_ref, alpha_ref, acc_ref,
                     mask_ref, s_ref):
    tq = q_ref.shape[1]
    tk = tq
    n_lane_tiles = tk // LANES
    i = pl.program_id(1)

    @pl.when((pl.program_id(0) == 0) & (i == 0))
    def _():
        shift = CHUNK.bit_length() - 1
        rc = jnp.right_shift(lax.broadcasted_iota(jnp.int32, (tq, tk), 0), shift)
        kc = jnp.right_shift(lax.broadcasted_iota(jnp.int32, (tq, tk), 1), shift)
        mask_ref[...] = jnp.where(kc <= rc, 0.0, NEG)

    @pl.when(i == 0)
    def _():
        high = (lax.broadcasted_iota(jnp.int32, (1, LANES), 1) >= C_V).astype(F32)
        rows_per_chunk = 2 * tk

        def expand(c, carry):
            rows = pl.ds(pl.multiple_of(c * rows_per_chunk, rows_per_chunk), rows_per_chunk)
            ckv = ckv_ref[0, rows, :]
            kd = _dot(ckv, wk_ref[...])
            vd = _dot(ckv, wv_ref[...])
            kr = kr_ref[0, rows, :]
            for h in range(C_HEADS):
                sl = slice(h * LANES, (h + 1) * LANES)
                k_ref[rows, sl] = (kd[:, sl] + kr).astype(BF16)
                v_ref[rows, sl] = (vd[:, sl] + (high if h % 2 == 0 else 1.0 - high)).astype(BF16)
            return carry

        lax.fori_loop(0, k_ref.shape[0] // rows_per_chunk, expand, 0)

    def scores(j, first):
        start = pl.multiple_of(j * tk, tk)
        for h in range(C_HEADS):
            sl = slice(h * LANES, (h + 1) * LANES)
            s = _dot_nt(q_ref[0, :, sl], k_ref[pl.ds(start, tk), sl])
            parts = [s[:, c * LANES:(c + 1) * LANES] for c in range(n_lane_tiles)]
            if first:
                parts = [pt + mask_ref[:, c * LANES:(c + 1) * LANES] for c, pt in enumerate(parts)]
            mx = parts[0]
            for pt in parts[1:]:
                mx = jnp.maximum(mx, pt)
            row_max = jnp.max(mx, axis=-1, keepdims=True)
            if first:
                m_new = jnp.broadcast_to(row_max, (tq, LANES))
            else:
                m_new = jnp.maximum(m_ref[h], row_max)
                alpha_ref[h] = jnp.exp2(m_ref[h] - m_new)
            m_ref[h] = m_new
            for c, pt in enumerate(parts):
                s_ref[h, :, c * LANES:(c + 1) * LANES] = pt

    def attend(j, first, last):
        start = pl.multiple_of(j * tk, tk)
        accs = []
        for h in range(C_HEADS):
            sl = slice(h * LANES, (h + 1) * LANES)
            m_new = m_ref[h]
            p = jnp.concatenate([jnp.exp2(s_ref[h, :, c * LANES:(c + 1) * LANES] - m_new)
                                 for c in range(n_lane_tiles)], axis=1).astype(BF16)
            acc = _dot(p, v_ref[pl.ds(start, tk), sl])
            if not first:
                acc = alpha_ref[h] * acc_ref[h] + acc
            if not last:
                acc_ref[h] = acc
            elif h % 2 == 0:
                accs = [acc]
            else:
                out_ref[0, :, h // 2 * LANES:(h // 2 + 1) * LANES] = _merge_head_pair(accs[0], acc).astype(BF16)

    scores(i, True)

    @pl.when(i == 0)
    def _():
        attend(0, True, True)

    @pl.when(i >= 1)
    def _():
        attend(i, True, False)
        scores(0, False)

    def full(j, carry):
        attend(j - 1, False, False)
        scores(j, False)
        return carry

    lax.fori_loop(1, i, full, 0)

    @pl.when(i >= 1)
    def _():
        attend(i - 1, False, True)


def _c_prompt(q, ckv, kr, wk, wv):
    b, s, w = q.shape
    tq = ATT_TILE
    scratch = pltpu.VMEM((C_HEADS, tq, LANES), F32)
    per_batch = lambda a: pl.BlockSpec((1, s, a.shape[2]), lambda i, j: (i, 0, 0))
    return pl.pallas_call(
        _c_prompt_kernel,
        grid=(b, s // tq),
        in_specs=[pl.BlockSpec((1, tq, w), lambda i, j: (i, j, 0)), per_batch(ckv), per_batch(kr),
                  _const_spec(wk), _const_spec(wv)],
        out_specs=pl.BlockSpec((1, tq, C_HEADS * C_V), lambda i, j: (i, j, 0)),
        out_shape=jax.ShapeDtypeStruct((b, s, C_HEADS * C_V), BF16),
        scratch_shapes=[pltpu.VMEM((s, w), BF16), pltpu.VMEM((s, w), BF16),
                        scratch, scratch, scratch, pltpu.VMEM((tq, tq), F32),
                        pltpu.VMEM((C_HEADS, tq, tq), F32)],
        compiler_params=_params(("arbitrary", "arbitrary")),
        name="c_prompt",
    )(q, ckv, kr, wk, wv)


def _c_sample_kernel(q_ref, ckv_c_ref, kr_c_ref, ckv_n_ref, kr_n_ref, wk_ref, wv_ref, mask_ref, out_ref):
    tq, n_c = q_ref.shape[1], ckv_c_ref.shape[1]
    q = q_ref[0]
    heads = [q[:, h * LANES:(h + 1) * LANES] for h in range(C_HEADS)]
    qr = jnp.concatenate(heads, axis=0)
    qa = jnp.concatenate([_dot(qh, wk_ref[h]) for h, qh in enumerate(heads)], axis=0).astype(BF16)
    ckv_c, ckv_n = ckv_c_ref[0].astype(BF16), ckv_n_ref[0]
    kr_n = kr_n_ref[0].astype(BF16)
    kr_ct = kr_c_ref[0].astype(BF16)
    kr_ct = jnp.concatenate([jnp.zeros((C_NOPE, n_c), BF16), kr_ct,
                             jnp.zeros((LANES - C_NOPE - C_ROPE, n_c), BF16)], axis=0)
    s_c = _dot_nt(qa, ckv_c) + _dot(qr, kr_ct) + mask_ref[:, :n_c]
    s_n = _dot_nt(qa, ckv_n) + _dot_nt(qr, kr_n) + mask_ref[:, n_c:n_c + tq]
    m = jnp.maximum(jnp.max(s_c, axis=-1, keepdims=True), jnp.max(s_n, axis=-1, keepdims=True))
    e_c, e_n = jnp.exp2(s_c - m), jnp.exp2(s_n - m)
    l = jnp.sum(e_c, axis=-1, keepdims=True) + jnp.sum(e_n, axis=-1, keepdims=True)
    o = ((_dot(e_c.astype(BF16), ckv_c) + _dot(e_n.astype(BF16), ckv_n)) / l).astype(BF16)
    for p in range(C_HEADS // 2):
        rows = lambda h: o[h * tq:(h + 1) * tq]
        pair = _dot(rows(2 * p), wv_ref[2 * p]) + _dot(rows(2 * p + 1), wv_ref[2 * p + 1])
        out_ref[0, :, p * LANES:(p + 1) * LANES] = pair.astype(BF16)


def _c_sample(q, ckv_cache, kr_cache, ckv_new, kr_new, wk, wv, mask):
    b, tq, _ = q.shape
    blk = lambda a: pl.BlockSpec((1,) + a.shape[1:], lambda i: (i,) + (0,) * (a.ndim - 1))
    return pl.pallas_call(
        _c_sample_kernel,
        grid=(b,),
        in_specs=[blk(q), blk(ckv_cache), blk(kr_cache), blk(ckv_new), blk(kr_new),
                  _const_spec(wk), _const_spec(wv), _const_spec(mask)],
        out_specs=pl.BlockSpec((1, tq, C_HEADS * C_V), lambda i: (i, 0, 0)),
        out_shape=jax.ShapeDtypeStruct((b, tq, C_HEADS * C_V), BF16),
        compiler_params=_params(("arbitrary",)),
        name="c_sample",
    )(q, ckv_cache, kr_cache, ckv_new, kr_new, wk, wv, mask)


def _t5_bucket(rel):
    nb = T5_BUCKETS // 2
    max_exact = nb // 2
    ret = jnp.where(rel > 0, nb, 0)
    n = jnp.abs(rel)
    n_f = jnp.maximum(n, 1).astype(F32)
    large = max_exact + (jnp.log(n_f / max_exact) / math.log(T5_MAX_DIST / max_exact)
                         * (nb - max_exact)).astype(jnp.int32)
    large = jnp.minimum(large, nb - 1)
    return ret + jnp.where(n < max_exact, n, large)


def _a_index(q_pos, k_pos, k_real):
    rel = np.clip(q_pos[:, None] - k_pos[None, :], -A_REL_CLIP, A_REL_CLIP) + A_REL_CLIP
    qc, kc = q_pos[:, None] // CHUNK, k_pos[None, :] // CHUNK
    ok = k_real[None, :] & (kc <= qc) & (kc >= qc - A_PAST_CHUNKS)
    return np.where(ok, rel, -1).astype(np.int32)


def _b_index(q_pos, k_pos, k_real):
    idx = _t5_bucket(k_pos[None, :] - q_pos[:, None])
    ok = k_real[None, :] & ((k_pos[None, :] // CHUNK) <= (q_pos[:, None] // CHUNK))
    return jnp.where(ok, idx, -1).astype(jnp.int32)


def _rope_tables(pos, scale):
    half = C_ROPE // 2
    inv = ROPE_THETA ** (-jnp.arange(half, dtype=F32) / half)
    ang = pos.astype(F32)[:, None] * inv[None, :]
    cos = jnp.concatenate([jnp.cos(ang)] * 2, axis=-1)
    sin = jnp.concatenate([jnp.sin(ang)] * 2, axis=-1)
    n = pos.shape[0]
    z_nope = jnp.zeros((n, C_NOPE), F32)
    z_tail = jnp.zeros((n, LANES - C_NOPE - C_ROPE), F32)
    cq = jnp.concatenate([jnp.full((n, C_NOPE), scale, F32), cos * scale, z_tail], axis=-1)
    sq = jnp.concatenate([z_nope, sin * scale, z_tail], axis=-1)
    ck = jnp.concatenate([z_nope, cos, z_tail], axis=-1)
    sk = jnp.concatenate([z_nope, sin, z_tail], axis=-1)
    return cq, sq, ck, sk


def _rot_cols(w):
    half = w.shape[-1] // 2
    return jnp.concatenate([-w[..., half:], w[..., :half]], axis=-1)


def kernel(x_prompt, x_sample, cache_a_k, cache_a_v, cache_b_k, cache_b_v, cache_c_kv, cache_c_kr,
           t5_bias, ffn1_norm, ffn1_w_gu, ffn1_w_down, mix_norm, ffn2_norm, ffn2_w_gu, ffn2_w_down,
           e_w_in, a_rel_bias, b_lambda_q1, b_lambda_k1, b_lambda_q2, b_lambda_k2, b_subln, e_w_out,
           c_w_in, c_q_norm, c_kv_norm, c_w_q_up, c_w_kv_up, c_w_out, final_norm):
    batch, seq, d = x_prompt.shape
    dec_batch, t_new, _ = x_sample.shape
    past = cache_b_k.shape[2]
    n_cache_a = cache_a_k.shape[2]
    a_w = A_HEADS * A_HEAD_DIM
    b_w = B_HEADS * 2 * B_HEAD_DIM
    tq = ATT_TILE
    a_pad = A_PAST_CHUNKS * CHUNK
    a_keep = min(a_pad, seq)
    assert tq + 1 >= T5_MAX_DIST and tq % CHUNK == 0 and a_pad % tq == 0
    far_bucket = T5_BUCKETS // 2 - 1
    lam_init = 0.8 - 0.6 * math.exp(-0.3 * 0)
    c_scale = (C_NOPE + C_ROPE) ** -0.5 * math.log2(math.e)
    row2 = lambda v: v.reshape(1, -1)

    e_in = e_w_in[0].astype(BF16)
    e_out_a, e_out_b = e_w_out[0, :a_w].astype(BF16), e_w_out[0, a_w:].astype(BF16)
    w_in = c_w_in[0]
    w_kr = w_in[:, C_Q_LORA + C_KV_LORA:]
    c_in = jnp.concatenate([w_in[:, :C_Q_LORA + C_KV_LORA], jnp.zeros((d, C_NOPE), F32),
                            w_kr, _rot_cols(w_kr)], axis=-1).astype(BF16)
    wq = c_w_q_up[0].reshape(C_Q_LORA, C_HEADS, C_NOPE + C_ROPE)
    wq_rope = wq[..., C_NOPE:]
    c_q = jnp.concatenate([wq[..., :C_NOPE], wq_rope, _rot_cols(wq_rope)], axis=-1)
    c_q = c_q.reshape(C_Q_LORA, C_HEADS * LANES).astype(BF16)
    wkv = c_w_kv_up[0].reshape(C_KV_LORA, C_HEADS, C_NOPE + C_V)
    c_k = jnp.concatenate([wkv[..., :C_NOPE], jnp.zeros_like(wkv[..., C_NOPE:])], axis=-1)
    c_k = c_k.reshape(C_KV_LORA, C_HEADS * LANES).astype(BF16)
    wv_pairs = wkv[..., C_NOPE:].reshape(C_KV_LORA, C_HEADS // 2, 2, C_V)
    z_v = jnp.zeros_like(wv_pairs[:, :, 0])
    c_v = jnp.stack([jnp.concatenate([wv_pairs[:, :, 0], z_v], axis=-1),
                     jnp.concatenate([z_v, wv_pairs[:, :, 1]], axis=-1)], axis=2)
    c_v = c_v.reshape(C_KV_LORA, C_HEADS * LANES).astype(BF16)
    c_k_abs = jnp.pad(jnp.transpose(wkv[..., :C_NOPE], (1, 2, 0)), ((0, 0), (0, LANES - C_NOPE), (0, 0)))
    c_k_abs = c_k_abs.astype(BF16)
    c_v_abs = jnp.transpose(c_v.reshape(C_KV_LORA, C_HEADS, LANES), (1, 0, 2))
    c_out_w = c_w_out[0].astype(BF16)
    lams = [row2(b_lambda_q1[0]), row2(b_lambda_k1[0]), row2(b_lambda_q2[0]), row2(b_lambda_k2[0])]
    subln = row2(b_subln[0])

    pos_s = past + jnp.arange(t_new)
    r = jnp.arange(tq)
    assert A_TILE % CHUNK == 0 and a_pad % A_TILE == 0
    a_idx_p, a_lay_p = _unique_blocks(
        _a_index(a_pad + np.arange(A_TILE), np.arange(a_pad + A_TILE), np.ones((a_pad + A_TILE,), bool)),
        CHUNK, LANES)
    ka = n_cache_a + t_new
    ka_pad = -(-ka // LANES) * LANES
    pos_s_np = past + np.arange(t_new)
    a_kpos = np.concatenate([past - n_cache_a + np.arange(n_cache_a), pos_s_np,
                             np.zeros((ka_pad - ka,), np.int64)])
    a_real = np.arange(ka_pad) < ka
    a_idx_s, a_lay_s = _unique_blocks(_a_index(pos_s_np, a_kpos, a_real & (a_kpos >= 0)), t_new, LANES)
    b_idx_p = jnp.stack([_b_index(tq + r, r, jnp.ones((tq,), bool)),
                         _b_index(r, r, jnp.ones((tq,), bool))])
    kb = past + t_new
    kb_pad = -(-kb // LANES) * LANES
    b_kpos = jnp.arange(kb_pad)
    b_real = b_kpos < kb
    b_idx_s = _b_index(pos_s, b_kpos, b_real)
    c_mask_s = jnp.where(b_real[None, :] & ((b_kpos[None, :] // CHUNK) <= (pos_s[:, None] // CHUNK)),
                         0.0, NEG).astype(F32)

    a_bias_p = _bias_expand(a_idx_p, a_rel_bias[0], None, a_lay_p)
    a_bias_s = _bias_expand(a_idx_s, a_rel_bias[0], None, a_lay_s)
    t5_t = t5_bias.T
    b_bias_p = jnp.stack([_bias_expand(b_idx_p[0], t5_t, far_bucket),
                          _bias_expand(b_idx_p[1], t5_t, far_bucket)])
    b_bias_s = _bias_expand(b_idx_s, t5_t, far_bucket)

    xp = x_prompt.reshape(batch * seq, d)
    xs = x_sample.reshape(dec_batch * t_new, d)
    g1, gm, g2 = row2(ffn1_norm[0]), row2(mix_norm[0]), row2(ffn2_norm[0])

    xp, xs = _ffn(xp, [], xs, [], [], g1, ffn1_w_gu, ffn1_w_down, 0)

    aq, ak, av, bq, bk, bv, akf, avf, bkf, bvf = _eproj(xp, gm, e_in, seq, a_keep)
    to3 = lambda t, n: t.reshape(n, -1, t.shape[-1])
    a_out_p = _a_prompt(to3(aq, batch), to3(ak, batch), to3(av, batch), a_bias_p)
    b_out_p = _b_prompt(to3(bq, batch), to3(bk, batch), to3(bv, batch), b_bias_p, lams, subln, lam_init)
    p_a_k = akf.reshape(1, batch, a_keep, A_HEADS, A_HEAD_DIM)
    p_a_v = avf.reshape(1, batch, a_keep, A_HEADS, A_HEAD_DIM)
    p_b_k = bkf.reshape(1, batch, seq, B_HEADS, 2 * B_HEAD_DIM)
    p_b_v = bvf.reshape(1, batch, seq, B_HEADS, 2 * B_HEAD_DIM)

    aq, ak, av, bq, bk, bv, akf, avf, bkf, bvf = _eproj(xs, gm, e_in, dec_batch * t_new, dec_batch * t_new)
    dims_major = lambda c: jnp.transpose(c, (0, 2, 3, 1)).reshape(dec_batch, a_w, n_cache_a)
    a_out_s = _a_sample(to3(aq, dec_batch), dims_major(cache_a_k[0]), dims_major(cache_a_v[0]),
                        to3(ak, dec_batch), to3(av, dec_batch), a_bias_s)
    pos_head_rows = lambda c: c.reshape(dec_batch, past * B_HEADS, 2 * B_HEAD_DIM)
    b_out_s = _b_sample(to3(bq, dec_batch), pos_head_rows(cache_b_k[0]), pos_head_rows(cache_b_v[0]),
                        to3(bk, dec_batch), to3(bv, dec_batch), b_bias_s, lams, subln, lam_init)
    s_a_k = akf.reshape(1, dec_batch, t_new, A_HEADS, A_HEAD_DIM)
    s_a_v = avf.reshape(1, dec_batch, t_new, A_HEADS, A_HEAD_DIM)
    s_b_k = bkf.reshape(1, dec_batch, t_new, B_HEADS, 2 * B_HEAD_DIM)
    s_b_v = bvf.reshape(1, dec_batch, t_new, B_HEADS, 2 * B_HEAD_DIM)
    flat = lambda t: t.reshape(-1, t.shape[-1])
    xp, xs = _ffn(xp, [flat(a_out_p), flat(b_out_p)], xs, [flat(a_out_s), flat(b_out_s)], [e_out_a, e_out_b],
                  g2, ffn2_w_gu, ffn2_w_down, 0)

    g1, gm, g2 = row2(ffn1_norm[1]), row2(mix_norm[1]), row2(ffn2_norm[1])
    gq, gkv, gfin = row2(c_q_norm[0]), row2(c_kv_norm[0]), row2(final_norm)
    xp, xs = _ffn(xp, [], xs, [], [], g1, ffn1_w_gu, ffn1_w_down, 1)

    q, ckvf, ckvb, kr, krf = _cproj(xp, gm, c_in, gq, gkv, c_q, _rope_tables(jnp.arange(seq), c_scale))
    c_out_p = _c_prompt(to3(q, batch), to3(ckvb, batch), to3(kr, batch), c_k, c_v)
    p_c_kv = ckvf.reshape(1, batch, seq, C_KV_LORA)
    p_c_kr = krf.reshape(1, batch, seq, C_ROPE)

    q, ckvf, ckvb, kr, krf = _cproj(xs, gm, c_in, gq, gkv, c_q,
                                    _rope_tables(jnp.tile(pos_s, dec_batch), c_scale))
    kr_cache = jnp.transpose(cache_c_kr[0], (0, 2, 1))
    c_out_s = _c_sample(to3(q, dec_batch), cache_c_kv[0], kr_cache, to3(ckvb, dec_batch), to3(kr, dec_batch),
                        c_k_abs, c_v_abs, jnp.tile(c_mask_s, (C_HEADS, 1)))
    s_c_kv = ckvf.reshape(1, dec_batch, t_new, C_KV_LORA)
    s_c_kr = krf.reshape(1, dec_batch, t_new, C_ROPE)
    y_prompt, y_sample = _ffn(xp, [flat(c_out_p)], xs, [flat(c_out_s)], [c_out_w], g2, ffn2_w_gu, ffn2_w_down,
                              1, gfin)

    return (y_prompt.reshape(batch, seq, d), y_sample.reshape(dec_batch, t_new, d),
            p_a_k, p_a_v, p_b_k, p_b_v, p_c_kv, p_c_kr,
            s_a_k, s_a_v, s_b_k, s_b_v, s_c_kv, s_c_kr)
```

```python
import functools
import math

import jax
import jax.numpy as jnp
import numpy as np
from jax import lax
from jax.experimental import pallas as pl
from jax.experimental.pallas import tpu as pltpu

F32 = jnp.float32
BF16 = jnp.bfloat16

EPS = 1e-6
NEG = -1e30
LOG2E = math.log2(math.e)
CHUNK = 64
A_HEADS = 8
A_HEAD_DIM = 64
A_PAST_CHUNKS = 8
A_REL_CLIP = 64
B_HEADS = 4
B_HEAD_DIM = 64
T5_BUCKETS = 32
T5_MAX_DIST = 128
C_HEADS = 16
C_NOPE = 64
C_ROPE = 32
C_V = 64
C_Q_LORA = 384
C_KV_LORA = 256
ROPE_THETA = 10000.0

LANES = 128
SUBLANES = 8
VMEM_LIMIT = 56 * 1024 * 1024
ROW_TILE = 512
ATT_TILE = 256
A_TILE = 256
BIAS_LOOP_UNROLL = 8
BIAS_GROUP_VREGS = 32
TOKEN_SUBTILES = 2
MIN_SUBTILE_ROWS = 256
WEIGHT_CAST_CHUNKS = 8
MXU_WIDTH = 256
FFN_CHUNKS = 2


def _subtile_rows(tm):
    sub = tm // TOKEN_SUBTILES
    return sub if sub * TOKEN_SUBTILES == tm and sub >= MIN_SUBTILE_ROWS else tm


def _ffn_chunks(d_ff):
    tiles = d_ff // MXU_WIDTH
    assert tiles * MXU_WIDTH == d_ff
    per = -(-tiles // FFN_CHUNKS)
    edges = [min(c * per, tiles) * MXU_WIDTH for c in range(FFN_CHUNKS + 1)]
    return [(lo, hi) for lo, hi in zip(edges[:-1], edges[1:]) if hi > lo]


def _rmsnorm(x, g):
    return x * lax.rsqrt(jnp.mean(x * x, axis=-1, keepdims=True) + EPS) * g


def _dot(a, b):
    return jnp.dot(a, b, preferred_element_type=F32)


def _dot_nt(a, b):
    return lax.dot_general(a, b, (((1,), (1,)), ((), ())), preferred_element_type=F32)


def _const_spec(a):
    nd = a.ndim
    return pl.BlockSpec(a.shape, lambda *_: (0,) * nd, pipeline_mode=pl.Buffered(1))


def _params(sem, flags=None):
    return pltpu.CompilerParams(dimension_semantics=sem, vmem_limit_bytes=VMEM_LIMIT, flags=flags)


def _weight_chunk_copy(src, stage, sems, c, slot):
    rows = stage.shape[1]
    return pltpu.make_async_copy(src.at[pl.ds(c * rows, rows), :], stage.at[slot], sems.at[slot])


def _load_weight_as_bf16(src, stage, sems, dst):
    rows = stage.shape[1]
    n = src.shape[0] // rows
    _weight_chunk_copy(src, stage, sems, 0, 0).start()
    for c in range(n):
        slot = c % 2
        if c + 1 < n:
            _weight_chunk_copy(src, stage, sems, c + 1, 1 - slot).start()
        _weight_chunk_copy(src, stage, sems, c, slot).wait()
        dst[c * rows:(c + 1) * rows, :] = stage[slot].astype(BF16)


def _ffn_kernel(n_pre, with_final, layer, *refs):
    group = 1 + n_pre
    groups = [refs[:group], refs[group:2 * group]]
    idx = 2 * group
    pre_w = refs[idx:idx + n_pre]
    idx += n_pre
    g_ref, wgu_in, wd_in = refs[idx:idx + 3]
    idx += 3
    gf_ref = refs[idx] if with_final else None
    idx += with_final
    outs = refs[idx:idx + 2]
    wgu_ref, wd_ref, gu_stage, d_stage, sems = refs[idx + 2:]
    step, last = pl.program_id(0), pl.num_programs(0) - 1

    @pl.when(step == 0)
    def _():
        _load_weight_as_bf16(wgu_in.at[layer], gu_stage, sems, wgu_ref)
        _load_weight_as_bf16(wd_in.at[layer], d_stage, sems, wd_ref)

    def run(x_ref, o_refs, out_ref):
        x = x_ref[...]
        for o_ref, w_ref in zip(o_refs, pre_w):
            x = x + _dot(o_ref[...], w_ref[...])
        xn = _rmsnorm(x, g_ref[...]).astype(BF16)
        d_ff = wd_ref.shape[0]
        acc = jnp.zeros_like(x)
        for lo, hi in _ffn_chunks(d_ff):
            gate = _dot(xn, wgu_ref[:, lo:hi])
            up = _dot(xn, wgu_ref[:, d_ff + lo:d_ff + hi])
            act = (jax.nn.silu(gate) * up).astype(BF16)
            acc = acc + _dot(act, wd_ref[lo:hi, :])
        y = x + 0.5 * acc
        if with_final:
            y = _rmsnorm(y, gf_ref[...])
        out_ref[...] = y

    @pl.when(step < last)
    def _():
        run(groups[0][0], groups[0][1:], outs[0])

    @pl.when(step == last)
    def _():
        run(groups[1][0], groups[1][1:], outs[1])


def _ffn(x_prompt, pres_prompt, x_sample, pres_sample, pre_weights, g, wgu, wd, layer, g_final=None):
    t, d = x_prompt.shape
    tm = min(ROW_TILE, t)
    n = t // tm
    row = lambda w: pl.BlockSpec((tm, w), lambda i: (jnp.minimum(i, n - 1), 0))
    whole = lambda a: pl.BlockSpec(a.shape, lambda i: (0, 0))
    hbm = pl.BlockSpec(memory_space=pl.ANY)
    args = [x_prompt, *pres_prompt, x_sample, *pres_sample, *pre_weights, g, wgu, wd]
    specs = ([row(d)] + [row(o.shape[1]) for o in pres_prompt] + [whole(x_sample)] + [whole(o) for o in pres_sample]
             + [_const_spec(w) for w in pre_weights] + [_const_spec(g), hbm, hbm])
    if g_final is not None:
        args.append(g_final)
        specs.append(_const_spec(g_final))
    gu_shape, d_shape = wgu.shape[1:], wd.shape[1:]
    chunks = WEIGHT_CAST_CHUNKS
    return pl.pallas_call(
        functools.partial(_ffn_kernel, len(pre_weights), g_final is not None, layer),
        grid=(n + 1,),
        in_specs=specs,
        out_specs=[row(d), whole(x_sample)],
        out_shape=[jax.ShapeDtypeStruct((t, d), F32), jax.ShapeDtypeStruct(x_sample.shape, F32)],
        scratch_shapes=[pltpu.VMEM(gu_shape, BF16), pltpu.VMEM(d_shape, BF16),
                        pltpu.VMEM((2, gu_shape[0] // chunks, gu_shape[1]), F32),
                        pltpu.VMEM((2, d_shape[0] // chunks, d_shape[1]), F32),
                        pltpu.SemaphoreType.DMA((2,))],
        compiler_params=_params(("arbitrary",)),
        name="ffn",
    )(*args)


def _store_values_with_ones(out_ref, v):
    low = lax.broadcasted_iota(jnp.int32, (1, LANES), 1) < LANES // 2
    for p in range(v.shape[1] // LANES):
        vp = v[:, p * LANES:(p + 1) * LANES]
        out_ref[:, 2 * p * LANES:(2 * p + 1) * LANES] = jnp.where(low, vp, 1.0).astype(out_ref.dtype)
        out_ref[:, (2 * p + 1) * LANES:(2 * p + 2) * LANES] = jnp.where(low, 1.0, vp).astype(out_ref.dtype)


def _merge_head_pair(acc_even, acc_odd):
    half = LANES // 2
    low = lax.broadcasted_iota(jnp.int32, (1, LANES), 1) < half
    den = pltpu.roll(jnp.where(low, acc_odd, acc_even), half, 1)
    return jnp.where(low, acc_even, acc_odd) / den


def _eproj_kernel(x_ref, g_ref, w_ref, aq, ak, av, bq, bk, bv, akf, avf, bkf, bvf):
    tm = x_ref.shape[0]
    sub = _subtile_rows(tm)
    w = aq.shape[1]
    for r0 in range(0, tm, sub):
        rows = slice(r0, r0 + sub)
        hn = _rmsnorm(x_ref[rows, :], g_ref[...]).astype(BF16)
        t = _dot(hn, w_ref[...])
        parts = [t[:, i * w:(i + 1) * w] for i in range(6)]
        aq[rows, :] = (parts[0] * (A_HEAD_DIM ** -0.5 * LOG2E)).astype(BF16)
        ak[rows, :] = parts[1].astype(BF16)
        _store_values_with_ones(av.at[rows, :], parts[2])
        bq[rows, :] = (parts[3] * (B_HEAD_DIM ** -0.5 * LOG2E)).astype(BF16)
        bk[rows, :] = parts[4].astype(BF16)
        ones = jnp.ones((sub, LANES), BF16)
        for h in range(B_HEADS):
            bv[rows, 2 * h * LANES:(2 * h + 1) * LANES] = parts[5][:, h * LANES:(h + 1) * LANES].astype(BF16)
            bv[rows, (2 * h + 1) * LANES:(2 * h + 2) * LANES] = ones
        akf[rows] = parts[1].reshape((sub,) + akf.shape[1:])
        avf[rows] = parts[2].reshape((sub,) + avf.shape[1:])
        bkf[rows] = parts[4].reshape((sub,) + bkf.shape[1:])
        bvf[rows] = parts[5].reshape((sub,) + bvf.shape[1:])


def _eproj(x, g, w, rows_per_batch, a_keep):
    t, d = x.shape
    tm = min(ROW_TILE, t, a_keep)
    assert rows_per_batch % tm == 0 and a_keep % tm == 0
    per, kept = rows_per_batch // tm, a_keep // tm
    wd = w.shape[1] // 6
    row = lambda n: pl.BlockSpec((tm, n), lambda i: (i, 0))
    a_map = lambda i: ((i // per) * kept + jnp.maximum(i % per - (per - kept), 0), 0, 0)
    a_f32 = pl.BlockSpec((tm, A_HEADS, A_HEAD_DIM), a_map)
    b_f32 = pl.BlockSpec((tm, B_HEADS, 2 * B_HEAD_DIM), lambda i: (i, 0, 0))
    widths = [wd, wd, 2 * wd, wd, wd, 2 * wd]
    shapes = ([jax.ShapeDtypeStruct((t, n), BF16) for n in widths]
              + [jax.ShapeDtypeStruct((t // per * kept, A_HEADS, A_HEAD_DIM), F32)] * 2
              + [jax.ShapeDtypeStruct((t, B_HEADS, 2 * B_HEAD_DIM), F32)] * 2)
    return pl.pallas_call(
        _eproj_kernel,
        grid=(t // tm,),
        in_specs=[row(d), _const_spec(g), _const_spec(w)],
        out_specs=[row(n) for n in widths] + [a_f32, a_f32, b_f32, b_f32],
        out_shape=shapes,
        compiler_params=_params(("arbitrary",)),
        name="eproj",
    )(x, g, w)


def _cproj_kernel(x_ref, g_ref, win_ref, gq_ref, gkv_ref, wq_ref, cq_ref, sq_ref, ck_ref, sk_ref,
                  q_out, ckvf_out, ckvb_out, kr_out, krf_out):
    tm = x_ref.shape[0]
    sub = _subtile_rows(tm)
    for r0 in range(0, tm, sub):
        rows = slice(r0, r0 + sub)
        hn = _rmsnorm(x_ref[rows, :], g_ref[...]).astype(BF16)
        t = _dot(hn, win_ref[...])
        cqn = _rmsnorm(t[:, :C_Q_LORA], gq_ref[...]).astype(BF16)
        ckv = _rmsnorm(t[:, C_Q_LORA:C_Q_LORA + C_KV_LORA], gkv_ref[...])
        ckvf_out[rows, :] = ckv
        ckvb_out[rows, :] = ckv.astype(BF16)
        tg = t[:, C_Q_LORA + C_KV_LORA:]
        kr = tg * ck_ref[rows, :] + pltpu.roll(tg, LANES - C_ROPE, 1) * sk_ref[rows, :]
        kr_out[rows, :] = kr
        krf_out[rows, :] = kr[:, C_NOPE:C_NOPE + C_ROPE]
        q = _dot(cqn, wq_ref[...])
        cq, sq = cq_ref[rows, :], sq_ref[rows, :]
        for h in range(C_HEADS):
            qh = q[:, h * LANES:(h + 1) * LANES]
            qf = qh * cq + pltpu.roll(qh, LANES - C_ROPE, 1) * sq
            q_out[rows, h * LANES:(h + 1) * LANES] = qf.astype(BF16)


def _cproj(x, g, win, gq, gkv, wq, tabs):
    t, d = x.shape
    tm = min(ROW_TILE, t)
    nper = tabs[0].shape[0] // tm
    row = lambda n: pl.BlockSpec((tm, n), lambda i: (i, 0))
    tab = pl.BlockSpec((tm, LANES), lambda i: (i % nper, 0))
    shapes = [jax.ShapeDtypeStruct((t, C_HEADS * LANES), BF16),
              jax.ShapeDtypeStruct((t, C_KV_LORA), F32),
              jax.ShapeDtypeStruct((t, C_KV_LORA), BF16),
              jax.ShapeDtypeStruct((t, LANES), F32),
              jax.ShapeDtypeStruct((t, C_ROPE), F32)]
    return pl.pallas_call(
        _cproj_kernel,
        grid=(t // tm,),
        in_specs=[row(d), _const_spec(g), _const_spec(win), _const_spec(gq), _const_spec(gkv),
                  _const_spec(wq), tab, tab, tab, tab],
        out_specs=[row(C_HEADS * LANES), row(C_KV_LORA), row(C_KV_LORA), row(LANES), row(C_ROPE)],
        out_shape=shapes,
        compiler_params=_params(("arbitrary",)),
        name="cproj",
    )(x, g, win, gq, gkv, wq, *tabs)


def _bias_kernel(n_rows, shift, layout, idx_ref, tab_ref, out_ref):
    n_heads = out_ref.shape[0]
    bq = out_ref.shape[1] // len(layout)
    bk = out_ref.shape[2] // len(layout[0])
    block_vregs = max(1, bq * bk // (SUBLANES * LANES))
    group = max(1, min(n_heads, BIAS_GROUP_VREGS // block_vregs))
    for u in range(idx_ref.shape[0] // bq):
        idx = idx_ref[u * bq:(u + 1) * bq, :]
        for h0 in range(0, n_heads, group):
            heads = list(range(h0, min(h0 + group, n_heads)))
            bases = [tab_ref[h, shift] if shift is not None else 0.0 for h in heads]

            def body(r, accs):
                hit = idx == r
                return tuple(jnp.where(hit, (tab_ref[h, r] - b) * LOG2E, acc)
                             for h, b, acc in zip(heads, bases, accs))

            vals = lax.fori_loop(0, n_rows, body, tuple(jnp.full(idx.shape, NEG, F32) for _ in heads),
                                 unroll=BIAS_LOOP_UNROLL)
            for h, val in zip(heads, vals):
                for a, row in enumerate(layout):
                    for t, uu in enumerate(row):
                        if uu == u:
                            out_ref[h, a * bq:(a + 1) * bq, t * bk:(t + 1) * bk] = val


def _bias_expand(idx, table, shift=None, layout=((0,),)):
    n_heads, n_rows = table.shape
    n_blocks = 1 + max(max(row) for row in layout)
    out_shape = (n_heads, idx.shape[0] // n_blocks * len(layout), idx.shape[1] * len(layout[0]))
    return pl.pallas_call(
        functools.partial(_bias_kernel, n_rows, shift, layout),
        in_specs=[pl.BlockSpec(idx.shape, lambda: (0, 0)),
                  pl.BlockSpec(memory_space=pltpu.SMEM)],
        out_specs=pl.BlockSpec(out_shape, lambda: (0, 0, 0)),
        out_shape=jax.ShapeDtypeStruct(out_shape, F32),
        compiler_params=pltpu.CompilerParams(vmem_limit_bytes=VMEM_LIMIT),
        name="bias_expand",
    )(idx, table)


def _unique_blocks(idx, bq, bk):
    q, k = idx.shape
    blocks = idx.reshape(q // bq, bq, k // bk, bk).transpose(0, 2, 1, 3).reshape(-1, bq, bk)
    uniq, inv = np.unique(blocks, axis=0, return_inverse=True)
    layout = tuple(tuple(int(u) for u in row) for row in inv.reshape(q // bq, k // bk))
    return jnp.asarray(uniq.reshape(-1, bk), jnp.int32), layout


def _a_attend(q_ref, k_ref, v_ref, starts, tk, bias_ref, valid, out_ref, s_ref):
    low = lax.broadcasted_iota(jnp.int32, (1, LANES), 1) < A_HEAD_DIM
    window = lambda ref, sl: jnp.concatenate([ref[0, pl.ds(st, tk), sl] for st in starts], axis=0)
    for h in range(A_HEADS):
        sl = slice(h // 2 * LANES, (h // 2 + 1) * LANES)
        qp = q_ref[0, :, sl]
        qm = jnp.where(low if h % 2 == 0 else jnp.logical_not(low), qp, jnp.zeros_like(qp))
        s = _dot_nt(qm, window(k_ref, sl)) + bias_ref[h]
        if valid is not None:
            s = jnp.where(valid, s, NEG)
        s_ref[h] = s - jnp.max(s, axis=-1, keepdims=True)
    accs = []
    for h in range(A_HEADS):
        accs.append(_dot(jnp.exp2(s_ref[h]).astype(BF16), window(v_ref, slice(h * LANES, (h + 1) * LANES))))
        if h % 2 == 1:
            out_ref[0, :, h // 2 * LANES:(h // 2 + 1) * LANES] = _merge_head_pair(*accs).astype(BF16)
            accs = []


def _a_prompt_kernel(q_ref, k_ref, v_ref, bias_ref, out_ref, s_ref):
    tq = q_ref.shape[1]
    win = bias_ref.shape[2]
    n_tiles = win // tq
    j = pl.program_id(1)

    def run(check_positions):
        starts = [pl.multiple_of(jnp.maximum(j - (n_tiles - 1 - t), 0) * tq, tq) for t in range(n_tiles)]
        valid = None
        if check_positions:
            valid = lax.broadcasted_iota(jnp.int32, (1, win), 1) + (j - (n_tiles - 1)) * tq >= 0
        _a_attend(q_ref, k_ref, v_ref, starts, tq, bias_ref, valid, out_ref, s_ref)

    @pl.when(j < n_tiles - 1)
    def _():
        run(True)

    @pl.when(j >= n_tiles - 1)
    def _():
        run(False)


def _a_prompt(q, k, v, bias):
    b, s, w = q.shape
    tq = bias.shape[1]
    kv = lambda a: pl.BlockSpec((1, s, a.shape[2]), lambda i, j: (i, 0, 0))
    return pl.pallas_call(
        _a_prompt_kernel,
        grid=(b, s // tq),
        in_specs=[pl.BlockSpec((1, tq, w), lambda i, j: (i, j, 0)), kv(k), kv(v), _const_spec(bias)],
        out_specs=pl.BlockSpec((1, tq, w), lambda i, j: (i, j, 0)),
        out_shape=jax.ShapeDtypeStruct((b, s, w), BF16),
        scratch_shapes=[pltpu.VMEM(bias.shape, F32)],
        compiler_params=_params(("arbitrary", "arbitrary")),
        name="a_prompt",
    )(q, k, v, bias)


def _a_sample_kernel(q_ref, kct_ref, vct_ref, kn_ref, vn_ref, bias_ref, out_ref):
    q = q_ref[0]
    tq, n_c = q.shape[0], kct_ref.shape[2]
    low = lax.broadcasted_iota(jnp.int32, (1, LANES), 1) < A_HEAD_DIM
    for p in range(A_HEADS // 2):
        sl = slice(p * LANES, (p + 1) * LANES)
        qp = q[:, sl]
        kct, vct = kct_ref[0, sl, :].astype(BF16), vct_ref[0, sl, :].astype(BF16)
        outs = []
        for sub in range(2):
            h = 2 * p + sub
            qm = jnp.where(low if sub == 0 else jnp.logical_not(low), qp, jnp.zeros_like(qp))
            s_c = _dot(qm, kct) + bias_ref[h, :, :n_c]
            s_n = _dot_nt(qm, kn_ref[0, :, sl]) + bias_ref[h, :, n_c:n_c + tq]
            m = jnp.maximum(jnp.max(s_c, axis=-1, keepdims=True), jnp.max(s_n, axis=-1, keepdims=True))
            e_c, e_n = jnp.exp2(s_c - m), jnp.exp2(s_n - m)
            l = jnp.sum(e_c, axis=-1, keepdims=True) + jnp.sum(e_n, axis=-1, keepdims=True)
            acc = _dot_nt(e_c.astype(BF16), vct) + _dot(e_n.astype(BF16), vn_ref[0, :, h * LANES:(h + 1) * LANES])
            outs.append(acc / l)
        out_ref[0, :, sl] = jnp.where(low, outs[0], outs[1]).astype(BF16)


def _a_sample(q, k_cache, v_cache, k_new, v_new, bias):
    b, tq, w = q.shape
    blk = lambda a: pl.BlockSpec((1,) + a.shape[1:], lambda i: (i,) + (0,) * (a.ndim - 1))
    return pl.pallas_call(
        _a_sample_kernel,
        grid=(b,),
        in_specs=[blk(q), blk(k_cache), blk(v_cache), blk(k_new), blk(v_new), _const_spec(bias)],
        out_specs=blk(q),
        out_shape=jax.ShapeDtypeStruct((b, tq, w), BF16),
        compiler_params=_params(("arbitrary",)),
        name="a_sample",
    )(q, k_cache, v_cache, k_new, v_new, bias)


def _b_lambda(lq1, lk1, lq2, lk2, lam_init):
    s1 = jnp.sum(lq1[...] * lk1[...], axis=-1, keepdims=True)
    s2 = jnp.sum(lq2[...] * lk2[...], axis=-1, keepdims=True)
    return jnp.exp(s1) - jnp.exp(s2) + lam_init


def _b_stack_queries(q):
    lane = lax.broadcasted_iota(jnp.int32, (1, LANES), 1)
    low = lane < B_HEAD_DIM
    qs = []
    for h in range(B_HEADS):
        qh = q[:, h * LANES:(h + 1) * LANES]
        zero = jnp.zeros_like(qh)
        qs.append(jnp.concatenate([jnp.where(low, qh, zero), jnp.where(low, zero, qh)], axis=0))
    return qs


def _b_finish(o, lam, g, lam_init, tq):
    ob = o[:tq] - lam * o[tq:]
    return _rmsnorm(ob, g) * (1.0 - lam_init)


def _b_prompt_kernel(lam_init, q_ref, k_ref, v_ref, bias_ref, lq1, lk1, lq2, lk2, g_ref, out_ref,
                     qs_ref, m_ref, alpha_ref, acc_ref, s_ref):
    tq = q_ref.shape[1]
    tk = bias_ref.shape[3]
    i = pl.program_id(1)
    for h, qh in enumerate(_b_stack_queries(q_ref[0])):
        qs_ref[h] = qh

    lane_tiles = range(tk // LANES)
    lam = _b_lambda(lq1, lk1, lq2, lk2, lam_init)

    def scores(j, bias_sel, first):
        start = pl.multiple_of(j * tk, tk)
        for h in range(B_HEADS):
            sl = slice(h * LANES, (h + 1) * LANES)
            s = _dot_nt(qs_ref[h], k_ref[0, pl.ds(start, tk), sl])
            parts = [s[:, c * LANES:(c + 1) * LANES] for c in lane_tiles]
            if bias_sel is not None:
                parts = [jnp.concatenate([pt[:tq] + bias_ref[bias_sel, h, :, c * LANES:(c + 1) * LANES],
                                          pt[tq:] + bias_ref[bias_sel, h, :, c * LANES:(c + 1) * LANES]], axis=0)
                         for c, pt in enumerate(parts)]
            mx = parts[0]
            for pt in parts[1:]:
                mx = jnp.maximum(mx, pt)
            row_max = jnp.max(mx, axis=-1, keepdims=True)
            if first:
                m_new = jnp.broadcast_to(row_max, (2 * tq, LANES))
            else:
                m_new = jnp.maximum(m_ref[h], row_max)
                alpha_ref[h] = jnp.exp2(m_ref[h] - m_new)
            m_ref[h] = m_new
            for c, pt in enumerate(parts):
                s_ref[h, :, c * LANES:(c + 1) * LANES] = pt

    def attend(j, first, last):
        start = pl.multiple_of(j * tk, tk)
        for h in range(B_HEADS):
            m_new = m_ref[h]
            p = jnp.concatenate([jnp.exp2(s_ref[h, :, c * LANES:(c + 1) * LANES] - m_new) for c in lane_tiles],
                                axis=1).astype(BF16)
            pv = _dot(p, v_ref[0, pl.ds(start, tk), 2 * h * LANES:2 * (h + 1) * LANES])
            num, den = pv[:, :LANES], pv[:, LANES:]
            if not first:
                alpha = alpha_ref[h]
                num = alpha * acc_ref[h, :, :LANES] + num
                den = alpha * acc_ref[h, :, LANES:] + den
            if last:
                bn = _b_finish(num / den, lam, g_ref[...], lam_init, tq)
                out_ref[0, :, h * LANES:(h + 1) * LANES] = bn.astype(BF16)
            else:
                acc_ref[h, :, :LANES] = num
                acc_ref[h, :, LANES:] = den

    scores(i, 1, True)

    @pl.when(i == 0)
    def _():
        attend(0, True, True)

    @pl.when(i >= 1)
    def _():
        attend(i, True, False)
        scores(i - 1, 0, False)

    def trip(t, carry):
        attend(i - t, False, False)
        scores(i - 1 - t, None, False)
        return carry

    lax.fori_loop(1, i, trip, 0)

    @pl.when(i >= 1)
    def _():
        attend(0, False, True)


def _b_prompt(q, k, v, bias, lams, g, lam_init):
    b, s, w = q.shape
    tq = bias.shape[2]
    kv = lambda a: pl.BlockSpec((1, s, a.shape[2]), lambda i, j: (i, 0, 0))
    scratch = pltpu.VMEM((B_HEADS, 2 * tq, LANES), F32)
    return pl.pallas_call(
        functools.partial(_b_prompt_kernel, lam_init),
        grid=(b, s // tq),
        in_specs=[pl.BlockSpec((1, tq, w), lambda i, j: (i, j, 0)), kv(k), kv(v), _const_spec(bias)]
                 + [_const_spec(x) for x in lams] + [_const_spec(g)],
        out_specs=pl.BlockSpec((1, tq, w), lambda i, j: (i, j, 0)),
        out_shape=jax.ShapeDtypeStruct((b, s, w), BF16),
        scratch_shapes=[pltpu.VMEM((B_HEADS, 2 * tq, LANES), BF16), scratch, scratch,
                        pltpu.VMEM((B_HEADS, 2 * tq, 2 * LANES), F32), pltpu.VMEM((B_HEADS, 2 * tq, tq), F32)],
        compiler_params=_params(("arbitrary", "arbitrary")),
        name="b_prompt",
    )(q, k, v, bias, *lams, g)


def _b_sample_kernel(lam_init, q_ref, kc_ref, vc_ref, kn_ref, vn_ref, bias_ref, lq1, lk1, lq2, lk2, g_ref,
                     out_ref):
    tq, n_c = q_ref.shape[1], kc_ref.shape[1] // B_HEADS
    qs = _b_stack_queries(q_ref[0])
    lam = _b_lambda(lq1, lk1, lq2, lk2, lam_init)
    both = lambda bias: jnp.concatenate([bias, bias], axis=0)
    for h in range(B_HEADS):
        sl = slice(h * LANES, (h + 1) * LANES)
        head_rows = pl.ds(h, n_c, stride=B_HEADS)
        kc, vc = kc_ref[0, head_rows, :].astype(BF16), vc_ref[0, head_rows, :].astype(BF16)
        s_c = _dot_nt(qs[h], kc) + both(bias_ref[h, :, :n_c])
        s_n = _dot_nt(qs[h], kn_ref[0, :, sl]) + both(bias_ref[h, :, n_c:n_c + tq])
        m = jnp.maximum(jnp.max(s_c, axis=-1, keepdims=True), jnp.max(s_n, axis=-1, keepdims=True))
        e_c, e_n = jnp.exp2(s_c - m), jnp.exp2(s_n - m)
        l = jnp.sum(e_c, axis=-1, keepdims=True) + jnp.sum(e_n, axis=-1, keepdims=True)
        o = (_dot(e_c.astype(BF16), vc) + _dot(e_n.astype(BF16), vn_ref[0, :, 2 * h * LANES:(2 * h + 1) * LANES])) / l
        bn = _b_finish(o, lam, g_ref[...], lam_init, tq)
        out_ref[0, :, sl] = bn.astype(BF16)


def _b_sample(q, k_cache, v_cache, k_new, v_new, bias, lams, g, lam_init):
    b, tq, w = q.shape
    blk = lambda a: pl.BlockSpec((1,) + a.shape[1:], lambda i: (i,) + (0,) * (a.ndim - 1))
    return pl.pallas_call(
        functools.partial(_b_sample_kernel, lam_init),
        grid=(b,),
        in_specs=[blk(q), blk(k_cache), blk(v_cache), blk(k_new), blk(v_new), _const_spec(bias)]
                 + [_const_spec(x) for x in lams] + [_const_spec(g)],
        out_specs=blk(q),
        out_shape=jax.ShapeDtypeStruct((b, tq, w), BF16),
        compiler_params=_params(("arbitrary",)),
        name="b_sample",
    )(q, k_cache, v_cache, k_new, v_new, bias, *lams, g)


def _c_prompt_kernel(q_ref, ckv_ref, kr_ref, wk_ref, wv_ref, out_ref, k_ref, v_ref, m_ref, alpha_ref, acc_ref,
                     mask_ref, s_ref):
    tq = q_ref.shape[1]
    tk = tq
    n_lane_tiles = tk // LANES
    i = pl.program_id(1)

    @pl.when((pl.program_id(0) == 0) & (i == 0))
    def _():
        shift = CHUNK.bit_length() - 1
        rc = jnp.right_shift(lax.broadcasted_iota(jnp.int32, (tq, tk), 0), shift)
        kc = jnp.right_shift(lax.broadcasted_iota(jnp.int32, (tq, tk), 1), shift)
        mask_ref[...] = jnp.where(kc <= rc, 0.0, NEG)

    @pl.when(i == 0)
    def _():
        high = (lax.broadcasted_iota(jnp.int32, (1, LANES), 1) >= C_V).astype(F32)
        rows_per_chunk = 2 * tk

        def expand(c, carry):
            rows = pl.ds(pl.multiple_of(c * rows_per_chunk, rows_per_chunk), rows_per_chunk)
            ckv = ckv_ref[0, rows, :]
            kd = _dot(ckv, wk_ref[...])
            vd = _dot(ckv, wv_ref[...])
            kr = kr_ref[0, rows, :]
            for h in range(C_HEADS):
                sl = slice(h * LANES, (h + 1) * LANES)
                k_ref[rows, sl] = (kd[:, sl] + kr).astype(BF16)
                v_ref[rows, sl] = (vd[:, sl] + (high if h % 2 == 0 else 1.0 - high)).astype(BF16)
            return carry

        lax.fori_loop(0, k_ref.shape[0] // rows_per_chunk, expand, 0)

    def scores(j, first):
        start = pl.multiple_of(j * tk, tk)
        for h in range(C_HEADS):
            sl = slice(h * LANES, (h + 1) * LANES)
            s = _dot_nt(q_ref[0, :, sl], k_ref[pl.ds(start, tk), sl])
            parts = [s[:, c * LANES:(c + 1) * LANES] for c in range(n_lane_tiles)]
            if first:
                parts = [pt + mask_ref[:, c * LANES:(c + 1) * LANES] for c, pt in enumerate(parts)]
            mx = parts[0]
            for pt in parts[1:]:
                mx = jnp.maximum(mx, pt)
            row_max = jnp.max(mx, axis=-1, keepdims=True)
            if first:
                m_new = jnp.broadcast_to(row_max, (tq, LANES))
            else:
                m_new = jnp.maximum(m_ref[h], row_max)
                alpha_ref[h] = jnp.exp2(m_ref[h] - m_new)
            m_ref[h] = m_new
            for c, pt in enumerate(parts):
                s_ref[h, :, c * LANES:(c + 1) * LANES] = pt

    def attend(j, first, last):
        start = pl.multiple_of(j * tk, tk)
        accs = []
        for h in range(C_HEADS):
            sl = slice(h * LANES, (h + 1) * LANES)
            m_new = m_ref[h]
            p = jnp.concatenate([jnp.exp2(s_ref[h, :, c * LANES:(c + 1) * LANES] - m_new)
                                 for c in range(n_lane_tiles)], axis=1).astype(BF16)
            acc = _dot(p, v_ref[pl.ds(start, tk), sl])
            if not first:
                acc = alpha_ref[h] * acc_ref[h] + acc
            if not last:
                acc_ref[h] = acc
            elif h % 2 == 0:
                accs = [acc]
            else:
                out_ref[0, :, h // 2 * LANES:(h // 2 + 1) * LANES] = _merge_head_pair(accs[0], acc).astype(BF16)

    scores(i, True)

    @pl.when(i == 0)
    def _():
        attend(0, True, True)

    @pl.when(i >= 1)
    def _():
        attend(i, True, False)
        scores(0, False)

    def full(j, carry):
        attend(j - 1, False, False)
        scores(j, False)
        return carry

    lax.fori_loop(1, i, full, 0)

    @pl.when(i >= 1)
    def _():
        attend(i - 1, False, True)


def _c_prompt(q, ckv, kr, wk, wv):
    b, s, w = q.shape
    tq = ATT_TILE
    scratch = pltpu.VMEM((C_HEADS, tq, LANES), F32)
    per_batch = lambda a: pl.BlockSpec((1, s, a.shape[2]), lambda i, j: (i, 0, 0))
    return pl.pallas_call(
        _c_prompt_kernel,
        grid=(b, s // tq),
        in_specs=[pl.BlockSpec((1, tq, w), lambda i, j: (i, j, 0)), per_batch(ckv), per_batch(kr),
                  _const_spec(wk), _const_spec(wv)],
        out_specs=pl.BlockSpec((1, tq, C_HEADS * C_V), lambda i, j: (i, j, 0)),
        out_shape=jax.ShapeDtypeStruct((b, s, C_HEADS * C_V), BF16),
        scratch_shapes=[pltpu.VMEM((s, w), BF16), pltpu.VMEM((s, w), BF16),
                        scratch, scratch, scratch, pltpu.VMEM((tq, tq), F32),
                        pltpu.VMEM((C_HEADS, tq, tq), F32)],
        compiler_params=_params(("arbitrary", "arbitrary")),
        name="c_prompt",
    )(q, ckv, kr, wk, wv)


def _c_sample_kernel(q_ref, ckv_c_ref, kr_c_ref, ckv_n_ref, kr_n_ref, wk_ref, wv_ref, mask_ref, out_ref):
    tq, n_c = q_ref.shape[1], ckv_c_ref.shape[1]
    q = q_ref[0]
    heads = [q[:, h * LANES:(h + 1) * LANES] for h in range(C_HEADS)]
    qr = jnp.concatenate(heads, axis=0)
    qa = jnp.concatenate([_dot(qh, wk_ref[h]) for h, qh in enumerate(heads)], axis=0).astype(BF16)
    ckv_c, ckv_n = ckv_c_ref[0].astype(BF16), ckv_n_ref[0]
    kr_n = kr_n_ref[0].astype(BF16)
    kr_ct = kr_c_ref[0].astype(BF16)
    kr_ct = jnp.concatenate([jnp.zeros((C_NOPE, n_c), BF16), kr_ct,
                             jnp.zeros((LANES - C_NOPE - C_ROPE, n_c), BF16)], axis=0)
    s_c = _dot_nt(qa, ckv_c) + _dot(qr, kr_ct) + mask_ref[:, :n_c]
    s_n = _dot_nt(qa, ckv_n) + _dot_nt(qr, kr_n) + mask_ref[:, n_c:n_c + tq]
    m = jnp.maximum(jnp.max(s_c, axis=-1, keepdims=True), jnp.max(s_n, axis=-1, keepdims=True))
    e_c, e_n = jnp.exp2(s_c - m), jnp.exp2(s_n - m)
    l = jnp.sum(e_c, axis=-1, keepdims=True) + jnp.sum(e_n, axis=-1, keepdims=True)
    o = ((_dot(e_c.astype(BF16), ckv_c) + _dot(e_n.astype(BF16), ckv_n)) / l).astype(BF16)
    for p in range(C_HEADS // 2):
        rows = lambda h: o[h * tq:(h + 1) * tq]
        pair = _dot(rows(2 * p), wv_ref[2 * p]) + _dot(rows(2 * p + 1), wv_ref[2 * p + 1])
        out_ref[0, :, p * LANES:(p + 1) * LANES] = pair.astype(BF16)


def _c_sample(q, ckv_cache, kr_cache, ckv_new, kr_new, wk, wv, mask):
    b, tq, _ = q.shape
    blk = lambda a: pl.BlockSpec((1,) + a.shape[1:], lambda i: (i,) + (0,) * (a.ndim - 1))
    return pl.pallas_call(
        _c_sample_kernel,
        grid=(b,),
        in_specs=[blk(q), blk(ckv_cache), blk(kr_cache), blk(ckv_new), blk(kr_new),
                  _const_spec(wk), _const_spec(wv), _const_spec(mask)],
        out_specs=pl.BlockSpec((1, tq, C_HEADS * C_V), lambda i: (i, 0, 0)),
        out_shape=jax.ShapeDtypeStruct((b, tq, C_HEADS * C_V), BF16),
        compiler_params=_params(("arbitrary",)),
        name="c_sample",
    )(q, ckv_cache, kr_cache, ckv_new, kr_new, wk, wv, mask)


def _t5_bucket(rel):
    nb = T5_BUCKETS // 2
    max_exact = nb // 2
    ret = jnp.where(rel > 0, nb, 0)
    n = jnp.abs(rel)
    n_f = jnp.maximum(n, 1).astype(F32)
    large = max_exact + (jnp.log(n_f / max_exact) / math.log(T5_MAX_DIST / max_exact)
                         * (nb - max_exact)).astype(jnp.int32)
    large = jnp.minimum(large, nb - 1)
    return ret + jnp.where(n < max_exact, n, large)


def _a_index(q_pos, k_pos, k_real):
    rel = np.clip(q_pos[:, None] - k_pos[None, :], -A_REL_CLIP, A_REL_CLIP) + A_REL_CLIP
    qc, kc = q_pos[:, None] // CHUNK, k_pos[None, :] // CHUNK
    ok = k_real[None, :] & (kc <= qc) & (kc >= qc - A_PAST_CHUNKS)
    return np.where(ok, rel, -1).astype(np.int32)


def _b_index(q_pos, k_pos, k_real):
    idx = _t5_bucket(k_pos[None, :] - q_pos[:, None])
    ok = k_real[None, :] & ((k_pos[None, :] // CHUNK) <= (q_pos[:, None] // CHUNK))
    return jnp.where(ok, idx, -1).astype(jnp.int32)


def _rope_tables(pos, scale):
    half = C_ROPE // 2
    inv = ROPE_THETA ** (-jnp.arange(half, dtype=F32) / half)
    ang = pos.astype(F32)[:, None] * inv[None, :]
    cos = jnp.concatenate([jnp.cos(ang)] * 2, axis=-1)
    sin = jnp.concatenate([jnp.sin(ang)] * 2, axis=-1)
    n = pos.shape[0]
    z_nope = jnp.zeros((n, C_NOPE), F32)
    z_tail = jnp.zeros((n, LANES - C_NOPE - C_ROPE), F32)
    cq = jnp.concatenate([jnp.full((n, C_NOPE), scale, F32), cos * scale, z_tail], axis=-1)
    sq = jnp.concatenate([z_nope, sin * scale, z_tail], axis=-1)
    ck = jnp.concatenate([z_nope, cos, z_tail], axis=-1)
    sk = jnp.concatenate([z_nope, sin, z_tail], axis=-1)
    return cq, sq, ck, sk


def _rot_cols(w):
    half = w.shape[-1] // 2
    return jnp.concatenate([-w[..., half:], w[..., :half]], axis=-1)


def kernel(x_prompt, x_sample, cache_a_k, cache_a_v, cache_b_k, cache_b_v, cache_c_kv, cache_c_kr,
           t5_bias, ffn1_norm, ffn1_w_gu, ffn1_w_down, mix_norm, ffn2_norm, ffn2_w_gu, ffn2_w_down,
           e_w_in, a_rel_bias, b_lambda_q1, b_lambda_k1, b_lambda_q2, b_lambda_k2, b_subln, e_w_out,
           c_w_in, c_q_norm, c_kv_norm, c_w_q_up, c_w_kv_up, c_w_out, final_norm):
    batch, seq, d = x_prompt.shape
    dec_batch, t_new, _ = x_sample.shape
    past = cache_b_k.shape[2]
    n_cache_a = cache_a_k.shape[2]
    a_w = A_HEADS * A_HEAD_DIM
    b_w = B_HEADS * 2 * B_HEAD_DIM
    tq = ATT_TILE
    a_pad = A_PAST_CHUNKS * CHUNK
    a_keep = min(a_pad, seq)
    assert tq + 1 >= T5_MAX_DIST and tq % CHUNK == 0 and a_pad % tq == 0
    far_bucket = T5_BUCKETS // 2 - 1
    lam_init = 0.8 - 0.6 * math.exp(-0.3 * 0)
    c_scale = (C_NOPE + C_ROPE) ** -0.5 * math.log2(math.e)
    row2 = lambda v: v.reshape(1, -1)

    e_in = e_w_in[0].astype(BF16)
    e_out_a, e_out_b = e_w_out[0, :a_w].astype(BF16), e_w_out[0, a_w:].astype(BF16)
    w_in = c_w_in[0]
    w_kr = w_in[:, C_Q_LORA + C_KV_LORA:]
    c_in = jnp.concatenate([w_in[:, :C_Q_LORA + C_KV_LORA], jnp.zeros((d, C_NOPE), F32),
                            w_kr, _rot_cols(w_kr)], axis=-1).astype(BF16)
    wq = c_w_q_up[0].reshape(C_Q_LORA, C_HEADS, C_NOPE + C_ROPE)
    wq_rope = wq[..., C_NOPE:]
    c_q = jnp.concatenate([wq[..., :C_NOPE], wq_rope, _rot_cols(wq_rope)], axis=-1)
    c_q = c_q.reshape(C_Q_LORA, C_HEADS * LANES).astype(BF16)
    wkv = c_w_kv_up[0].reshape(C_KV_LORA, C_HEADS, C_NOPE + C_V)
    c_k = jnp.concatenate([wkv[..., :C_NOPE], jnp.zeros_like(wkv[..., C_NOPE:])], axis=-1)
    c_k = c_k.reshape(C_KV_LORA, C_HEADS * LANES).astype(BF16)
    wv_pairs = wkv[..., C_NOPE:].reshape(C_KV_LORA, C_HEADS // 2, 2, C_V)
    z_v = jnp.zeros_like(wv_pairs[:, :, 0])
    c_v = jnp.stack([jnp.concatenate([wv_pairs[:, :, 0], z_v], axis=-1),
                     jnp.concatenate([z_v, wv_pairs[:, :, 1]], axis=-1)], axis=2)
    c_v = c_v.reshape(C_KV_LORA, C_HEADS * LANES).astype(BF16)
    c_k_abs = jnp.pad(jnp.transpose(wkv[..., :C_NOPE], (1, 2, 0)), ((0, 0), (0, LANES - C_NOPE), (0, 0)))
    c_k_abs = c_k_abs.astype(BF16)
    c_v_abs = jnp.transpose(c_v.reshape(C_KV_LORA, C_HEADS, LANES), (1, 0, 2))
    c_out_w = c_w_out[0].astype(BF16)
    lams = [row2(b_lambda_q1[0]), row2(b_lambda_k1[0]), row2(b_lambda_q2[0]), row2(b_lambda_k2[0])]
    subln = row2(b_subln[0])

    pos_s = past + jnp.arange(t_new)
    r = jnp.arange(tq)
    assert A_TILE % CHUNK == 0 and a_pad % A_TILE == 0
    a_idx_p, a_lay_p = _unique_blocks(
        _a_index(a_pad + np.arange(A_TILE), np.arange(a_pad + A_TILE), np.ones((a_pad + A_TILE,), bool)),
        CHUNK, LANES)
    ka = n_cache_a + t_new
    ka_pad = -(-ka // LANES) * LANES
    pos_s_np = past + np.arange(t_new)
    a_kpos = np.concatenate([past - n_cache_a + np.arange(n_cache_a), pos_s_np,
                             np.zeros((ka_pad - ka,), np.int64)])
    a_real = np.arange(ka_pad) < ka
    a_idx_s, a_lay_s = _unique_blocks(_a_index(pos_s_np, a_kpos, a_real & (a_kpos >= 0)), t_new, LANES)
    b_idx_p = jnp.stack([_b_index(tq + r, r, jnp.ones((tq,), bool)),
                         _b_index(r, r, jnp.ones((tq,), bool))])
    kb = past + t_new
    kb_pad = -(-kb // LANES) * LANES
    b_kpos = jnp.arange(kb_pad)
    b_real = b_kpos < kb
    b_idx_s = _b_index(pos_s, b_kpos, b_real)
    c_mask_s = jnp.where(b_real[None, :] & ((b_kpos[None, :] // CHUNK) <= (pos_s[:, None] // CHUNK)),
                         0.0, NEG).astype(F32)

    a_bias_p = _bias_expand(a_idx_p, a_rel_bias[0], None, a_lay_p)
    a_bias_s = _bias_expand(a_idx_s, a_rel_bias[0], None, a_lay_s)
    t5_t = t5_bias.T
    b_bias_p = jnp.stack([_bias_expand(b_idx_p[0], t5_t, far_bucket),
                          _bias_expand(b_idx_p[1], t5_t, far_bucket)])
    b_bias_s = _bias_expand(b_idx_s, t5_t, far_bucket)

    xp = x_prompt.reshape(batch * seq, d)
    xs = x_sample.reshape(dec_batch * t_new, d)
    g1, gm, g2 = row2(ffn1_norm[0]), row2(mix_norm[0]), row2(ffn2_norm[0])

    xp, xs = _ffn(xp, [], xs, [], [], g1, ffn1_w_gu, ffn1_w_down, 0)

    aq, ak, av, bq, bk, bv, akf, avf, bkf, bvf = _eproj(xp, gm, e_in, seq, a_keep)
    to3 = lambda t, n: t.reshape(n, -1, t.shape[-1])
    a_out_p = _a_prompt(to3(aq, batch), to3(ak, batch), to3(av, batch), a_bias_p)
    b_out_p = _b_prompt(to3(bq, batch), to3(bk, batch), to3(bv, batch), b_bias_p, lams, subln, lam_init)
    p_a_k = akf.reshape(1, batch, a_keep, A_HEADS, A_HEAD_DIM)
    p_a_v = avf.reshape(1, batch, a_keep, A_HEADS, A_HEAD_DIM)
    p_b_k = bkf.reshape(1, batch, seq, B_HEADS, 2 * B_HEAD_DIM)
    p_b_v = bvf.reshape(1, batch, seq, B_HEADS, 2 * B_HEAD_DIM)

    aq, ak, av, bq, bk, bv, akf, avf, bkf, bvf = _eproj(xs, gm, e_in, dec_batch * t_new, dec_batch * t_new)
    dims_major = lambda c: jnp.transpose(c, (0, 2, 3, 1)).reshape(dec_batch, a_w, n_cache_a)
    a_out_s = _a_sample(to3(aq, dec_batch), dims_major(cache_a_k[0]), dims_major(cache_a_v[0]),
                        to3(ak, dec_batch), to3(av, dec_batch), a_bias_s)
    pos_head_rows = lambda c: c.reshape(dec_batch, past * B_HEADS, 2 * B_HEAD_DIM)
    b_out_s = _b_sample(to3(bq, dec_batch), pos_head_rows(cache_b_k[0]), pos_head_rows(cache_b_v[0]),
                        to3(bk, dec_batch), to3(bv, dec_batch), b_bias_s, lams, subln, lam_init)
    s_a_k = akf.reshape(1, dec_batch, t_new, A_HEADS, A_HEAD_DIM)
    s_a_v = avf.reshape(1, dec_batch, t_new, A_HEADS, A_HEAD_DIM)
    s_b_k = bkf.reshape(1, dec_batch, t_new, B_HEADS, 2 * B_HEAD_DIM)
    s_b_v = bvf.reshape(1, dec_batch, t_new, B_HEADS, 2 * B_HEAD_DIM)
    flat = lambda t: t.reshape(-1, t.shape[-1])
    xp, xs = _ffn(xp, [flat(a_out_p), flat(b_out_p)], xs, [flat(a_out_s), flat(b_out_s)], [e_out_a, e_out_b],
                  g2, ffn2_w_gu, ffn2_w_down, 0)

    g1, gm, g2 = row2(ffn1_norm[1]), row2(mix_norm[1]), row2(ffn2_norm[1])
    gq, gkv, gfin = row2(c_q_norm[0]), row2(c_kv_norm[0]), row2(final_norm)
    xp, xs = _ffn(xp, [], xs, [], [], g1, ffn1_w_gu, ffn1_w_down, 1)

    q, ckvf, ckvb, kr, krf = _cproj(xp, gm, c_in, gq, gkv, c_q, _rope_tables(jnp.arange(seq), c_scale))
    c_out_p = _c_prompt(to3(q, batch), to3(ckvb, batch), to3(kr, batch), c_k, c_v)
    p_c_kv = ckvf.reshape(1, batch, seq, C_KV_LORA)
    p_c_kr = krf.reshape(1, batch, seq, C_ROPE)

    q, ckvf, ckvb, kr, krf = _cproj(xs, gm, c_in, gq, gkv, c_q,
                                    _rope_tables(jnp.tile(pos_s, dec_batch), c_scale))
    kr_cache = jnp.transpose(cache_c_kr[0], (0, 2, 1))
    c_out_s = _c_sample(to3(q, dec_batch), cache_c_kv[0], kr_cache, to3(ckvb, dec_batch), to3(kr, dec_batch),
                        c_k_abs, c_v_abs, jnp.tile(c_mask_s, (C_HEADS, 1)))
    s_c_kv = ckvf.reshape(1, dec_batch, t_new, C_KV_LORA)
    s_c_kr = krf.reshape(1, dec_batch, t_new, C_ROPE)
    y_prompt, y_sample = _ffn(xp, [flat(c_out_p)], xs, [flat(c_out_s)], [c_out_w], g2, ffn2_w_gu, ffn2_w_down,
                              1, gfin)

    return (y_prompt.reshape(batch, seq, d), y_sample.reshape(dec_batch, t_new, d),
            p_a_k, p_a_v, p_b_k, p_b_v, p_c_kv, p_c_kr,
            s_a_k, s_a_v, s_b_k, s_b_v, s_c_kv, s_c_kr)
```
